```python
import math
import jax, jax.numpy as jnp
from jax import lax
import numpy as np

D_MODEL = 1024
BATCH = 16
SEQ = 4096
DEPTH = 1

PLE_DIM = 256
SSM_WIDTH = D_MODEL // 2
SSM_GROUP = 16
SSM_GROUPS = SSM_WIDTH // SSM_GROUP
SSM_STATE = 64
POOL_WIDTH = D_MODEL // 2
POOL_WINDOWS = (2, 4, 8, 16)
POOL_GROUPS = len(POOL_WINDOWS)
POOL_GROUP_DIM = POOL_WIDTH // POOL_GROUPS
N_BRANCHES = 2
IN_PROJ_WIDTH = SSM_WIDTH + POOL_WIDTH + N_BRANCHES * D_MODEL
N_EXPERTS = 32
TOP_K = 4
D_EXPERT = D_MODEL
SWIGLU_LIMIT = 7.0
SWIGLU_ALPHA = 1.702
EXPERT_BLOCK = 128
LN_EPS = 1e-5
DEEPNORM_ALPHA = (2.0 * DEPTH) ** 0.25
DEEPNORM_BETA = (8.0 * DEPTH) ** -0.25

kernel_name = "hybrid_s5_pool_moe_deepnorm"


def _layer_norm(x, g, b):
    xf = x.astype(jnp.float32)
    mu = jnp.mean(xf, axis=-1, keepdims=True)
    xc = xf - mu
    var = jnp.mean(xc * xc, axis=-1, keepdims=True)
    y = xc * lax.rsqrt(var + LN_EPS)
    return (y * g.astype(jnp.float32) + b.astype(jnp.float32)).astype(x.dtype)


def _linear_recurrence_op(e1, e2):
    a1, b1 = e1
    a2, b2 = e2
    return a2 * a1, a2 * b1 + b2


def _s5_branch(u, lam_re, lam_im, log_step, b_re, b_im, c_re, c_im, d_skip, w_val, w_gate):
    bsz, seq, _ = u.shape
    f32 = jnp.float32
    uf = u.astype(f32).reshape(bsz, seq, SSM_GROUPS, SSM_GROUP)
    lam = lax.complex(lam_re.astype(f32), lam_im.astype(f32))
    step = jnp.exp(log_step.astype(f32))[:, None]
    lam_bar = jnp.exp(lam * step)
    b_cplx = lax.complex(b_re.astype(f32), b_im.astype(f32))
    b_bar = ((lam_bar - 1.0) / lam)[..., None] * b_cplx
    bu = jnp.einsum('blgh,gph->blgp', uf.astype(jnp.complex64), b_bar)
    a = jnp.broadcast_to(lam_bar, (1, seq) + lam_bar.shape)
    _, states = lax.associative_scan(_linear_recurrence_op, (a, bu), axis=1)
    c_cplx = lax.complex(c_re.astype(f32), c_im.astype(f32))
    y = jnp.einsum('blgp,ghp->blgh', states, c_cplx).real + d_skip.astype(f32) * uf
    z = jax.nn.gelu(y.reshape(bsz, seq, SSM_WIDTH))
    return (z @ w_val) * jax.nn.sigmoid(z @ w_gate)


def _pool_branch(u, w_group, scale, w_proj):
    bsz, seq, _ = u.shape
    f32 = jnp.float32
    uf = u.astype(f32).reshape(bsz, seq, POOL_GROUPS, POOL_GROUP_DIM)
    cs = jnp.cumsum(uf, axis=1)
    pos = jnp.arange(1, seq + 1, dtype=f32)
    outs = []
    for g, w in enumerate(POOL_WINDOWS):
        cs_g = cs[:, :, g]
        lagged = jnp.pad(cs_g[:, :seq - w], ((0, 0), (w, 0), (0, 0)))
        count = jnp.minimum(pos, float(w))[None, :, None]
        outs.append((cs_g - lagged) / count - uf[:, :, g])
    pooled = jnp.stack(outs, axis=2)
    mixed = jnp.einsum('blgc,gcd->blgd', pooled, w_group.astype(f32)).reshape(bsz, seq, POOL_WIDTH)
    return (mixed * scale) @ w_proj


def _moe(h, w_router, b_router, w_gate, b_gate, w_up, b_up, w_down, b_down):
    bsz, seq, d = h.shape
    t = h.reshape(-1, d)
    n_tok = t.shape[0]
    logits = (t @ w_router + b_router).astype(jnp.float32)
    top_val, top_idx = lax.top_k(logits, TOP_K)
    weights = jax.nn.softmax(top_val, axis=-1)
    n_slots = n_tok * TOP_K
    n_blocks = -(-n_slots // EXPERT_BLOCK) + N_EXPERTS
    n_rows = n_blocks * EXPERT_BLOCK
    flat_e = top_idx.reshape(-1).astype(jnp.int32)
    order = jnp.argsort(flat_e)
    sorted_e = flat_e[order]
    counts = jnp.bincount(flat_e, length=N_EXPERTS)
    start = jnp.cumsum(counts) - counts
    padded = ((counts + EXPERT_BLOCK - 1) // EXPERT_BLOCK) * EXPERT_BLOCK
    pad_end = jnp.cumsum(padded)
    pad_start = pad_end - padded
    rank = jnp.arange(n_slots, dtype=jnp.int32) - start[sorted_e]
    dest_sorted = (pad_start[sorted_e] + rank).astype(jnp.int32)
    row_token = jnp.full((n_rows,), n_tok, jnp.int32).at[dest_sorted].set((order // TOP_K).astype(jnp.int32))
    t_pad = jnp.concatenate([t, jnp.zeros((1, d), t.dtype)], axis=0)
    xs = t_pad[row_token].reshape(n_blocks, EXPERT_BLOCK, d)
    block_start = jnp.arange(n_blocks, dtype=pad_end.dtype) * EXPERT_BLOCK
    block_expert = jnp.minimum(jnp.searchsorted(pad_end, block_start, side='right'), N_EXPERTS - 1)

    def expert_block(args):
        xb, e = args
        g = xb @ w_gate[e] + b_gate[e]
        u = xb @ w_up[e] + b_up[e]
        g = jnp.minimum(g, SWIGLU_LIMIT)
        u = jnp.clip(u, -SWIGLU_LIMIT, SWIGLU_LIMIT)
        glu = g * jax.nn.sigmoid(SWIGLU_ALPHA * g)
        return ((u + 1.0) * glu) @ w_down[e] + b_down[e]

    ys = lax.map(expert_block, (xs, block_expert)).reshape(n_rows, d)
    dest = jnp.zeros((n_slots,), jnp.int32).at[order].set(dest_sorted).reshape(n_tok, TOP_K)
    out = jnp.einsum('tk,tkd->td', weights.astype(ys.dtype), ys[dest])
    return out.reshape(bsz, seq, d)


def setup_inputs(seed: int = 0) -> dict:
    key = jax.random.key(seed)
    ks = iter(jax.random.split(key, 40))
    f32 = jnp.float32

    def nrm(shape, scale):
        return jax.random.normal(next(ks), shape, f32) * scale

    L = DEPTH
    D = D_MODEL
    G, P, H = SSM_GROUPS, SSM_STATE, SSM_GROUP
    n_idx = jnp.arange(P, dtype=f32)
    return {
        "x": nrm((BATCH, SEQ, D), 1.0),
        "p": nrm((DEPTH, BATCH, SEQ, PLE_DIM), 1.0),
        "w_in": nrm((L, D, IN_PROJ_WIDTH), D ** -0.5),
        "ssm_lambda_re": -0.5 + nrm((L, G, P), 0.01),
        "ssm_lambda_im": math.pi * n_idx + nrm((L, G, P), 0.01),
        "ssm_log_step": jax.random.uniform(next(ks), (L, G), f32, math.log(1e-3), math.log(1e-1)),
        "ssm_b_re": nrm((L, G, P, H), (2.0 * H) ** -0.5),
        "ssm_b_im": nrm((L, G, P, H), (2.0 * H) ** -0.5),
        "ssm_c_re": nrm((L, G, H, P), (2.0 * P) ** -0.5),
        "ssm_c_im": nrm((L, G, H, P), (2.0 * P) ** -0.5),
        "ssm_d": nrm((L, G, H), 1.0),
        "w_glu_val": nrm((L, SSM_WIDTH, D), SSM_WIDTH ** -0.5),
        "w_glu_gate": nrm((L, SSM_WIDTH, D), SSM_WIDTH ** -0.5),
        "w_pool_group": nrm((L, POOL_GROUPS, POOL_GROUP_DIM, POOL_GROUP_DIM), POOL_GROUP_DIM ** -0.5),
        "pool_scale": 1.0 + nrm((L, POOL_WIDTH), 0.02),
        "w_pool_proj": nrm((L, POOL_WIDTH, D), POOL_WIDTH ** -0.5),
        "w_out": nrm((L, D, D), DEEPNORM_BETA * D ** -0.5),
        "ln1_g": 1.0 + nrm((L, D), 0.02),
        "ln1_b": nrm((L, D), 0.02),
        "w_router": nrm((L, D, N_EXPERTS), D ** -0.5),
        "b_router": nrm((L, N_EXPERTS), 0.01),
        "w_gate": nrm((L, N_EXPERTS, D, D_EXPERT), D ** -0.5),
        "b_gate": nrm((L, N_EXPERTS, D_EXPERT), 0.01),
        "w_up": nrm((L, N_EXPERTS, D, D_EXPERT), D ** -0.5),
        "b_up": nrm((L, N_EXPERTS, D_EXPERT), 0.01),
        "w_down": nrm((L, N_EXPERTS, D_EXPERT, D), DEEPNORM_BETA * D_EXPERT ** -0.5),
        "b_down": nrm((L, N_EXPERTS, D), 0.01),
        "w_ple_gate": nrm((L, D, D), D ** -0.5),
        "w_ple_proj": nrm((L, PLE_DIM, D), PLE_DIM ** -0.5),
        "ln2_g": 1.0 + nrm((L, D), 0.02),
        "ln2_b": nrm((L, D), 0.02),
    }


def reference(x, p, w_in, ssm_lambda_re, ssm_lambda_im, ssm_log_step, ssm_b_re, ssm_b_im,
              ssm_c_re, ssm_c_im, ssm_d, w_glu_val, w_glu_gate, w_pool_group, pool_scale,
              w_pool_proj, w_out, ln1_g, ln1_b, w_router, b_router, w_gate, b_gate, w_up, b_up,
              w_down, b_down, w_ple_gate, w_ple_proj, ln2_g, ln2_b):
    h = x
    for i in range(DEPTH):
        proj = h @ w_in[i]
        u_ssm = proj[..., :SSM_WIDTH]
        u_pool = proj[..., SSM_WIDTH:SSM_WIDTH + POOL_WIDTH]
        gate_a = jax.nn.sigmoid(proj[..., SSM_WIDTH + POOL_WIDTH:SSM_WIDTH + POOL_WIDTH + D_MODEL])
        gate_b = jax.nn.sigmoid(proj[..., SSM_WIDTH + POOL_WIDTH + D_MODEL:])
        y_ssm = _s5_branch(u_ssm, ssm_lambda_re[i], ssm_lambda_im[i], ssm_log_step[i],
                           ssm_b_re[i], ssm_b_im[i], ssm_c_re[i], ssm_c_im[i], ssm_d[i],
                           w_glu_val[i], w_glu_gate[i])
        y_pool = _pool_branch(u_pool, w_pool_group[i], pool_scale[i], w_pool_proj[i])
        merged = gate_a * y_ssm + gate_b * y_pool
        h = _layer_norm(DEEPNORM_ALPHA * h + merged @ w_out[i], ln1_g[i], ln1_b[i])
        moe_out = _moe(h, w_router[i], b_router[i], w_gate[i], b_gate[i], w_up[i], b_up[i],
                       w_down[i], b_down[i])
        ple = jax.nn.sigmoid(h @ w_ple_gate[i]) * (p[i] @ w_ple_proj[i])
        h = _layer_norm(DEEPNORM_ALPHA * h + moe_out + ple, ln2_g[i], ln2_b[i])
    return h
```

```python
import functools
import math

import jax
import jax.numpy as jnp
from jax import lax
from jax.experimental import pallas as pl
from jax.experimental.pallas import tpu as pltpu

F32 = jnp.float32
BF16 = jnp.bfloat16

LN_EPS = 1e-5
SWIGLU_LIMIT = 7.0
SWIGLU_ALPHA = 1.702
POOL_WINDOWS = (2, 4, 8, 16)
TOP_K = 4

MIX_STEPS = 32
SCAN_LANES = 512
MOE_BLOCK = 512
CMB_ROWS = 512
VMEM_LIMIT = 60 * 1024 * 1024


def _sigmoid(v):
    return jax.nn.sigmoid(v)


def _layer_norm(v, g, b):
    mu = jnp.mean(v, axis=-1, keepdims=True)
    vc = v - mu
    var = jnp.mean(vc * vc, axis=-1, keepdims=True)
    return vc * lax.rsqrt(var + LN_EPS) * g + b


def _dot(a, b):
    return jnp.dot(a, b, preferred_element_type=F32)


def _mixer_kernel(alpha, nb, n_exp,
                  x_ref, p_ref, w_in_ref, bm_ref, cm_ref, lam_ref, dskip_ref,
                  w_glu_ref, w_pg_ref, pscale_ref, w_pp_ref, w_out_ref,
                  ln_g_ref, ln_b_ref, wrt_ref, brt_ref, w_pleg_ref, w_plep_ref,
                  h_ref, r_ref, idx_ref, wts_ref, rank_ref, cnt_ref,
                  proj_ref, bu_ref, st_ref, upool_ref, state_ref, carry_ref, tri_ref):
    i = pl.program_id(0)
    rows = x_ref.shape[0]
    steps = rows // nb
    d_model = x_ref.shape[1]
    ssm_w = dskip_ref.shape[1]
    pool_w = pscale_ref.shape[1]
    half_w = ssm_w // 2
    half_s = bu_ref.shape[1] // 2
    plane = half_s // 2
    halo = upool_ref.shape[0] - rows

    @pl.when(i == 0)
    def _init():
        state_ref[...] = jnp.zeros_like(state_ref)
        carry_ref[...] = jnp.zeros_like(carry_ref)
        upool_ref[0:halo, :] = jnp.zeros((halo, pool_w), F32)
        ri = lax.broadcasted_iota(jnp.int32, (rows, rows), 0)
        ci = lax.broadcasted_iota(jnp.int32, (rows, rows), 1)
        tri_ref[...] = (ri < ci).astype(BF16)

    xf = x_ref[...]
    proj_ref[...] = _dot(xf.astype(BF16), w_in_ref[...])

    us = proj_ref[:, 0:ssm_w]
    usb = us.astype(BF16)
    for hf in range(2):
        bu_ref[:, hf * half_s:(hf + 1) * half_s] = _dot(
            usb[:, hf * half_w:(hf + 1) * half_w], bm_ref[hf])

    for hf in range(2):
        for q in range(plane // SCAN_LANES):
            cre = hf * half_s + q * SCAN_LANES
            cim = cre + plane
            cl = hf * plane + q * SCAN_LANES
            a_re = jnp.broadcast_to(lam_ref[0:1, cl:cl + SCAN_LANES], (nb, SCAN_LANES))
            a_im = jnp.broadcast_to(lam_ref[1:2, cl:cl + SCAN_LANES], (nb, SCAN_LANES))

            def body(t, carry, cre=cre, cim=cim, a_re=a_re, a_im=a_im):
                s_re, s_im = carry
                r0 = pl.multiple_of(t * nb, nb)
                b_re = bu_ref[pl.ds(r0, nb), cre:cre + SCAN_LANES]
                b_im = bu_ref[pl.ds(r0, nb), cim:cim + SCAN_LANES]
                n_re = a_re * s_re - a_im * s_im + b_re
                n_im = a_re * s_im + a_im * s_re + b_im
                st_ref[pl.ds(r0, nb), cre:cre + SCAN_LANES] = n_re.astype(BF16)
                st_ref[pl.ds(r0, nb), cim:cim + SCAN_LANES] = n_im.astype(BF16)
                return n_re, n_im

            s_re, s_im = lax.fori_loop(
                0, steps, body,
                (state_ref[:, cre:cre + SCAN_LANES], state_ref[:, cim:cim + SCAN_LANES]))
            state_ref[:, cre:cre + SCAN_LANES] = s_re
            state_ref[:, cim:cim + SCAN_LANES] = s_im

    y = jnp.concatenate(
        [_dot(st_ref[:, hf * half_s:(hf + 1) * half_s], cm_ref[hf]) for hf in range(2)],
        axis=1) + dskip_ref[...] * us
    z = 0.5 * y * (1.0 + jnp.tanh(math.sqrt(2.0 / math.pi) * (y + 0.044715 * (y * y * y))))
    vg = _dot(z.astype(BF16), w_glu_ref[...])
    y_ssm = vg[:, 0:d_model] * _sigmoid(vg[:, d_model:2 * d_model])

    up = proj_ref[:, ssm_w:ssm_w + pool_w]
    upool_ref[halo:halo + rows, :] = up
    t_abs = (i * steps + lax.broadcasted_iota(jnp.int32, (rows, 1), 0) // nb).astype(F32)
    gdim = pool_w // len(POOL_WINDOWS)
    pooled = []
    for g, w in enumerate(POOL_WINDOWS):
        cur = upool_ref[:, g * gdim:(g + 1) * gdim]
        span = 1
        while span < w:
            sh = span * nb
            cur = cur[sh:] + cur[:-sh]
            span *= 2
        win = cur[cur.shape[0] - rows:]
        inv = 1.0 / jnp.minimum(t_abs + 1.0, float(w))
        pooled.append(win * inv - up[:, g * gdim:(g + 1) * gdim])
    upool_ref[0:halo, :] = upool_ref[rows:rows + halo, :]
    pooled = jnp.concatenate(pooled, axis=1)
    mixed = _dot(pooled.astype(BF16), w_pg_ref[...])
    y_pool = _dot((mixed * pscale_ref[...]).astype(BF16), w_pp_ref[...])

    g0 = ssm_w + pool_w
    merged = (_sigmoid(proj_ref[:, g0:g0 + d_model]) * y_ssm
              + _sigmoid(proj_ref[:, g0 + d_model:g0 + 2 * d_model]) * y_pool)
    pre = alpha * xf + _dot(merged.astype(BF16), w_out_ref[...])
    h1 = _layer_norm(pre, ln_g_ref[...], ln_b_ref[...])
    hb = h1.astype(BF16)
    h_ref[...] = hb

    lt = lax.dot_general(wrt_ref[...], h1, (((1,), (1,)), ((), ())),
                         precision=lax.Precision.HIGHEST,
                         preferred_element_type=F32) + brt_ref[:, 0:1]
    eio = lax.broadcasted_iota(jnp.int32, (n_exp, rows), 0)
    vals, idxs = [], []
    for _ in range(TOP_K):
        m = jnp.max(lt, axis=0, keepdims=True)
        sel = jnp.min(jnp.where(lt == m, eio, n_exp), axis=0, keepdims=True)
        vals.append(m)
        idxs.append(sel)
        lt = jnp.where(eio == sel, -jnp.inf, lt)
    exps = [jnp.exp(v - vals[0]) for v in vals]
    den = exps[0] + exps[1] + exps[2] + exps[3]
    wts_ref[...] = jnp.concatenate([e / den for e in exps], axis=0)
    idx_ref[...] = jnp.concatenate(idxs, axis=0)

    run = carry_ref[:, 0:1]
    ranks = []
    for k in range(TOP_K):
        oh = (eio == idxs[k]).astype(F32)
        excl = _dot(oh.astype(BF16), tri_ref[...])
        ranks.append(jnp.sum(oh * (run + excl), axis=0, keepdims=True))
        run = run + jnp.sum(oh, axis=1, keepdims=True)
    rank_ref[...] = jnp.concatenate(ranks, axis=0).astype(jnp.int32)
    new_carry = jnp.broadcast_to(run, carry_ref.shape)
    carry_ref[...] = new_carry
    cnt_ref[...] = new_carry

    pp = _dot(p_ref[...].astype(BF16), w_plep_ref[...])
    gate = _sigmoid(_dot(hb, w_pleg_ref[...]))
    r_ref[...] = alpha * h1 + gate * pp


def _const_spec(shape):
    zeros = (0,) * len(shape)
    return pl.BlockSpec(shape, lambda i: zeros, pipeline_mode=pl.Buffered(1))


def _mixer_call(x2, p2, wts, alpha, nb, n_exp):
    n_tok, d_model = x2.shape
    rows = MIX_STEPS * nb
    grid = n_tok // rows
    ple_dim = p2.shape[1]
    state_cols = 2 * wts["bm"].shape[2]
    pool_w = wts["pscale"].shape[1]
    halo = max(POOL_WINDOWS) * nb

    def row_spec(width):
        return pl.BlockSpec((rows, width), lambda i: (i, 0))

    names = ["w_in", "bm", "cm", "lam", "dskip", "w_glu", "w_pg", "pscale", "w_pp",
             "w_out", "ln_g", "ln_b", "wrt", "brt", "w_pleg", "w_plep"]
    ops = [wts[n] for n in names]
    in_specs = [row_spec(d_model), row_spec(ple_dim)] + [_const_spec(o.shape) for o in ops]
    k_spec = pl.BlockSpec((TOP_K, rows), lambda i: (0, i))
    out_shape = (
        jax.ShapeDtypeStruct((n_tok, d_model), BF16),
        jax.ShapeDtypeStruct((n_tok, d_model), F32),
        jax.ShapeDtypeStruct((TOP_K, n_tok), jnp.int32),
        jax.ShapeDtypeStruct((TOP_K, n_tok), F32),
        jax.ShapeDtypeStruct((TOP_K, n_tok), jnp.int32),
        jax.ShapeDtypeStruct((n_exp, 128), F32),
    )
    out_specs = (row_spec(d_model), row_spec(d_model), k_spec, k_spec, k_spec,
                 pl.BlockSpec((n_exp, 128), lambda i: (0, 0)))
    scratch = [
        pltpu.VMEM((rows, wts["w_in"].shape[1]), F32),
        pltpu.VMEM((rows, state_cols), F32),
        pltpu.VMEM((rows, state_cols), BF16),
        pltpu.VMEM((halo + rows, pool_w), F32),
        pltpu.VMEM((nb, state_cols), F32),
        pltpu.VMEM((n_exp, 128), F32),
        pltpu.VMEM((rows, rows), BF16),
    ]
    return pl.pallas_call(
        functools.partial(_mixer_kernel, alpha, nb, n_exp),
        grid=(grid,),
        in_specs=in_specs,
        out_specs=out_specs,
        out_shape=out_shape,
        scratch_shapes=scratch,
        compiler_params=pltpu.CompilerParams(
            dimension_semantics=("arbitrary",), vmem_limit_bytes=VMEM_LIMIT),
        name="mixer",
    )(x2, p2, *ops)


def _expert_kernel(be_ref, nv_ref, xs_ref, wg_ref, bg_ref, wu_ref, bu_ref, wd_ref, bd_ref,
                   ys_ref):
    i = pl.program_id(0)

    @pl.when(nv_ref[i] > 0)
    def _compute():
        xb = xs_ref[...]
        g = _dot(xb, wg_ref[0]) + bg_ref[0]
        u = _dot(xb, wu_ref[0]) + bu_ref[0]
        g = jnp.minimum(g, SWIGLU_LIMIT)
        u = jnp.clip(u, -SWIGLU_LIMIT, SWIGLU_LIMIT)
        act = (u + 1.0) * (g * _sigmoid(SWIGLU_ALPHA * g))
        ys_ref[...] = (_dot(act.astype(BF16), wd_ref[0]) + bd_ref[0]).astype(ys_ref.dtype)

    @pl.when(nv_ref[i] == 0)
    def _skip():
        ys_ref[...] = jnp.zeros_like(ys_ref)


def _expert_call(block_expert, block_valid, xs, wg, bg, wu, bu, wd, bd):
    n_rows, d_model = xs.shape
    d_exp = wg.shape[2]
    n_blocks = n_rows // MOE_BLOCK

    def w_spec(k, n):
        return pl.BlockSpec((1, k, n), lambda i, be, nv: (be[i], 0, 0))

    grid_spec = pltpu.PrefetchScalarGridSpec(
        num_scalar_prefetch=2,
        grid=(n_blocks,),
        in_specs=[
            pl.BlockSpec((MOE_BLOCK, d_model), lambda i, be, nv: (i, 0)),
            w_spec(d_model, d_exp), w_spec(1, d_exp),
            w_spec(d_model, d_exp), w_spec(1, d_exp),
            w_spec(d_exp, d_model), w_spec(1, d_model),
        ],
        out_specs=pl.BlockSpec((MOE_BLOCK, d_model), lambda i, be, nv: (i, 0)),
    )
    return pl.pallas_call(
        _expert_kernel,
        grid_spec=grid_spec,
        out_shape=jax.ShapeDtypeStruct((n_rows, d_model), BF16),
        compiler_params=pltpu.CompilerParams(
            dimension_semantics=("arbitrary",), vmem_limit_bytes=VMEM_LIMIT),
        name="experts",
    )(block_expert, block_valid, xs, wg, bg, wu, bu, wd, bd)


def _combine_kernel(r_ref, yg_ref, wt_ref, ln_g_ref, ln_b_ref, o_ref):
    d_model = r_ref.shape[1]
    acc = r_ref[...]
    for k in range(TOP_K):
        acc = acc + wt_ref[:, k:k + 1] * yg_ref[:, k * d_model:(k + 1) * d_model].astype(F32)
    o_ref[...] = _layer_norm(acc, ln_g_ref[...], ln_b_ref[...])


def _combine_call(r, yg, wt, ln_g, ln_b):
    n_tok, d_model = r.shape
    grid = n_tok // CMB_ROWS
    return pl.pallas_call(
        _combine_kernel,
        grid=(grid,),
        in_specs=[
            pl.BlockSpec((CMB_ROWS, d_model), lambda i: (i, 0)),
            pl.BlockSpec((CMB_ROWS, TOP_K * d_model), lambda i: (i, 0)),
            pl.BlockSpec((CMB_ROWS, TOP_K), lambda i: (i, 0)),
            pl.BlockSpec((1, d_model), lambda i: (0, 0)),
            pl.BlockSpec((1, d_model), lambda i: (0, 0)),
        ],
        out_specs=pl.BlockSpec((CMB_ROWS, d_model), lambda i: (i, 0)),
        out_shape=jax.ShapeDtypeStruct((n_tok, d_model), F32),
        compiler_params=pltpu.CompilerParams(
            dimension_semantics=("arbitrary",), vmem_limit_bytes=VMEM_LIMIT),
        name="combine",
    )(r, yg, wt, ln_g, ln_b)


def _ssm_matrices(lam_re, lam_im, log_step, b_re, b_im, c_re, c_im):
    n_grp, n_state, n_ch = b_re.shape
    lam = lax.complex(lam_re.astype(F32), lam_im.astype(F32))
    step = jnp.exp(log_step.astype(F32))[:, None]
    lam_bar = jnp.exp(lam * step)
    b_bar = ((lam_bar - 1.0) / lam)[..., None] * lax.complex(b_re.astype(F32), b_im.astype(F32))
    hg = n_grp // 2
    eye = jnp.eye(hg, dtype=F32)

    def b_half(bpart):
        return jnp.einsum('gph,gk->ghkp', bpart, eye).reshape(hg * n_ch, hg * n_state)

    def c_half(cpart):
        return jnp.einsum('ghp,gk->gpkh', cpart, eye).reshape(hg * n_state, hg * n_ch)

    bm, cm = [], []
    for hf in range(2):
        sl = slice(hf * hg, (hf + 1) * hg)
        bm.append(jnp.concatenate([b_half(jnp.real(b_bar)[sl]), b_half(jnp.imag(b_bar)[sl])], axis=1))
        cm.append(jnp.concatenate([c_half(c_re.astype(F32)[sl]), -c_half(c_im.astype(F32)[sl])], axis=0))
    lam_rows = jnp.stack([jnp.real(lam_bar).reshape(-1), jnp.imag(lam_bar).reshape(-1)], axis=0)
    return jnp.stack(bm).astype(BF16), jnp.stack(cm).astype(BF16), lam_rows


def _block_diag(w):
    g, c, _ = w.shape
    return jnp.einsum('gcd,gk->gckd', w, jnp.eye(g, dtype=w.dtype)).reshape(g * c, g * c)


def kernel(x, p, w_in, ssm_lambda_re, ssm_lambda_im, ssm_log_step, ssm_b_re, ssm_b_im, ssm_c_re, ssm_c_im, ssm_d, w_glu_val, w_glu_gate, w_pool_group, pool_scale, w_pool_proj, w_out, ln1_g, ln1_b, w_router, b_router, w_gate, b_gate, w_up, b_up, w_down, b_down, w_ple_gate, w_ple_proj, ln2_g, ln2_b):
    bsz, seq, d_model = x.shape
    depth = w_in.shape[0]
    n_exp = w_router.shape[2]
    n_tok = bsz * seq
    alpha = (2.0 * depth) ** 0.25

    h = jnp.transpose(x, (1, 0, 2)).reshape(n_tok, d_model)
    for l in range(depth):
        p2 = jnp.transpose(p[l], (1, 0, 2)).reshape(n_tok, p.shape[-1])
        bm, cm, lam_rows = _ssm_matrices(ssm_lambda_re[l], ssm_lambda_im[l], ssm_log_step[l],
                                         ssm_b_re[l], ssm_b_im[l], ssm_c_re[l], ssm_c_im[l])
        wts = {
            "w_in": w_in[l].astype(BF16), "bm": bm, "cm": cm, "lam": lam_rows,
            "dskip": ssm_d[l].reshape(1, -1).astype(F32),
            "w_glu": jnp.concatenate([w_glu_val[l], w_glu_gate[l]], axis=1).astype(BF16),
            "w_pg": _block_diag(w_pool_group[l]).astype(BF16),
            "pscale": pool_scale[l].reshape(1, -1).astype(F32),
            "w_pp": w_pool_proj[l].astype(BF16),
            "w_out": w_out[l].astype(BF16),
            "ln_g": ln1_g[l].reshape(1, -1).astype(F32),
            "ln_b": ln1_b[l].reshape(1, -1).astype(F32),
            "wrt": jnp.transpose(w_router[l]).astype(F32),
            "brt": jnp.broadcast_to(b_router[l].astype(F32)[:, None], (n_exp, 128)),
            "w_pleg": w_ple_gate[l].astype(BF16),
            "w_plep": w_ple_proj[l].astype(BF16),
        }
        hb, r, idx, wt, rank, cnt = _mixer_call(h, p2, wts, alpha, bsz, n_exp)

        counts = cnt[:, 0].astype(jnp.int32)
        padded = ((counts + MOE_BLOCK - 1) // MOE_BLOCK) * MOE_BLOCK
        pad_end = jnp.cumsum(padded)
        pad_start = pad_end - padded
        dest = pad_start[idx] + rank
        n_rows = n_tok * TOP_K + n_exp * MOE_BLOCK
        tok_ids = jnp.broadcast_to(jnp.arange(n_tok, dtype=jnp.int32)[None, :], dest.shape)
        row_token = jnp.zeros((n_rows,), jnp.int32).at[dest.reshape(-1)].set(tok_ids.reshape(-1))
        blk_start = jnp.arange(n_rows // MOE_BLOCK, dtype=jnp.int32) * MOE_BLOCK
        block_expert = jnp.minimum(jnp.searchsorted(pad_end, blk_start, side='right'),
                                   n_exp - 1).astype(jnp.int32)
        block_valid = (blk_start < pad_end[-1]).astype(jnp.int32)

        xs = jnp.take(hb, row_token, axis=0)
        ys = _expert_call(block_expert, block_valid, xs,
                          w_gate[l].astype(BF16), b_gate[l].astype(F32)[:, None, :],
                          w_up[l].astype(BF16), b_up[l].astype(F32)[:, None, :],
                          w_down[l].astype(BF16), b_down[l].astype(F32)[:, None, :])
        yg = jnp.take(ys, jnp.transpose(dest).reshape(-1), axis=0).reshape(n_tok, TOP_K * d_model)
        h = _combine_call(r, yg, jnp.transpose(wt), ln2_g[l].reshape(1, -1).astype(F32),
                          ln2_b[l].reshape(1, -1).astype(F32))
    return jnp.transpose(h.reshape(seq, bsz, d_model), (1, 0, 2))
```

```python
import functools
import math

import jax
import jax.numpy as jnp
from jax import lax
from jax.experimental import pallas as pl
from jax.experimental.pallas import tpu as pltpu

F32 = jnp.float32
BF16 = jnp.bfloat16

LN_EPS = 1e-5
SWIGLU_LIMIT = 7.0
SWIGLU_ALPHA = 1.702
POOL_WINDOWS = (2, 4, 8, 16)
TOP_K = 4

MIX_STEPS = 32
SCAN_LANES = 512
MOE_BLOCK = 512
CMB_ROWS = 512
VMEM_LIMIT = 60 * 1024 * 1024


def _sigmoid(v):
    return jax.nn.sigmoid(v)


def _layer_norm(v, g, b):
    mu = jnp.mean(v, axis=-1, keepdims=True)
    vc = v - mu
    var = jnp.mean(vc * vc, axis=-1, keepdims=True)
    return vc * lax.rsqrt(var + LN_EPS) * g + b


def _dot(a, b):
    return jnp.dot(a, b, preferred_element_type=F32)


def _mixer_kernel(alpha, nb, n_exp,
                  x_ref, p_ref, w_in_ref, bm_ref, cm_ref, lam_ref, dskip_ref,
                  w_glu_ref, w_pg_ref, pscale_ref, w_pp_ref, w_out_ref,
                  ln_g_ref, ln_b_ref, wrt_ref, brt_ref, w_pleg_ref, w_plep_ref,
                  h_ref, r_ref, idx_ref, wts_ref, rank_ref, cnt_ref,
                  proj_ref, bu_ref, st_ref, upool_ref, state_ref, carry_ref, tri_ref):
    i = pl.program_id(0)
    rows = x_ref.shape[0]
    steps = rows // nb
    d_model = x_ref.shape[1]
    ssm_w = dskip_ref.shape[1]
    pool_w = pscale_ref.shape[1]
    half_w = ssm_w // 2
    half_s = bu_ref.shape[1] // 2
    plane = half_s // 2
    halo = upool_ref.shape[0] - rows

    @pl.when(i == 0)
    def _init():
        state_ref[...] = jnp.zeros_like(state_ref)
        carry_ref[...] = jnp.zeros_like(carry_ref)
        upool_ref[0:halo, :] = jnp.zeros((halo, pool_w), F32)
        ri = lax.broadcasted_iota(jnp.int32, (rows, rows), 0)
        ci = lax.broadcasted_iota(jnp.int32, (rows, rows), 1)
        tri_ref[...] = (ri < ci).astype(BF16)

    xf = x_ref[...]
    proj_ref[...] = _dot(xf.astype(BF16), w_in_ref[...])

    us = proj_ref[:, 0:ssm_w]
    usb = us.astype(BF16)
    for hf in range(2):
        bu_ref[:, hf * half_s:(hf + 1) * half_s] = _dot(
            usb[:, hf * half_w:(hf + 1) * half_w], bm_ref[hf])

    for hf in range(2):
        for q in range(plane // SCAN_LANES):
            cre = hf * half_s + q * SCAN_LANES
            cim = cre + plane
            cl = hf * plane + q * SCAN_LANES
            a_re = jnp.broadcast_to(lam_ref[0:1, cl:cl + SCAN_LANES], (nb, SCAN_LANES))
            a_im = jnp.broadcast_to(lam_ref[1:2, cl:cl + SCAN_LANES], (nb, SCAN_LANES))

            def body(t, carry, cre=cre, cim=cim, a_re=a_re, a_im=a_im):
                s_re, s_im = carry
                r0 = pl.multiple_of(t * nb, nb)
                b_re = bu_ref[pl.ds(r0, nb), cre:cre + SCAN_LANES]
                b_im = bu_ref[pl.ds(r0, nb), cim:cim + SCAN_LANES]
                n_re = a_re * s_re - a_im * s_im + b_re
                n_im = a_re * s_im + a_im * s_re + b_im
                st_ref[pl.ds(r0, nb), cre:cre + SCAN_LANES] = n_re.astype(BF16)
                st_ref[pl.ds(r0, nb), cim:cim + SCAN_LANES] = n_im.astype(BF16)
                return n_re, n_im

            s_re, s_im = lax.fori_loop(
                0, steps, body,
                (state_ref[:, cre:cre + SCAN_LANES], state_ref[:, cim:cim + SCAN_LANES]))
            state_ref[:, cre:cre + SCAN_LANES] = s_re
            state_ref[:, cim:cim + SCAN_LANES] = s_im

    y = jnp.concatenate(
        [_dot(st_ref[:, hf * half_s:(hf + 1) * half_s], cm_ref[hf]) for hf in range(2)],
        axis=1) + dskip_ref[...] * us
    z = 0.5 * y * (1.0 + jnp.tanh(math.sqrt(2.0 / math.pi) * (y + 0.044715 * (y * y * y))))
    vg = _dot(z.astype(BF16), w_glu_ref[...])
    y_ssm = vg[:, 0:d_model] * _sigmoid(vg[:, d_model:2 * d_model])

    up = proj_ref[:, ssm_w:ssm_w + pool_w]
    upool_ref[halo:halo + rows, :] = up
    t_abs = (i * steps + lax.broadcasted_iota(jnp.int32, (rows, 1), 0) // nb).astype(F32)
    gdim = pool_w // len(POOL_WINDOWS)
    pooled = []
    for g, w in enumerate(POOL_WINDOWS):
        cur = upool_ref[:, g * gdim:(g + 1) * gdim]
        span = 1
        while span < w:
            sh = span * nb
            cur = cur[sh:] + cur[:-sh]
            span *= 2
        win = cur[cur.shape[0] - rows:]
        inv = 1.0 / jnp.minimum(t_abs + 1.0, float(w))
        pooled.append(win * inv - up[:, g * gdim:(g + 1) * gdim])
    upool_ref[0:halo, :] = upool_ref[rows:rows + halo, :]
    pooled = jnp.concatenate(pooled, axis=1)
    mixed = _dot(pooled.astype(BF16), w_pg_ref[...])
    y_pool = _dot((mixed * pscale_ref[...]).astype(BF16), w_pp_ref[...])

    g0 = ssm_w + pool_w
    merged = (_sigmoid(proj_ref[:, g0:g0 + d_model]) * y_ssm
              + _sigmoid(proj_ref[:, g0 + d_model:g0 + 2 * d_model]) * y_pool)
    pre = alpha * xf + _dot(merged.astype(BF16), w_out_ref[...])
    h1 = _layer_norm(pre, ln_g_ref[...], ln_b_ref[...])
    hb = h1.astype(BF16)
    h_ref[...] = hb

    lt = lax.dot_general(wrt_ref[...], h1, (((1,), (1,)), ((), ())),
                         precision=lax.Precision.HIGHEST,
                         preferred_element_type=F32) + brt_ref[:, 0:1]
    eio = lax.broadcasted_iota(jnp.int32, (n_exp, rows), 0)
    vals, idxs = [], []
    for _ in range(TOP_K):
        m = jnp.max(lt, axis=0, keepdims=True)
        sel = jnp.min(jnp.where(lt == m, eio, n_exp), axis=0, keepdims=True)
        vals.append(m)
        idxs.append(sel)
        lt = jnp.where(eio == sel, -jnp.inf, lt)
    exps = [jnp.exp(v - vals[0]) for v in vals]
    den = exps[0] + exps[1] + exps[2] + exps[3]
    wts_ref[...] = jnp.concatenate([e / den for e in exps], axis=0)
    idx_ref[...] = jnp.concatenate(idxs, axis=0)

    run = carry_ref[:, 0:1]
    ranks = []
    for k in range(TOP_K):
        oh = (eio == idxs[k]).astype(F32)
        excl = _dot(oh.astype(BF16), tri_ref[...])
        ranks.append(jnp.sum(oh * (run + excl), axis=0, keepdims=True))
        run = run + jnp.sum(oh, axis=1, keepdims=True)
    rank_ref[...] = jnp.concatenate(ranks, axis=0).astype(jnp.int32)
    new_carry = jnp.broadcast_to(run, carry_ref.shape)
    carry_ref[...] = new_carry
    cnt_ref[...] = new_carry

    pp = _dot(p_ref[...].astype(BF16), w_plep_ref[...])
    gate = _sigmoid(_dot(hb, w_pleg_ref[...]))
    r_ref[...] = alpha * h1 + gate * pp


def _const_spec(shape):
    zeros = (0,) * len(shape)
    return pl.BlockSpec(shape, lambda i: zeros, pipeline_mode=pl.Buffered(1))


def _mixer_call(x2, p2, wts, alpha, nb, n_exp):
    n_tok, d_model = x2.shape
    rows = MIX_STEPS * nb
    grid = n_tok // rows
    ple_dim = p2.shape[1]
    state_cols = 2 * wts["bm"].shape[2]
    pool_w = wts["pscale"].shape[1]
    halo = max(POOL_WINDOWS) * nb

    def row_spec(width):
        return pl.BlockSpec((rows, width), lambda i: (i, 0))

    names = ["w_in", "bm", "cm", "lam", "dskip", "w_glu", "w_pg", "pscale", "w_pp",
             "w_out", "ln_g", "ln_b", "wrt", "brt", "w_pleg", "w_plep"]
    ops = [wts[n] for n in names]
    in_specs = [row_spec(d_model), row_spec(ple_dim)] + [_const_spec(o.shape) for o in ops]
    k_spec = pl.BlockSpec((TOP_K, rows), lambda i: (0, i))
    out_shape = (
        jax.ShapeDtypeStruct((n_tok, d_model), BF16),
        jax.ShapeDtypeStruct((n_tok, d_model), F32),
        jax.ShapeDtypeStruct((TOP_K, n_tok), jnp.int32),
        jax.ShapeDtypeStruct((TOP_K, n_tok), F32),
        jax.ShapeDtypeStruct((TOP_K, n_tok), jnp.int32),
        jax.ShapeDtypeStruct((n_exp, 128), F32),
    )
    out_specs = (row_spec(d_model), row_spec(d_model), k_spec, k_spec, k_spec,
                 pl.BlockSpec((n_exp, 128), lambda i: (0, 0)))
    scratch = [
        pltpu.VMEM((rows, wts["w_in"].shape[1]), F32),
        pltpu.VMEM((rows, state_cols), F32),
        pltpu.VMEM((rows, state_cols), BF16),
        pltpu.VMEM((halo + rows, pool_w), F32),
        pltpu.VMEM((nb, state_cols), F32),
        pltpu.VMEM((n_exp, 128), F32),
        pltpu.VMEM((rows, rows), BF16),
    ]
    return pl.pallas_call(
        functools.partial(_mixer_kernel, alpha, nb, n_exp),
        grid=(grid,),
        in_specs=in_specs,
        out_specs=out_specs,
        out_shape=out_shape,
        scratch_shapes=scratch,
        compiler_params=pltpu.CompilerParams(
            dimension_semantics=("arbitrary",), vmem_limit_bytes=VMEM_LIMIT),
        name="mixer",
    )(x2, p2, *ops)


def _expert_kernel(be_ref, nv_ref, xs_ref, wg_ref, bg_ref, wu_ref, bu_ref, wd_ref, bd_ref,
                   ys_ref):
    i = pl.program_id(0)

    @pl.when(nv_ref[i] > 0)
    def _compute():
        xb = xs_ref[...]
        g = _dot(xb, wg_ref[0]) + bg_ref[0]
        u = _dot(xb, wu_ref[0]) + bu_ref[0]
        g = jnp.minimum(g, SWIGLU_LIMIT)
        u = jnp.clip(u, -SWIGLU_LIMIT, SWIGLU_LIMIT)
        act = (u + 1.0) * (g * _sigmoid(SWIGLU_ALPHA * g))
        ys_ref[...] = (_dot(act.astype(BF16), wd_ref[0]) + bd_ref[0]).astype(ys_ref.dtype)

    @pl.when(nv_ref[i] == 0)
    def _skip():
        ys_ref[...] = jnp.zeros_like(ys_ref)


def _expert_call(block_expert, block_valid, xs, wg, bg, wu, bu, wd, bd):
    n_rows, d_model = xs.shape
    d_exp = wg.shape[2]
    n_blocks = n_rows // MOE_BLOCK

    def w_spec(k, n):
        return pl.BlockSpec((1, k, n), lambda i, be, nv: (be[i], 0, 0))

    grid_spec = pltpu.PrefetchScalarGridSpec(
        num_scalar_prefetch=2,
        grid=(n_blocks,),
        in_specs=[
            pl.BlockSpec((MOE_BLOCK, d_model), lambda i, be, nv: (i, 0)),
            w_spec(d_model, d_exp), w_spec(1, d_exp),
            w_spec(d_model, d_exp), w_spec(1, d_exp),
            w_spec(d_exp, d_model), w_spec(1, d_model),
        ],
        out_specs=pl.BlockSpec((MOE_BLOCK, d_model), lambda i, be, nv: (i, 0)),
    )
    return pl.pallas_call(
        _expert_kernel,
        grid_spec=grid_spec,
        out_shape=jax.ShapeDtypeStruct((n_rows, d_model), BF16),
        compiler_params=pltpu.CompilerParams(
            dimension_semantics=("arbitrary",), vmem_limit_bytes=VMEM_LIMIT),
        name="experts",
    )(block_expert, block_valid, xs, wg, bg, wu, bu, wd, bd)


def _combine_kernel(r_ref, *refs):
    yg_refs = refs[:TOP_K]
    wt_ref, ln_g_ref, ln_b_ref, o_ref = refs[TOP_K:]
    acc = r_ref[...]
    for k in range(TOP_K):
        acc = acc + wt_ref[:, k:k + 1] * yg_refs[k][...].astype(F32)
    o_ref[...] = _layer_norm(acc, ln_g_ref[...], ln_b_ref[...])


def _combine_call(r, yg, wt, ln_g, ln_b):
    n_tok, d_model = r.shape
    grid = n_tok // CMB_ROWS

    def slot_spec(k):
        return pl.BlockSpec((CMB_ROWS, d_model), lambda i: (k * grid + i, 0))

    return pl.pallas_call(
        _combine_kernel,
        grid=(grid,),
        in_specs=[
            pl.BlockSpec((CMB_ROWS, d_model), lambda i: (i, 0)),
            *[slot_spec(k) for k in range(TOP_K)],
            pl.BlockSpec((CMB_ROWS, TOP_K), lambda i: (i, 0)),
            pl.BlockSpec((1, d_model), lambda i: (0, 0)),
            pl.BlockSpec((1, d_model), lambda i: (0, 0)),
        ],
        out_specs=pl.BlockSpec((CMB_ROWS, d_model), lambda i: (i, 0)),
        out_shape=jax.ShapeDtypeStruct((n_tok, d_model), F32),
        compiler_params=pltpu.CompilerParams(
            dimension_semantics=("arbitrary",), vmem_limit_bytes=VMEM_LIMIT),
        name="combine",
    )(r, *([yg] * TOP_K), wt, ln_g, ln_b)


def _ssm_matrices(lam_re, lam_im, log_step, b_re, b_im, c_re, c_im):
    n_grp, n_state, n_ch = b_re.shape
    lam = lax.complex(lam_re.astype(F32), lam_im.astype(F32))
    step = jnp.exp(log_step.astype(F32))[:, None]
    lam_bar = jnp.exp(lam * step)
    b_bar = ((lam_bar - 1.0) / lam)[..., None] * lax.complex(b_re.astype(F32), b_im.astype(F32))
    hg = n_grp // 2
    eye = jnp.eye(hg, dtype=F32)

    def b_half(bpart):
        return jnp.einsum('gph,gk->ghkp', bpart, eye).reshape(hg * n_ch, hg * n_state)

    def c_half(cpart):
        return jnp.einsum('ghp,gk->gpkh', cpart, eye).reshape(hg * n_state, hg * n_ch)

    bm, cm = [], []
    for hf in range(2):
        sl = slice(hf * hg, (hf + 1) * hg)
        bm.append(jnp.concatenate([b_half(jnp.real(b_bar)[sl]), b_half(jnp.imag(b_bar)[sl])], axis=1))
        cm.append(jnp.concatenate([c_half(c_re.astype(F32)[sl]), -c_half(c_im.astype(F32)[sl])], axis=0))
    lam_rows = jnp.stack([jnp.real(lam_bar).reshape(-1), jnp.imag(lam_bar).reshape(-1)], axis=0)
    return jnp.stack(bm).astype(BF16), jnp.stack(cm).astype(BF16), lam_rows


def _block_diag(w):
    g, c, _ = w.shape
    return jnp.einsum('gcd,gk->gckd', w, jnp.eye(g, dtype=w.dtype)).reshape(g * c, g * c)


def kernel(x, p, w_in, ssm_lambda_re, ssm_lambda_im, ssm_log_step, ssm_b_re, ssm_b_im, ssm_c_re, ssm_c_im, ssm_d, w_glu_val, w_glu_gate, w_pool_group, pool_scale, w_pool_proj, w_out, ln1_g, ln1_b, w_router, b_router, w_gate, b_gate, w_up, b_up, w_down, b_down, w_ple_gate, w_ple_proj, ln2_g, ln2_b):
    bsz, seq, d_model = x.shape
    depth = w_in.shape[0]
    n_exp = w_router.shape[2]
    n_tok = bsz * seq
    alpha = (2.0 * depth) ** 0.25

    h = jnp.transpose(x, (1, 0, 2)).reshape(n_tok, d_model)
    for l in range(depth):
        p2 = jnp.transpose(p[l], (1, 0, 2)).reshape(n_tok, p.shape[-1])
        bm, cm, lam_rows = _ssm_matrices(ssm_lambda_re[l], ssm_lambda_im[l], ssm_log_step[l],
                                         ssm_b_re[l], ssm_b_im[l], ssm_c_re[l], ssm_c_im[l])
        wts = {
            "w_in": w_in[l].astype(BF16), "bm": bm, "cm": cm, "lam": lam_rows,
            "dskip": ssm_d[l].reshape(1, -1).astype(F32),
            "w_glu": jnp.concatenate([w_glu_val[l], w_glu_gate[l]], axis=1).astype(BF16),
            "w_pg": _block_diag(w_pool_group[l]).astype(BF16),
            "pscale": pool_scale[l].reshape(1, -1).astype(F32),
            "w_pp": w_pool_proj[l].astype(BF16),
            "w_out": w_out[l].astype(BF16),
            "ln_g": ln1_g[l].reshape(1, -1).astype(F32),
            "ln_b": ln1_b[l].reshape(1, -1).astype(F32),
            "wrt": jnp.transpose(w_router[l]).astype(F32),
            "brt": jnp.broadcast_to(b_router[l].astype(F32)[:, None], (n_exp, 128)),
            "w_pleg": w_ple_gate[l].astype(BF16),
            "w_plep": w_ple_proj[l].astype(BF16),
        }
        hb, r, idx, wt, rank, cnt = _mixer_call(h, p2, wts, alpha, bsz, n_exp)

        counts = cnt[:, 0].astype(jnp.int32)
        padded = ((counts + MOE_BLOCK - 1) // MOE_BLOCK) * MOE_BLOCK
        pad_end = jnp.cumsum(padded)
        pad_start = pad_end - padded
        start_of = jnp.zeros(idx.shape, jnp.int32)
        for e in range(n_exp):
            start_of = jnp.where(idx == e, pad_start[e], start_of)
        dest = start_of + rank
        n_rows = n_tok * TOP_K + n_exp * MOE_BLOCK
        tok_ids = jnp.broadcast_to(jnp.arange(n_tok, dtype=jnp.int32)[None, :], dest.shape)
        row_token = jnp.zeros((n_rows,), jnp.int32).at[dest.reshape(-1)].set(
            tok_ids.reshape(-1), mode="promise_in_bounds", unique_indices=True)
        blk_start = jnp.arange(n_rows // MOE_BLOCK, dtype=jnp.int32) * MOE_BLOCK
        block_expert = jnp.minimum(
            jnp.sum((blk_start[:, None] >= pad_end[None, :]).astype(jnp.int32), axis=1),
            n_exp - 1)
        block_valid = (blk_start < pad_end[-1]).astype(jnp.int32)

        xs = hb.at[row_token].get(mode="promise_in_bounds")
        ys = _expert_call(block_expert, block_valid, xs,
                          w_gate[l].astype(BF16), b_gate[l].astype(F32)[:, None, :],
                          w_up[l].astype(BF16), b_up[l].astype(F32)[:, None, :],
                          w_down[l].astype(BF16), b_down[l].astype(F32)[:, None, :])
        yg = ys.at[dest.reshape(-1)].get(mode="promise_in_bounds")
        h = _combine_call(r, yg, jnp.transpose(wt), ln2_g[l].reshape(1, -1).astype(F32),
                          ln2_b[l].reshape(1, -1).astype(F32))
    return jnp.transpose(h.reshape(seq, bsz, d_model), (1, 0, 2))
```

```python
import functools
import math

import jax
import jax.numpy as jnp
from jax import lax
from jax.experimental import pallas as pl
from jax.experimental.pallas import tpu as pltpu
from jax.experimental.pallas import tpu_sc as plsc

F32 = jnp.float32
BF16 = jnp.bfloat16

LN_EPS = 1e-5
SWIGLU_LIMIT = 7.0
SWIGLU_ALPHA = 1.702
POOL_WINDOWS = (2, 4, 8, 16)
TOP_K = 4

MIX_STEPS = 32
SCAN_LANES = 512
MOE_BLOCK = 512
CMB_ROWS = 512
VMEM_LIMIT = 60 * 1024 * 1024


def _sigmoid(v):
    return jax.nn.sigmoid(v)


def _layer_norm(v, g, b):
    mu = jnp.mean(v, axis=-1, keepdims=True)
    vc = v - mu
    var = jnp.mean(vc * vc, axis=-1, keepdims=True)
    return vc * lax.rsqrt(var + LN_EPS) * g + b


def _dot(a, b):
    return jnp.dot(a, b, preferred_element_type=F32)


def _pack_rows(v):
    n = v.shape[1] // 2
    lo = lax.bitcast_convert_type(v[:, :n].astype(BF16).astype(F32), jnp.int32)
    hi = lax.bitcast_convert_type(v[:, n:].astype(BF16).astype(F32), jnp.int32)
    return hi | lax.shift_right_logical(lo, 16)


def _unpack_rows(w):
    lo = lax.bitcast_convert_type(lax.shift_left(w, 16), F32).astype(BF16)
    hi = lax.bitcast_convert_type(w & jnp.int32(-65536), F32).astype(BF16)
    return lo, hi


def _mixer_kernel(alpha, nb, n_exp,
                  x_ref, p_ref, w_in_ref, bm_ref, cm_ref, lam_ref, dskip_ref,
                  w_glu_ref, w_pg_ref, pscale_ref, w_pp_ref, w_out_ref,
                  ln_g_ref, ln_b_ref, wrt_ref, brt_ref, w_pleg_ref, w_plep_ref,
                  h_ref, r_ref, idx_ref, wts_ref, rank_ref, cnt_ref,
                  proj_ref, bu_ref, st_ref, upool_ref, state_ref, carry_ref, tri_ref):
    i = pl.program_id(0)
    rows = x_ref.shape[0]
    steps = rows // nb
    d_model = x_ref.shape[1]
    ssm_w = dskip_ref.shape[1]
    pool_w = pscale_ref.shape[1]
    half_w = ssm_w // 2
    half_s = bu_ref.shape[1] // 2
    plane = half_s // 2
    halo = upool_ref.shape[0] - rows

    @pl.when(i == 0)
    def _init():
        state_ref[...] = jnp.zeros_like(state_ref)
        carry_ref[...] = jnp.zeros_like(carry_ref)
        upool_ref[0:halo, :] = jnp.zeros((halo, pool_w), F32)
        ri = lax.broadcasted_iota(jnp.int32, (rows, rows), 0)
        ci = lax.broadcasted_iota(jnp.int32, (rows, rows), 1)
        tri_ref[...] = (ri < ci).astype(BF16)

    xf = x_ref[...]
    proj_ref[...] = _dot(xf.astype(BF16), w_in_ref[...])

    us = proj_ref[:, 0:ssm_w]
    usb = us.astype(BF16)
    for hf in range(2):
        bu_ref[:, hf * half_s:(hf + 1) * half_s] = _dot(
            usb[:, hf * half_w:(hf + 1) * half_w], bm_ref[hf])

    for hf in range(2):
        for q in range(plane // SCAN_LANES):
            cre = hf * half_s + q * SCAN_LANES
            cim = cre + plane
            cl = hf * plane + q * SCAN_LANES
            a_re = jnp.broadcast_to(lam_ref[0:1, cl:cl + SCAN_LANES], (nb, SCAN_LANES))
            a_im = jnp.broadcast_to(lam_ref[1:2, cl:cl + SCAN_LANES], (nb, SCAN_LANES))

            def body(t, carry, cre=cre, cim=cim, a_re=a_re, a_im=a_im):
                s_re, s_im = carry
                r0 = pl.multiple_of(t * nb, nb)
                b_re = bu_ref[pl.ds(r0, nb), cre:cre + SCAN_LANES]
                b_im = bu_ref[pl.ds(r0, nb), cim:cim + SCAN_LANES]
                n_re = a_re * s_re - a_im * s_im + b_re
                n_im = a_re * s_im + a_im * s_re + b_im
                st_ref[pl.ds(r0, nb), cre:cre + SCAN_LANES] = n_re.astype(BF16)
                st_ref[pl.ds(r0, nb), cim:cim + SCAN_LANES] = n_im.astype(BF16)
                return n_re, n_im

            s_re, s_im = lax.fori_loop(
                0, steps, body,
                (state_ref[:, cre:cre + SCAN_LANES], state_ref[:, cim:cim + SCAN_LANES]))
            state_ref[:, cre:cre + SCAN_LANES] = s_re
            state_ref[:, cim:cim + SCAN_LANES] = s_im

    y = jnp.concatenate(
        [_dot(st_ref[:, hf * half_s:(hf + 1) * half_s], cm_ref[hf]) for hf in range(2)],
        axis=1) + dskip_ref[...] * us
    z = 0.5 * y * (1.0 + jnp.tanh(math.sqrt(2.0 / math.pi) * (y + 0.044715 * (y * y * y))))
    vg = _dot(z.astype(BF16), w_glu_ref[...])
    y_ssm = vg[:, 0:d_model] * _sigmoid(vg[:, d_model:2 * d_model])

    up = proj_ref[:, ssm_w:ssm_w + pool_w]
    upool_ref[halo:halo + rows, :] = up
    t_abs = (i * steps + lax.broadcasted_iota(jnp.int32, (rows, 1), 0) // nb).astype(F32)
    gdim = pool_w // len(POOL_WINDOWS)
    pooled = []
    for g, w in enumerate(POOL_WINDOWS):
        cur = upool_ref[:, g * gdim:(g + 1) * gdim]
        span = 1
        while span < w:
            sh = span * nb
            cur = cur[sh:] + cur[:-sh]
            span *= 2
        win = cur[cur.shape[0] - rows:]
        inv = 1.0 / jnp.minimum(t_abs + 1.0, float(w))
        pooled.append(win * inv - up[:, g * gdim:(g + 1) * gdim])
    upool_ref[0:halo, :] = upool_ref[rows:rows + halo, :]
    pooled = jnp.concatenate(pooled, axis=1)
    mixed = _dot(pooled.astype(BF16), w_pg_ref[...])
    y_pool = _dot((mixed * pscale_ref[...]).astype(BF16), w_pp_ref[...])

    g0 = ssm_w + pool_w
    merged = (_sigmoid(proj_ref[:, g0:g0 + d_model]) * y_ssm
              + _sigmoid(proj_ref[:, g0 + d_model:g0 + 2 * d_model]) * y_pool)
    pre = alpha * xf + _dot(merged.astype(BF16), w_out_ref[...])
    h1 = _layer_norm(pre, ln_g_ref[...], ln_b_ref[...])
    hb = h1.astype(BF16)
    h_ref[...] = _pack_rows(h1)

    lt = lax.dot_general(wrt_ref[...], h1, (((1,), (1,)), ((), ())),
                         precision=lax.Precision.HIGHEST,
                         preferred_element_type=F32) + brt_ref[:, 0:1]
    eio = lax.broadcasted_iota(jnp.int32, (n_exp, rows), 0)
    vals, idxs = [], []
    for _ in range(TOP_K):
        m = jnp.max(lt, axis=0, keepdims=True)
        sel = jnp.min(jnp.where(lt == m, eio, n_exp), axis=0, keepdims=True)
        vals.append(m)
        idxs.append(sel)
        lt = jnp.where(eio == sel, -jnp.inf, lt)
    exps = [jnp.exp(v - vals[0]) for v in vals]
    den = exps[0] + exps[1] + exps[2] + exps[3]
    wts_ref[...] = jnp.concatenate([e / den for e in exps], axis=0)
    idx_ref[...] = jnp.concatenate(idxs, axis=0)

    run = carry_ref[:, 0:1]
    ranks = []
    for k in range(TOP_K):
        oh = (eio == idxs[k]).astype(F32)
        excl = _dot(oh.astype(BF16), tri_ref[...])
        ranks.append(jnp.sum(oh * (run + excl), axis=0, keepdims=True))
        run = run + jnp.sum(oh, axis=1, keepdims=True)
    rank_ref[...] = jnp.concatenate(ranks, axis=0).astype(jnp.int32)
    new_carry = jnp.broadcast_to(run, carry_ref.shape)
    carry_ref[...] = new_carry
    cnt_ref[...] = new_carry

    pp = _dot(p_ref[...].astype(BF16), w_plep_ref[...])
    gate = _sigmoid(_dot(hb, w_pleg_ref[...]))
    r_ref[...] = alpha * h1 + gate * pp


def _const_spec(shape):
    zeros = (0,) * len(shape)
    return pl.BlockSpec(shape, lambda i: zeros, pipeline_mode=pl.Buffered(1))


def _mixer_call(x2, p2, wts, alpha, nb, n_exp):
    n_tok, d_model = x2.shape
    rows = MIX_STEPS * nb
    grid = n_tok // rows
    ple_dim = p2.shape[1]
    state_cols = 2 * wts["bm"].shape[2]
    pool_w = wts["pscale"].shape[1]
    halo = max(POOL_WINDOWS) * nb

    def row_spec(width):
        return pl.BlockSpec((rows, width), lambda i: (i, 0))

    names = ["w_in", "bm", "cm", "lam", "dskip", "w_glu", "w_pg", "pscale", "w_pp",
             "w_out", "ln_g", "ln_b", "wrt", "brt", "w_pleg", "w_plep"]
    ops = [wts[n] for n in names]
    in_specs = [row_spec(d_model), row_spec(ple_dim)] + [_const_spec(o.shape) for o in ops]
    k_spec = pl.BlockSpec((TOP_K, rows), lambda i: (0, i))
    out_shape = (
        jax.ShapeDtypeStruct((n_tok, d_model // 2), jnp.int32),
        jax.ShapeDtypeStruct((n_tok, d_model), F32),
        jax.ShapeDtypeStruct((TOP_K, n_tok), jnp.int32),
        jax.ShapeDtypeStruct((TOP_K, n_tok), F32),
        jax.ShapeDtypeStruct((TOP_K, n_tok), jnp.int32),
        jax.ShapeDtypeStruct((n_exp, 128), F32),
    )
    out_specs = (row_spec(d_model // 2), row_spec(d_model), k_spec, k_spec, k_spec,
                 pl.BlockSpec((n_exp, 128), lambda i: (0, 0)))
    scratch = [
        pltpu.VMEM((rows, wts["w_in"].shape[1]), F32),
        pltpu.VMEM((rows, state_cols), F32),
        pltpu.VMEM((rows, state_cols), BF16),
        pltpu.VMEM((halo + rows, pool_w), F32),
        pltpu.VMEM((nb, state_cols), F32),
        pltpu.VMEM((n_exp, 128), F32),
        pltpu.VMEM((rows, rows), BF16),
    ]
    return pl.pallas_call(
        functools.partial(_mixer_kernel, alpha, nb, n_exp),
        grid=(grid,),
        in_specs=in_specs,
        out_specs=out_specs,
        out_shape=out_shape,
        scratch_shapes=scratch,
        compiler_params=pltpu.CompilerParams(
            dimension_semantics=("arbitrary",), vmem_limit_bytes=VMEM_LIMIT),
        name="mixer",
    )(x2, p2, *ops)


def _expert_kernel(be_ref, nv_ref, xs_ref, wg_ref, bg_ref, wu_ref, bu_ref, wd_ref, bd_ref,
                   ys_ref):
    i = pl.program_id(0)

    half = xs_ref.shape[1]

    @pl.when(nv_ref[i] > 0)
    def _compute():
        row = lax.broadcasted_iota(jnp.int32, xs_ref.shape, 0)
        x_lo, x_hi = _unpack_rows(jnp.where(row < nv_ref[i], xs_ref[...], 0))
        g = _dot(x_lo, wg_ref[0, 0:half, :]) + _dot(x_hi, wg_ref[0, half:, :]) + bg_ref[0]
        u = _dot(x_lo, wu_ref[0, 0:half, :]) + _dot(x_hi, wu_ref[0, half:, :]) + bu_ref[0]
        g = jnp.minimum(g, SWIGLU_LIMIT)
        u = jnp.clip(u, -SWIGLU_LIMIT, SWIGLU_LIMIT)
        act = (u + 1.0) * (g * _sigmoid(SWIGLU_ALPHA * g))
        ys_ref[...] = _pack_rows(_dot(act.astype(BF16), wd_ref[0]) + bd_ref[0])

    @pl.when(nv_ref[i] == 0)
    def _skip():
        ys_ref[...] = jnp.zeros_like(ys_ref)


def _expert_call(block_expert, block_valid, xs, wg, bg, wu, bu, wd, bd):
    n_rows = xs.shape[0]
    d_model = wg.shape[1]
    d_exp = wg.shape[2]
    n_blocks = n_rows // MOE_BLOCK

    def w_spec(k, n):
        return pl.BlockSpec((1, k, n), lambda i, be, nv: (be[i], 0, 0))

    grid_spec = pltpu.PrefetchScalarGridSpec(
        num_scalar_prefetch=2,
        grid=(n_blocks,),
        in_specs=[
            pl.BlockSpec((MOE_BLOCK, d_model // 2), lambda i, be, nv: (i, 0)),
            w_spec(d_model, d_exp), w_spec(1, d_exp),
            w_spec(d_model, d_exp), w_spec(1, d_exp),
            w_spec(d_exp, d_model), w_spec(1, d_model),
        ],
        out_specs=pl.BlockSpec((MOE_BLOCK, d_model // 2), lambda i, be, nv: (i, 0)),
    )
    return pl.pallas_call(
        _expert_kernel,
        grid_spec=grid_spec,
        out_shape=jax.ShapeDtypeStruct((n_rows, d_model // 2), jnp.int32),
        compiler_params=pltpu.CompilerParams(
            dimension_semantics=("arbitrary",), vmem_limit_bytes=VMEM_LIMIT),
        name="experts",
    )(block_expert, block_valid, xs, wg, bg, wu, bu, wd, bd)


def _combine_kernel(r_ref, *refs):
    yg_refs = refs[:TOP_K]
    wt_ref, ln_g_ref, ln_b_ref, o_ref = refs[TOP_K:]
    moe = None
    for k in range(TOP_K):
        lo, hi = _unpack_rows(yg_refs[k][...])
        term = wt_ref[:, k:k + 1] * jnp.concatenate([lo, hi], axis=1).astype(F32)
        moe = term if moe is None else moe + term
    o_ref[...] = _layer_norm(r_ref[...] + moe, ln_g_ref[...], ln_b_ref[...])


def _combine_call(r, yg, wt, ln_g, ln_b):
    n_tok, d_model = r.shape
    grid = n_tok // CMB_ROWS

    def slot_spec(k):
        return pl.BlockSpec((CMB_ROWS, d_model // 2), lambda i: (k * grid + i, 0))

    return pl.pallas_call(
        _combine_kernel,
        grid=(grid,),
        in_specs=[
            pl.BlockSpec((CMB_ROWS, d_model), lambda i: (i, 0)),
            *[slot_spec(k) for k in range(TOP_K)],
            pl.BlockSpec((CMB_ROWS, TOP_K), lambda i: (i, 0)),
            pl.BlockSpec((1, d_model), lambda i: (0, 0)),
            pl.BlockSpec((1, d_model), lambda i: (0, 0)),
        ],
        out_specs=pl.BlockSpec((CMB_ROWS, d_model), lambda i: (i, 0)),
        out_shape=jax.ShapeDtypeStruct((n_tok, d_model), F32),
        compiler_params=pltpu.CompilerParams(
            dimension_semantics=("arbitrary",), vmem_limit_bytes=VMEM_LIMIT),
        name="combine",
    )(r, *([yg] * TOP_K), wt, ln_g, ln_b)


SC_WINDOW = 128


def _sc_mesh():
    return plsc.VectorSubcoreMesh(core_axis_name="c", subcore_axis_name="s")


def _sc_dispatch(rows, dest, n_out):
    n_tok, width = rows.shape
    n_slot = dest.shape[0]
    mesh = _sc_mesh()
    n_workers = mesh.num_cores * mesh.num_subcores
    per_worker = n_tok // n_workers
    assert per_worker * n_workers == n_tok and per_worker % SC_WINDOW == 0

    @functools.partial(
        pl.kernel, out_type=jax.ShapeDtypeStruct((n_out, width), rows.dtype), mesh=mesh,
        scratch_types=[pltpu.VMEM((n_slot, SC_WINDOW), jnp.int32),
                       pltpu.VMEM((SC_WINDOW, width), rows.dtype),
                       pltpu.SemaphoreType.DMA])
    def dispatch(x_hbm, i_hbm, o_hbm, idx_v, rows_v, sem):
        wid = lax.axis_index("s") * mesh.num_cores + lax.axis_index("c")

        @pl.loop(0, per_worker // SC_WINDOW)
        def _(j):
            base = pl.multiple_of(wid * per_worker + j * SC_WINDOW, SC_WINDOW)
            pltpu.sync_copy(i_hbm.at[:, pl.ds(base, SC_WINDOW)], idx_v)
            pltpu.sync_copy(x_hbm.at[pl.ds(base, SC_WINDOW)], rows_v)
            copies = [pltpu.async_copy(rows_v, o_hbm.at[idx_v.at[k]], sem) for k in range(n_slot)]
            for cp in copies:
                cp.wait()

    return dispatch(rows, dest)


def _sc_gather(table, index):
    n_out = index.shape[0]
    width = table.shape[1]
    mesh = _sc_mesh()
    n_workers = mesh.num_cores * mesh.num_subcores
    per_worker = n_out // n_workers
    assert per_worker * n_workers == n_out and per_worker % SC_WINDOW == 0
    half = SC_WINDOW // 2

    @functools.partial(
        pl.kernel, out_type=jax.ShapeDtypeStruct((n_out, width), table.dtype), mesh=mesh,
        scratch_types=[pltpu.VMEM((SC_WINDOW,), jnp.int32),
                       pltpu.VMEM((2, half, width), table.dtype)]
        + [pltpu.SemaphoreType.DMA] * 4)
    def gather(x_hbm, i_hbm, o_hbm, idx_v, rows_v, g0_sem, g1_sem, w0_sem, w1_sem):
        wid = lax.axis_index("s") * mesh.num_cores + lax.axis_index("c")

        @pl.loop(0, per_worker // SC_WINDOW)
        def _(j):
            base = pl.multiple_of(wid * per_worker + j * SC_WINDOW, SC_WINDOW)
            pltpu.sync_copy(i_hbm.at[pl.ds(base, SC_WINDOW)], idx_v)
            g0 = pltpu.async_copy(x_hbm.at[idx_v.at[pl.ds(0, half)]], rows_v.at[0], g0_sem)
            g1 = pltpu.async_copy(x_hbm.at[idx_v.at[pl.ds(half, half)]], rows_v.at[1], g1_sem)
            g0.wait()
            w0 = pltpu.async_copy(rows_v.at[0], o_hbm.at[pl.ds(base, half)], w0_sem)
            g1.wait()
            w1 = pltpu.async_copy(rows_v.at[1], o_hbm.at[pl.ds(base + half, half)], w1_sem)
            w0.wait()
            w1.wait()

    return gather(table, index)


def _ssm_matrices(lam_re, lam_im, log_step, b_re, b_im, c_re, c_im):
    n_grp, n_state, n_ch = b_re.shape
    lam = lax.complex(lam_re.astype(F32), lam_im.astype(F32))
    step = jnp.exp(log_step.astype(F32))[:, None]
    lam_bar = jnp.exp(lam * step)
    b_bar = ((lam_bar - 1.0) / lam)[..., None] * lax.complex(b_re.astype(F32), b_im.astype(F32))
    hg = n_grp // 2
    eye = jnp.eye(hg, dtype=F32)

    def b_half(bpart):
        return jnp.einsum('gph,gk->ghkp', bpart, eye).reshape(hg * n_ch, hg * n_state)

    def c_half(cpart):
        return jnp.einsum('ghp,gk->gpkh', cpart, eye).reshape(hg * n_state, hg * n_ch)

    bm, cm = [], []
    for hf in range(2):
        sl = slice(hf * hg, (hf + 1) * hg)
        bm.append(jnp.concatenate([b_half(jnp.real(b_bar)[sl]), b_half(jnp.imag(b_bar)[sl])], axis=1))
        cm.append(jnp.concatenate([c_half(c_re.astype(F32)[sl]), -c_half(c_im.astype(F32)[sl])], axis=0))
    lam_rows = jnp.stack([jnp.real(lam_bar).reshape(-1), jnp.imag(lam_bar).reshape(-1)], axis=0)
    return jnp.stack(bm).astype(BF16), jnp.stack(cm).astype(BF16), lam_rows


def _block_diag(w):
    g, c, _ = w.shape
    return jnp.einsum('gcd,gk->gckd', w, jnp.eye(g, dtype=w.dtype)).reshape(g * c, g * c)


def kernel(x, p, w_in, ssm_lambda_re, ssm_lambda_im, ssm_log_step, ssm_b_re, ssm_b_im, ssm_c_re, ssm_c_im, ssm_d, w_glu_val, w_glu_gate, w_pool_group, pool_scale, w_pool_proj, w_out, ln1_g, ln1_b, w_router, b_router, w_gate, b_gate, w_up, b_up, w_down, b_down, w_ple_gate, w_ple_proj, ln2_g, ln2_b):
    bsz, seq, d_model = x.shape
    depth = w_in.shape[0]
    n_exp = w_router.shape[2]
    n_tok = bsz * seq
    alpha = (2.0 * depth) ** 0.25

    h = jnp.transpose(x, (1, 0, 2)).reshape(n_tok, d_model)
    for l in range(depth):
        p2 = jnp.transpose(p[l], (1, 0, 2)).reshape(n_tok, p.shape[-1])
        bm, cm, lam_rows = _ssm_matrices(ssm_lambda_re[l], ssm_lambda_im[l], ssm_log_step[l],
                                         ssm_b_re[l], ssm_b_im[l], ssm_c_re[l], ssm_c_im[l])
        wts = {
            "w_in": w_in[l].astype(BF16), "bm": bm, "cm": cm, "lam": lam_rows,
            "dskip": ssm_d[l].reshape(1, -1).astype(F32),
            "w_glu": jnp.concatenate([w_glu_val[l], w_glu_gate[l]], axis=1).astype(BF16),
            "w_pg": _block_diag(w_pool_group[l]).astype(BF16),
            "pscale": pool_scale[l].reshape(1, -1).astype(F32),
            "w_pp": w_pool_proj[l].astype(BF16),
            "w_out": w_out[l].astype(BF16),
            "ln_g": ln1_g[l].reshape(1, -1).astype(F32),
            "ln_b": ln1_b[l].reshape(1, -1).astype(F32),
            "wrt": jnp.transpose(w_router[l]).astype(F32),
            "brt": jnp.broadcast_to(b_router[l].astype(F32)[:, None], (n_exp, 128)),
            "w_pleg": w_ple_gate[l].astype(BF16),
            "w_plep": w_ple_proj[l].astype(BF16),
        }
        hp, r, idx, wt, rank, cnt = _mixer_call(h, p2, wts, alpha, bsz, n_exp)

        counts = cnt[:, 0].astype(jnp.int32)
        padded = ((counts + MOE_BLOCK - 1) // MOE_BLOCK) * MOE_BLOCK
        pad_end = jnp.cumsum(padded)
        pad_start = pad_end - padded
        start_of = jnp.zeros(idx.shape, jnp.int32)
        for e in range(n_exp):
            start_of = jnp.where(idx == e, pad_start[e], start_of)
        dest = start_of + rank
        n_rows = n_tok * TOP_K + n_exp * MOE_BLOCK
        blk_start = jnp.arange(n_rows // MOE_BLOCK, dtype=jnp.int32) * MOE_BLOCK
        block_expert = jnp.minimum(
            jnp.sum((blk_start[:, None] >= pad_end[None, :]).astype(jnp.int32), axis=1),
            n_exp - 1)
        used_end = (pad_start + counts)[block_expert]
        block_valid = jnp.clip(used_end - blk_start, 0, MOE_BLOCK).astype(jnp.int32)

        xs = _sc_dispatch(hp, dest, n_rows)
        ys = _expert_call(block_expert, block_valid, xs,
                          w_gate[l].astype(BF16), b_gate[l].astype(F32)[:, None, :],
                          w_up[l].astype(BF16), b_up[l].astype(F32)[:, None, :],
                          w_down[l].astype(BF16), b_down[l].astype(F32)[:, None, :])
        yg = _sc_gather(ys, dest.reshape(-1))
        h = _combine_call(r, yg, jnp.transpose(wt), ln2_g[l].reshape(1, -1).astype(F32),
                          ln2_b[l].reshape(1, -1).astype(F32))
    return jnp.transpose(h.reshape(seq, bsz, d_model), (1, 0, 2))
```

```python
import functools
import math

import jax
import jax.numpy as jnp
from jax import lax
from jax.experimental import pallas as pl
from jax.experimental.pallas import tpu as pltpu
from jax.experimental.pallas import tpu_sc as plsc

F32 = jnp.float32
BF16 = jnp.bfloat16

LN_EPS = 1e-5
SWIGLU_LIMIT = 7.0
SWIGLU_ALPHA = 1.702
POOL_WINDOWS = (2, 4, 8, 16)
TOP_K = 4

MIX_STEPS = 32
SCAN_LANES = 512
MOE_BLOCK = 512
CMB_ROWS = 512
VMEM_LIMIT = 60 * 1024 * 1024


def _sigmoid(v):
    return jax.nn.sigmoid(v)


def _layer_norm(v, g, b):
    mu = jnp.mean(v, axis=-1, keepdims=True)
    vc = v - mu
    var = jnp.mean(vc * vc, axis=-1, keepdims=True)
    return vc * lax.rsqrt(var + LN_EPS) * g + b


def _dot(a, b):
    return jnp.dot(a, b, preferred_element_type=F32)


def _pack_rows(v):
    n = v.shape[1] // 2
    lo = lax.bitcast_convert_type(v[:, :n].astype(BF16).astype(F32), jnp.int32)
    hi = lax.bitcast_convert_type(v[:, n:].astype(BF16).astype(F32), jnp.int32)
    return hi | lax.shift_right_logical(lo, 16)


def _unpack_rows(w):
    lo = lax.bitcast_convert_type(lax.shift_left(w, 16), F32).astype(BF16)
    hi = lax.bitcast_convert_type(w & jnp.int32(-65536), F32).astype(BF16)
    return lo, hi


def _time_major_copies(seq_hbm, tm_buf, sem, chunk, slot, to_hbm=False):
    steps, nb = tm_buf.shape[1], tm_buf.shape[2]
    copies = []
    for b in range(nb):
        hbm = seq_hbm.at[b, pl.ds(chunk * steps, steps), :]
        vmem = tm_buf.at[slot, :, b, :]
        src, dst = (vmem, hbm) if to_hbm else (hbm, vmem)
        copies.append(pltpu.make_async_copy(src, dst, sem.at[slot]))
    return copies


def _mixer_kernel(alpha, nb, n_exp,
                  x_hbm, p_hbm, w_in_ref, bm_ref, cm_ref, lam_ref, dskip_ref,
                  w_glu_ref, w_pg_ref, pscale_ref, w_pp_ref, w_out_ref,
                  ln_g_ref, ln_b_ref, wrt_ref, brt_ref, w_pleg_ref, w_plep_ref,
                  h_ref, r_ref, idx_ref, wts_ref, rank_ref, cnt_ref,
                  proj_ref, bu_ref, st_ref, upool_ref, state_ref, carry_ref, tri_ref,
                  x_buf, p_buf, x_sem, p_sem):
    i = pl.program_id(0)
    steps = x_buf.shape[1]
    rows = steps * nb
    d_model = x_buf.shape[3]

    slot = i % 2

    def fetch(chunk, into):
        for cp in (_time_major_copies(x_hbm, x_buf, x_sem, chunk, into)
                   + _time_major_copies(p_hbm, p_buf, p_sem, chunk, into)):
            cp.start()

    @pl.when(i == 0)
    def _first_fetch():
        fetch(0, 0)

    @pl.when(i + 1 < pl.num_programs(0))
    def _next_fetch():
        fetch(i + 1, 1 - slot)

    for cp in (_time_major_copies(x_hbm, x_buf, x_sem, i, slot)
               + _time_major_copies(p_hbm, p_buf, p_sem, i, slot)):
        cp.wait()
    ssm_w = dskip_ref.shape[1]
    pool_w = pscale_ref.shape[1]
    half_w = ssm_w // 2
    half_s = bu_ref.shape[1] // 2
    plane = half_s // 2
    halo = upool_ref.shape[0] - rows

    @pl.when(i == 0)
    def _init():
        state_ref[...] = jnp.zeros_like(state_ref)
        carry_ref[...] = jnp.zeros_like(carry_ref)
        upool_ref[0:halo, :] = jnp.zeros((halo, pool_w), F32)
        ri = lax.broadcasted_iota(jnp.int32, (rows, rows), 0)
        ci = lax.broadcasted_iota(jnp.int32, (rows, rows), 1)
        tri_ref[...] = (ri < ci).astype(BF16)

    xf = x_buf[slot].reshape(rows, d_model)
    proj_ref[...] = _dot(xf.astype(BF16), w_in_ref[...])

    us = proj_ref[:, 0:ssm_w]
    usb = us.astype(BF16)
    for hf in range(2):
        bu_ref[:, hf * half_s:(hf + 1) * half_s] = _dot(
            usb[:, hf * half_w:(hf + 1) * half_w], bm_ref[hf])

    for hf in range(2):
        for q in range(plane // SCAN_LANES):
            cre = hf * half_s + q * SCAN_LANES
            cim = cre + plane
            cl = hf * plane + q * SCAN_LANES
            a_re = jnp.broadcast_to(lam_ref[0:1, cl:cl + SCAN_LANES], (nb, SCAN_LANES))
            a_im = jnp.broadcast_to(lam_ref[1:2, cl:cl + SCAN_LANES], (nb, SCAN_LANES))

            def body(t, carry, cre=cre, cim=cim, a_re=a_re, a_im=a_im):
                s_re, s_im = carry
                r0 = pl.multiple_of(t * nb, nb)
                b_re = bu_ref[pl.ds(r0, nb), cre:cre + SCAN_LANES]
                b_im = bu_ref[pl.ds(r0, nb), cim:cim + SCAN_LANES]
                n_re = a_re * s_re - a_im * s_im + b_re
                n_im = a_re * s_im + a_im * s_re + b_im
                st_ref[pl.ds(r0, nb), cre:cre + SCAN_LANES] = n_re.astype(BF16)
                st_ref[pl.ds(r0, nb), cim:cim + SCAN_LANES] = n_im.astype(BF16)
                return n_re, n_im

            s_re, s_im = lax.fori_loop(
                0, steps, body,
                (state_ref[:, cre:cre + SCAN_LANES], state_ref[:, cim:cim + SCAN_LANES]))
            state_ref[:, cre:cre + SCAN_LANES] = s_re
            state_ref[:, cim:cim + SCAN_LANES] = s_im

    y = jnp.concatenate(
        [_dot(st_ref[:, hf * half_s:(hf + 1) * half_s], cm_ref[hf]) for hf in range(2)],
        axis=1) + dskip_ref[...] * us
    z = 0.5 * y * (1.0 + jnp.tanh(math.sqrt(2.0 / math.pi) * (y + 0.044715 * (y * y * y))))
    vg = _dot(z.astype(BF16), w_glu_ref[...])
    y_ssm = vg[:, 0:d_model] * _sigmoid(vg[:, d_model:2 * d_model])

    up = proj_ref[:, ssm_w:ssm_w + pool_w]
    upool_ref[halo:halo + rows, :] = up
    t_abs = (i * steps + lax.broadcasted_iota(jnp.int32, (rows, 1), 0) // nb).astype(F32)
    gdim = pool_w // len(POOL_WINDOWS)
    pooled = []
    for g, w in enumerate(POOL_WINDOWS):
        cur = upool_ref[:, g * gdim:(g + 1) * gdim]
        span = 1
        while span < w:
            sh = span * nb
            cur = cur[sh:] + cur[:-sh]
            span *= 2
        win = cur[cur.shape[0] - rows:]
        inv = 1.0 / jnp.minimum(t_abs + 1.0, float(w))
        pooled.append(win * inv - up[:, g * gdim:(g + 1) * gdim])
    upool_ref[0:halo, :] = upool_ref[rows:rows + halo, :]
    pooled = jnp.concatenate(pooled, axis=1)
    mixed = _dot(pooled.astype(BF16), w_pg_ref[...])
    y_pool = _dot((mixed * pscale_ref[...]).astype(BF16), w_pp_ref[...])

    g0 = ssm_w + pool_w
    merged = (_sigmoid(proj_ref[:, g0:g0 + d_model]) * y_ssm
              + _sigmoid(proj_ref[:, g0 + d_model:g0 + 2 * d_model]) * y_pool)
    pre = alpha * xf + _dot(merged.astype(BF16), w_out_ref[...])
    h1 = _layer_norm(pre, ln_g_ref[...], ln_b_ref[...])
    hb = h1.astype(BF16)
    h_ref[...] = _pack_rows(h1)

    lt = lax.dot_general(wrt_ref[...], h1, (((1,), (1,)), ((), ())),
                         precision=lax.Precision.HIGHEST,
                         preferred_element_type=F32) + brt_ref[:, 0:1]
    eio = lax.broadcasted_iota(jnp.int32, (n_exp, rows), 0)
    vals, idxs = [], []
    for _ in range(TOP_K):
        m = jnp.max(lt, axis=0, keepdims=True)
        sel = jnp.min(jnp.where(lt == m, eio, n_exp), axis=0, keepdims=True)
        vals.append(m)
        idxs.append(sel)
        lt = jnp.where(eio == sel, -jnp.inf, lt)
    exps = [jnp.exp(v - vals[0]) for v in vals]
    den = exps[0] + exps[1] + exps[2] + exps[3]
    wts_ref[...] = jnp.concatenate([e / den for e in exps], axis=0)
    idx_ref[...] = jnp.concatenate(idxs, axis=0)

    run = carry_ref[:, 0:1]
    ranks = []
    for k in range(TOP_K):
        oh = (eio == idxs[k]).astype(F32)
        excl = _dot(oh.astype(BF16), tri_ref[...])
        ranks.append(jnp.sum(oh * (run + excl), axis=0, keepdims=True))
        run = run + jnp.sum(oh, axis=1, keepdims=True)
    rank_ref[...] = jnp.concatenate(ranks, axis=0).astype(jnp.int32)
    new_carry = jnp.broadcast_to(run, carry_ref.shape)
    carry_ref[...] = new_carry
    cnt_ref[...] = new_carry

    pp = _dot(p_buf[slot].reshape(rows, p_buf.shape[3]).astype(BF16), w_plep_ref[...])
    gate = _sigmoid(_dot(hb, w_pleg_ref[...]))
    r_ref[...] = alpha * h1 + gate * pp


def _const_spec(shape):
    zeros = (0,) * len(shape)
    return pl.BlockSpec(shape, lambda i: zeros, pipeline_mode=pl.Buffered(1))


def _mixer_call(x3, p3, wts, alpha, n_exp):
    nb, seq, d_model = x3.shape
    n_tok = nb * seq
    rows = MIX_STEPS * nb
    grid = seq // MIX_STEPS
    ple_dim = p3.shape[2]
    state_cols = 2 * wts["bm"].shape[2]
    pool_w = wts["pscale"].shape[1]
    halo = max(POOL_WINDOWS) * nb

    def row_spec(width):
        return pl.BlockSpec((rows, width), lambda i: (i, 0))

    names = ["w_in", "bm", "cm", "lam", "dskip", "w_glu", "w_pg", "pscale", "w_pp",
             "w_out", "ln_g", "ln_b", "wrt", "brt", "w_pleg", "w_plep"]
    ops = [wts[n] for n in names]
    hbm_spec = pl.BlockSpec(memory_space=pl.ANY)
    in_specs = [hbm_spec, hbm_spec] + [_const_spec(o.shape) for o in ops]
    k_spec = pl.BlockSpec((TOP_K, rows), lambda i: (0, i))
    out_shape = (
        jax.ShapeDtypeStruct((n_tok, d_model // 2), jnp.int32),
        jax.ShapeDtypeStruct((n_tok, d_model), F32),
        jax.ShapeDtypeStruct((TOP_K, n_tok), jnp.int32),
        jax.ShapeDtypeStruct((TOP_K, n_tok), F32),
        jax.ShapeDtypeStruct((TOP_K, n_tok), jnp.int32),
        jax.ShapeDtypeStruct((n_exp, 128), F32),
    )
    out_specs = (row_spec(d_model // 2), row_spec(d_model), k_spec, k_spec, k_spec,
                 pl.BlockSpec((n_exp, 128), lambda i: (0, 0)))
    scratch = [
        pltpu.VMEM((rows, wts["w_in"].shape[1]), F32),
        pltpu.VMEM((rows, state_cols), F32),
        pltpu.VMEM((rows, state_cols), BF16),
        pltpu.VMEM((halo + rows, pool_w), F32),
        pltpu.VMEM((nb, state_cols), F32),
        pltpu.VMEM((n_exp, 128), F32),
        pltpu.VMEM((rows, rows), BF16),
        pltpu.VMEM((2, MIX_STEPS, nb, d_model), F32),
        pltpu.VMEM((2, MIX_STEPS, nb, ple_dim), F32),
        pltpu.SemaphoreType.DMA((2,)),
        pltpu.SemaphoreType.DMA((2,)),
    ]
    return pl.pallas_call(
        functools.partial(_mixer_kernel, alpha, nb, n_exp),
        grid=(grid,),
        in_specs=in_specs,
        out_specs=out_specs,
        out_shape=out_shape,
        scratch_shapes=scratch,
        compiler_params=pltpu.CompilerParams(
            dimension_semantics=("arbitrary",), vmem_limit_bytes=VMEM_LIMIT),
        name="mixer",
    )(x3, p3, *ops)


def _expert_kernel(be_ref, nv_ref, xs_ref, wg_ref, bg_ref, wu_ref, bu_ref, wd_ref, bd_ref,
                   ys_ref, wg_bf, wu_bf, wd_bf):
    i = pl.program_id(0)
    half = xs_ref.shape[1]

    @pl.when((i == 0) | (be_ref[i] != be_ref[jnp.maximum(i - 1, 0)]))
    def _new_expert():
        wg_bf[...] = wg_ref[0].astype(BF16)
        wu_bf[...] = wu_ref[0].astype(BF16)
        wd_bf[...] = wd_ref[0].astype(BF16)

    @pl.when(nv_ref[i] > 0)
    def _compute():
        row = lax.broadcasted_iota(jnp.int32, xs_ref.shape, 0)
        x_lo, x_hi = _unpack_rows(jnp.where(row < nv_ref[i], xs_ref[...], 0))
        g = _dot(x_lo, wg_bf[0:half, :]) + _dot(x_hi, wg_bf[half:, :]) + bg_ref[0]
        u = _dot(x_lo, wu_bf[0:half, :]) + _dot(x_hi, wu_bf[half:, :]) + bu_ref[0]
        g = jnp.minimum(g, SWIGLU_LIMIT)
        u = jnp.clip(u, -SWIGLU_LIMIT, SWIGLU_LIMIT)
        act = (u + 1.0) * (g * _sigmoid(SWIGLU_ALPHA * g))
        ys_ref[...] = _pack_rows(_dot(act.astype(BF16), wd_bf[...]) + bd_ref[0])

    @pl.when(nv_ref[i] == 0)
    def _skip():
        ys_ref[...] = jnp.zeros_like(ys_ref)


def _expert_call(block_expert, block_valid, xs, wg, bg, wu, bu, wd, bd):
    n_rows = xs.shape[0]
    d_model = wg.shape[1]
    d_exp = wg.shape[2]
    n_blocks = n_rows // MOE_BLOCK

    def w_spec(k, n):
        return pl.BlockSpec((1, k, n), lambda i, be, nv: (be[i], 0, 0))

    grid_spec = pltpu.PrefetchScalarGridSpec(
        num_scalar_prefetch=2,
        grid=(n_blocks,),
        in_specs=[
            pl.BlockSpec((MOE_BLOCK, d_model // 2), lambda i, be, nv: (i, 0)),
            w_spec(d_model, d_exp), w_spec(1, d_exp),
            w_spec(d_model, d_exp), w_spec(1, d_exp),
            w_spec(d_exp, d_model), w_spec(1, d_model),
        ],
        out_specs=pl.BlockSpec((MOE_BLOCK, d_model // 2), lambda i, be, nv: (i, 0)),
        scratch_shapes=[pltpu.VMEM((d_model, d_exp), BF16), pltpu.VMEM((d_model, d_exp), BF16),
                        pltpu.VMEM((d_exp, d_model), BF16)],
    )
    return pl.pallas_call(
        _expert_kernel,
        grid_spec=grid_spec,
        out_shape=jax.ShapeDtypeStruct((n_rows, d_model // 2), jnp.int32),
        compiler_params=pltpu.CompilerParams(
            dimension_semantics=("arbitrary",), vmem_limit_bytes=VMEM_LIMIT),
        name="experts",
    )(block_expert, block_valid, xs, wg, bg, wu, bu, wd, bd)


def _combine_kernel(r_ref, *refs):
    yg_refs = refs[:TOP_K]
    wt_ref, ln_g_ref, ln_b_ref, o_hbm, o_buf, o_sem = refs[TOP_K:]
    i = pl.program_id(0)
    last = pl.num_programs(0) - 1
    slot = i % 2

    def drain(chunk, from_slot):
        for cp in _time_major_copies(o_hbm, o_buf, o_sem, chunk, from_slot, to_hbm=True):
            cp.wait()

    @pl.when(i >= 2)
    def _reuse_slot():
        drain(i - 2, slot)

    moe = None
    for k in range(TOP_K):
        lo, hi = _unpack_rows(yg_refs[k][...])
        term = wt_ref[:, k:k + 1] * jnp.concatenate([lo, hi], axis=1).astype(F32)
        moe = term if moe is None else moe + term
    out = _layer_norm(r_ref[...] + moe, ln_g_ref[...], ln_b_ref[...])
    o_buf[slot] = out.reshape(o_buf.shape[1:])
    for cp in _time_major_copies(o_hbm, o_buf, o_sem, i, slot, to_hbm=True):
        cp.start()

    @pl.when(i == last)
    def _finish():
        @pl.when(i >= 1)
        def _previous():
            drain(i - 1, 1 - slot)
        drain(i, slot)


def _combine_call(r, yg, wt, ln_g, ln_b, nb):
    n_tok, d_model = r.shape
    grid = n_tok // CMB_ROWS
    steps = CMB_ROWS // nb

    def slot_spec(k):
        return pl.BlockSpec((CMB_ROWS, d_model // 2), lambda i: (k * grid + i, 0))

    return pl.pallas_call(
        _combine_kernel,
        grid=(grid,),
        in_specs=[
            pl.BlockSpec((CMB_ROWS, d_model), lambda i: (i, 0)),
            *[slot_spec(k) for k in range(TOP_K)],
            pl.BlockSpec((CMB_ROWS, TOP_K), lambda i: (i, 0)),
            pl.BlockSpec((1, d_model), lambda i: (0, 0)),
            pl.BlockSpec((1, d_model), lambda i: (0, 0)),
        ],
        out_specs=pl.BlockSpec(memory_space=pl.ANY),
        out_shape=jax.ShapeDtypeStruct((nb, n_tok // nb, d_model), F32),
        scratch_shapes=[pltpu.VMEM((2, steps, nb, d_model), F32), pltpu.SemaphoreType.DMA((2,))],
        compiler_params=pltpu.CompilerParams(
            dimension_semantics=("arbitrary",), vmem_limit_bytes=VMEM_LIMIT),
        name="combine",
    )(r, *([yg] * TOP_K), wt, ln_g, ln_b)


SC_WINDOW = 128


def _sc_mesh():
    return plsc.VectorSubcoreMesh(core_axis_name="c", subcore_axis_name="s")


def _sc_dispatch(rows, dest, n_out):
    n_tok, width = rows.shape
    n_slot = dest.shape[0]
    mesh = _sc_mesh()
    n_workers = mesh.num_cores * mesh.num_subcores
    per_worker = n_tok // n_workers
    assert per_worker * n_workers == n_tok and per_worker % SC_WINDOW == 0

    @functools.partial(
        pl.kernel, out_type=jax.ShapeDtypeStruct((n_out, width), rows.dtype), mesh=mesh,
        scratch_types=[pltpu.VMEM((n_slot, SC_WINDOW), jnp.int32),
                       pltpu.VMEM((SC_WINDOW, width), rows.dtype),
                       pltpu.SemaphoreType.DMA])
    def dispatch(x_hbm, i_hbm, o_hbm, idx_v, rows_v, sem):
        wid = lax.axis_index("s") * mesh.num_cores + lax.axis_index("c")

        @pl.loop(0, per_worker // SC_WINDOW)
        def _(j):
            base = pl.multiple_of(wid * per_worker + j * SC_WINDOW, SC_WINDOW)
            pltpu.sync_copy(i_hbm.at[:, pl.ds(base, SC_WINDOW)], idx_v)
            pltpu.sync_copy(x_hbm.at[pl.ds(base, SC_WINDOW)], rows_v)
            copies = [pltpu.async_copy(rows_v, o_hbm.at[idx_v.at[k]], sem) for k in range(n_slot)]
            for cp in copies:
                cp.wait()

    return dispatch(rows, dest)


def _sc_gather(table, index):
    n_out = index.shape[0]
    width = table.shape[1]
    mesh = _sc_mesh()
    n_workers = mesh.num_cores * mesh.num_subcores
    per_worker = n_out // n_workers
    assert per_worker * n_workers == n_out and per_worker % SC_WINDOW == 0
    half = SC_WINDOW // 2

    @functools.partial(
        pl.kernel, out_type=jax.ShapeDtypeStruct((n_out, width), table.dtype), mesh=mesh,
        scratch_types=[pltpu.VMEM((SC_WINDOW,), jnp.int32),
                       pltpu.VMEM((2, half, width), table.dtype)]
        + [pltpu.SemaphoreType.DMA] * 4)
    def gather(x_hbm, i_hbm, o_hbm, idx_v, rows_v, g0_sem, g1_sem, w0_sem, w1_sem):
        wid = lax.axis_index("s") * mesh.num_cores + lax.axis_index("c")

        @pl.loop(0, per_worker // SC_WINDOW)
        def _(j):
            base = pl.multiple_of(wid * per_worker + j * SC_WINDOW, SC_WINDOW)
            pltpu.sync_copy(i_hbm.at[pl.ds(base, SC_WINDOW)], idx_v)
            g0 = pltpu.async_copy(x_hbm.at[idx_v.at[pl.ds(0, half)]], rows_v.at[0], g0_sem)
            g1 = pltpu.async_copy(x_hbm.at[idx_v.at[pl.ds(half, half)]], rows_v.at[1], g1_sem)
            g0.wait()
            w0 = pltpu.async_copy(rows_v.at[0], o_hbm.at[pl.ds(base, half)], w0_sem)
            g1.wait()
            w1 = pltpu.async_copy(rows_v.at[1], o_hbm.at[pl.ds(base + half, half)], w1_sem)
            w0.wait()
            w1.wait()

    return gather(table, index)


def _ssm_matrices(lam_re, lam_im, log_step, b_re, b_im, c_re, c_im):
    n_grp, n_state, n_ch = b_re.shape
    lam = lax.complex(lam_re.astype(F32), lam_im.astype(F32))
    step = jnp.exp(log_step.astype(F32))[:, None]
    lam_bar = jnp.exp(lam * step)
    b_bar = ((lam_bar - 1.0) / lam)[..., None] * lax.complex(b_re.astype(F32), b_im.astype(F32))
    hg = n_grp // 2
    eye = jnp.eye(hg, dtype=F32)

    def b_half(bpart):
        return jnp.einsum('gph,gk->ghkp', bpart, eye).reshape(hg * n_ch, hg * n_state)

    def c_half(cpart):
        return jnp.einsum('ghp,gk->gpkh', cpart, eye).reshape(hg * n_state, hg * n_ch)

    bm, cm = [], []
    for hf in range(2):
        sl = slice(hf * hg, (hf + 1) * hg)
        bm.append(jnp.concatenate([b_half(jnp.real(b_bar)[sl]), b_half(jnp.imag(b_bar)[sl])], axis=1))
        cm.append(jnp.concatenate([c_half(c_re.astype(F32)[sl]), -c_half(c_im.astype(F32)[sl])], axis=0))
    lam_rows = jnp.stack([jnp.real(lam_bar).reshape(-1), jnp.imag(lam_bar).reshape(-1)], axis=0)
    return jnp.stack(bm).astype(BF16), jnp.stack(cm).astype(BF16), lam_rows


def _block_diag(w):
    g, c, _ = w.shape
    return jnp.einsum('gcd,gk->gckd', w, jnp.eye(g, dtype=w.dtype)).reshape(g * c, g * c)


def kernel(x, p, w_in, ssm_lambda_re, ssm_lambda_im, ssm_log_step, ssm_b_re, ssm_b_im, ssm_c_re, ssm_c_im, ssm_d, w_glu_val, w_glu_gate, w_pool_group, pool_scale, w_pool_proj, w_out, ln1_g, ln1_b, w_router, b_router, w_gate, b_gate, w_up, b_up, w_down, b_down, w_ple_gate, w_ple_proj, ln2_g, ln2_b):
    bsz, seq, d_model = x.shape
    depth = w_in.shape[0]
    n_exp = w_router.shape[2]
    n_tok = bsz * seq
    alpha = (2.0 * depth) ** 0.25

    h = x
    for l in range(depth):
        bm, cm, lam_rows = _ssm_matrices(ssm_lambda_re[l], ssm_lambda_im[l], ssm_log_step[l],
                                         ssm_b_re[l], ssm_b_im[l], ssm_c_re[l], ssm_c_im[l])
        wts = {
            "w_in": w_in[l].astype(BF16), "bm": bm, "cm": cm, "lam": lam_rows,
            "dskip": ssm_d[l].reshape(1, -1).astype(F32),
            "w_glu": jnp.concatenate([w_glu_val[l], w_glu_gate[l]], axis=1).astype(BF16),
            "w_pg": _block_diag(w_pool_group[l]).astype(BF16),
            "pscale": pool_scale[l].reshape(1, -1).astype(F32),
            "w_pp": w_pool_proj[l].astype(BF16),
            "w_out": w_out[l].astype(BF16),
            "ln_g": ln1_g[l].reshape(1, -1).astype(F32),
            "ln_b": ln1_b[l].reshape(1, -1).astype(F32),
            "wrt": jnp.transpose(w_router[l]).astype(F32),
            "brt": jnp.broadcast_to(b_router[l].astype(F32)[:, None], (n_exp, 128)),
            "w_pleg": w_ple_gate[l].astype(BF16),
            "w_plep": w_ple_proj[l].astype(BF16),
        }
        hp, r, idx, wt, rank, cnt = _mixer_call(h, p[l], wts, alpha, n_exp)

        counts = cnt[:, 0].astype(jnp.int32)
        padded = ((counts + MOE_BLOCK - 1) // MOE_BLOCK) * MOE_BLOCK
        pad_end = jnp.cumsum(padded)
        pad_start = pad_end - padded
        start_of = jnp.zeros(idx.shape, jnp.int32)
        for e in range(n_exp):
            start_of = jnp.where(idx == e, pad_start[e], start_of)
        dest = start_of + rank
        n_rows = n_tok * TOP_K + n_exp * MOE_BLOCK
        blk_start = jnp.arange(n_rows // MOE_BLOCK, dtype=jnp.int32) * MOE_BLOCK
        block_expert = jnp.minimum(
            jnp.sum((blk_start[:, None] >= pad_end[None, :]).astype(jnp.int32), axis=1),
            n_exp - 1)
        used_end = (pad_start + counts)[block_expert]
        block_valid = jnp.clip(used_end - blk_start, 0, MOE_BLOCK).astype(jnp.int32)

        xs = _sc_dispatch(hp, dest, n_rows)
        ys = _expert_call(block_expert, block_valid, xs,
                          w_gate[l].astype(F32), b_gate[l].astype(F32)[:, None, :],
                          w_up[l].astype(F32), b_up[l].astype(F32)[:, None, :],
                          w_down[l].astype(F32), b_down[l].astype(F32)[:, None, :])
        yg = _sc_gather(ys, dest.reshape(-1))
        h = _combine_call(r, yg, jnp.transpose(wt), ln2_g[l].reshape(1, -1).astype(F32),
                          ln2_b[l].reshape(1, -1).astype(F32), bsz)
    return h
```

```python
import functools
import math

import jax
import jax.numpy as jnp
from jax import lax
from jax.experimental import pallas as pl
from jax.experimental.pallas import tpu as pltpu
from jax.experimental.pallas import tpu_sc as plsc

F32 = jnp.float32
BF16 = jnp.bfloat16

LN_EPS = 1e-5
SWIGLU_LIMIT = 7.0
SWIGLU_ALPHA = 1.702
POOL_WINDOWS = (2, 4, 8, 16)
TOP_K = 4

MIX_STEPS = 32
SCAN_LANES = 512
MOE_BLOCK = 512
CMB_ROWS = 512
TIME_PIECES = 2
VMEM_LIMIT = 60 * 1024 * 1024


def _sigmoid(v):
    return jax.nn.sigmoid(v)


def _layer_norm(v, g, b):
    mu = jnp.mean(v, axis=-1, keepdims=True)
    vc = v - mu
    var = jnp.mean(vc * vc, axis=-1, keepdims=True)
    return vc * lax.rsqrt(var + LN_EPS) * g + b


def _dot(a, b):
    return jnp.dot(a, b, preferred_element_type=F32)


def _pack_rows(v):
    n = v.shape[1] // 2
    lo = lax.bitcast_convert_type(v[:, :n].astype(BF16).astype(F32), jnp.int32)
    hi = lax.bitcast_convert_type(v[:, n:].astype(BF16).astype(F32), jnp.int32)
    return hi | lax.shift_right_logical(lo, 16)


def _unpack_rows(w):
    lo = lax.bitcast_convert_type(lax.shift_left(w, 16), F32).astype(BF16)
    hi = lax.bitcast_convert_type(w & jnp.int32(-65536), F32).astype(BF16)
    return lo, hi


def _time_major_copies(seq_hbm, tm_buf, sem, chunk, slot, to_hbm=False):
    steps, nb = tm_buf.shape[1], tm_buf.shape[2]
    copies = []
    for b in range(nb):
        hbm = seq_hbm.at[b, pl.ds(chunk * steps, steps), :]
        vmem = tm_buf.at[slot, :, b, :]
        src, dst = (vmem, hbm) if to_hbm else (hbm, vmem)
        copies.append(pltpu.make_async_copy(src, dst, sem.at[slot]))
    return copies


def _mixer_kernel(alpha, nb, n_exp, chunk0,
                  x_hbm, p_hbm, w_in_ref, bm_ref, cm_ref, lam_ref, dskip_ref,
                  w_glu_ref, w_pg_ref, pscale_ref, w_pp_ref, w_out_ref,
                  ln_g_ref, ln_b_ref, wrt_ref, brt_ref, w_pleg_ref, w_plep_ref,
                  state_in_ref, hist_in_ref,
                  h_ref, r_ref, idx_ref, wts_ref, rank_ref, cnt_ref, state_out_ref, hist_out_ref,
                  proj_ref, bu_ref, st_ref, upool_ref, state_ref, carry_ref, tri_ref,
                  x_buf, p_buf, x_sem, p_sem):
    i = pl.program_id(0)
    last = pl.num_programs(0) - 1
    steps = x_buf.shape[1]
    rows = steps * nb
    d_model = x_buf.shape[3]

    slot = i % 2

    def fetch_copies(step, which):
        return (_time_major_copies(x_hbm, x_buf, x_sem, chunk0 + step, which)
                + _time_major_copies(p_hbm, p_buf, p_sem, chunk0 + step, which))

    @pl.when(i == 0)
    def _first_fetch():
        for cp in fetch_copies(0, 0):
            cp.start()

    @pl.when(i < last)
    def _next_fetch():
        for cp in fetch_copies(i + 1, 1 - slot):
            cp.start()

    for cp in fetch_copies(i, slot):
        cp.wait()
    ssm_w = dskip_ref.shape[1]
    pool_w = pscale_ref.shape[1]
    half_w = ssm_w // 2
    half_s = bu_ref.shape[1] // 2
    plane = half_s // 2
    halo = upool_ref.shape[0] - rows

    @pl.when(i == 0)
    def _init():
        state_ref[...] = state_in_ref[...]
        carry_ref[...] = jnp.zeros_like(carry_ref)
        upool_ref[0:halo, :] = hist_in_ref[...]
        ri = lax.broadcasted_iota(jnp.int32, (rows, rows), 0)
        ci = lax.broadcasted_iota(jnp.int32, (rows, rows), 1)
        tri_ref[...] = (ri < ci).astype(BF16)

    xf = x_buf[slot].reshape(rows, d_model)
    proj_ref[...] = _dot(xf.astype(BF16), w_in_ref[...])

    us = proj_ref[:, 0:ssm_w]
    usb = us.astype(BF16)
    for hf in range(2):
        bu_ref[:, hf * half_s:(hf + 1) * half_s] = _dot(
            usb[:, hf * half_w:(hf + 1) * half_w], bm_ref[hf])

    for hf in range(2):
        for q in range(plane // SCAN_LANES):
            cre = hf * half_s + q * SCAN_LANES
            cim = cre + plane
            cl = hf * plane + q * SCAN_LANES
            a_re = jnp.broadcast_to(lam_ref[0:1, cl:cl + SCAN_LANES], (nb, SCAN_LANES))
            a_im = jnp.broadcast_to(lam_ref[1:2, cl:cl + SCAN_LANES], (nb, SCAN_LANES))

            def body(t, carry, cre=cre, cim=cim, a_re=a_re, a_im=a_im):
                s_re, s_im = carry
                r0 = pl.multiple_of(t * nb, nb)
                b_re = bu_ref[pl.ds(r0, nb), cre:cre + SCAN_LANES]
                b_im = bu_ref[pl.ds(r0, nb), cim:cim + SCAN_LANES]
                n_re = a_re * s_re - a_im * s_im + b_re
                n_im = a_re * s_im + a_im * s_re + b_im
                st_ref[pl.ds(r0, nb), cre:cre + SCAN_LANES] = n_re.astype(BF16)
                st_ref[pl.ds(r0, nb), cim:cim + SCAN_LANES] = n_im.astype(BF16)
                return n_re, n_im

            s_re, s_im = lax.fori_loop(
                0, steps, body,
                (state_ref[:, cre:cre + SCAN_LANES], state_ref[:, cim:cim + SCAN_LANES]))
            state_ref[:, cre:cre + SCAN_LANES] = s_re
            state_ref[:, cim:cim + SCAN_LANES] = s_im

    y = jnp.concatenate(
        [_dot(st_ref[:, hf * half_s:(hf + 1) * half_s], cm_ref[hf]) for hf in range(2)],
        axis=1) + dskip_ref[...] * us
    z = 0.5 * y * (1.0 + jnp.tanh(math.sqrt(2.0 / math.pi) * (y + 0.044715 * (y * y * y))))
    vg = _dot(z.astype(BF16), w_glu_ref[...])
    y_ssm = vg[:, 0:d_model] * _sigmoid(vg[:, d_model:2 * d_model])

    up = proj_ref[:, ssm_w:ssm_w + pool_w]
    upool_ref[halo:halo + rows, :] = up
    t_abs = ((chunk0 + i) * steps
             + lax.broadcasted_iota(jnp.int32, (rows, 1), 0) // nb).astype(F32)
    gdim = pool_w // len(POOL_WINDOWS)
    pooled = []
    for g, w in enumerate(POOL_WINDOWS):
        cur = upool_ref[:, g * gdim:(g + 1) * gdim]
        span = 1
        while span < w:
            sh = span * nb
            cur = cur[sh:] + cur[:-sh]
            span *= 2
        win = cur[cur.shape[0] - rows:]
        inv = 1.0 / jnp.minimum(t_abs + 1.0, float(w))
        pooled.append(win * inv - up[:, g * gdim:(g + 1) * gdim])
    upool_ref[0:halo, :] = upool_ref[rows:rows + halo, :]
    pooled = jnp.concatenate(pooled, axis=1)
    mixed = _dot(pooled.astype(BF16), w_pg_ref[...])
    y_pool = _dot((mixed * pscale_ref[...]).astype(BF16), w_pp_ref[...])

    g0 = ssm_w + pool_w
    merged = (_sigmoid(proj_ref[:, g0:g0 + d_model]) * y_ssm
              + _sigmoid(proj_ref[:, g0 + d_model:g0 + 2 * d_model]) * y_pool)
    pre = alpha * xf + _dot(merged.astype(BF16), w_out_ref[...])
    h1 = _layer_norm(pre, ln_g_ref[...], ln_b_ref[...])
    hb = h1.astype(BF16)
    h_ref[...] = _pack_rows(h1)

    lt = lax.dot_general(wrt_ref[...], h1, (((1,), (1,)), ((), ())),
                         precision=lax.Precision.HIGHEST,
                         preferred_element_type=F32) + brt_ref[:, 0:1]
    eio = lax.broadcasted_iota(jnp.int32, (n_exp, rows), 0)
    vals, idxs = [], []
    for _ in range(TOP_K):
        m = jnp.max(lt, axis=0, keepdims=True)
        sel = jnp.min(jnp.where(lt == m, eio, n_exp), axis=0, keepdims=True)
        vals.append(m)
        idxs.append(sel)
        lt = jnp.where(eio == sel, -jnp.inf, lt)
    exps = [jnp.exp(v - vals[0]) for v in vals]
    den = exps[0] + exps[1] + exps[2] + exps[3]
    wts_ref[...] = jnp.concatenate([e / den for e in exps], axis=0)
    idx_ref[...] = jnp.concatenate(idxs, axis=0)

    run = carry_ref[:, 0:1]
    ranks = []
    for k in range(TOP_K):
        oh = (eio == idxs[k]).astype(F32)
        excl = _dot(oh.astype(BF16), tri_ref[...])
        ranks.append(jnp.sum(oh * (run + excl), axis=0, keepdims=True))
        run = run + jnp.sum(oh, axis=1, keepdims=True)
    rank_ref[...] = jnp.concatenate(ranks, axis=0).astype(jnp.int32)
    new_carry = jnp.broadcast_to(run, carry_ref.shape)
    carry_ref[...] = new_carry
    cnt_ref[...] = new_carry

    pp = _dot(p_buf[slot].reshape(rows, p_buf.shape[3]).astype(BF16), w_plep_ref[...])
    gate = _sigmoid(_dot(hb, w_pleg_ref[...]))
    r_ref[...] = alpha * h1 + gate * pp

    @pl.when(i == last)
    def _hand_over():
        state_out_ref[...] = state_ref[...]
        hist_out_ref[...] = upool_ref[0:halo, :]


def _const_spec(shape):
    zeros = (0,) * len(shape)
    return pl.BlockSpec(shape, lambda i: zeros, pipeline_mode=pl.Buffered(1))


def _mixer_call(x3, p3, wts, alpha, n_exp, chunk0, n_chunks, state, hist):
    nb, _, d_model = x3.shape
    n_tok = nb * n_chunks * MIX_STEPS
    rows = MIX_STEPS * nb
    grid = n_chunks
    ple_dim = p3.shape[2]
    state_cols = 2 * wts["bm"].shape[2]
    pool_w = wts["pscale"].shape[1]
    halo = max(POOL_WINDOWS) * nb

    def row_spec(width):
        return pl.BlockSpec((rows, width), lambda i: (i, 0))

    names = ["w_in", "bm", "cm", "lam", "dskip", "w_glu", "w_pg", "pscale", "w_pp",
             "w_out", "ln_g", "ln_b", "wrt", "brt", "w_pleg", "w_plep"]
    ops = [wts[n] for n in names]
    hbm_spec = pl.BlockSpec(memory_space=pl.ANY)
    in_specs = ([hbm_spec, hbm_spec] + [_const_spec(o.shape) for o in ops]
                + [_const_spec(state.shape), _const_spec(hist.shape)])
    k_spec = pl.BlockSpec((TOP_K, rows), lambda i: (0, i))
    out_shape = (
        jax.ShapeDtypeStruct((n_tok, d_model // 2), jnp.int32),
        jax.ShapeDtypeStruct((n_tok, d_model), F32),
        jax.ShapeDtypeStruct((TOP_K, n_tok), jnp.int32),
        jax.ShapeDtypeStruct((TOP_K, n_tok), F32),
        jax.ShapeDtypeStruct((TOP_K, n_tok), jnp.int32),
        jax.ShapeDtypeStruct((n_exp, 128), F32),
        jax.ShapeDtypeStruct(state.shape, F32),
        jax.ShapeDtypeStruct(hist.shape, F32),
    )
    out_specs = (row_spec(d_model // 2), row_spec(d_model), k_spec, k_spec, k_spec,
                 pl.BlockSpec((n_exp, 128), lambda i: (0, 0)),
                 pl.BlockSpec(state.shape, lambda i: (0, 0)),
                 pl.BlockSpec(hist.shape, lambda i: (0, 0)))
    scratch = [
        pltpu.VMEM((rows, wts["w_in"].shape[1]), F32),
        pltpu.VMEM((rows, state_cols), F32),
        pltpu.VMEM((rows, state_cols), BF16),
        pltpu.VMEM((halo + rows, pool_w), F32),
        pltpu.VMEM((nb, state_cols), F32),
        pltpu.VMEM((n_exp, 128), F32),
        pltpu.VMEM((rows, rows), BF16),
        pltpu.VMEM((2, MIX_STEPS, nb, d_model), F32),
        pltpu.VMEM((2, MIX_STEPS, nb, ple_dim), F32),
        pltpu.SemaphoreType.DMA((2,)),
        pltpu.SemaphoreType.DMA((2,)),
    ]
    return pl.pallas_call(
        functools.partial(_mixer_kernel, alpha, nb, n_exp, chunk0),
        grid=(grid,),
        in_specs=in_specs,
        out_specs=out_specs,
        out_shape=out_shape,
        scratch_shapes=scratch,
        compiler_params=pltpu.CompilerParams(
            dimension_semantics=("arbitrary",), vmem_limit_bytes=VMEM_LIMIT),
        name="mixer",
    )(x3, p3, *ops, state, hist)


def _expert_kernel(be_ref, nv_ref, xs_ref, wg_ref, bg_ref, wu_ref, bu_ref, wd_ref, bd_ref,
                   ys_ref, wg_bf, wu_bf, wd_bf):
    i = pl.program_id(0)
    half = xs_ref.shape[1]

    @pl.when((i == 0) | (be_ref[i] != be_ref[jnp.maximum(i - 1, 0)]))
    def _new_expert():
        wg_bf[...] = wg_ref[0].astype(BF16)
        wu_bf[...] = wu_ref[0].astype(BF16)
        wd_bf[...] = wd_ref[0].astype(BF16)

    @pl.when(nv_ref[i] > 0)
    def _compute():
        row = lax.broadcasted_iota(jnp.int32, xs_ref.shape, 0)
        x_lo, x_hi = _unpack_rows(jnp.where(row < nv_ref[i], xs_ref[...], 0))
        g = _dot(x_lo, wg_bf[0:half, :]) + _dot(x_hi, wg_bf[half:, :]) + bg_ref[0]
        u = _dot(x_lo, wu_bf[0:half, :]) + _dot(x_hi, wu_bf[half:, :]) + bu_ref[0]
        g = jnp.minimum(g, SWIGLU_LIMIT)
        u = jnp.clip(u, -SWIGLU_LIMIT, SWIGLU_LIMIT)
        act = (u + 1.0) * (g * _sigmoid(SWIGLU_ALPHA * g))
        ys_ref[...] = _pack_rows(_dot(act.astype(BF16), wd_bf[...]) + bd_ref[0])

    @pl.when(nv_ref[i] == 0)
    def _skip():
        ys_ref[...] = jnp.zeros_like(ys_ref)


def _expert_call(block_expert, block_valid, xs, wg, bg, wu, bu, wd, bd):
    n_rows = xs.shape[0]
    d_model = wg.shape[1]
    d_exp = wg.shape[2]
    n_blocks = n_rows // MOE_BLOCK

    def w_spec(k, n):
        return pl.BlockSpec((1, k, n), lambda i, be, nv: (be[i], 0, 0))

    grid_spec = pltpu.PrefetchScalarGridSpec(
        num_scalar_prefetch=2,
        grid=(n_blocks,),
        in_specs=[
            pl.BlockSpec((MOE_BLOCK, d_model // 2), lambda i, be, nv: (i, 0)),
            w_spec(d_model, d_exp), w_spec(1, d_exp),
            w_spec(d_model, d_exp), w_spec(1, d_exp),
            w_spec(d_exp, d_model), w_spec(1, d_model),
        ],
        out_specs=pl.BlockSpec((MOE_BLOCK, d_model // 2), lambda i, be, nv: (i, 0)),
        scratch_shapes=[pltpu.VMEM((d_model, d_exp), BF16), pltpu.VMEM((d_model, d_exp), BF16),
                        pltpu.VMEM((d_exp, d_model), BF16)],
    )
    return pl.pallas_call(
        _expert_kernel,
        grid_spec=grid_spec,
        out_shape=jax.ShapeDtypeStruct((n_rows, d_model // 2), jnp.int32),
        compiler_params=pltpu.CompilerParams(
            dimension_semantics=("arbitrary",), vmem_limit_bytes=VMEM_LIMIT),
        name="experts",
    )(block_expert, block_valid, xs, wg, bg, wu, bu, wd, bd)


def _combine_kernel(chunk0, r_ref, *refs):
    yg_refs = refs[:TOP_K]
    wt_ref, ln_g_ref, ln_b_ref = refs[TOP_K:TOP_K + 3]
    o_hbm, o_buf, o_sem = refs[-3:]
    i = pl.program_id(0)
    last = pl.num_programs(0) - 1
    slot = i % 2

    def write_back(step, from_slot):
        return _time_major_copies(o_hbm, o_buf, o_sem, chunk0 + step, from_slot, to_hbm=True)

    def drain(step, from_slot):
        for cp in write_back(step, from_slot):
            cp.wait()

    @pl.when(i >= 2)
    def _reuse_slot():
        drain(i - 2, slot)

    moe = None
    for k in range(TOP_K):
        lo, hi = _unpack_rows(yg_refs[k][...])
        term = wt_ref[:, k:k + 1] * jnp.concatenate([lo, hi], axis=1).astype(F32)
        moe = term if moe is None else moe + term
    out = _layer_norm(r_ref[...] + moe, ln_g_ref[...], ln_b_ref[...])
    o_buf[slot] = out.reshape(o_buf.shape[1:])
    for cp in write_back(i, slot):
        cp.start()

    @pl.when(i == last)
    def _finish():
        @pl.when(i >= 1)
        def _previous():
            drain(i - 1, 1 - slot)
        drain(i, slot)


def _combine_call(r, yg, wt, ln_g, ln_b, nb, seq, chunk0, earlier):
    n_tok, d_model = r.shape
    grid = n_tok // CMB_ROWS
    steps = CMB_ROWS // nb
    extra = [] if earlier is None else [earlier]

    def slot_spec(k):
        return pl.BlockSpec((CMB_ROWS, d_model // 2), lambda i: (k * grid + i, 0))

    return pl.pallas_call(
        functools.partial(_combine_kernel, chunk0),
        grid=(grid,),
        in_specs=[
            pl.BlockSpec((CMB_ROWS, d_model), lambda i: (i, 0)),
            *[slot_spec(k) for k in range(TOP_K)],
            pl.BlockSpec((CMB_ROWS, TOP_K), lambda i: (i, 0)),
            pl.BlockSpec((1, d_model), lambda i: (0, 0)),
            pl.BlockSpec((1, d_model), lambda i: (0, 0)),
        ] + [pl.BlockSpec(memory_space=pl.ANY)] * len(extra),
        out_specs=pl.BlockSpec(memory_space=pl.ANY),
        out_shape=jax.ShapeDtypeStruct((nb, seq, d_model), F32),
        input_output_aliases={TOP_K + 4: 0} if extra else {},
        scratch_shapes=[pltpu.VMEM((2, steps, nb, d_model), F32), pltpu.SemaphoreType.DMA((2,))],
        compiler_params=pltpu.CompilerParams(
            dimension_semantics=("arbitrary",), vmem_limit_bytes=VMEM_LIMIT),
        name="combine",
    )(r, *([yg] * TOP_K), wt, ln_g, ln_b, *extra)


SC_WINDOW = 128


def _sc_mesh():
    return plsc.VectorSubcoreMesh(core_axis_name="c", subcore_axis_name="s")


def _sc_dispatch(rows, dest, n_out):
    n_tok, width = rows.shape
    n_slot = dest.shape[0]
    mesh = _sc_mesh()
    n_workers = mesh.num_cores * mesh.num_subcores
    per_worker = n_tok // n_workers
    assert per_worker * n_workers == n_tok and per_worker % SC_WINDOW == 0

    @functools.partial(
        pl.kernel, out_type=jax.ShapeDtypeStruct((n_out, width), rows.dtype), mesh=mesh,
        scratch_types=[pltpu.VMEM((n_slot, SC_WINDOW), jnp.int32),
                       pltpu.VMEM((SC_WINDOW, width), rows.dtype),
                       pltpu.SemaphoreType.DMA])
    def dispatch(x_hbm, i_hbm, o_hbm, idx_v, rows_v, sem):
        wid = lax.axis_index("s") * mesh.num_cores + lax.axis_index("c")

        @pl.loop(0, per_worker // SC_WINDOW)
        def _(j):
            base = pl.multiple_of(wid * per_worker + j * SC_WINDOW, SC_WINDOW)
            pltpu.sync_copy(i_hbm.at[:, pl.ds(base, SC_WINDOW)], idx_v)
            pltpu.sync_copy(x_hbm.at[pl.ds(base, SC_WINDOW)], rows_v)
            copies = [pltpu.async_copy(rows_v, o_hbm.at[idx_v.at[k]], sem) for k in range(n_slot)]
            for cp in copies:
                cp.wait()

    return dispatch(rows, dest)


def _sc_gather(table, index):
    n_out = index.shape[0]
    width = table.shape[1]
    mesh = _sc_mesh()
    n_workers = mesh.num_cores * mesh.num_subcores
    per_worker = n_out // n_workers
    assert per_worker * n_workers == n_out and per_worker % SC_WINDOW == 0
    half = SC_WINDOW // 2

    @functools.partial(
        pl.kernel, out_type=jax.ShapeDtypeStruct((n_out, width), table.dtype), mesh=mesh,
        scratch_types=[pltpu.VMEM((SC_WINDOW,), jnp.int32),
                       pltpu.VMEM((2, half, width), table.dtype)]
        + [pltpu.SemaphoreType.DMA] * 4)
    def gather(x_hbm, i_hbm, o_hbm, idx_v, rows_v, g0_sem, g1_sem, w0_sem, w1_sem):
        wid = lax.axis_index("s") * mesh.num_cores + lax.axis_index("c")

        @pl.loop(0, per_worker // SC_WINDOW)
        def _(j):
            base = pl.multiple_of(wid * per_worker + j * SC_WINDOW, SC_WINDOW)
            pltpu.sync_copy(i_hbm.at[pl.ds(base, SC_WINDOW)], idx_v)
            g0 = pltpu.async_copy(x_hbm.at[idx_v.at[pl.ds(0, half)]], rows_v.at[0], g0_sem)
            g1 = pltpu.async_copy(x_hbm.at[idx_v.at[pl.ds(half, half)]], rows_v.at[1], g1_sem)
            g0.wait()
            w0 = pltpu.async_copy(rows_v.at[0], o_hbm.at[pl.ds(base, half)], w0_sem)
            g1.wait()
            w1 = pltpu.async_copy(rows_v.at[1], o_hbm.at[pl.ds(base + half, half)], w1_sem)
            w0.wait()
            w1.wait()

    return gather(table, index)


def _ssm_matrices(lam_re, lam_im, log_step, b_re, b_im, c_re, c_im):
    n_grp, n_state, n_ch = b_re.shape
    lam = lax.complex(lam_re.astype(F32), lam_im.astype(F32))
    step = jnp.exp(log_step.astype(F32))[:, None]
    lam_bar = jnp.exp(lam * step)
    b_bar = ((lam_bar - 1.0) / lam)[..., None] * lax.complex(b_re.astype(F32), b_im.astype(F32))
    hg = n_grp // 2
    eye = jnp.eye(hg, dtype=F32)

    def b_half(bpart):
        return jnp.einsum('gph,gk->ghkp', bpart, eye).reshape(hg * n_ch, hg * n_state)

    def c_half(cpart):
        return jnp.einsum('ghp,gk->gpkh', cpart, eye).reshape(hg * n_state, hg * n_ch)

    bm, cm = [], []
    for hf in range(2):
        sl = slice(hf * hg, (hf + 1) * hg)
        bm.append(jnp.concatenate([b_half(jnp.real(b_bar)[sl]), b_half(jnp.imag(b_bar)[sl])], axis=1))
        cm.append(jnp.concatenate([c_half(c_re.astype(F32)[sl]), -c_half(c_im.astype(F32)[sl])], axis=0))
    lam_rows = jnp.stack([jnp.real(lam_bar).reshape(-1), jnp.imag(lam_bar).reshape(-1)], axis=0)
    return jnp.stack(bm).astype(BF16), jnp.stack(cm).astype(BF16), lam_rows


def _block_diag(w):
    g, c, _ = w.shape
    return jnp.einsum('gcd,gk->gckd', w, jnp.eye(g, dtype=w.dtype)).reshape(g * c, g * c)


def kernel(x, p, w_in, ssm_lambda_re, ssm_lambda_im, ssm_log_step, ssm_b_re, ssm_b_im, ssm_c_re, ssm_c_im, ssm_d, w_glu_val, w_glu_gate, w_pool_group, pool_scale, w_pool_proj, w_out, ln1_g, ln1_b, w_router, b_router, w_gate, b_gate, w_up, b_up, w_down, b_down, w_ple_gate, w_ple_proj, ln2_g, ln2_b):
    bsz, seq, d_model = x.shape
    depth = w_in.shape[0]
    n_exp = w_router.shape[2]
    n_tok = bsz * seq
    alpha = (2.0 * depth) ** 0.25

    h = x
    for l in range(depth):
        bm, cm, lam_rows = _ssm_matrices(ssm_lambda_re[l], ssm_lambda_im[l], ssm_log_step[l],
                                         ssm_b_re[l], ssm_b_im[l], ssm_c_re[l], ssm_c_im[l])
        wts = {
            "w_in": w_in[l].astype(BF16), "bm": bm, "cm": cm, "lam": lam_rows,
            "dskip": ssm_d[l].reshape(1, -1).astype(F32),
            "w_glu": jnp.concatenate([w_glu_val[l], w_glu_gate[l]], axis=1).astype(BF16),
            "w_pg": _block_diag(w_pool_group[l]).astype(BF16),
            "pscale": pool_scale[l].reshape(1, -1).astype(F32),
            "w_pp": w_pool_proj[l].astype(BF16),
            "w_out": w_out[l].astype(BF16),
            "ln_g": ln1_g[l].reshape(1, -1).astype(F32),
            "ln_b": ln1_b[l].reshape(1, -1).astype(F32),
            "wrt": jnp.transpose(w_router[l]).astype(F32),
            "brt": jnp.broadcast_to(b_router[l].astype(F32)[:, None], (n_exp, 128)),
            "w_pleg": w_ple_gate[l].astype(BF16),
            "w_plep": w_ple_proj[l].astype(BF16),
        }
        expert_w = (w_gate[l].astype(F32), b_gate[l].astype(F32)[:, None, :],
                    w_up[l].astype(F32), b_up[l].astype(F32)[:, None, :],
                    w_down[l].astype(F32), b_down[l].astype(F32)[:, None, :])
        ln2 = (ln2_g[l].reshape(1, -1).astype(F32), ln2_b[l].reshape(1, -1).astype(F32))

        chunks = seq // MIX_STEPS
        per_piece = chunks // TIME_PIECES
        assert per_piece * TIME_PIECES == chunks and MIX_STEPS * bsz == CMB_ROWS
        state = jnp.zeros((bsz, 2 * bm.shape[2]), F32)
        hist = jnp.zeros((max(POOL_WINDOWS) * bsz, pool_scale.shape[1]), F32)
        out = None
        for piece in range(TIME_PIECES):
            chunk0 = piece * per_piece
            hp, r, idx, wt, rank, cnt, state, hist = _mixer_call(
                h, p[l], wts, alpha, n_exp, chunk0, per_piece, state, hist)
            n_piece = hp.shape[0]

            counts = cnt[:, 0].astype(jnp.int32)
            padded = ((counts + MOE_BLOCK - 1) // MOE_BLOCK) * MOE_BLOCK
            pad_end = jnp.cumsum(padded)
            pad_start = pad_end - padded
            start_of = jnp.zeros(idx.shape, jnp.int32)
            for e in range(n_exp):
                start_of = jnp.where(idx == e, pad_start[e], start_of)
            dest = start_of + rank
            n_rows = n_piece * TOP_K + n_exp * MOE_BLOCK
            blk_start = jnp.arange(n_rows // MOE_BLOCK, dtype=jnp.int32) * MOE_BLOCK
            block_expert = jnp.minimum(
                jnp.sum((blk_start[:, None] >= pad_end[None, :]).astype(jnp.int32), axis=1),
                n_exp - 1)
            used_end = (pad_start + counts)[block_expert]
            block_valid = jnp.clip(used_end - blk_start, 0, MOE_BLOCK).astype(jnp.int32)

            xs = _sc_dispatch(hp, dest, n_rows)
            ys = _expert_call(block_expert, block_valid, xs, *expert_w)
            yg = _sc_gather(ys, dest.reshape(-1))
            out = _combine_call(r, yg, jnp.transpose(wt), *ln2, bsz, seq, chunk0, out)
        h = out
    return h
```

```python
import functools
import math

import jax
import jax.numpy as jnp
from jax import lax
from jax.experimental import pallas as pl
from jax.experimental.pallas import tpu as pltpu
from jax.experimental.pallas import tpu_sc as plsc

F32 = jnp.float32
BF16 = jnp.bfloat16

LN_EPS = 1e-5
SWIGLU_LIMIT = 7.0
SWIGLU_ALPHA = 1.702
POOL_WINDOWS = (2, 4, 8, 16)
TOP_K = 4

MIX_STEPS = 32
SCAN_LANES = 512
SCAN_SEGMENTS = 2
MOE_BLOCK = 512
CMB_ROWS = 512
TIME_PIECES = 2
VMEM_LIMIT = 60 * 1024 * 1024


def _sigmoid(v):
    return 0.5 * jnp.tanh(0.5 * v) + 0.5


def _layer_norm(v, g, b):
    mu = jnp.mean(v, axis=-1, keepdims=True)
    vc = v - mu
    var = jnp.mean(vc * vc, axis=-1, keepdims=True)
    return vc * lax.rsqrt(var + LN_EPS) * g + b


def _dot(a, b):
    return jnp.dot(a, b, preferred_element_type=F32)


def _pack_rows(v):
    n = v.shape[1] // 2
    lo = lax.bitcast_convert_type(v[:, :n].astype(BF16).astype(F32), jnp.int32)
    hi = lax.bitcast_convert_type(v[:, n:].astype(BF16).astype(F32), jnp.int32)
    return hi | lax.shift_right_logical(lo, 16)


def _unpack_rows(w):
    lo = lax.bitcast_convert_type(lax.shift_left(w, 16), F32).astype(BF16)
    hi = lax.bitcast_convert_type(w & jnp.int32(-65536), F32).astype(BF16)
    return lo, hi


def _time_major_copies(seq_hbm, tm_buf, sem, chunk, slot, to_hbm=False):
    steps, nb = tm_buf.shape[1], tm_buf.shape[2]
    copies = []
    for b in range(nb):
        hbm = seq_hbm.at[b, pl.ds(chunk * steps, steps), :]
        vmem = tm_buf.at[slot, :, b, :]
        src, dst = (vmem, hbm) if to_hbm else (hbm, vmem)
        copies.append(pltpu.make_async_copy(src, dst, sem.at[slot]))
    return copies


def _mixer_kernel(alpha, nb, n_exp, chunk0,
                  x_hbm, p_hbm, w_in_ref, bm_ref, cm_ref, lam_ref, dskip_ref,
                  w_glu_ref, w_pg_ref, pscale_ref, w_pp_ref, w_out_ref,
                  ln_g_ref, ln_b_ref, wr_ref, brt_ref, w_pleg_ref, w_plep_ref,
                  state_in_ref, hist_in_ref,
                  h_ref, r_ref, idx_ref, wts_ref, rank_ref, cnt_ref, state_out_ref, hist_out_ref,
                  proj_ref, bu_ref, st_ref, upool_ref, state_ref, carry_ref, tri_ref, xb_ref,
                  x_buf, p_buf, x_sem, p_sem):
    i = pl.program_id(0)
    last = pl.num_programs(0) - 1
    steps = x_buf.shape[1]
    rows = steps * nb
    d_model = x_buf.shape[3]

    slot = i % 2

    def fetch_copies(step, which):
        return (_time_major_copies(x_hbm, x_buf, x_sem, chunk0 + step, which)
                + _time_major_copies(p_hbm, p_buf, p_sem, chunk0 + step, which))

    @pl.when(i == 0)
    def _first_fetch():
        for cp in fetch_copies(0, 0):
            cp.start()

    @pl.when(i < last)
    def _next_fetch():
        for cp in fetch_copies(i + 1, 1 - slot):
            cp.start()

    for cp in fetch_copies(i, slot):
        cp.wait()
    ssm_w = dskip_ref.shape[1]
    pool_w = pscale_ref.shape[1]
    half_w = ssm_w // 2
    half_s = bu_ref.shape[1] // 2
    plane = half_s // 2
    halo = upool_ref.shape[0] - rows

    @pl.when(i == 0)
    def _init():
        state_ref[...] = state_in_ref[...]
        carry_ref[...] = jnp.zeros_like(carry_ref)
        upool_ref[0:halo, :] = hist_in_ref[...]
        ri = lax.broadcasted_iota(jnp.int32, (rows, rows), 0)
        ci = lax.broadcasted_iota(jnp.int32, (rows, rows), 1)
        tri_ref[...] = (ri < ci).astype(BF16)

    xf = x_buf[slot].reshape(rows, d_model)
    xb_ref[...] = xf.astype(BF16)
    n_mix = ssm_w + pool_w
    proj_ref[:, 0:n_mix] = _dot(xb_ref[...], w_in_ref[:, 0:n_mix])

    us = proj_ref[:, 0:ssm_w]
    usb = us.astype(BF16)
    for hf in range(2):
        bu_ref[:, hf * half_s:(hf + 1) * half_s] = _dot(
            usb[:, hf * half_w:(hf + 1) * half_w], bm_ref[hf])

    gate_cols = w_in_ref.shape[1] - n_mix
    lane_groups = [(hf, q) for hf in range(2) for q in range(plane // SCAN_LANES)]
    segments = len(lane_groups) * SCAN_SEGMENTS
    tile = gate_cols // segments
    seg_steps = steps // SCAN_SEGMENTS
    for gi, (hf, q) in enumerate(lane_groups):
        cre = hf * half_s + q * SCAN_LANES
        cim = cre + plane
        cl = hf * plane + q * SCAN_LANES
        a_re = jnp.broadcast_to(lam_ref[0:1, cl:cl + SCAN_LANES], (nb, SCAN_LANES))
        a_im = jnp.broadcast_to(lam_ref[1:2, cl:cl + SCAN_LANES], (nb, SCAN_LANES))
        s_re = state_ref[:, cre:cre + SCAN_LANES]
        s_im = state_ref[:, cim:cim + SCAN_LANES]
        for seg in range(SCAN_SEGMENTS):
            c0 = n_mix + (gi * SCAN_SEGMENTS + seg) * tile
            proj_ref[:, c0:c0 + tile] = _dot(xb_ref[...], w_in_ref[:, c0:c0 + tile])
            for t in range(seg * seg_steps, (seg + 1) * seg_steps):
                r0 = t * nb
                b_re = bu_ref[r0:r0 + nb, cre:cre + SCAN_LANES]
                b_im = bu_ref[r0:r0 + nb, cim:cim + SCAN_LANES]
                s_re, s_im = (a_re * s_re - a_im * s_im + b_re,
                              a_re * s_im + a_im * s_re + b_im)
                st_ref[r0:r0 + nb, cre:cre + SCAN_LANES] = s_re.astype(BF16)
                st_ref[r0:r0 + nb, cim:cim + SCAN_LANES] = s_im.astype(BF16)
        state_ref[:, cre:cre + SCAN_LANES] = s_re
        state_ref[:, cim:cim + SCAN_LANES] = s_im

    y = jnp.concatenate(
        [_dot(st_ref[:, hf * half_s:(hf + 1) * half_s], cm_ref[hf]) for hf in range(2)],
        axis=1) + dskip_ref[...] * us
    z = 0.5 * y * (1.0 + jnp.tanh(math.sqrt(2.0 / math.pi) * (y + 0.044715 * (y * y * y))))
    vg = _dot(z.astype(BF16), w_glu_ref[...])
    y_ssm = vg[:, 0:d_model] * _sigmoid(vg[:, d_model:2 * d_model])

    up = proj_ref[:, ssm_w:ssm_w + pool_w]
    upool_ref[halo:halo + rows, :] = up
    t_abs = ((chunk0 + i) * steps
             + lax.broadcasted_iota(jnp.int32, (rows, 1), 0) // nb).astype(F32)
    gdim = pool_w // len(POOL_WINDOWS)
    pooled = []
    for g, w in enumerate(POOL_WINDOWS):
        cur = upool_ref[:, g * gdim:(g + 1) * gdim]
        span = 1
        while span < w:
            sh = span * nb
            cur = cur[sh:] + cur[:-sh]
            span *= 2
        win = cur[cur.shape[0] - rows:]
        inv = 1.0 / jnp.minimum(t_abs + 1.0, float(w))
        pooled.append(win * inv - up[:, g * gdim:(g + 1) * gdim])
    upool_ref[0:halo, :] = upool_ref[rows:rows + halo, :]
    pooled = jnp.concatenate(pooled, axis=1)
    mixed = _dot(pooled.astype(BF16), w_pg_ref[...])
    y_pool = _dot((mixed * pscale_ref[...]).astype(BF16), w_pp_ref[...])

    g0 = ssm_w + pool_w
    merged = (_sigmoid(proj_ref[:, g0:g0 + d_model]) * y_ssm
              + _sigmoid(proj_ref[:, g0 + d_model:g0 + 2 * d_model]) * y_pool)
    pre = alpha * xf + _dot(merged.astype(BF16), w_out_ref[...])
    h1 = _layer_norm(pre, ln_g_ref[...], ln_b_ref[...])
    hb = h1.astype(BF16)
    h_ref[...] = _pack_rows(h1)

    pad_e = wr_ref.shape[1] // 2
    h_lo = (h1 - hb.astype(F32)).astype(BF16)
    l_hi = _dot(hb, wr_ref[...])
    lg = l_hi[:, 0:pad_e] + l_hi[:, pad_e:] + _dot(h_lo, wr_ref[:, 0:pad_e])
    lt = jnp.transpose(lg)[0:n_exp, :] + brt_ref[:, 0:1]
    eio = lax.broadcasted_iota(jnp.int32, (n_exp, rows), 0)
    vals, idxs = [], []
    for _ in range(TOP_K):
        m = jnp.max(lt, axis=0, keepdims=True)
        sel = jnp.min(jnp.where(lt == m, eio, n_exp), axis=0, keepdims=True)
        vals.append(m)
        idxs.append(sel)
        lt = jnp.where(eio == sel, -jnp.inf, lt)
    exps = [jnp.exp(v - vals[0]) for v in vals]
    den = exps[0] + exps[1] + exps[2] + exps[3]
    wts_ref[...] = jnp.concatenate([e / den for e in exps], axis=0)
    idx_ref[...] = jnp.concatenate(idxs, axis=0)

    run = carry_ref[:, 0:1]
    ranks = []
    one_hot = [(eio == idxs[k]).astype(F32) for k in range(TOP_K)]
    before = _dot(jnp.concatenate(one_hot, axis=0).astype(BF16), tri_ref[...])
    for k in range(TOP_K):
        oh = one_hot[k]
        ranks.append(jnp.sum(oh * (run + before[k * n_exp:(k + 1) * n_exp, :]),
                             axis=0, keepdims=True))
        run = run + jnp.sum(oh, axis=1, keepdims=True)
    rank_ref[...] = jnp.concatenate(ranks, axis=0).astype(jnp.int32)
    new_carry = jnp.broadcast_to(run, carry_ref.shape)
    carry_ref[...] = new_carry
    cnt_ref[...] = new_carry

    pp = _dot(p_buf[slot].reshape(rows, p_buf.shape[3]).astype(BF16), w_plep_ref[...])
    gate = _sigmoid(_dot(hb, w_pleg_ref[...]))
    r_ref[...] = alpha * h1 + gate * pp

    @pl.when(i == last)
    def _hand_over():
        state_out_ref[...] = state_ref[...]
        hist_out_ref[...] = upool_ref[0:halo, :]


def _const_spec(shape):
    zeros = (0,) * len(shape)
    return pl.BlockSpec(shape, lambda i: zeros, pipeline_mode=pl.Buffered(1))


def _mixer_call(x3, p3, wts, alpha, n_exp, chunk0, n_chunks, state, hist):
    nb, _, d_model = x3.shape
    n_tok = nb * n_chunks * MIX_STEPS
    rows = MIX_STEPS * nb
    grid = n_chunks
    ple_dim = p3.shape[2]
    state_cols = 2 * wts["bm"].shape[2]
    pool_w = wts["pscale"].shape[1]
    halo = max(POOL_WINDOWS) * nb

    def row_spec(width):
        return pl.BlockSpec((rows, width), lambda i: (i, 0))

    names = ["w_in", "bm", "cm", "lam", "dskip", "w_glu", "w_pg", "pscale", "w_pp",
             "w_out", "ln_g", "ln_b", "wr", "brt", "w_pleg", "w_plep"]
    ops = [wts[n] for n in names]
    hbm_spec = pl.BlockSpec(memory_space=pl.ANY)
    in_specs = ([hbm_spec, hbm_spec] + [_const_spec(o.shape) for o in ops]
                + [_const_spec(state.shape), _const_spec(hist.shape)])
    k_spec = pl.BlockSpec((TOP_K, rows), lambda i: (0, i))
    out_shape = (
        jax.ShapeDtypeStruct((n_tok, d_model // 2), jnp.int32),
        jax.ShapeDtypeStruct((n_tok, d_model), F32),
        jax.ShapeDtypeStruct((TOP_K, n_tok), jnp.int32),
        jax.ShapeDtypeStruct((TOP_K, n_tok), F32),
        jax.ShapeDtypeStruct((TOP_K, n_tok), jnp.int32),
        jax.ShapeDtypeStruct((n_exp, 128), F32),
        jax.ShapeDtypeStruct(state.shape, F32),
        jax.ShapeDtypeStruct(hist.shape, F32),
    )
    out_specs = (row_spec(d_model // 2), row_spec(d_model), k_spec, k_spec, k_spec,
                 pl.BlockSpec((n_exp, 128), lambda i: (0, 0)),
                 pl.BlockSpec(state.shape, lambda i: (0, 0)),
                 pl.BlockSpec(hist.shape, lambda i: (0, 0)))
    scratch = [
        pltpu.VMEM((rows, wts["w_in"].shape[1]), F32),
        pltpu.VMEM((rows, state_cols), F32),
        pltpu.VMEM((rows, state_cols), BF16),
        pltpu.VMEM((halo + rows, pool_w), F32),
        pltpu.VMEM((nb, state_cols), F32),
        pltpu.VMEM((n_exp, 128), F32),
        pltpu.VMEM((rows, rows), BF16),
        pltpu.VMEM((rows, d_model), BF16),
        pltpu.VMEM((2, MIX_STEPS, nb, d_model), F32),
        pltpu.VMEM((2, MIX_STEPS, nb, ple_dim), F32),
        pltpu.SemaphoreType.DMA((2,)),
        pltpu.SemaphoreType.DMA((2,)),
    ]
    return pl.pallas_call(
        functools.partial(_mixer_kernel, alpha, nb, n_exp, chunk0),
        grid=(grid,),
        in_specs=in_specs,
        out_specs=out_specs,
        out_shape=out_shape,
        scratch_shapes=scratch,
        compiler_params=pltpu.CompilerParams(
            dimension_semantics=("arbitrary",), vmem_limit_bytes=VMEM_LIMIT),
        name="mixer",
    )(x3, p3, *ops, state, hist)


def _expert_kernel(be_ref, nv_ref, xs_ref, wg_ref, bg_ref, wu_ref, bu_ref, wd_ref, bd_ref,
                   ys_ref, wg_bf, wu_bf, wd_bf):
    i = pl.program_id(0)
    half = xs_ref.shape[1]

    @pl.when((i == 0) | (be_ref[i] != be_ref[jnp.maximum(i - 1, 0)]))
    def _new_expert():
        wg_bf[...] = wg_ref[0].astype(BF16)
        wu_bf[...] = wu_ref[0].astype(BF16)
        wd_bf[...] = wd_ref[0].astype(BF16)

    @pl.when(nv_ref[i] > 0)
    def _compute():
        row = lax.broadcasted_iota(jnp.int32, xs_ref.shape, 0)
        x_lo, x_hi = _unpack_rows(jnp.where(row < nv_ref[i], xs_ref[...], 0))
        g = _dot(x_lo, wg_bf[0:half, :]) + _dot(x_hi, wg_bf[half:, :]) + bg_ref[0]
        u = _dot(x_lo, wu_bf[0:half, :]) + _dot(x_hi, wu_bf[half:, :]) + bu_ref[0]
        g = jnp.minimum(g, SWIGLU_LIMIT)
        u = jnp.clip(u, -SWIGLU_LIMIT, SWIGLU_LIMIT)
        act = (u + 1.0) * (g * _sigmoid(SWIGLU_ALPHA * g))
        ys_ref[...] = _pack_rows(_dot(act.astype(BF16), wd_bf[...]) + bd_ref[0])

    @pl.when(nv_ref[i] == 0)
    def _skip():
        ys_ref[...] = jnp.zeros_like(ys_ref)


def _expert_call(block_expert, block_valid, xs, wg, bg, wu, bu, wd, bd):
    n_rows = xs.shape[0]
    d_model = wg.shape[1]
    d_exp = wg.shape[2]
    n_blocks = n_rows // MOE_BLOCK

    def w_spec(k, n):
        return pl.BlockSpec((1, k, n), lambda i, be, nv: (be[i], 0, 0))

    grid_spec = pltpu.PrefetchScalarGridSpec(
        num_scalar_prefetch=2,
        grid=(n_blocks,),
        in_specs=[
            pl.BlockSpec((MOE_BLOCK, d_model // 2), lambda i, be, nv: (i, 0)),
            w_spec(d_model, d_exp), w_spec(1, d_exp),
            w_spec(d_model, d_exp), w_spec(1, d_exp),
            w_spec(d_exp, d_model), w_spec(1, d_model),
        ],
        out_specs=pl.BlockSpec((MOE_BLOCK, d_model // 2), lambda i, be, nv: (i, 0)),
        scratch_shapes=[pltpu.VMEM((d_model, d_exp), BF16), pltpu.VMEM((d_model, d_exp), BF16),
                        pltpu.VMEM((d_exp, d_model), BF16)],
    )
    return pl.pallas_call(
        _expert_kernel,
        grid_spec=grid_spec,
        out_shape=jax.ShapeDtypeStruct((n_rows, d_model // 2), jnp.int32),
        compiler_params=pltpu.CompilerParams(
            dimension_semantics=("arbitrary",), vmem_limit_bytes=VMEM_LIMIT),
        name="experts",
    )(block_expert, block_valid, xs, wg, bg, wu, bu, wd, bd)


def _combine_kernel(chunk0, r_ref, *refs):
    yg_refs = refs[:TOP_K]
    wt_ref, ln_g_ref, ln_b_ref = refs[TOP_K:TOP_K + 3]
    o_hbm, o_buf, o_sem = refs[-3:]
    i = pl.program_id(0)
    last = pl.num_programs(0) - 1
    slot = i % 2

    def write_back(step, from_slot):
        return _time_major_copies(o_hbm, o_buf, o_sem, chunk0 + step, from_slot, to_hbm=True)

    def drain(step, from_slot):
        for cp in write_back(step, from_slot):
            cp.wait()

    @pl.when(i >= 2)
    def _reuse_slot():
        drain(i - 2, slot)

    moe = None
    for k in range(TOP_K):
        lo, hi = _unpack_rows(yg_refs[k][...])
        term = wt_ref[:, k:k + 1] * jnp.concatenate([lo, hi], axis=1).astype(F32)
        moe = term if moe is None else moe + term
    out = _layer_norm(r_ref[...] + moe, ln_g_ref[...], ln_b_ref[...])
    o_buf[slot] = out.reshape(o_buf.shape[1:])
    for cp in write_back(i, slot):
        cp.start()

    @pl.when(i == last)
    def _finish():
        @pl.when(i >= 1)
        def _previous():
            drain(i - 1, 1 - slot)
        drain(i, slot)


def _combine_call(r, yg, wt, ln_g, ln_b, nb, seq, chunk0, earlier):
    n_tok, d_model = r.shape
    grid = n_tok // CMB_ROWS
    steps = CMB_ROWS // nb
    extra = [] if earlier is None else [earlier]

    def slot_spec(k):
        return pl.BlockSpec((CMB_ROWS, d_model // 2), lambda i: (k * grid + i, 0))

    return pl.pallas_call(
        functools.partial(_combine_kernel, chunk0),
        grid=(grid,),
        in_specs=[
            pl.BlockSpec((CMB_ROWS, d_model), lambda i: (i, 0)),
            *[slot_spec(k) for k in range(TOP_K)],
            pl.BlockSpec((CMB_ROWS, TOP_K), lambda i: (i, 0)),
            pl.BlockSpec((1, d_model), lambda i: (0, 0)),
            pl.BlockSpec((1, d_model), lambda i: (0, 0)),
        ] + [pl.BlockSpec(memory_space=pl.ANY)] * len(extra),
        out_specs=pl.BlockSpec(memory_space=pl.ANY),
        out_shape=jax.ShapeDtypeStruct((nb, seq, d_model), F32),
        input_output_aliases={TOP_K + 4: 0} if extra else {},
        scratch_shapes=[pltpu.VMEM((2, steps, nb, d_model), F32), pltpu.SemaphoreType.DMA((2,))],
        compiler_params=pltpu.CompilerParams(
            dimension_semantics=("arbitrary",), vmem_limit_bytes=VMEM_LIMIT),
        name="combine",
    )(r, *([yg] * TOP_K), wt, ln_g, ln_b, *extra)


SC_WINDOW = 128


def _sc_mesh():
    return plsc.VectorSubcoreMesh(core_axis_name="c", subcore_axis_name="s")


def _sc_dispatch(rows, dest, n_out):
    n_tok, width = rows.shape
    n_slot = dest.shape[0]
    mesh = _sc_mesh()
    n_workers = mesh.num_cores * mesh.num_subcores
    per_worker = n_tok // n_workers
    assert per_worker * n_workers == n_tok and per_worker % SC_WINDOW == 0

    @functools.partial(
        pl.kernel, out_type=jax.ShapeDtypeStruct((n_out, width), rows.dtype), mesh=mesh,
        scratch_types=[pltpu.VMEM((n_slot, SC_WINDOW), jnp.int32),
                       pltpu.VMEM((SC_WINDOW, width), rows.dtype),
                       pltpu.SemaphoreType.DMA])
    def dispatch(x_hbm, i_hbm, o_hbm, idx_v, rows_v, sem):
        wid = lax.axis_index("s") * mesh.num_cores + lax.axis_index("c")

        @pl.loop(0, per_worker // SC_WINDOW)
        def _(j):
            base = pl.multiple_of(wid * per_worker + j * SC_WINDOW, SC_WINDOW)
            pltpu.sync_copy(i_hbm.at[:, pl.ds(base, SC_WINDOW)], idx_v)
            pltpu.sync_copy(x_hbm.at[pl.ds(base, SC_WINDOW)], rows_v)
            copies = [pltpu.async_copy(rows_v, o_hbm.at[idx_v.at[k]], sem) for k in range(n_slot)]
            for cp in copies:
                cp.wait()

    return dispatch(rows, dest)


def _sc_gather(table, index):
    n_out = index.shape[0]
    width = table.shape[1]
    mesh = _sc_mesh()
    n_workers = mesh.num_cores * mesh.num_subcores
    per_worker = n_out // n_workers
    assert per_worker * n_workers == n_out and per_worker % SC_WINDOW == 0
    half = SC_WINDOW // 2

    @functools.partial(
        pl.kernel, out_type=jax.ShapeDtypeStruct((n_out, width), table.dtype), mesh=mesh,
        scratch_types=[pltpu.VMEM((SC_WINDOW,), jnp.int32),
                       pltpu.VMEM((2, half, width), table.dtype)]
        + [pltpu.SemaphoreType.DMA] * 4)
    def gather(x_hbm, i_hbm, o_hbm, idx_v, rows_v, g0_sem, g1_sem, w0_sem, w1_sem):
        wid = lax.axis_index("s") * mesh.num_cores + lax.axis_index("c")

        @pl.loop(0, per_worker // SC_WINDOW)
        def _(j):
            base = pl.multiple_of(wid * per_worker + j * SC_WINDOW, SC_WINDOW)
            pltpu.sync_copy(i_hbm.at[pl.ds(base, SC_WINDOW)], idx_v)
            g0 = pltpu.async_copy(x_hbm.at[idx_v.at[pl.ds(0, half)]], rows_v.at[0], g0_sem)
            g1 = pltpu.async_copy(x_hbm.at[idx_v.at[pl.ds(half, half)]], rows_v.at[1], g1_sem)
            g0.wait()
            w0 = pltpu.async_copy(rows_v.at[0], o_hbm.at[pl.ds(base, half)], w0_sem)
            g1.wait()
            w1 = pltpu.async_copy(rows_v.at[1], o_hbm.at[pl.ds(base + half, half)], w1_sem)
            w0.wait()
            w1.wait()

    return gather(table, index)


def _ssm_matrices(lam_re, lam_im, log_step, b_re, b_im, c_re, c_im):
    n_grp, n_state, n_ch = b_re.shape
    lam = lax.complex(lam_re.astype(F32), lam_im.astype(F32))
    step = jnp.exp(log_step.astype(F32))[:, None]
    lam_bar = jnp.exp(lam * step)
    b_bar = ((lam_bar - 1.0) / lam)[..., None] * lax.complex(b_re.astype(F32), b_im.astype(F32))
    hg = n_grp // 2
    eye = jnp.eye(hg, dtype=F32)

    def b_half(bpart):
        return jnp.einsum('gph,gk->ghkp', bpart, eye).reshape(hg * n_ch, hg * n_state)

    def c_half(cpart):
        return jnp.einsum('ghp,gk->gpkh', cpart, eye).reshape(hg * n_state, hg * n_ch)

    bm, cm = [], []
    for hf in range(2):
        sl = slice(hf * hg, (hf + 1) * hg)
        bm.append(jnp.concatenate([b_half(jnp.real(b_bar)[sl]), b_half(jnp.imag(b_bar)[sl])], axis=1))
        cm.append(jnp.concatenate([c_half(c_re.astype(F32)[sl]), -c_half(c_im.astype(F32)[sl])], axis=0))
    lam_rows = jnp.stack([jnp.real(lam_bar).reshape(-1), jnp.imag(lam_bar).reshape(-1)], axis=0)
    return jnp.stack(bm).astype(BF16), jnp.stack(cm).astype(BF16), lam_rows


def _split_router(w):
    w = w.astype(F32)
    hi32 = lax.bitcast_convert_type(
        lax.bitcast_convert_type(w, jnp.uint32) & jnp.uint32(0xFFFF0000), F32)
    hi = hi32.astype(BF16)
    lo = (w - hi32).astype(BF16)
    pad = ((0, 0), (0, 128 - w.shape[1]))
    return jnp.concatenate([jnp.pad(hi, pad), jnp.pad(lo, pad)], axis=1)


def _block_diag(w):
    g, c, _ = w.shape
    return jnp.einsum('gcd,gk->gckd', w, jnp.eye(g, dtype=w.dtype)).reshape(g * c, g * c)


def kernel(x, p, w_in, ssm_lambda_re, ssm_lambda_im, ssm_log_step, ssm_b_re, ssm_b_im, ssm_c_re, ssm_c_im, ssm_d, w_glu_val, w_glu_gate, w_pool_group, pool_scale, w_pool_proj, w_out, ln1_g, ln1_b, w_router, b_router, w_gate, b_gate, w_up, b_up, w_down, b_down, w_ple_gate, w_ple_proj, ln2_g, ln2_b):
    bsz, seq, d_model = x.shape
    depth = w_in.shape[0]
    n_exp = w_router.shape[2]
    n_tok = bsz * seq
    alpha = (2.0 * depth) ** 0.25

    h = x
    for l in range(depth):
        bm, cm, lam_rows = _ssm_matrices(ssm_lambda_re[l], ssm_lambda_im[l], ssm_log_step[l],
                                         ssm_b_re[l], ssm_b_im[l], ssm_c_re[l], ssm_c_im[l])
        wts = {
            "w_in": w_in[l].astype(BF16), "bm": bm, "cm": cm, "lam": lam_rows,
            "dskip": ssm_d[l].reshape(1, -1).astype(F32),
            "w_glu": jnp.concatenate([w_glu_val[l], w_glu_gate[l]], axis=1).astype(BF16),
            "w_pg": _block_diag(w_pool_group[l]).astype(BF16),
            "pscale": pool_scale[l].reshape(1, -1).astype(F32),
            "w_pp": w_pool_proj[l].astype(BF16),
            "w_out": w_out[l].astype(BF16),
            "ln_g": ln1_g[l].reshape(1, -1).astype(F32),
            "ln_b": ln1_b[l].reshape(1, -1).astype(F32),
            "wr": _split_router(w_router[l]),
            "brt": jnp.broadcast_to(b_router[l].astype(F32)[:, None], (n_exp, 128)),
            "w_pleg": w_ple_gate[l].astype(BF16),
            "w_plep": w_ple_proj[l].astype(BF16),
        }
        expert_w = (w_gate[l].astype(F32), b_gate[l].astype(F32)[:, None, :],
                    w_up[l].astype(F32), b_up[l].astype(F32)[:, None, :],
                    w_down[l].astype(F32), b_down[l].astype(F32)[:, None, :])
        ln2 = (ln2_g[l].reshape(1, -1).astype(F32), ln2_b[l].reshape(1, -1).astype(F32))

        chunks = seq // MIX_STEPS
        per_piece = chunks // TIME_PIECES
        assert per_piece * TIME_PIECES == chunks and MIX_STEPS * bsz == CMB_ROWS
        state = jnp.zeros((bsz, 2 * bm.shape[2]), F32)
        hist = jnp.zeros((max(POOL_WINDOWS) * bsz, pool_scale.shape[1]), F32)
        out = None
        for piece in range(TIME_PIECES):
            chunk0 = piece * per_piece
            hp, r, idx, wt, rank, cnt, state, hist = _mixer_call(
                h, p[l], wts, alpha, n_exp, chunk0, per_piece, state, hist)
            n_piece = hp.shape[0]

            counts = cnt[:, 0].astype(jnp.int32)
            padded = ((counts + MOE_BLOCK - 1) // MOE_BLOCK) * MOE_BLOCK
            pad_end = jnp.cumsum(padded)
            pad_start = pad_end - padded
            start_of = jnp.zeros(idx.shape, jnp.int32)
            for e in range(n_exp):
                start_of = jnp.where(idx == e, pad_start[e], start_of)
            dest = start_of + rank
            n_rows = n_piece * TOP_K + n_exp * MOE_BLOCK
            blk_start = jnp.arange(n_rows // MOE_BLOCK, dtype=jnp.int32) * MOE_BLOCK
            block_expert = jnp.minimum(
                jnp.sum((blk_start[:, None] >= pad_end[None, :]).astype(jnp.int32), axis=1),
                n_exp - 1)
            used_end = (pad_start + counts)[block_expert]
            block_valid = jnp.clip(used_end - blk_start, 0, MOE_BLOCK).astype(jnp.int32)

            xs = _sc_dispatch(hp, dest, n_rows)
            ys = _expert_call(block_expert, block_valid, xs, *expert_w)
            yg = _sc_gather(ys, dest.reshape(-1))
            out = _combine_call(r, yg, jnp.transpose(wt), *ln2, bsz, seq, chunk0, out)
        h = out
    return h
```

```python
import functools
import math

import jax
import jax.numpy as jnp
from jax import lax
from jax.experimental import pallas as pl
from jax.experimental.pallas import tpu as pltpu
from jax.experimental.pallas import tpu_sc as plsc

F32 = jnp.float32
BF16 = jnp.bfloat16

LN_EPS = 1e-5
SWIGLU_LIMIT = 7.0
SWIGLU_ALPHA = 1.702
POOL_WINDOWS = (2, 4, 8, 16)
TOP_K = 4

MIX_STEPS = 32
SCAN_LANES = 512
SCAN_SEGMENTS = 2
MOE_BLOCK = 512
CMB_ROWS = 512
TIME_PIECES = 2
VMEM_LIMIT = 60 * 1024 * 1024


def _sigmoid(v):
    return 0.5 * jnp.tanh(0.5 * v) + 0.5


def _layer_norm(v, g, b):
    mu = jnp.mean(v, axis=-1, keepdims=True)
    vc = v - mu
    var = jnp.mean(vc * vc, axis=-1, keepdims=True)
    return vc * lax.rsqrt(var + LN_EPS) * g + b


def _dot(a, b):
    return jnp.dot(a, b, preferred_element_type=F32)


def _pack_rows(v):
    n = v.shape[1] // 2
    lo = lax.bitcast_convert_type(v[:, :n].astype(BF16).astype(F32), jnp.int32)
    hi = lax.bitcast_convert_type(v[:, n:].astype(BF16).astype(F32), jnp.int32)
    return hi | lax.shift_right_logical(lo, 16)


def _unpack_rows(w):
    lo = lax.bitcast_convert_type(lax.shift_left(w, 16), F32).astype(BF16)
    hi = lax.bitcast_convert_type(w & jnp.int32(-65536), F32).astype(BF16)
    return lo, hi


def _time_major_copies(seq_hbm, tm_buf, sem, chunk, slot, to_hbm=False):
    steps, nb = tm_buf.shape[1], tm_buf.shape[2]
    copies = []
    for b in range(nb):
        hbm = seq_hbm.at[b, pl.ds(chunk * steps, steps), :]
        vmem = tm_buf.at[slot, :, b, :]
        src, dst = (vmem, hbm) if to_hbm else (hbm, vmem)
        copies.append(pltpu.make_async_copy(src, dst, sem.at[slot]))
    return copies


def _mixer_kernel(alpha, nb, n_exp, chunk0,
                  x_hbm, p_hbm, w_in_ref, bm_ref, cm_ref, lam_ref, dskip_ref,
                  w_glu_ref, w_pg_ref, pscale_ref, w_pp_ref, w_out_ref,
                  ln_g_ref, ln_b_ref, wr_ref, brt_ref, w_pleg_ref, w_plep_ref,
                  state_in_ref, hist_in_ref,
                  h_ref, r_ref, idx_ref, wts_ref, rank_ref, cnt_ref, state_out_ref, hist_out_ref,
                  proj_ref, bu_ref, st_ref, upool_ref, state_ref, carry_ref, tri_ref, xb_ref,
                  x_buf, p_buf, x_sem, p_sem):
    i = pl.program_id(0)
    last = pl.num_programs(0) - 1
    steps = x_buf.shape[1]
    rows = steps * nb
    d_model = x_buf.shape[3]

    slot = i % 2

    def fetch_copies(step, which):
        return (_time_major_copies(x_hbm, x_buf, x_sem, chunk0 + step, which)
                + _time_major_copies(p_hbm, p_buf, p_sem, chunk0 + step, which))

    @pl.when(i == 0)
    def _first_fetch():
        for cp in fetch_copies(0, 0):
            cp.start()

    @pl.when(i < last)
    def _next_fetch():
        for cp in fetch_copies(i + 1, 1 - slot):
            cp.start()

    for cp in fetch_copies(i, slot):
        cp.wait()
    ssm_w = dskip_ref.shape[1]
    pool_w = pscale_ref.shape[1]
    half_w = ssm_w // 2
    half_s = bu_ref.shape[1] // 2
    plane = half_s // 2
    halo = upool_ref.shape[0] - rows

    @pl.when(i == 0)
    def _init():
        state_ref[...] = state_in_ref[...]
        carry_ref[...] = jnp.zeros_like(carry_ref)
        upool_ref[0:halo, :] = hist_in_ref[...]
        ri = lax.broadcasted_iota(jnp.int32, (rows, rows), 0)
        ci = lax.broadcasted_iota(jnp.int32, (rows, rows), 1)
        tri_ref[...] = (ri < ci).astype(BF16)

    xf = x_buf[slot].reshape(rows, d_model)
    xb_ref[...] = xf.astype(BF16)
    n_mix = ssm_w + pool_w
    proj_ref[:, 0:n_mix] = _dot(xb_ref[...], w_in_ref[:, 0:n_mix])

    us = proj_ref[:, 0:ssm_w]
    usb = us.astype(BF16)
    for hf in range(2):
        bu_ref[:, hf * half_s:(hf + 1) * half_s] = _dot(
            usb[:, hf * half_w:(hf + 1) * half_w], bm_ref[hf])

    gate_cols = w_in_ref.shape[1] - n_mix
    lane_groups = [(hf, q) for hf in range(2) for q in range(plane // SCAN_LANES)]
    segments = len(lane_groups) * SCAN_SEGMENTS
    tile = gate_cols // segments
    seg_steps = steps // SCAN_SEGMENTS
    for gi, (hf, q) in enumerate(lane_groups):
        cre = hf * half_s + q * SCAN_LANES
        cim = cre + plane
        cl = hf * plane + q * SCAN_LANES
        a_re = jnp.broadcast_to(lam_ref[0:1, cl:cl + SCAN_LANES], (nb, SCAN_LANES))
        a_im = jnp.broadcast_to(lam_ref[1:2, cl:cl + SCAN_LANES], (nb, SCAN_LANES))
        s_re = state_ref[:, cre:cre + SCAN_LANES]
        s_im = state_ref[:, cim:cim + SCAN_LANES]
        for seg in range(SCAN_SEGMENTS):
            c0 = n_mix + (gi * SCAN_SEGMENTS + seg) * tile
            proj_ref[:, c0:c0 + tile] = _dot(xb_ref[...], w_in_ref[:, c0:c0 + tile])
            for t in range(seg * seg_steps, (seg + 1) * seg_steps):
                r0 = t * nb
                b_re = bu_ref[r0:r0 + nb, cre:cre + SCAN_LANES]
                b_im = bu_ref[r0:r0 + nb, cim:cim + SCAN_LANES]
                s_re, s_im = (a_re * s_re - a_im * s_im + b_re,
                              a_re * s_im + a_im * s_re + b_im)
                st_ref[r0:r0 + nb, cre:cre + SCAN_LANES] = s_re.astype(BF16)
                st_ref[r0:r0 + nb, cim:cim + SCAN_LANES] = s_im.astype(BF16)
        state_ref[:, cre:cre + SCAN_LANES] = s_re
        state_ref[:, cim:cim + SCAN_LANES] = s_im

    y = jnp.concatenate(
        [_dot(st_ref[:, hf * half_s:(hf + 1) * half_s], cm_ref[hf]) for hf in range(2)],
        axis=1) + dskip_ref[...] * us
    z = 0.5 * y * (1.0 + jnp.tanh(math.sqrt(2.0 / math.pi) * (y + 0.044715 * (y * y * y))))
    vg = _dot(z.astype(BF16), w_glu_ref[...])
    y_ssm = vg[:, 0:d_model] * _sigmoid(vg[:, d_model:2 * d_model])

    up = proj_ref[:, ssm_w:ssm_w + pool_w]
    upool_ref[halo:halo + rows, :] = up
    t_abs = ((chunk0 + i) * steps
             + lax.broadcasted_iota(jnp.int32, (rows, 1), 0) // nb).astype(F32)
    gdim = pool_w // len(POOL_WINDOWS)
    pooled = []
    for g, w in enumerate(POOL_WINDOWS):
        cur = upool_ref[:, g * gdim:(g + 1) * gdim]
        span = 1
        while span < w:
            sh = span * nb
            cur = cur[sh:] + cur[:-sh]
            span *= 2
        win = cur[cur.shape[0] - rows:]
        inv = 1.0 / jnp.minimum(t_abs + 1.0, float(w))
        pooled.append(win * inv - up[:, g * gdim:(g + 1) * gdim])
    upool_ref[0:halo, :] = upool_ref[rows:rows + halo, :]
    pooled = jnp.concatenate(pooled, axis=1)
    mixed = _dot(pooled.astype(BF16), w_pg_ref[...])
    y_pool = _dot((mixed * pscale_ref[...]).astype(BF16), w_pp_ref[...])

    g0 = ssm_w + pool_w
    merged = (_sigmoid(proj_ref[:, g0:g0 + d_model]) * y_ssm
              + _sigmoid(proj_ref[:, g0 + d_model:g0 + 2 * d_model]) * y_pool)
    pre = alpha * xf + _dot(merged.astype(BF16), w_out_ref[...])
    h1 = _layer_norm(pre, ln_g_ref[...], ln_b_ref[...])
    hb = h1.astype(BF16)
    h_ref[...] = _pack_rows(h1)

    pad_e = wr_ref.shape[1] // 2
    h_lo = (h1 - hb.astype(F32)).astype(BF16)
    l_hi = _dot(hb, wr_ref[...])
    lg = l_hi[:, 0:pad_e] + l_hi[:, pad_e:] + _dot(h_lo, wr_ref[:, 0:pad_e])
    lt = jnp.transpose(lg)[0:n_exp, :] + brt_ref[:, 0:1]
    eio = lax.broadcasted_iota(jnp.int32, (n_exp, rows), 0)
    vals, idxs = [], []
    for _ in range(TOP_K):
        m = jnp.max(lt, axis=0, keepdims=True)
        sel = jnp.min(jnp.where(lt == m, eio, n_exp), axis=0, keepdims=True)
        vals.append(m)
        idxs.append(sel)
        lt = jnp.where(eio == sel, -jnp.inf, lt)
    exps = [jnp.exp(v - vals[0]) for v in vals]
    den = exps[0] + exps[1] + exps[2] + exps[3]
    wts_ref[...] = jnp.concatenate([e / den for e in exps], axis=0)
    idx_ref[...] = jnp.concatenate(idxs, axis=0)

    run = carry_ref[:, 0:1]
    ranks = []
    one_hot = [(eio == idxs[k]).astype(F32) for k in range(TOP_K)]
    before = _dot(jnp.concatenate(one_hot, axis=0).astype(BF16), tri_ref[...])
    for k in range(TOP_K):
        oh = one_hot[k]
        ranks.append(jnp.sum(oh * (run + before[k * n_exp:(k + 1) * n_exp, :]),
                             axis=0, keepdims=True))
        run = run + jnp.sum(oh, axis=1, keepdims=True)
    rank_ref[...] = jnp.concatenate(ranks, axis=0).astype(jnp.int32)
    new_carry = jnp.broadcast_to(run, carry_ref.shape)
    carry_ref[...] = new_carry
    cnt_ref[...] = new_carry

    pp = _dot(p_buf[slot].reshape(rows, p_buf.shape[3]).astype(BF16), w_plep_ref[...])
    gate = _sigmoid(_dot(hb, w_pleg_ref[...]))
    r_ref[...] = alpha * h1 + gate * pp

    @pl.when(i == last)
    def _hand_over():
        state_out_ref[...] = state_ref[...]
        hist_out_ref[...] = upool_ref[0:halo, :]


def _const_spec(shape):
    zeros = (0,) * len(shape)
    return pl.BlockSpec(shape, lambda i: zeros, pipeline_mode=pl.Buffered(1))


def _mixer_call(x3, p3, wts, alpha, n_exp, chunk0, n_chunks, state, hist):
    nb, _, d_model = x3.shape
    n_tok = nb * n_chunks * MIX_STEPS
    rows = MIX_STEPS * nb
    grid = n_chunks
    ple_dim = p3.shape[2]
    state_cols = 2 * wts["bm"].shape[2]
    pool_w = wts["pscale"].shape[1]
    halo = max(POOL_WINDOWS) * nb

    def row_spec(width):
        return pl.BlockSpec((rows, width), lambda i: (i, 0))

    names = ["w_in", "bm", "cm", "lam", "dskip", "w_glu", "w_pg", "pscale", "w_pp",
             "w_out", "ln_g", "ln_b", "wr", "brt", "w_pleg", "w_plep"]
    ops = [wts[n] for n in names]
    hbm_spec = pl.BlockSpec(memory_space=pl.ANY)
    in_specs = ([hbm_spec, hbm_spec] + [_const_spec(o.shape) for o in ops]
                + [_const_spec(state.shape), _const_spec(hist.shape)])
    k_spec = pl.BlockSpec((TOP_K, rows), lambda i: (0, i))
    out_shape = (
        jax.ShapeDtypeStruct((n_tok, d_model // 2), jnp.int32),
        jax.ShapeDtypeStruct((n_tok, d_model), F32),
        jax.ShapeDtypeStruct((TOP_K, n_tok), jnp.int32),
        jax.ShapeDtypeStruct((TOP_K, n_tok), F32),
        jax.ShapeDtypeStruct((TOP_K, n_tok), jnp.int32),
        jax.ShapeDtypeStruct((n_exp, 128), F32),
        jax.ShapeDtypeStruct(state.shape, F32),
        jax.ShapeDtypeStruct(hist.shape, F32),
    )
    out_specs = (row_spec(d_model // 2), row_spec(d_model), k_spec, k_spec, k_spec,
                 pl.BlockSpec((n_exp, 128), lambda i: (0, 0)),
                 pl.BlockSpec(state.shape, lambda i: (0, 0)),
                 pl.BlockSpec(hist.shape, lambda i: (0, 0)))
    scratch = [
        pltpu.VMEM((rows, wts["w_in"].shape[1]), F32),
        pltpu.VMEM((rows, state_cols), F32),
        pltpu.VMEM((rows, state_cols), BF16),
        pltpu.VMEM((halo + rows, pool_w), F32),
        pltpu.VMEM((nb, state_cols), F32),
        pltpu.VMEM((n_exp, 128), F32),
        pltpu.VMEM((rows, rows), BF16),
        pltpu.VMEM((rows, d_model), BF16),
        pltpu.VMEM((2, MIX_STEPS, nb, d_model), F32),
        pltpu.VMEM((2, MIX_STEPS, nb, ple_dim), F32),
        pltpu.SemaphoreType.DMA((2,)),
        pltpu.SemaphoreType.DMA((2,)),
    ]
    return pl.pallas_call(
        functools.partial(_mixer_kernel, alpha, nb, n_exp, chunk0),
        grid=(grid,),
        in_specs=in_specs,
        out_specs=out_specs,
        out_shape=out_shape,
        scratch_shapes=scratch,
        compiler_params=pltpu.CompilerParams(
            dimension_semantics=("arbitrary",), vmem_limit_bytes=VMEM_LIMIT),
        name="mixer",
    )(x3, p3, *ops, state, hist)


def _expert_kernel(be_ref, nv_ref, first_ref, next_ref, slot_ref,
                   xs_ref, wg_hbm, bg_ref, wu_hbm, bu_ref, wd_hbm, bd_ref,
                   ys_ref, wg_bf, wu_bf, wd_bf, stage, w_sem):
    i = pl.program_id(0)
    half = xs_ref.shape[1]

    def weight_copies(expert, slot):
        return [pltpu.make_async_copy(w_hbm.at[expert], stage.at[slot, m], w_sem.at[slot, m])
                for m, w_hbm in enumerate((wg_hbm, wu_hbm, wd_hbm))]

    @pl.when(first_ref[i] == 1)
    def _new_expert():
        slot = slot_ref[i]

        @pl.when(i == 0)
        def _nothing_prefetched_yet():
            for cp in weight_copies(be_ref[i], slot):
                cp.start()

        @pl.when(next_ref[i] >= 0)
        def _prefetch_next_run():
            for cp in weight_copies(next_ref[i], 1 - slot):
                cp.start()

        for cp in weight_copies(be_ref[i], slot):
            cp.wait()
        wg_bf[...] = stage[slot, 0].astype(BF16)
        wu_bf[...] = stage[slot, 1].astype(BF16)
        wd_bf[...] = stage[slot, 2].astype(BF16)

    @pl.when(nv_ref[i] > 0)
    def _compute():
        row = lax.broadcasted_iota(jnp.int32, xs_ref.shape, 0)
        x_lo, x_hi = _unpack_rows(jnp.where(row < nv_ref[i], xs_ref[...], 0))
        g = _dot(x_lo, wg_bf[0:half, :]) + _dot(x_hi, wg_bf[half:, :]) + bg_ref[0]
        u = _dot(x_lo, wu_bf[0:half, :]) + _dot(x_hi, wu_bf[half:, :]) + bu_ref[0]
        g = jnp.minimum(g, SWIGLU_LIMIT)
        u = jnp.clip(u, -SWIGLU_LIMIT, SWIGLU_LIMIT)
        act = (u + 1.0) * (g * _sigmoid(SWIGLU_ALPHA * g))
        ys_ref[...] = _pack_rows(_dot(act.astype(BF16), wd_bf[...]) + bd_ref[0])

    @pl.when(nv_ref[i] == 0)
    def _skip():
        ys_ref[...] = jnp.zeros_like(ys_ref)


def _expert_call(schedule, xs, wg, bg, wu, bu, wd, bd):
    n_rows = xs.shape[0]
    d_model = wg.shape[1]
    d_exp = wg.shape[2]
    assert d_model == d_exp, "the weight staging buffer assumes square expert matrices"
    n_blocks = n_rows // MOE_BLOCK

    def b_spec(n):
        return pl.BlockSpec((1, 1, n), lambda i, be, *_: (be[i], 0, 0))

    row_spec = pl.BlockSpec((MOE_BLOCK, d_model // 2), lambda i, *_: (i, 0))
    hbm_spec = pl.BlockSpec(memory_space=pl.ANY)
    grid_spec = pltpu.PrefetchScalarGridSpec(
        num_scalar_prefetch=len(schedule),
        grid=(n_blocks,),
        in_specs=[row_spec, hbm_spec, b_spec(d_exp), hbm_spec, b_spec(d_exp), hbm_spec, b_spec(d_model)],
        out_specs=row_spec,
        scratch_shapes=[pltpu.VMEM((d_model, d_exp), BF16), pltpu.VMEM((d_model, d_exp), BF16),
                        pltpu.VMEM((d_exp, d_model), BF16),
                        pltpu.VMEM((2, 3, d_model, d_exp), F32),
                        pltpu.SemaphoreType.DMA((2, 3))],
    )
    return pl.pallas_call(
        _expert_kernel,
        grid_spec=grid_spec,
        out_shape=jax.ShapeDtypeStruct((n_rows, d_model // 2), jnp.int32),
        compiler_params=pltpu.CompilerParams(
            dimension_semantics=("arbitrary",), vmem_limit_bytes=VMEM_LIMIT),
        name="experts",
    )(*schedule, xs, wg, bg, wu, bu, wd, bd)


def _combine_kernel(chunk0, r_ref, *refs):
    yg_refs = refs[:TOP_K]
    wt_ref, ln_g_ref, ln_b_ref = refs[TOP_K:TOP_K + 3]
    o_hbm, o_buf, o_sem = refs[-3:]
    i = pl.program_id(0)
    last = pl.num_programs(0) - 1
    slot = i % 2

    def write_back(step, from_slot):
        return _time_major_copies(o_hbm, o_buf, o_sem, chunk0 + step, from_slot, to_hbm=True)

    def drain(step, from_slot):
        for cp in write_back(step, from_slot):
            cp.wait()

    @pl.when(i >= 2)
    def _reuse_slot():
        drain(i - 2, slot)

    moe = None
    for k in range(TOP_K):
        lo, hi = _unpack_rows(yg_refs[k][...])
        term = wt_ref[:, k:k + 1] * jnp.concatenate([lo, hi], axis=1).astype(F32)
        moe = term if moe is None else moe + term
    out = _layer_norm(r_ref[...] + moe, ln_g_ref[...], ln_b_ref[...])
    o_buf[slot] = out.reshape(o_buf.shape[1:])
    for cp in write_back(i, slot):
        cp.start()

    @pl.when(i == last)
    def _finish():
        @pl.when(i >= 1)
        def _previous():
            drain(i - 1, 1 - slot)
        drain(i, slot)


def _combine_call(r, yg, wt, ln_g, ln_b, nb, seq, chunk0, earlier):
    n_tok, d_model = r.shape
    grid = n_tok // CMB_ROWS
    steps = CMB_ROWS // nb
    extra = [] if earlier is None else [earlier]

    def slot_spec(k):
        return pl.BlockSpec((CMB_ROWS, d_model // 2), lambda i: (k * grid + i, 0))

    return pl.pallas_call(
        functools.partial(_combine_kernel, chunk0),
        grid=(grid,),
        in_specs=[
            pl.BlockSpec((CMB_ROWS, d_model), lambda i: (i, 0)),
            *[slot_spec(k) for k in range(TOP_K)],
            pl.BlockSpec((CMB_ROWS, TOP_K), lambda i: (i, 0)),
            pl.BlockSpec((1, d_model), lambda i: (0, 0)),
            pl.BlockSpec((1, d_model), lambda i: (0, 0)),
        ] + [pl.BlockSpec(memory_space=pl.ANY)] * len(extra),
        out_specs=pl.BlockSpec(memory_space=pl.ANY),
        out_shape=jax.ShapeDtypeStruct((nb, seq, d_model), F32),
        input_output_aliases={TOP_K + 4: 0} if extra else {},
        scratch_shapes=[pltpu.VMEM((2, steps, nb, d_model), F32), pltpu.SemaphoreType.DMA((2,))],
        compiler_params=pltpu.CompilerParams(
            dimension_semantics=("arbitrary",), vmem_limit_bytes=VMEM_LIMIT),
        name="combine",
    )(r, *([yg] * TOP_K), wt, ln_g, ln_b, *extra)


SC_WINDOW = 128


def _sc_mesh():
    return plsc.VectorSubcoreMesh(core_axis_name="c", subcore_axis_name="s")


def _sc_dispatch(rows, dest, n_out):
    n_tok, width = rows.shape
    n_slot = dest.shape[0]
    mesh = _sc_mesh()
    n_workers = mesh.num_cores * mesh.num_subcores
    per_worker = n_tok // n_workers
    assert per_worker * n_workers == n_tok and per_worker % SC_WINDOW == 0

    @functools.partial(
        pl.kernel, out_type=jax.ShapeDtypeStruct((n_out, width), rows.dtype), mesh=mesh,
        scratch_types=[pltpu.VMEM((n_slot, SC_WINDOW), jnp.int32),
                       pltpu.VMEM((SC_WINDOW, width), rows.dtype),
                       pltpu.SemaphoreType.DMA])
    def dispatch(x_hbm, i_hbm, o_hbm, idx_v, rows_v, sem):
        wid = lax.axis_index("s") * mesh.num_cores + lax.axis_index("c")

        @pl.loop(0, per_worker // SC_WINDOW)
        def _(j):
            base = pl.multiple_of(wid * per_worker + j * SC_WINDOW, SC_WINDOW)
            pltpu.sync_copy(i_hbm.at[:, pl.ds(base, SC_WINDOW)], idx_v)
            pltpu.sync_copy(x_hbm.at[pl.ds(base, SC_WINDOW)], rows_v)
            copies = [pltpu.async_copy(rows_v, o_hbm.at[idx_v.at[k]], sem) for k in range(n_slot)]
            for cp in copies:
                cp.wait()

    return dispatch(rows, dest)


def _sc_gather(table, index):
    n_out = index.shape[0]
    width = table.shape[1]
    mesh = _sc_mesh()
    n_workers = mesh.num_cores * mesh.num_subcores
    per_worker = n_out // n_workers
    assert per_worker * n_workers == n_out and per_worker % SC_WINDOW == 0
    half = SC_WINDOW // 2

    @functools.partial(
        pl.kernel, out_type=jax.ShapeDtypeStruct((n_out, width), table.dtype), mesh=mesh,
        scratch_types=[pltpu.VMEM((SC_WINDOW,), jnp.int32),
                       pltpu.VMEM((2, half, width), table.dtype)]
        + [pltpu.SemaphoreType.DMA] * 4)
    def gather(x_hbm, i_hbm, o_hbm, idx_v, rows_v, g0_sem, g1_sem, w0_sem, w1_sem):
        wid = lax.axis_index("s") * mesh.num_cores + lax.axis_index("c")

        @pl.loop(0, per_worker // SC_WINDOW)
        def _(j):
            base = pl.multiple_of(wid * per_worker + j * SC_WINDOW, SC_WINDOW)
            pltpu.sync_copy(i_hbm.at[pl.ds(base, SC_WINDOW)], idx_v)
            g0 = pltpu.async_copy(x_hbm.at[idx_v.at[pl.ds(0, half)]], rows_v.at[0], g0_sem)
            g1 = pltpu.async_copy(x_hbm.at[idx_v.at[pl.ds(half, half)]], rows_v.at[1], g1_sem)
            g0.wait()
            w0 = pltpu.async_copy(rows_v.at[0], o_hbm.at[pl.ds(base, half)], w0_sem)
            g1.wait()
            w1 = pltpu.async_copy(rows_v.at[1], o_hbm.at[pl.ds(base + half, half)], w1_sem)
            w0.wait()
            w1.wait()

    return gather(table, index)


def _ssm_matrices(lam_re, lam_im, log_step, b_re, b_im, c_re, c_im):
    n_grp, n_state, n_ch = b_re.shape
    lam = lax.complex(lam_re.astype(F32), lam_im.astype(F32))
    step = jnp.exp(log_step.astype(F32))[:, None]
    lam_bar = jnp.exp(lam * step)
    b_bar = ((lam_bar - 1.0) / lam)[..., None] * lax.complex(b_re.astype(F32), b_im.astype(F32))
    hg = n_grp // 2
    eye = jnp.eye(hg, dtype=F32)

    def b_half(bpart):
        return jnp.einsum('gph,gk->ghkp', bpart, eye).reshape(hg * n_ch, hg * n_state)

    def c_half(cpart):
        return jnp.einsum('ghp,gk->gpkh', cpart, eye).reshape(hg * n_state, hg * n_ch)

    bm, cm = [], []
    for hf in range(2):
        sl = slice(hf * hg, (hf + 1) * hg)
        bm.append(jnp.concatenate([b_half(jnp.real(b_bar)[sl]), b_half(jnp.imag(b_bar)[sl])], axis=1))
        cm.append(jnp.concatenate([c_half(c_re.astype(F32)[sl]), -c_half(c_im.astype(F32)[sl])], axis=0))
    lam_rows = jnp.stack([jnp.real(lam_bar).reshape(-1), jnp.imag(lam_bar).reshape(-1)], axis=0)
    return jnp.stack(bm).astype(BF16), jnp.stack(cm).astype(BF16), lam_rows


def _split_router(w):
    w = w.astype(F32)
    hi32 = lax.bitcast_convert_type(
        lax.bitcast_convert_type(w, jnp.uint32) & jnp.uint32(0xFFFF0000), F32)
    hi = hi32.astype(BF16)
    lo = (w - hi32).astype(BF16)
    pad = ((0, 0), (0, 128 - w.shape[1]))
    return jnp.concatenate([jnp.pad(hi, pad), jnp.pad(lo, pad)], axis=1)


def _block_diag(w):
    g, c, _ = w.shape
    return jnp.einsum('gcd,gk->gckd', w, jnp.eye(g, dtype=w.dtype)).reshape(g * c, g * c)


def kernel(x, p, w_in, ssm_lambda_re, ssm_lambda_im, ssm_log_step, ssm_b_re, ssm_b_im, ssm_c_re, ssm_c_im, ssm_d, w_glu_val, w_glu_gate, w_pool_group, pool_scale, w_pool_proj, w_out, ln1_g, ln1_b, w_router, b_router, w_gate, b_gate, w_up, b_up, w_down, b_down, w_ple_gate, w_ple_proj, ln2_g, ln2_b):
    bsz, seq, d_model = x.shape
    depth = w_in.shape[0]
    n_exp = w_router.shape[2]
    n_tok = bsz * seq
    alpha = (2.0 * depth) ** 0.25

    h = x
    for l in range(depth):
        bm, cm, lam_rows = _ssm_matrices(ssm_lambda_re[l], ssm_lambda_im[l], ssm_log_step[l],
                                         ssm_b_re[l], ssm_b_im[l], ssm_c_re[l], ssm_c_im[l])
        wts = {
            "w_in": w_in[l].astype(BF16), "bm": bm, "cm": cm, "lam": lam_rows,
            "dskip": ssm_d[l].reshape(1, -1).astype(F32),
            "w_glu": jnp.concatenate([w_glu_val[l], w_glu_gate[l]], axis=1).astype(BF16),
            "w_pg": _block_diag(w_pool_group[l]).astype(BF16),
            "pscale": pool_scale[l].reshape(1, -1).astype(F32),
            "w_pp": w_pool_proj[l].astype(BF16),
            "w_out": w_out[l].astype(BF16),
            "ln_g": ln1_g[l].reshape(1, -1).astype(F32),
            "ln_b": ln1_b[l].reshape(1, -1).astype(F32),
            "wr": _split_router(w_router[l]),
            "brt": jnp.broadcast_to(b_router[l].astype(F32)[:, None], (n_exp, 128)),
            "w_pleg": w_ple_gate[l].astype(BF16),
            "w_plep": w_ple_proj[l].astype(BF16),
        }
        expert_w = (w_gate[l].astype(F32), b_gate[l].astype(F32)[:, None, :],
                    w_up[l].astype(F32), b_up[l].astype(F32)[:, None, :],
                    w_down[l].astype(F32), b_down[l].astype(F32)[:, None, :])
        ln2 = (ln2_g[l].reshape(1, -1).astype(F32), ln2_b[l].reshape(1, -1).astype(F32))

        chunks = seq // MIX_STEPS
        per_piece = chunks // TIME_PIECES
        assert per_piece * TIME_PIECES == chunks and MIX_STEPS * bsz == CMB_ROWS
        state = jnp.zeros((bsz, 2 * bm.shape[2]), F32)
        hist = jnp.zeros((max(POOL_WINDOWS) * bsz, pool_scale.shape[1]), F32)
        out = None
        for piece in range(TIME_PIECES):
            chunk0 = piece * per_piece
            hp, r, idx, wt, rank, cnt, state, hist = _mixer_call(
                h, p[l], wts, alpha, n_exp, chunk0, per_piece, state, hist)
            n_piece = hp.shape[0]

            counts = cnt[:, 0].astype(jnp.int32)
            padded = ((counts + MOE_BLOCK - 1) // MOE_BLOCK) * MOE_BLOCK
            pad_end = jnp.cumsum(padded)
            pad_start = pad_end - padded
            start_of = jnp.zeros(idx.shape, jnp.int32)
            for e in range(n_exp):
                start_of = jnp.where(idx == e, pad_start[e], start_of)
            dest = start_of + rank
            n_rows = n_piece * TOP_K + n_exp * MOE_BLOCK
            blk_start = jnp.arange(n_rows // MOE_BLOCK, dtype=jnp.int32) * MOE_BLOCK
            block_expert = jnp.minimum(
                jnp.sum((blk_start[:, None] >= pad_end[None, :]).astype(jnp.int32), axis=1),
                n_exp - 1)
            used_end = (pad_start + counts)[block_expert]
            block_valid = jnp.clip(used_end - blk_start, 0, MOE_BLOCK).astype(jnp.int32)
            live = block_valid > 0
            prev_expert = jnp.concatenate([jnp.full((1,), -1, jnp.int32), block_expert[:-1]])
            block_first = (live & (block_expert != prev_expert)).astype(jnp.int32)
            ids = jnp.arange(n_exp, dtype=jnp.int32)
            later = (ids[None, :] > ids[:, None]) & (counts[None, :] > 0)
            next_live = jnp.min(jnp.where(later, ids[None, :], n_exp), axis=1)
            next_live = jnp.where(next_live == n_exp, -1, next_live).astype(jnp.int32)
            run_index = jnp.cumsum((counts > 0).astype(jnp.int32)) - 1
            schedule = (block_expert, block_valid, block_first, next_live[block_expert],
                        (run_index[block_expert] % 2).astype(jnp.int32))

            xs = _sc_dispatch(hp, dest, n_rows)
            ys = _expert_call(schedule, xs, *expert_w)
            yg = _sc_gather(ys, dest.reshape(-1))
            out = _combine_call(r, yg, jnp.transpose(wt), *ln2, bsz, seq, chunk0, out)
        h = out
    return h
```

```python
import functools
import math

import jax
import jax.numpy as jnp
from jax import lax
from jax.experimental import pallas as pl
from jax.experimental.pallas import tpu as pltpu
from jax.experimental.pallas import tpu_sc as plsc

F32 = jnp.float32
BF16 = jnp.bfloat16

LN_EPS = 1e-5
SWIGLU_LIMIT = 7.0
SWIGLU_ALPHA = 1.702
POOL_WINDOWS = (2, 4, 8, 16)
TOP_K = 4

MIX_STEPS = 32
SCAN_LANES = 512
SCAN_SEGMENTS = 2
MOE_BLOCK = 512
CMB_ROWS = 512
TIME_PIECES = 2
VMEM_LIMIT = 60 * 1024 * 1024


def _sigmoid(v):
    return 0.5 * jnp.tanh(0.5 * v) + 0.5


def _layer_norm(v, g, b):
    mu = jnp.mean(v, axis=-1, keepdims=True)
    vc = v - mu
    var = jnp.mean(vc * vc, axis=-1, keepdims=True)
    return vc * lax.rsqrt(var + LN_EPS) * g + b


def _dot(a, b):
    return jnp.dot(a, b, preferred_element_type=F32)


def _pack_rows(v):
    n = v.shape[1] // 2
    lo = lax.bitcast_convert_type(v[:, :n].astype(BF16).astype(F32), jnp.int32)
    hi = lax.bitcast_convert_type(v[:, n:].astype(BF16).astype(F32), jnp.int32)
    return hi | lax.shift_right_logical(lo, 16)


def _unpack_rows(w):
    lo = lax.bitcast_convert_type(lax.shift_left(w, 16), F32).astype(BF16)
    hi = lax.bitcast_convert_type(w & jnp.int32(-65536), F32).astype(BF16)
    return lo, hi


SCHED_ROWS = 8
ROW_EXPERT, ROW_VALID, ROW_FIRST, ROW_NEXT, ROW_SLOT, ROW_BLOCK = range(6)


def _n_expert_blocks(n_tok, n_exp):
    return n_tok * TOP_K // MOE_BLOCK + n_exp


def _sched_lanes(n_tok, n_exp):
    return -(-_n_expert_blocks(n_tok, n_exp) // 128) * 128


def _block_schedule(counts, n_lanes, cap_blocks):
    n_exp = counts.shape[0]
    sub = lax.broadcasted_iota(jnp.int32, (n_exp, n_exp), 0)
    lane = lax.broadcasted_iota(jnp.int32, (n_exp, n_exp), 1)

    def as_row(col):
        return jnp.sum(jnp.where(sub == lane, col, 0.0), axis=0, keepdims=True)

    def running(col):
        return jnp.sum(jnp.where(lane <= sub, as_row(col), 0.0), axis=1, keepdims=True)

    blocks = jnp.floor((counts + (MOE_BLOCK - 1)) * (1.0 / MOE_BLOCK))
    ends = running(blocks)
    starts = ends - blocks
    live = counts > 0.0
    live_upto = running(jnp.where(live, 1.0, 0.0))

    step = lax.broadcasted_iota(jnp.int32, (1, n_lanes), 1).astype(F32)
    eid = lax.broadcasted_iota(jnp.int32, (n_exp, n_lanes), 0)
    expert = jnp.minimum(
        jnp.sum(jnp.where(ends <= step, 1, 0), axis=0, keepdims=True), n_exp - 1)
    mine = eid == expert

    def pick(col):
        return jnp.sum(jnp.where(mine, col, 0.0), axis=0, keepdims=True)

    used = step < jnp.max(ends, axis=0, keepdims=True)
    j = step - pick(starts)
    valid = jnp.where(used, jnp.clip(pick(counts) - j * MOE_BLOCK, 0.0, float(MOE_BLOCK)), 0.0)
    first = used & (j == 0.0)
    nxt = jnp.min(jnp.where((eid > expert) & live, eid, n_exp), axis=0, keepdims=True)
    nxt = jnp.where(nxt == n_exp, -1, nxt)
    slot = (pick(live_upto).astype(jnp.int32) + 1) & 1
    block = jnp.where(used, expert * cap_blocks + j.astype(jnp.int32), n_exp * cap_blocks)
    zero = jnp.zeros_like(expert)
    table = [zero] * SCHED_ROWS
    table[ROW_EXPERT], table[ROW_VALID], table[ROW_FIRST] = expert, valid.astype(jnp.int32), first.astype(jnp.int32)
    table[ROW_NEXT], table[ROW_SLOT], table[ROW_BLOCK] = nxt, slot, block
    return jnp.concatenate(table, axis=0)


def _time_major_copies(seq_hbm, tm_buf, sem, chunk, slot, to_hbm=False):
    steps, nb = tm_buf.shape[1], tm_buf.shape[2]
    copies = []
    for b in range(nb):
        hbm = seq_hbm.at[b, pl.ds(chunk * steps, steps), :]
        vmem = tm_buf.at[slot, :, b, :]
        src, dst = (vmem, hbm) if to_hbm else (hbm, vmem)
        copies.append(pltpu.make_async_copy(src, dst, sem.at[slot]))
    return copies


def _mixer_kernel(alpha, nb, n_exp, chunk0, n_chunks,
                  x_hbm, p_hbm, w_in_ref, bm_ref, cm_ref, lam_ref, dskip_ref,
                  w_glu_ref, w_pg_ref, pscale_ref, w_pp_ref, w_out_ref,
                  ln_g_ref, ln_b_ref, wr_ref, brt_ref, w_pleg_ref, w_plep_ref,
                  state_in_ref, hist_in_ref,
                  h_ref, r_ref, dest_ref, wts_ref, sched_ref, state_out_ref, hist_out_ref,
                  proj_ref, bu_ref, st_ref, upool_ref, state_ref, carry_ref, tri_ref, xb_ref,
                  x_buf, p_buf, x_sem, p_sem):
    i = pl.program_id(0)
    last = pl.num_programs(0) - 1
    steps = x_buf.shape[1]
    rows = steps * nb
    d_model = x_buf.shape[3]

    slot = i % 2

    def fetch_copies(step, which):
        return (_time_major_copies(x_hbm, x_buf, x_sem, chunk0 + step, which)
                + _time_major_copies(p_hbm, p_buf, p_sem, chunk0 + step, which))

    @pl.when(i == 0)
    def _first_fetch():
        for cp in fetch_copies(0, 0):
            cp.start()

    @pl.when(i < last)
    def _next_fetch():
        for cp in fetch_copies(i + 1, 1 - slot):
            cp.start()

    for cp in fetch_copies(i, slot):
        cp.wait()
    ssm_w = dskip_ref.shape[1]
    pool_w = pscale_ref.shape[1]
    half_w = ssm_w // 2
    half_s = bu_ref.shape[1] // 2
    plane = half_s // 2
    halo = upool_ref.shape[0] - rows

    @pl.when(i == 0)
    def _init():
        state_ref[...] = state_in_ref[...]
        carry_ref[...] = jnp.zeros_like(carry_ref)
        upool_ref[0:halo, :] = hist_in_ref[...]
        ri = lax.broadcasted_iota(jnp.int32, (rows, rows), 0)
        ci = lax.broadcasted_iota(jnp.int32, (rows, rows), 1)
        tri_ref[...] = (ri < ci).astype(BF16)

    xf = x_buf[slot].reshape(rows, d_model)
    xb_ref[...] = xf.astype(BF16)
    n_mix = ssm_w + pool_w
    proj_ref[:, 0:n_mix] = _dot(xb_ref[...], w_in_ref[:, 0:n_mix])

    us = proj_ref[:, 0:ssm_w]
    usb = us.astype(BF16)
    for hf in range(2):
        bu_ref[:, hf * half_s:(hf + 1) * half_s] = _dot(
            usb[:, hf * half_w:(hf + 1) * half_w], bm_ref[hf])

    gate_cols = w_in_ref.shape[1] - n_mix
    lane_groups = [(hf, q) for hf in range(2) for q in range(plane // SCAN_LANES)]
    segments = len(lane_groups) * SCAN_SEGMENTS
    tile = gate_cols // segments
    seg_steps = steps // SCAN_SEGMENTS
    for gi, (hf, q) in enumerate(lane_groups):
        cre = hf * half_s + q * SCAN_LANES
        cim = cre + plane
        cl = hf * plane + q * SCAN_LANES
        a_re = jnp.broadcast_to(lam_ref[0:1, cl:cl + SCAN_LANES], (nb, SCAN_LANES))
        a_im = jnp.broadcast_to(lam_ref[1:2, cl:cl + SCAN_LANES], (nb, SCAN_LANES))
        s_re = state_ref[:, cre:cre + SCAN_LANES]
        s_im = state_ref[:, cim:cim + SCAN_LANES]
        for seg in range(SCAN_SEGMENTS):
            c0 = n_mix + (gi * SCAN_SEGMENTS + seg) * tile
            proj_ref[:, c0:c0 + tile] = _dot(xb_ref[...], w_in_ref[:, c0:c0 + tile])
            for t in range(seg * seg_steps, (seg + 1) * seg_steps):
                r0 = t * nb
                b_re = bu_ref[r0:r0 + nb, cre:cre + SCAN_LANES]
                b_im = bu_ref[r0:r0 + nb, cim:cim + SCAN_LANES]
                s_re, s_im = (a_re * s_re - a_im * s_im + b_re,
                              a_re * s_im + a_im * s_re + b_im)
                st_ref[r0:r0 + nb, cre:cre + SCAN_LANES] = s_re.astype(BF16)
                st_ref[r0:r0 + nb, cim:cim + SCAN_LANES] = s_im.astype(BF16)
        state_ref[:, cre:cre + SCAN_LANES] = s_re
        state_ref[:, cim:cim + SCAN_LANES] = s_im

    y = jnp.concatenate(
        [_dot(st_ref[:, hf * half_s:(hf + 1) * half_s], cm_ref[hf]) for hf in range(2)],
        axis=1) + dskip_ref[...] * us
    z = 0.5 * y * (1.0 + jnp.tanh(math.sqrt(2.0 / math.pi) * (y + 0.044715 * (y * y * y))))
    vg = _dot(z.astype(BF16), w_glu_ref[...])
    y_ssm = vg[:, 0:d_model] * _sigmoid(vg[:, d_model:2 * d_model])

    up = proj_ref[:, ssm_w:ssm_w + pool_w]
    upool_ref[halo:halo + rows, :] = up
    t_abs = ((chunk0 + i) * steps
             + lax.broadcasted_iota(jnp.int32, (rows, 1), 0) // nb).astype(F32)
    gdim = pool_w // len(POOL_WINDOWS)
    pooled = []
    for g, w in enumerate(POOL_WINDOWS):
        cur = upool_ref[:, g * gdim:(g + 1) * gdim]
        span = 1
        while span < w:
            sh = span * nb
            cur = cur[sh:] + cur[:-sh]
            span *= 2
        win = cur[cur.shape[0] - rows:]
        inv = 1.0 / jnp.minimum(t_abs + 1.0, float(w))
        pooled.append(win * inv - up[:, g * gdim:(g + 1) * gdim])
    upool_ref[0:halo, :] = upool_ref[rows:rows + halo, :]
    pooled = jnp.concatenate(pooled, axis=1)
    mixed = _dot(pooled.astype(BF16), w_pg_ref[...])
    y_pool = _dot((mixed * pscale_ref[...]).astype(BF16), w_pp_ref[...])

    g0 = ssm_w + pool_w
    merged = (_sigmoid(proj_ref[:, g0:g0 + d_model]) * y_ssm
              + _sigmoid(proj_ref[:, g0 + d_model:g0 + 2 * d_model]) * y_pool)
    pre = alpha * xf + _dot(merged.astype(BF16), w_out_ref[...])
    h1 = _layer_norm(pre, ln_g_ref[...], ln_b_ref[...])
    hb = h1.astype(BF16)
    h_ref[...] = _pack_rows(h1)

    pad_e = wr_ref.shape[1] // 2
    h_lo = (h1 - hb.astype(F32)).astype(BF16)
    l_hi = _dot(hb, wr_ref[...])
    lg = l_hi[:, 0:pad_e] + l_hi[:, pad_e:] + _dot(h_lo, wr_ref[:, 0:pad_e])
    lt = jnp.transpose(lg)[0:n_exp, :] + brt_ref[:, 0:1]
    eio = lax.broadcasted_iota(jnp.int32, (n_exp, rows), 0)
    vals, idxs = [], []
    for _ in range(TOP_K):
        m = jnp.max(lt, axis=0, keepdims=True)
        sel = jnp.min(jnp.where(lt == m, eio, n_exp), axis=0, keepdims=True)
        vals.append(m)
        idxs.append(sel)
        lt = jnp.where(eio == sel, -jnp.inf, lt)
    exps = [jnp.exp(v - vals[0]) for v in vals]
    den = exps[0] + exps[1] + exps[2] + exps[3]
    wts_ref[...] = jnp.concatenate([e / den for e in exps], axis=0)

    run = carry_ref[:, 0:1]
    ranks = []
    one_hot = [(eio == idxs[k]).astype(F32) for k in range(TOP_K)]
    before = _dot(jnp.concatenate(one_hot, axis=0).astype(BF16), tri_ref[...])
    for k in range(TOP_K):
        oh = one_hot[k]
        ranks.append(jnp.sum(oh * (run + before[k * n_exp:(k + 1) * n_exp, :]),
                             axis=0, keepdims=True))
        run = run + jnp.sum(oh, axis=1, keepdims=True)
    capacity = rows * n_chunks
    dest_ref[...] = (jnp.concatenate(idxs, axis=0) * capacity
                     + jnp.concatenate(ranks, axis=0).astype(jnp.int32))
    carry_ref[...] = jnp.broadcast_to(run, carry_ref.shape)

    pp = _dot(p_buf[slot].reshape(rows, p_buf.shape[3]).astype(BF16), w_plep_ref[...])
    gate = _sigmoid(_dot(hb, w_pleg_ref[...]))
    r_ref[...] = alpha * h1 + gate * pp

    @pl.when(i == last)
    def _hand_over():
        state_out_ref[...] = state_ref[...]
        hist_out_ref[...] = upool_ref[0:halo, :]
        sched_ref[...] = _block_schedule(carry_ref[:, 0:1], sched_ref.shape[1],
                                         capacity // MOE_BLOCK)


def _const_spec(shape):
    zeros = (0,) * len(shape)
    return pl.BlockSpec(shape, lambda i: zeros, pipeline_mode=pl.Buffered(1))


def _mixer_call(x3, p3, wts, alpha, n_exp, chunk0, n_chunks, state, hist):
    nb, _, d_model = x3.shape
    n_tok = nb * n_chunks * MIX_STEPS
    rows = MIX_STEPS * nb
    grid = n_chunks
    ple_dim = p3.shape[2]
    state_cols = 2 * wts["bm"].shape[2]
    pool_w = wts["pscale"].shape[1]
    halo = max(POOL_WINDOWS) * nb

    def row_spec(width):
        return pl.BlockSpec((rows, width), lambda i: (i, 0))

    names = ["w_in", "bm", "cm", "lam", "dskip", "w_glu", "w_pg", "pscale", "w_pp",
             "w_out", "ln_g", "ln_b", "wr", "brt", "w_pleg", "w_plep"]
    ops = [wts[n] for n in names]
    hbm_spec = pl.BlockSpec(memory_space=pl.ANY)
    in_specs = ([hbm_spec, hbm_spec] + [_const_spec(o.shape) for o in ops]
                + [_const_spec(state.shape), _const_spec(hist.shape)])
    k_spec = pl.BlockSpec((TOP_K, rows), lambda i: (0, i))
    sched_lanes = _sched_lanes(n_tok, n_exp)
    out_shape = (
        jax.ShapeDtypeStruct((n_tok, d_model // 2), jnp.int32),
        jax.ShapeDtypeStruct((n_tok, d_model), F32),
        jax.ShapeDtypeStruct((TOP_K, n_tok), jnp.int32),
        jax.ShapeDtypeStruct((TOP_K, n_tok), F32),
        jax.ShapeDtypeStruct((SCHED_ROWS, sched_lanes), jnp.int32),
        jax.ShapeDtypeStruct(state.shape, F32),
        jax.ShapeDtypeStruct(hist.shape, F32),
    )
    out_specs = (row_spec(d_model // 2), row_spec(d_model), k_spec, k_spec,
                 pl.BlockSpec((SCHED_ROWS, sched_lanes), lambda i: (0, 0)),
                 pl.BlockSpec(state.shape, lambda i: (0, 0)),
                 pl.BlockSpec(hist.shape, lambda i: (0, 0)))
    scratch = [
        pltpu.VMEM((rows, wts["w_in"].shape[1]), F32),
        pltpu.VMEM((rows, state_cols), F32),
        pltpu.VMEM((rows, state_cols), BF16),
        pltpu.VMEM((halo + rows, pool_w), F32),
        pltpu.VMEM((nb, state_cols), F32),
        pltpu.VMEM((n_exp, 128), F32),
        pltpu.VMEM((rows, rows), BF16),
        pltpu.VMEM((rows, d_model), BF16),
        pltpu.VMEM((2, MIX_STEPS, nb, d_model), F32),
        pltpu.VMEM((2, MIX_STEPS, nb, ple_dim), F32),
        pltpu.SemaphoreType.DMA((2,)),
        pltpu.SemaphoreType.DMA((2,)),
    ]
    return pl.pallas_call(
        functools.partial(_mixer_kernel, alpha, nb, n_exp, chunk0, n_chunks),
        grid=(grid,),
        in_specs=in_specs,
        out_specs=out_specs,
        out_shape=out_shape,
        scratch_shapes=scratch,
        compiler_params=pltpu.CompilerParams(
            dimension_semantics=("arbitrary",), vmem_limit_bytes=VMEM_LIMIT),
        name="mixer",
    )(x3, p3, *ops, state, hist)


def _expert_kernel(sched_ref, xs_ref, wg_hbm, bg_ref, wu_hbm, bu_ref, wd_hbm, bd_ref,
                   ys_ref, wg_bf, wu_bf, wd_bf, stage, w_sem):
    i = pl.program_id(0)
    half = xs_ref.shape[1]
    expert = sched_ref[ROW_EXPERT, i]
    n_valid = sched_ref[ROW_VALID, i]
    next_expert = sched_ref[ROW_NEXT, i]

    def weight_copies(which, slot):
        return [pltpu.make_async_copy(w_hbm.at[which], stage.at[slot, m], w_sem.at[slot, m])
                for m, w_hbm in enumerate((wg_hbm, wu_hbm, wd_hbm))]

    @pl.when(sched_ref[ROW_FIRST, i] == 1)
    def _new_expert():
        slot = sched_ref[ROW_SLOT, i]

        @pl.when(i == 0)
        def _nothing_prefetched_yet():
            for cp in weight_copies(expert, slot):
                cp.start()

        @pl.when(next_expert >= 0)
        def _prefetch_next_run():
            for cp in weight_copies(next_expert, 1 - slot):
                cp.start()

        for cp in weight_copies(expert, slot):
            cp.wait()
        wg_bf[...] = stage[slot, 0].astype(BF16)
        wu_bf[...] = stage[slot, 1].astype(BF16)
        wd_bf[...] = stage[slot, 2].astype(BF16)

    @pl.when(n_valid > 0)
    def _compute():
        row = lax.broadcasted_iota(jnp.int32, xs_ref.shape, 0)
        x_lo, x_hi = _unpack_rows(jnp.where(row < n_valid, xs_ref[...], 0))
        g = _dot(x_lo, wg_bf[0:half, :]) + _dot(x_hi, wg_bf[half:, :]) + bg_ref[0]
        u = _dot(x_lo, wu_bf[0:half, :]) + _dot(x_hi, wu_bf[half:, :]) + bu_ref[0]
        g = jnp.minimum(g, SWIGLU_LIMIT)
        u = jnp.clip(u, -SWIGLU_LIMIT, SWIGLU_LIMIT)
        act = (u + 1.0) * (g * _sigmoid(SWIGLU_ALPHA * g))
        ys_ref[...] = _pack_rows(_dot(act.astype(BF16), wd_bf[...]) + bd_ref[0])

    @pl.when(n_valid == 0)
    def _skip():
        ys_ref[...] = jnp.zeros_like(ys_ref)


def _expert_call(schedule, n_steps, xs, wg, bg, wu, bu, wd, bd):
    n_rows = xs.shape[0]
    d_model = wg.shape[1]
    d_exp = wg.shape[2]
    assert d_model == d_exp, "the weight staging buffer assumes square expert matrices"

    def b_spec(n):
        return pl.BlockSpec((1, 1, n), lambda i, sched: (sched[ROW_EXPERT, i], 0, 0))

    row_spec = pl.BlockSpec((MOE_BLOCK, d_model // 2), lambda i, sched: (sched[ROW_BLOCK, i], 0))
    hbm_spec = pl.BlockSpec(memory_space=pl.ANY)
    grid_spec = pltpu.PrefetchScalarGridSpec(
        num_scalar_prefetch=1,
        grid=(n_steps,),
        in_specs=[row_spec, hbm_spec, b_spec(d_exp), hbm_spec, b_spec(d_exp), hbm_spec, b_spec(d_model)],
        out_specs=row_spec,
        scratch_shapes=[pltpu.VMEM((d_model, d_exp), BF16), pltpu.VMEM((d_model, d_exp), BF16),
                        pltpu.VMEM((d_exp, d_model), BF16),
                        pltpu.VMEM((2, 3, d_model, d_exp), F32),
                        pltpu.SemaphoreType.DMA((2, 3))],
    )
    return pl.pallas_call(
        _expert_kernel,
        grid_spec=grid_spec,
        out_shape=jax.ShapeDtypeStruct((n_rows, d_model // 2), jnp.int32),
        compiler_params=pltpu.CompilerParams(
            dimension_semantics=("arbitrary",), vmem_limit_bytes=VMEM_LIMIT),
        name="experts",
    )(schedule, xs, wg, bg, wu, bu, wd, bd)


def _combine_kernel(chunk0, r_ref, *refs):
    yg_refs = refs[:TOP_K]
    wt_ref, ln_g_ref, ln_b_ref = refs[TOP_K:TOP_K + 3]
    o_hbm, o_buf, o_sem = refs[-3:]
    i = pl.program_id(0)
    last = pl.num_programs(0) - 1
    slot = i % 2

    def write_back(step, from_slot):
        return _time_major_copies(o_hbm, o_buf, o_sem, chunk0 + step, from_slot, to_hbm=True)

    def drain(step, from_slot):
        for cp in write_back(step, from_slot):
            cp.wait()

    @pl.when(i >= 2)
    def _reuse_slot():
        drain(i - 2, slot)

    moe = None
    for k in range(TOP_K):
        lo, hi = _unpack_rows(yg_refs[k][...])
        term = wt_ref[:, k:k + 1] * jnp.concatenate([lo, hi], axis=1).astype(F32)
        moe = term if moe is None else moe + term
    out = _layer_norm(r_ref[...] + moe, ln_g_ref[...], ln_b_ref[...])
    o_buf[slot] = out.reshape(o_buf.shape[1:])
    for cp in write_back(i, slot):
        cp.start()

    @pl.when(i == last)
    def _finish():
        @pl.when(i >= 1)
        def _previous():
            drain(i - 1, 1 - slot)
        drain(i, slot)


def _combine_call(r, yg, wt, ln_g, ln_b, nb, seq, chunk0, earlier):
    n_tok, d_model = r.shape
    grid = n_tok // CMB_ROWS
    steps = CMB_ROWS // nb
    extra = [] if earlier is None else [earlier]

    def slot_spec(k):
        return pl.BlockSpec((CMB_ROWS, d_model // 2), lambda i: (k * grid + i, 0))

    return pl.pallas_call(
        functools.partial(_combine_kernel, chunk0),
        grid=(grid,),
        in_specs=[
            pl.BlockSpec((CMB_ROWS, d_model), lambda i: (i, 0)),
            *[slot_spec(k) for k in range(TOP_K)],
            pl.BlockSpec((CMB_ROWS, TOP_K), lambda i: (i, 0)),
            pl.BlockSpec((1, d_model), lambda i: (0, 0)),
            pl.BlockSpec((1, d_model), lambda i: (0, 0)),
        ] + [pl.BlockSpec(memory_space=pl.ANY)] * len(extra),
        out_specs=pl.BlockSpec(memory_space=pl.ANY),
        out_shape=jax.ShapeDtypeStruct((nb, seq, d_model), F32),
        input_output_aliases={TOP_K + 4: 0} if extra else {},
        scratch_shapes=[pltpu.VMEM((2, steps, nb, d_model), F32), pltpu.SemaphoreType.DMA((2,))],
        compiler_params=pltpu.CompilerParams(
            dimension_semantics=("arbitrary",), vmem_limit_bytes=VMEM_LIMIT),
        name="combine",
    )(r, *([yg] * TOP_K), wt, ln_g, ln_b, *extra)


SC_WINDOW = 128


def _sc_mesh():
    return plsc.VectorSubcoreMesh(core_axis_name="c", subcore_axis_name="s")


def _sc_dispatch(rows, dest, n_out):
    n_tok, width = rows.shape
    n_slot = dest.shape[0]
    mesh = _sc_mesh()
    n_workers = mesh.num_cores * mesh.num_subcores
    per_worker = n_tok // n_workers
    assert per_worker * n_workers == n_tok and per_worker % SC_WINDOW == 0

    @functools.partial(
        pl.kernel, out_type=jax.ShapeDtypeStruct((n_out, width), rows.dtype), mesh=mesh,
        scratch_types=[pltpu.VMEM((n_slot, SC_WINDOW), jnp.int32),
                       pltpu.VMEM((SC_WINDOW, width), rows.dtype),
                       pltpu.SemaphoreType.DMA])
    def dispatch(x_hbm, i_hbm, o_hbm, idx_v, rows_v, sem):
        wid = lax.axis_index("s") * mesh.num_cores + lax.axis_index("c")

        @pl.loop(0, per_worker // SC_WINDOW)
        def _(j):
            base = pl.multiple_of(wid * per_worker + j * SC_WINDOW, SC_WINDOW)
            pltpu.sync_copy(i_hbm.at[:, pl.ds(base, SC_WINDOW)], idx_v)
            pltpu.sync_copy(x_hbm.at[pl.ds(base, SC_WINDOW)], rows_v)
            copies = [pltpu.async_copy(rows_v, o_hbm.at[idx_v.at[k]], sem) for k in range(n_slot)]
            for cp in copies:
                cp.wait()

    return dispatch(rows, dest)


def _sc_gather(table, index):
    n_out = index.shape[0]
    width = table.shape[1]
    mesh = _sc_mesh()
    n_workers = mesh.num_cores * mesh.num_subcores
    per_worker = n_out // n_workers
    assert per_worker * n_workers == n_out and per_worker % SC_WINDOW == 0
    half = SC_WINDOW // 2

    @functools.partial(
        pl.kernel, out_type=jax.ShapeDtypeStruct((n_out, width), table.dtype), mesh=mesh,
        scratch_types=[pltpu.VMEM((SC_WINDOW,), jnp.int32),
                       pltpu.VMEM((2, half, width), table.dtype)]
        + [pltpu.SemaphoreType.DMA] * 4)
    def gather(x_hbm, i_hbm, o_hbm, idx_v, rows_v, g0_sem, g1_sem, w0_sem, w1_sem):
        wid = lax.axis_index("s") * mesh.num_cores + lax.axis_index("c")

        @pl.loop(0, per_worker // SC_WINDOW)
        def _(j):
            base = pl.multiple_of(wid * per_worker + j * SC_WINDOW, SC_WINDOW)
            pltpu.sync_copy(i_hbm.at[pl.ds(base, SC_WINDOW)], idx_v)
            g0 = pltpu.async_copy(x_hbm.at[idx_v.at[pl.ds(0, half)]], rows_v.at[0], g0_sem)
            g1 = pltpu.async_copy(x_hbm.at[idx_v.at[pl.ds(half, half)]], rows_v.at[1], g1_sem)
            g0.wait()
            w0 = pltpu.async_copy(rows_v.at[0], o_hbm.at[pl.ds(base, half)], w0_sem)
            g1.wait()
            w1 = pltpu.async_copy(rows_v.at[1], o_hbm.at[pl.ds(base + half, half)], w1_sem)
            w0.wait()
            w1.wait()

    return gather(table, index)


def _ssm_matrices(lam_re, lam_im, log_step, b_re, b_im, c_re, c_im):
    n_grp, n_state, n_ch = b_re.shape
    lam = lax.complex(lam_re.astype(F32), lam_im.astype(F32))
    step = jnp.exp(log_step.astype(F32))[:, None]
    lam_bar = jnp.exp(lam * step)
    b_bar = ((lam_bar - 1.0) / lam)[..., None] * lax.complex(b_re.astype(F32), b_im.astype(F32))
    hg = n_grp // 2
    eye = jnp.eye(hg, dtype=F32)

    def b_half(bpart):
        return jnp.einsum('gph,gk->ghkp', bpart, eye).reshape(hg * n_ch, hg * n_state)

    def c_half(cpart):
        return jnp.einsum('ghp,gk->gpkh', cpart, eye).reshape(hg * n_state, hg * n_ch)

    bm, cm = [], []
    for hf in range(2):
        sl = slice(hf * hg, (hf + 1) * hg)
        bm.append(jnp.concatenate([b_half(jnp.real(b_bar)[sl]), b_half(jnp.imag(b_bar)[sl])], axis=1))
        cm.append(jnp.concatenate([c_half(c_re.astype(F32)[sl]), -c_half(c_im.astype(F32)[sl])], axis=0))
    lam_rows = jnp.stack([jnp.real(lam_bar).reshape(-1), jnp.imag(lam_bar).reshape(-1)], axis=0)
    return jnp.stack(bm).astype(BF16), jnp.stack(cm).astype(BF16), lam_rows


def _split_router(w):
    w = w.astype(F32)
    hi32 = lax.bitcast_convert_type(
        lax.bitcast_convert_type(w, jnp.uint32) & jnp.uint32(0xFFFF0000), F32)
    hi = hi32.astype(BF16)
    lo = (w - hi32).astype(BF16)
    pad = ((0, 0), (0, 128 - w.shape[1]))
    return jnp.concatenate([jnp.pad(hi, pad), jnp.pad(lo, pad)], axis=1)


def _block_diag(w):
    g, c, _ = w.shape
    return jnp.einsum('gcd,gk->gckd', w, jnp.eye(g, dtype=w.dtype)).reshape(g * c, g * c)


def kernel(x, p, w_in, ssm_lambda_re, ssm_lambda_im, ssm_log_step, ssm_b_re, ssm_b_im, ssm_c_re, ssm_c_im, ssm_d, w_glu_val, w_glu_gate, w_pool_group, pool_scale, w_pool_proj, w_out, ln1_g, ln1_b, w_router, b_router, w_gate, b_gate, w_up, b_up, w_down, b_down, w_ple_gate, w_ple_proj, ln2_g, ln2_b):
    bsz, seq, d_model = x.shape
    depth = w_in.shape[0]
    n_exp = w_router.shape[2]
    n_tok = bsz * seq
    alpha = (2.0 * depth) ** 0.25

    h = x
    for l in range(depth):
        bm, cm, lam_rows = _ssm_matrices(ssm_lambda_re[l], ssm_lambda_im[l], ssm_log_step[l],
                                         ssm_b_re[l], ssm_b_im[l], ssm_c_re[l], ssm_c_im[l])
        wts = {
            "w_in": w_in[l].astype(BF16), "bm": bm, "cm": cm, "lam": lam_rows,
            "dskip": ssm_d[l].reshape(1, -1).astype(F32),
            "w_glu": jnp.concatenate([w_glu_val[l], w_glu_gate[l]], axis=1).astype(BF16),
            "w_pg": _block_diag(w_pool_group[l]).astype(BF16),
            "pscale": pool_scale[l].reshape(1, -1).astype(F32),
            "w_pp": w_pool_proj[l].astype(BF16),
            "w_out": w_out[l].astype(BF16),
            "ln_g": ln1_g[l].reshape(1, -1).astype(F32),
            "ln_b": ln1_b[l].reshape(1, -1).astype(F32),
            "wr": _split_router(w_router[l]),
            "brt": jnp.broadcast_to(b_router[l].astype(F32)[:, None], (n_exp, 128)),
            "w_pleg": w_ple_gate[l].astype(BF16),
            "w_plep": w_ple_proj[l].astype(BF16),
        }
        expert_w = (w_gate[l].astype(F32), b_gate[l].astype(F32)[:, None, :],
                    w_up[l].astype(F32), b_up[l].astype(F32)[:, None, :],
                    w_down[l].astype(F32), b_down[l].astype(F32)[:, None, :])
        ln2 = (ln2_g[l].reshape(1, -1).astype(F32), ln2_b[l].reshape(1, -1).astype(F32))

        chunks = seq // MIX_STEPS
        per_piece = chunks // TIME_PIECES
        assert per_piece * TIME_PIECES == chunks and MIX_STEPS * bsz == CMB_ROWS
        state = jnp.zeros((bsz, 2 * bm.shape[2]), F32)
        hist = jnp.zeros((max(POOL_WINDOWS) * bsz, pool_scale.shape[1]), F32)
        out = None
        for piece in range(TIME_PIECES):
            chunk0 = piece * per_piece
            hp, r, dest, wt, schedule, state, hist = _mixer_call(
                h, p[l], wts, alpha, n_exp, chunk0, per_piece, state, hist)
            n_piece = hp.shape[0]
            n_rows = n_exp * n_piece + MOE_BLOCK
            xs = _sc_dispatch(hp, dest, n_rows)
            ys = _expert_call(schedule, _n_expert_blocks(n_piece, n_exp), xs, *expert_w)
            yg = _sc_gather(ys, dest.reshape(-1))
            out = _combine_call(r, yg, jnp.transpose(wt), *ln2, bsz, seq, chunk0, out)
        h = out
    return h
```

```python
import functools
import math

import jax
import jax.numpy as jnp
from jax import lax
from jax.experimental import pallas as pl
from jax.experimental.pallas import tpu as pltpu
from jax.experimental.pallas import tpu_sc as plsc

F32 = jnp.float32
BF16 = jnp.bfloat16

LN_EPS = 1e-5
SWIGLU_LIMIT = 7.0
SWIGLU_ALPHA = 1.702
POOL_WINDOWS = (2, 4, 8, 16)
TOP_K = 4

MIX_STEPS = 32
SCAN_LANES = 512
SCAN_SEGMENTS = 2
MOE_BLOCK = 512
CMB_ROWS = 512
TIME_PIECES = 2
VMEM_LIMIT = 60 * 1024 * 1024


def _sigmoid(v):
    return 0.5 * jnp.tanh(0.5 * v) + 0.5


def _layer_norm(v, g, b):
    mu = jnp.mean(v, axis=-1, keepdims=True)
    vc = v - mu
    var = jnp.mean(vc * vc, axis=-1, keepdims=True)
    return vc * lax.rsqrt(var + LN_EPS) * g + b


def _dot(a, b):
    return jnp.dot(a, b, preferred_element_type=F32)


def _pack_rows(v):
    n = v.shape[1] // 2
    lo = lax.bitcast_convert_type(v[:, :n].astype(BF16).astype(F32), jnp.int32)
    hi = lax.bitcast_convert_type(v[:, n:].astype(BF16).astype(F32), jnp.int32)
    return hi | lax.shift_right_logical(lo, 16)


def _unpack_rows(w):
    lo = lax.bitcast_convert_type(lax.shift_left(w, 16), F32).astype(BF16)
    hi = lax.bitcast_convert_type(w & jnp.int32(-65536), F32).astype(BF16)
    return lo, hi


SCHED_ROWS = 8
ROW_EXPERT, ROW_VALID, ROW_FIRST, ROW_NEXT, ROW_SLOT, ROW_BLOCK = range(6)


def _n_expert_blocks(n_tok, n_exp):
    return n_tok * TOP_K // MOE_BLOCK + n_exp


def _sched_lanes(n_tok, n_exp):
    return -(-_n_expert_blocks(n_tok, n_exp) // 128) * 128


def _block_schedule(counts, n_lanes, cap_blocks):
    n_exp = counts.shape[0]
    sub = lax.broadcasted_iota(jnp.int32, (n_exp, n_exp), 0)
    lane = lax.broadcasted_iota(jnp.int32, (n_exp, n_exp), 1)

    def as_row(col):
        return jnp.sum(jnp.where(sub == lane, col, 0.0), axis=0, keepdims=True)

    def running(col):
        return jnp.sum(jnp.where(lane <= sub, as_row(col), 0.0), axis=1, keepdims=True)

    blocks = jnp.floor((counts + (MOE_BLOCK - 1)) * (1.0 / MOE_BLOCK))
    ends = running(blocks)
    starts = ends - blocks
    live = counts > 0.0
    live_upto = running(jnp.where(live, 1.0, 0.0))

    step = lax.broadcasted_iota(jnp.int32, (1, n_lanes), 1).astype(F32)
    eid = lax.broadcasted_iota(jnp.int32, (n_exp, n_lanes), 0)
    expert = jnp.minimum(
        jnp.sum(jnp.where(ends <= step, 1, 0), axis=0, keepdims=True), n_exp - 1)
    mine = eid == expert

    def pick(col):
        return jnp.sum(jnp.where(mine, col, 0.0), axis=0, keepdims=True)

    used = step < jnp.max(ends, axis=0, keepdims=True)
    j = step - pick(starts)
    valid = jnp.where(used, jnp.clip(pick(counts) - j * MOE_BLOCK, 0.0, float(MOE_BLOCK)), 0.0)
    first = used & (j == 0.0)
    nxt = jnp.min(jnp.where((eid > expert) & live, eid, n_exp), axis=0, keepdims=True)
    nxt = jnp.where(nxt == n_exp, -1, nxt)
    slot = (pick(live_upto).astype(jnp.int32) + 1) & 1
    block = jnp.where(used, expert * cap_blocks + j.astype(jnp.int32), n_exp * cap_blocks)
    zero = jnp.zeros_like(expert)
    table = [zero] * SCHED_ROWS
    table[ROW_EXPERT], table[ROW_VALID], table[ROW_FIRST] = expert, valid.astype(jnp.int32), first.astype(jnp.int32)
    table[ROW_NEXT], table[ROW_SLOT], table[ROW_BLOCK] = nxt, slot, block
    return jnp.concatenate(table, axis=0)


def _time_major_copies(seq_hbm, tm_buf, sem, chunk, slot, to_hbm=False):
    steps, nb = tm_buf.shape[1], tm_buf.shape[2]
    copies = []
    for b in range(nb):
        hbm = seq_hbm.at[b, pl.ds(chunk * steps, steps), :]
        vmem = tm_buf.at[slot, :, b, :]
        src, dst = (vmem, hbm) if to_hbm else (hbm, vmem)
        copies.append(pltpu.make_async_copy(src, dst, sem.at[slot]))
    return copies


def _mixer_kernel(alpha, nb, n_exp, chunk0, n_chunks,
                  x_hbm, w_in_ref, bm_ref, cm_ref, lam_ref, dskip_ref,
                  w_glu_ref, w_pg_ref, pscale_ref, w_pp_ref, w_out_ref,
                  ln_g_ref, ln_b_ref, wr_ref, brt_ref,
                  state_in_ref, hist_in_ref,
                  h_ref, dest_ref, wts_ref, sched_ref, state_out_ref, hist_out_ref,
                  proj_ref, bu_ref, st_ref, upool_ref, state_ref, carry_ref, tri_ref, xb_ref,
                  x_buf, x_sem):
    i = pl.program_id(0)
    last = pl.num_programs(0) - 1
    steps = x_buf.shape[1]
    rows = steps * nb
    d_model = x_buf.shape[3]

    slot = i % 2

    def fetch_copies(step, which):
        return _time_major_copies(x_hbm, x_buf, x_sem, chunk0 + step, which)

    @pl.when(i == 0)
    def _first_fetch():
        for cp in fetch_copies(0, 0):
            cp.start()

    @pl.when(i < last)
    def _next_fetch():
        for cp in fetch_copies(i + 1, 1 - slot):
            cp.start()

    for cp in fetch_copies(i, slot):
        cp.wait()
    ssm_w = dskip_ref.shape[1]
    pool_w = pscale_ref.shape[1]
    half_w = ssm_w // 2
    half_s = bu_ref.shape[1] // 2
    plane = half_s // 2
    halo = upool_ref.shape[0] - rows

    @pl.when(i == 0)
    def _init():
        state_ref[...] = state_in_ref[...]
        carry_ref[...] = jnp.zeros_like(carry_ref)
        upool_ref[0:halo, :] = hist_in_ref[...]
        ri = lax.broadcasted_iota(jnp.int32, (rows, rows), 0)
        ci = lax.broadcasted_iota(jnp.int32, (rows, rows), 1)
        tri_ref[...] = (ri < ci).astype(BF16)

    xf = x_buf[slot].reshape(rows, d_model)
    xb_ref[...] = xf.astype(BF16)
    n_mix = ssm_w + pool_w
    proj_ref[:, 0:n_mix] = _dot(xb_ref[...], w_in_ref[:, 0:n_mix])

    us = proj_ref[:, 0:ssm_w]
    usb = us.astype(BF16)
    for hf in range(2):
        bu_ref[:, hf * half_s:(hf + 1) * half_s] = _dot(
            usb[:, hf * half_w:(hf + 1) * half_w], bm_ref[hf])

    gate_cols = w_in_ref.shape[1] - n_mix
    lane_groups = [(hf, q) for hf in range(2) for q in range(plane // SCAN_LANES)]
    segments = len(lane_groups) * SCAN_SEGMENTS
    tile = gate_cols // segments
    seg_steps = steps // SCAN_SEGMENTS
    for gi, (hf, q) in enumerate(lane_groups):
        cre = hf * half_s + q * SCAN_LANES
        cim = cre + plane
        cl = hf * plane + q * SCAN_LANES
        a_re = jnp.broadcast_to(lam_ref[0:1, cl:cl + SCAN_LANES], (nb, SCAN_LANES))
        a_im = jnp.broadcast_to(lam_ref[1:2, cl:cl + SCAN_LANES], (nb, SCAN_LANES))
        s_re = state_ref[:, cre:cre + SCAN_LANES]
        s_im = state_ref[:, cim:cim + SCAN_LANES]
        for seg in range(SCAN_SEGMENTS):
            c0 = n_mix + (gi * SCAN_SEGMENTS + seg) * tile
            proj_ref[:, c0:c0 + tile] = _dot(xb_ref[...], w_in_ref[:, c0:c0 + tile])
            for t in range(seg * seg_steps, (seg + 1) * seg_steps):
                r0 = t * nb
                b_re = bu_ref[r0:r0 + nb, cre:cre + SCAN_LANES]
                b_im = bu_ref[r0:r0 + nb, cim:cim + SCAN_LANES]
                s_re, s_im = (a_re * s_re - a_im * s_im + b_re,
                              a_re * s_im + a_im * s_re + b_im)
                st_ref[r0:r0 + nb, cre:cre + SCAN_LANES] = s_re.astype(BF16)
                st_ref[r0:r0 + nb, cim:cim + SCAN_LANES] = s_im.astype(BF16)
        state_ref[:, cre:cre + SCAN_LANES] = s_re
        state_ref[:, cim:cim + SCAN_LANES] = s_im

    y = jnp.concatenate(
        [_dot(st_ref[:, hf * half_s:(hf + 1) * half_s], cm_ref[hf]) for hf in range(2)],
        axis=1) + dskip_ref[...] * us
    z = 0.5 * y * (1.0 + jnp.tanh(math.sqrt(2.0 / math.pi) * (y + 0.044715 * (y * y * y))))
    vg = _dot(z.astype(BF16), w_glu_ref[...])
    y_ssm = vg[:, 0:d_model] * _sigmoid(vg[:, d_model:2 * d_model])

    up = proj_ref[:, ssm_w:ssm_w + pool_w]
    upool_ref[halo:halo + rows, :] = up
    t_abs = ((chunk0 + i) * steps
             + lax.broadcasted_iota(jnp.int32, (rows, 1), 0) // nb).astype(F32)
    gdim = pool_w // len(POOL_WINDOWS)
    pooled = []
    for g, w in enumerate(POOL_WINDOWS):
        cur = upool_ref[:, g * gdim:(g + 1) * gdim]
        span = 1
        while span < w:
            sh = span * nb
            cur = cur[sh:] + cur[:-sh]
            span *= 2
        win = cur[cur.shape[0] - rows:]
        inv = 1.0 / jnp.minimum(t_abs + 1.0, float(w))
        pooled.append(win * inv - up[:, g * gdim:(g + 1) * gdim])
    upool_ref[0:halo, :] = upool_ref[rows:rows + halo, :]
    pooled = jnp.concatenate(pooled, axis=1)
    mixed = _dot(pooled.astype(BF16), w_pg_ref[...])
    y_pool = _dot((mixed * pscale_ref[...]).astype(BF16), w_pp_ref[...])

    g0 = ssm_w + pool_w
    merged = (_sigmoid(proj_ref[:, g0:g0 + d_model]) * y_ssm
              + _sigmoid(proj_ref[:, g0 + d_model:g0 + 2 * d_model]) * y_pool)
    pre = alpha * xf + _dot(merged.astype(BF16), w_out_ref[...])
    h1 = _layer_norm(pre, ln_g_ref[...], ln_b_ref[...])
    hb = h1.astype(BF16)
    h_ref[...] = _pack_rows(h1)

    pad_e = wr_ref.shape[1] // 2
    h_lo = (h1 - hb.astype(F32)).astype(BF16)
    l_hi = _dot(hb, wr_ref[...])
    lg = l_hi[:, 0:pad_e] + l_hi[:, pad_e:] + _dot(h_lo, wr_ref[:, 0:pad_e])
    lt = jnp.transpose(lg)[0:n_exp, :] + brt_ref[:, 0:1]
    eio = lax.broadcasted_iota(jnp.int32, (n_exp, rows), 0)
    vals, idxs = [], []
    for _ in range(TOP_K):
        m = jnp.max(lt, axis=0, keepdims=True)
        sel = jnp.min(jnp.where(lt == m, eio, n_exp), axis=0, keepdims=True)
        vals.append(m)
        idxs.append(sel)
        lt = jnp.where(eio == sel, -jnp.inf, lt)
    exps = [jnp.exp(v - vals[0]) for v in vals]
    den = exps[0] + exps[1] + exps[2] + exps[3]
    wts_ref[...] = jnp.concatenate([e / den for e in exps], axis=0)

    run = carry_ref[:, 0:1]
    ranks = []
    one_hot = [(eio == idxs[k]).astype(F32) for k in range(TOP_K)]
    before = _dot(jnp.concatenate(one_hot, axis=0).astype(BF16), tri_ref[...])
    for k in range(TOP_K):
        oh = one_hot[k]
        ranks.append(jnp.sum(oh * (run + before[k * n_exp:(k + 1) * n_exp, :]),
                             axis=0, keepdims=True))
        run = run + jnp.sum(oh, axis=1, keepdims=True)
    capacity = rows * n_chunks
    dest_ref[...] = (jnp.concatenate(idxs, axis=0) * capacity
                     + jnp.concatenate(ranks, axis=0).astype(jnp.int32))
    carry_ref[...] = jnp.broadcast_to(run, carry_ref.shape)

    @pl.when(i == last)
    def _hand_over():
        state_out_ref[...] = state_ref[...]
        hist_out_ref[...] = upool_ref[0:halo, :]
        sched_ref[...] = _block_schedule(carry_ref[:, 0:1], sched_ref.shape[1],
                                         capacity // MOE_BLOCK)


def _const_spec(shape):
    zeros = (0,) * len(shape)
    return pl.BlockSpec(shape, lambda i: zeros, pipeline_mode=pl.Buffered(1))


def _mixer_call(x3, wts, alpha, n_exp, chunk0, n_chunks, state, hist):
    nb, _, d_model = x3.shape
    n_tok = nb * n_chunks * MIX_STEPS
    rows = MIX_STEPS * nb
    grid = n_chunks
    state_cols = 2 * wts["bm"].shape[2]
    pool_w = wts["pscale"].shape[1]
    halo = max(POOL_WINDOWS) * nb

    def row_spec(width):
        return pl.BlockSpec((rows, width), lambda i: (i, 0))

    names = ["w_in", "bm", "cm", "lam", "dskip", "w_glu", "w_pg", "pscale", "w_pp",
             "w_out", "ln_g", "ln_b", "wr", "brt"]
    ops = [wts[n] for n in names]
    hbm_spec = pl.BlockSpec(memory_space=pl.ANY)
    in_specs = ([hbm_spec] + [_const_spec(o.shape) for o in ops]
                + [_const_spec(state.shape), _const_spec(hist.shape)])
    k_spec = pl.BlockSpec((TOP_K, rows), lambda i: (0, i))
    sched_lanes = _sched_lanes(n_tok, n_exp)
    out_shape = (
        jax.ShapeDtypeStruct((n_tok, d_model // 2), jnp.int32),
        jax.ShapeDtypeStruct((TOP_K, n_tok), jnp.int32),
        jax.ShapeDtypeStruct((TOP_K, n_tok), F32),
        jax.ShapeDtypeStruct((SCHED_ROWS, sched_lanes), jnp.int32),
        jax.ShapeDtypeStruct(state.shape, F32),
        jax.ShapeDtypeStruct(hist.shape, F32),
    )
    out_specs = (row_spec(d_model // 2), k_spec, k_spec,
                 pl.BlockSpec((SCHED_ROWS, sched_lanes), lambda i: (0, 0)),
                 pl.BlockSpec(state.shape, lambda i: (0, 0)),
                 pl.BlockSpec(hist.shape, lambda i: (0, 0)))
    scratch = [
        pltpu.VMEM((rows, wts["w_in"].shape[1]), F32),
        pltpu.VMEM((rows, state_cols), F32),
        pltpu.VMEM((rows, state_cols), BF16),
        pltpu.VMEM((halo + rows, pool_w), F32),
        pltpu.VMEM((nb, state_cols), F32),
        pltpu.VMEM((n_exp, 128), F32),
        pltpu.VMEM((rows, rows), BF16),
        pltpu.VMEM((rows, d_model), BF16),
        pltpu.VMEM((2, MIX_STEPS, nb, d_model), F32),
        pltpu.SemaphoreType.DMA((2,)),
    ]
    return pl.pallas_call(
        functools.partial(_mixer_kernel, alpha, nb, n_exp, chunk0, n_chunks),
        grid=(grid,),
        in_specs=in_specs,
        out_specs=out_specs,
        out_shape=out_shape,
        scratch_shapes=scratch,
        compiler_params=pltpu.CompilerParams(
            dimension_semantics=("arbitrary",), vmem_limit_bytes=VMEM_LIMIT),
        name="mixer",
    )(x3, *ops, state, hist)


def _expert_kernel(sched_ref, xs_ref, wg_hbm, bg_ref, wu_hbm, bu_ref, wd_hbm, bd_ref,
                   ys_ref, wg_bf, wu_bf, wd_bf, stage, w_sem):
    i = pl.program_id(0)
    half = xs_ref.shape[1]
    expert = sched_ref[ROW_EXPERT, i]
    n_valid = sched_ref[ROW_VALID, i]
    next_expert = sched_ref[ROW_NEXT, i]

    def weight_copies(which, slot):
        return [pltpu.make_async_copy(w_hbm.at[which], stage.at[slot, m], w_sem.at[slot, m])
                for m, w_hbm in enumerate((wg_hbm, wu_hbm, wd_hbm))]

    @pl.when(sched_ref[ROW_FIRST, i] == 1)
    def _new_expert():
        slot = sched_ref[ROW_SLOT, i]

        @pl.when(i == 0)
        def _nothing_prefetched_yet():
            for cp in weight_copies(expert, slot):
                cp.start()

        @pl.when(next_expert >= 0)
        def _prefetch_next_run():
            for cp in weight_copies(next_expert, 1 - slot):
                cp.start()

        for cp in weight_copies(expert, slot):
            cp.wait()
        wg_bf[...] = stage[slot, 0].astype(BF16)
        wu_bf[...] = stage[slot, 1].astype(BF16)
        wd_bf[...] = stage[slot, 2].astype(BF16)

    @pl.when(n_valid > 0)
    def _compute():
        row = lax.broadcasted_iota(jnp.int32, xs_ref.shape, 0)
        x_lo, x_hi = _unpack_rows(jnp.where(row < n_valid, xs_ref[...], 0))
        g = _dot(x_lo, wg_bf[0:half, :]) + _dot(x_hi, wg_bf[half:, :]) + bg_ref[0]
        u = _dot(x_lo, wu_bf[0:half, :]) + _dot(x_hi, wu_bf[half:, :]) + bu_ref[0]
        g = jnp.minimum(g, SWIGLU_LIMIT)
        u = jnp.clip(u, -SWIGLU_LIMIT, SWIGLU_LIMIT)
        act = (u + 1.0) * (g * _sigmoid(SWIGLU_ALPHA * g))
        ys_ref[...] = _pack_rows(_dot(act.astype(BF16), wd_bf[...]) + bd_ref[0])

    @pl.when(n_valid == 0)
    def _skip():
        ys_ref[...] = jnp.zeros_like(ys_ref)


def _expert_call(schedule, n_steps, xs, wg, bg, wu, bu, wd, bd):
    n_rows = xs.shape[0]
    d_model = wg.shape[1]
    d_exp = wg.shape[2]
    assert d_model == d_exp, "the weight staging buffer assumes square expert matrices"

    def b_spec(n):
        return pl.BlockSpec((1, 1, n), lambda i, sched: (sched[ROW_EXPERT, i], 0, 0))

    row_spec = pl.BlockSpec((MOE_BLOCK, d_model // 2), lambda i, sched: (sched[ROW_BLOCK, i], 0))
    hbm_spec = pl.BlockSpec(memory_space=pl.ANY)
    grid_spec = pltpu.PrefetchScalarGridSpec(
        num_scalar_prefetch=1,
        grid=(n_steps,),
        in_specs=[row_spec, hbm_spec, b_spec(d_exp), hbm_spec, b_spec(d_exp), hbm_spec, b_spec(d_model)],
        out_specs=row_spec,
        scratch_shapes=[pltpu.VMEM((d_model, d_exp), BF16), pltpu.VMEM((d_model, d_exp), BF16),
                        pltpu.VMEM((d_exp, d_model), BF16),
                        pltpu.VMEM((2, 3, d_model, d_exp), F32),
                        pltpu.SemaphoreType.DMA((2, 3))],
    )
    return pl.pallas_call(
        _expert_kernel,
        grid_spec=grid_spec,
        out_shape=jax.ShapeDtypeStruct((n_rows, d_model // 2), jnp.int32),
        compiler_params=pltpu.CompilerParams(
            dimension_semantics=("arbitrary",), vmem_limit_bytes=VMEM_LIMIT),
        name="experts",
    )(schedule, xs, wg, bg, wu, bu, wd, bd)


def _combine_kernel(alpha, chunk0, h_ref, *refs):
    yg_refs = refs[:TOP_K]
    wt_ref, w_pleg_ref, w_plep_ref, ln_g_ref, ln_b_ref, p_hbm = refs[TOP_K:TOP_K + 6]
    o_hbm, o_buf, o_sem, p_buf, p_sem = refs[-5:]
    i = pl.program_id(0)
    last = pl.num_programs(0) - 1
    slot = i % 2
    rows = h_ref.shape[0]

    def fetch_p(step, which):
        return _time_major_copies(p_hbm, p_buf, p_sem, chunk0 + step, which)

    def write_back(step, from_slot):
        return _time_major_copies(o_hbm, o_buf, o_sem, chunk0 + step, from_slot, to_hbm=True)

    def drain(step, from_slot):
        for cp in write_back(step, from_slot):
            cp.wait()

    @pl.when(i == 0)
    def _first_fetch():
        for cp in fetch_p(0, 0):
            cp.start()

    @pl.when(i < last)
    def _next_fetch():
        for cp in fetch_p(i + 1, 1 - slot):
            cp.start()

    @pl.when(i >= 2)
    def _reuse_slot():
        drain(i - 2, slot)

    h_lo, h_hi = _unpack_rows(h_ref[...])
    hb = jnp.concatenate([h_lo, h_hi], axis=1)
    for cp in fetch_p(i, slot):
        cp.wait()
    pp = _dot(p_buf[slot].reshape(rows, p_buf.shape[3]).astype(BF16), w_plep_ref[...])
    gate = _sigmoid(_dot(hb, w_pleg_ref[...]))
    acc = alpha * hb.astype(F32) + gate * pp
    for k in range(TOP_K):
        lo, hi = _unpack_rows(yg_refs[k][...])
        acc = acc + wt_ref[:, k:k + 1] * jnp.concatenate([lo, hi], axis=1).astype(F32)
    out = _layer_norm(acc, ln_g_ref[...], ln_b_ref[...])
    o_buf[slot] = out.reshape(o_buf.shape[1:])
    for cp in write_back(i, slot):
        cp.start()

    @pl.when(i == last)
    def _finish():
        @pl.when(i >= 1)
        def _previous():
            drain(i - 1, 1 - slot)
        drain(i, slot)


def _combine_call(hp, yg, wt, w_pleg, w_plep, ln_g, ln_b, p3, alpha, chunk0, earlier):
    n_tok = hp.shape[0]
    d_model = 2 * hp.shape[1]
    nb, seq, ple_dim = p3.shape
    grid = n_tok // CMB_ROWS
    steps = CMB_ROWS // nb
    extra = [] if earlier is None else [earlier]
    operands = [hp, *([yg] * TOP_K), wt, w_pleg, w_plep, ln_g, ln_b, p3, *extra]

    def slot_spec(k):
        return pl.BlockSpec((CMB_ROWS, d_model // 2), lambda i: (k * grid + i, 0))

    hbm_spec = pl.BlockSpec(memory_space=pl.ANY)
    return pl.pallas_call(
        functools.partial(_combine_kernel, alpha, chunk0),
        grid=(grid,),
        in_specs=[
            pl.BlockSpec((CMB_ROWS, d_model // 2), lambda i: (i, 0)),
            *[slot_spec(k) for k in range(TOP_K)],
            pl.BlockSpec((CMB_ROWS, TOP_K), lambda i: (i, 0)),
            _const_spec(w_pleg.shape), _const_spec(w_plep.shape),
            pl.BlockSpec((1, d_model), lambda i: (0, 0)),
            pl.BlockSpec((1, d_model), lambda i: (0, 0)),
            hbm_spec,
        ] + [hbm_spec] * len(extra),
        out_specs=hbm_spec,
        out_shape=jax.ShapeDtypeStruct((nb, seq, d_model), F32),
        input_output_aliases={len(operands) - 1: 0} if extra else {},
        scratch_shapes=[pltpu.VMEM((2, steps, nb, d_model), F32), pltpu.SemaphoreType.DMA((2,)),
                        pltpu.VMEM((2, steps, nb, ple_dim), F32), pltpu.SemaphoreType.DMA((2,))],
        compiler_params=pltpu.CompilerParams(
            dimension_semantics=("arbitrary",), vmem_limit_bytes=VMEM_LIMIT),
        name="combine",
    )(*operands)


SC_WINDOW = 128


def _sc_mesh():
    return plsc.VectorSubcoreMesh(core_axis_name="c", subcore_axis_name="s")


def _sc_dispatch(rows, dest, n_out):
    n_tok, width = rows.shape
    n_slot = dest.shape[0]
    mesh = _sc_mesh()
    n_workers = mesh.num_cores * mesh.num_subcores
    per_worker = n_tok // n_workers
    assert per_worker * n_workers == n_tok and per_worker % SC_WINDOW == 0

    @functools.partial(
        pl.kernel, out_type=jax.ShapeDtypeStruct((n_out, width), rows.dtype), mesh=mesh,
        scratch_types=[pltpu.VMEM((n_slot, SC_WINDOW), jnp.int32),
                       pltpu.VMEM((SC_WINDOW, width), rows.dtype),
                       pltpu.SemaphoreType.DMA])
    def dispatch(x_hbm, i_hbm, o_hbm, idx_v, rows_v, sem):
        wid = lax.axis_index("s") * mesh.num_cores + lax.axis_index("c")

        @pl.loop(0, per_worker // SC_WINDOW)
        def _(j):
            base = pl.multiple_of(wid * per_worker + j * SC_WINDOW, SC_WINDOW)
            pltpu.sync_copy(i_hbm.at[:, pl.ds(base, SC_WINDOW)], idx_v)
            pltpu.sync_copy(x_hbm.at[pl.ds(base, SC_WINDOW)], rows_v)
            copies = [pltpu.async_copy(rows_v, o_hbm.at[idx_v.at[k]], sem) for k in range(n_slot)]
            for cp in copies:
                cp.wait()

    return dispatch(rows, dest)


def _sc_gather(table, index):
    n_out = index.shape[0]
    width = table.shape[1]
    mesh = _sc_mesh()
    n_workers = mesh.num_cores * mesh.num_subcores
    per_worker = n_out // n_workers
    assert per_worker * n_workers == n_out and per_worker % SC_WINDOW == 0
    half = SC_WINDOW // 2

    @functools.partial(
        pl.kernel, out_type=jax.ShapeDtypeStruct((n_out, width), table.dtype), mesh=mesh,
        scratch_types=[pltpu.VMEM((SC_WINDOW,), jnp.int32),
                       pltpu.VMEM((2, half, width), table.dtype)]
        + [pltpu.SemaphoreType.DMA] * 4)
    def gather(x_hbm, i_hbm, o_hbm, idx_v, rows_v, g0_sem, g1_sem, w0_sem, w1_sem):
        wid = lax.axis_index("s") * mesh.num_cores + lax.axis_index("c")

        @pl.loop(0, per_worker // SC_WINDOW)
        def _(j):
            base = pl.multiple_of(wid * per_worker + j * SC_WINDOW, SC_WINDOW)
            pltpu.sync_copy(i_hbm.at[pl.ds(base, SC_WINDOW)], idx_v)
            g0 = pltpu.async_copy(x_hbm.at[idx_v.at[pl.ds(0, half)]], rows_v.at[0], g0_sem)
            g1 = pltpu.async_copy(x_hbm.at[idx_v.at[pl.ds(half, half)]], rows_v.at[1], g1_sem)
            g0.wait()
            w0 = pltpu.async_copy(rows_v.at[0], o_hbm.at[pl.ds(base, half)], w0_sem)
            g1.wait()
            w1 = pltpu.async_copy(rows_v.at[1], o_hbm.at[pl.ds(base + half, half)], w1_sem)
            w0.wait()
            w1.wait()

    return gather(table, index)


def _ssm_matrices(lam_re, lam_im, log_step, b_re, b_im, c_re, c_im):
    n_grp, n_state, n_ch = b_re.shape
    lam = lax.complex(lam_re.astype(F32), lam_im.astype(F32))
    step = jnp.exp(log_step.astype(F32))[:, None]
    lam_bar = jnp.exp(lam * step)
    b_bar = ((lam_bar - 1.0) / lam)[..., None] * lax.complex(b_re.astype(F32), b_im.astype(F32))
    hg = n_grp // 2
    eye = jnp.eye(hg, dtype=F32)

    def b_half(bpart):
        return jnp.einsum('gph,gk->ghkp', bpart, eye).reshape(hg * n_ch, hg * n_state)

    def c_half(cpart):
        return jnp.einsum('ghp,gk->gpkh', cpart, eye).reshape(hg * n_state, hg * n_ch)

    bm, cm = [], []
    for hf in range(2):
        sl = slice(hf * hg, (hf + 1) * hg)
        bm.append(jnp.concatenate([b_half(jnp.real(b_bar)[sl]), b_half(jnp.imag(b_bar)[sl])], axis=1))
        cm.append(jnp.concatenate([c_half(c_re.astype(F32)[sl]), -c_half(c_im.astype(F32)[sl])], axis=0))
    lam_rows = jnp.stack([jnp.real(lam_bar).reshape(-1), jnp.imag(lam_bar).reshape(-1)], axis=0)
    return jnp.stack(bm).astype(BF16), jnp.stack(cm).astype(BF16), lam_rows


def _split_router(w):
    w = w.astype(F32)
    hi32 = lax.bitcast_convert_type(
        lax.bitcast_convert_type(w, jnp.uint32) & jnp.uint32(0xFFFF0000), F32)
    hi = hi32.astype(BF16)
    lo = (w - hi32).astype(BF16)
    pad = ((0, 0), (0, 128 - w.shape[1]))
    return jnp.concatenate([jnp.pad(hi, pad), jnp.pad(lo, pad)], axis=1)


def _block_diag(w):
    g, c, _ = w.shape
    return jnp.einsum('gcd,gk->gckd', w, jnp.eye(g, dtype=w.dtype)).reshape(g * c, g * c)


def kernel(x, p, w_in, ssm_lambda_re, ssm_lambda_im, ssm_log_step, ssm_b_re, ssm_b_im, ssm_c_re, ssm_c_im, ssm_d, w_glu_val, w_glu_gate, w_pool_group, pool_scale, w_pool_proj, w_out, ln1_g, ln1_b, w_router, b_router, w_gate, b_gate, w_up, b_up, w_down, b_down, w_ple_gate, w_ple_proj, ln2_g, ln2_b):
    bsz, seq, d_model = x.shape
    depth = w_in.shape[0]
    n_exp = w_router.shape[2]
    n_tok = bsz * seq
    alpha = (2.0 * depth) ** 0.25

    h = x
    for l in range(depth):
        bm, cm, lam_rows = _ssm_matrices(ssm_lambda_re[l], ssm_lambda_im[l], ssm_log_step[l],
                                         ssm_b_re[l], ssm_b_im[l], ssm_c_re[l], ssm_c_im[l])
        wts = {
            "w_in": w_in[l].astype(BF16), "bm": bm, "cm": cm, "lam": lam_rows,
            "dskip": ssm_d[l].reshape(1, -1).astype(F32),
            "w_glu": jnp.concatenate([w_glu_val[l], w_glu_gate[l]], axis=1).astype(BF16),
            "w_pg": _block_diag(w_pool_group[l]).astype(BF16),
            "pscale": pool_scale[l].reshape(1, -1).astype(F32),
            "w_pp": w_pool_proj[l].astype(BF16),
            "w_out": w_out[l].astype(BF16),
            "ln_g": ln1_g[l].reshape(1, -1).astype(F32),
            "ln_b": ln1_b[l].reshape(1, -1).astype(F32),
            "wr": _split_router(w_router[l]),
            "brt": jnp.broadcast_to(b_router[l].astype(F32)[:, None], (n_exp, 128)),
        }
        ple_w = (w_ple_gate[l].astype(BF16), w_ple_proj[l].astype(BF16))
        expert_w = (w_gate[l].astype(F32), b_gate[l].astype(F32)[:, None, :],
                    w_up[l].astype(F32), b_up[l].astype(F32)[:, None, :],
                    w_down[l].astype(F32), b_down[l].astype(F32)[:, None, :])
        ln2 = (ln2_g[l].reshape(1, -1).astype(F32), ln2_b[l].reshape(1, -1).astype(F32))

        chunks = seq // MIX_STEPS
        per_piece = chunks // TIME_PIECES
        assert per_piece * TIME_PIECES == chunks and MIX_STEPS * bsz == CMB_ROWS
        state = jnp.zeros((bsz, 2 * bm.shape[2]), F32)
        hist = jnp.zeros((max(POOL_WINDOWS) * bsz, pool_scale.shape[1]), F32)
        out = None
        for piece in range(TIME_PIECES):
            chunk0 = piece * per_piece
            hp, dest, wt, schedule, state, hist = _mixer_call(
                h, wts, alpha, n_exp, chunk0, per_piece, state, hist)
            n_piece = hp.shape[0]
            n_rows = n_exp * n_piece + MOE_BLOCK
            xs = _sc_dispatch(hp, dest, n_rows)
            ys = _expert_call(schedule, _n_expert_blocks(n_piece, n_exp), xs, *expert_w)
            yg = _sc_gather(ys, dest.reshape(-1))
            out = _combine_call(hp, yg, jnp.transpose(wt), *ple_w, *ln2, p[l], alpha, chunk0, out)
        h = out
    return h
```

```python
import functools
import math

import jax
import jax.numpy as jnp
from jax import lax
from jax.experimental import pallas as pl
from jax.experimental.pallas import tpu as pltpu
from jax.experimental.pallas import tpu_sc as plsc

F32 = jnp.float32
BF16 = jnp.bfloat16

LN_EPS = 1e-5
SWIGLU_LIMIT = 7.0
SWIGLU_ALPHA = 1.702
POOL_WINDOWS = (2, 4, 8, 16)
TOP_K = 4

MIX_STEPS = 32
SCAN_LANES = 512
SCAN_SEGMENTS = 2
MOE_BLOCK = 512
CMB_ROWS = 512
TIME_PIECES = 2
VMEM_LIMIT = 60 * 1024 * 1024


def _sigmoid(v):
    return 0.5 * jnp.tanh(0.5 * v) + 0.5


def _layer_norm(v, g, b):
    mu = jnp.mean(v, axis=-1, keepdims=True)
    vc = v - mu
    var = jnp.mean(vc * vc, axis=-1, keepdims=True)
    return vc * lax.rsqrt(var + LN_EPS) * g + b


def _dot(a, b):
    return jnp.dot(a, b, preferred_element_type=F32)


def _pack_rows(v):
    n = v.shape[1] // 2
    lo = lax.bitcast_convert_type(v[:, :n].astype(BF16).astype(F32), jnp.int32)
    hi = lax.bitcast_convert_type(v[:, n:].astype(BF16).astype(F32), jnp.int32)
    return hi | lax.shift_right_logical(lo, 16)


def _unpack_rows_f32(w):
    lo = lax.bitcast_convert_type(lax.shift_left(w, 16), F32)
    hi = lax.bitcast_convert_type(w & jnp.int32(-65536), F32)
    return lo, hi


def _unpack_rows(w):
    lo, hi = _unpack_rows_f32(w)
    return lo.astype(BF16), hi.astype(BF16)


SCHED_ROWS = 8
ROW_EXPERT, ROW_VALID, ROW_FIRST, ROW_NEXT, ROW_SLOT, ROW_BLOCK = range(6)


def _n_expert_blocks(n_tok, n_exp):
    return n_tok * TOP_K // MOE_BLOCK + n_exp


def _sched_lanes(n_tok, n_exp):
    return -(-_n_expert_blocks(n_tok, n_exp) // 128) * 128


def _block_schedule(counts, n_lanes, cap_blocks):
    n_exp = counts.shape[0]
    sub = lax.broadcasted_iota(jnp.int32, (n_exp, n_exp), 0)
    lane = lax.broadcasted_iota(jnp.int32, (n_exp, n_exp), 1)

    def as_row(col):
        return jnp.sum(jnp.where(sub == lane, col, 0.0), axis=0, keepdims=True)

    def running(col):
        return jnp.sum(jnp.where(lane <= sub, as_row(col), 0.0), axis=1, keepdims=True)

    blocks = jnp.floor((counts + (MOE_BLOCK - 1)) * (1.0 / MOE_BLOCK))
    ends = running(blocks)
    starts = ends - blocks
    live = counts > 0.0
    live_upto = running(jnp.where(live, 1.0, 0.0))

    step = lax.broadcasted_iota(jnp.int32, (1, n_lanes), 1).astype(F32)
    eid = lax.broadcasted_iota(jnp.int32, (n_exp, n_lanes), 0)
    expert = jnp.minimum(
        jnp.sum(jnp.where(ends <= step, 1, 0), axis=0, keepdims=True), n_exp - 1)
    mine = eid == expert

    def pick(col):
        return jnp.sum(jnp.where(mine, col, 0.0), axis=0, keepdims=True)

    used = step < jnp.max(ends, axis=0, keepdims=True)
    j = step - pick(starts)
    valid = jnp.where(used, jnp.clip(pick(counts) - j * MOE_BLOCK, 0.0, float(MOE_BLOCK)), 0.0)
    first = used & (j == 0.0)
    nxt = jnp.min(jnp.where((eid > expert) & live, eid, n_exp), axis=0, keepdims=True)
    nxt = jnp.where(nxt == n_exp, -1, nxt)
    slot = (pick(live_upto).astype(jnp.int32) + 1) & 1
    block = jnp.where(used, expert * cap_blocks + j.astype(jnp.int32), n_exp * cap_blocks)
    zero = jnp.zeros_like(expert)
    table = [zero] * SCHED_ROWS
    table[ROW_EXPERT], table[ROW_VALID], table[ROW_FIRST] = expert, valid.astype(jnp.int32), first.astype(jnp.int32)
    table[ROW_NEXT], table[ROW_SLOT], table[ROW_BLOCK] = nxt, slot, block
    return jnp.concatenate(table, axis=0)


def _time_major_copies(seq_hbm, tm_buf, sem, chunk, slot, to_hbm=False):
    steps, nb = tm_buf.shape[1], tm_buf.shape[2]
    copies = []
    for b in range(nb):
        hbm = seq_hbm.at[b, pl.ds(chunk * steps, steps), :]
        vmem = tm_buf.at[slot, :, b, :]
        src, dst = (vmem, hbm) if to_hbm else (hbm, vmem)
        copies.append(pltpu.make_async_copy(src, dst, sem.at[slot]))
    return copies


def _mixer_kernel(alpha, nb, n_exp, chunk0, n_chunks,
                  x_hbm, w_in_ref, bm_ref, cm_ref, lam_ref, dskip_ref,
                  w_glu_ref, w_pg_ref, pscale_ref, w_pp_ref, w_out_ref,
                  ln_g_ref, ln_b_ref, wr_ref, brt_ref,
                  state_in_ref, hist_in_ref,
                  h_ref, dest_ref, wts_ref, sched_ref, state_out_ref, hist_out_ref,
                  proj_ref, bu_ref, st_ref, upool_ref, state_ref, carry_ref, tri_ref, xb_ref,
                  x_buf, x_sem):
    i = pl.program_id(0)
    last = pl.num_programs(0) - 1
    steps = x_buf.shape[1]
    rows = steps * nb
    d_model = x_buf.shape[3]

    slot = i % 2

    def fetch_copies(step, which):
        return _time_major_copies(x_hbm, x_buf, x_sem, chunk0 + step, which)

    @pl.when(i == 0)
    def _first_fetch():
        for cp in fetch_copies(0, 0):
            cp.start()

    @pl.when(i < last)
    def _next_fetch():
        for cp in fetch_copies(i + 1, 1 - slot):
            cp.start()

    for cp in fetch_copies(i, slot):
        cp.wait()
    ssm_w = dskip_ref.shape[1]
    pool_w = pscale_ref.shape[1]
    half_w = ssm_w // 2
    half_s = bu_ref.shape[1] // 2
    plane = half_s // 2
    halo = upool_ref.shape[0] - rows

    @pl.when(i == 0)
    def _init():
        state_ref[...] = state_in_ref[...]
        carry_ref[...] = jnp.zeros_like(carry_ref)
        upool_ref[0:halo, :] = hist_in_ref[...]
        ri = lax.broadcasted_iota(jnp.int32, (rows, rows), 0)
        ci = lax.broadcasted_iota(jnp.int32, (rows, rows), 1)
        tri_ref[...] = (ri < ci).astype(BF16)

    xf = x_buf[slot].reshape(rows, d_model)
    xb_ref[...] = xf.astype(BF16)
    n_mix = ssm_w + pool_w
    proj_ref[:, 0:n_mix] = _dot(xb_ref[...], w_in_ref[:, 0:n_mix])

    us = proj_ref[:, 0:ssm_w]
    usb = us.astype(BF16)
    for hf in range(2):
        bu_ref[:, hf * half_s:(hf + 1) * half_s] = _dot(
            usb[:, hf * half_w:(hf + 1) * half_w], bm_ref[hf])

    gate_cols = w_in_ref.shape[1] - n_mix
    lane_groups = [(hf, q) for hf in range(2) for q in range(plane // SCAN_LANES)]
    segments = len(lane_groups) * SCAN_SEGMENTS
    tile = gate_cols // segments
    seg_steps = steps // SCAN_SEGMENTS
    for gi, (hf, q) in enumerate(lane_groups):
        cre = hf * half_s + q * SCAN_LANES
        cim = cre + plane
        cl = hf * plane + q * SCAN_LANES
        a_re = jnp.broadcast_to(lam_ref[0:1, cl:cl + SCAN_LANES], (nb, SCAN_LANES))
        a_im = jnp.broadcast_to(lam_ref[1:2, cl:cl + SCAN_LANES], (nb, SCAN_LANES))
        s_re = state_ref[:, cre:cre + SCAN_LANES]
        s_im = state_ref[:, cim:cim + SCAN_LANES]
        for seg in range(SCAN_SEGMENTS):
            c0 = n_mix + (gi * SCAN_SEGMENTS + seg) * tile
            proj_ref[:, c0:c0 + tile] = _dot(xb_ref[...], w_in_ref[:, c0:c0 + tile])
            for t in range(seg * seg_steps, (seg + 1) * seg_steps):
                r0 = t * nb
                b_re = bu_ref[r0:r0 + nb, cre:cre + SCAN_LANES]
                b_im = bu_ref[r0:r0 + nb, cim:cim + SCAN_LANES]
                s_re, s_im = (a_re * s_re - a_im * s_im + b_re,
                              a_re * s_im + a_im * s_re + b_im)
                st_ref[r0:r0 + nb, cre:cre + SCAN_LANES] = s_re.astype(BF16)
                st_ref[r0:r0 + nb, cim:cim + SCAN_LANES] = s_im.astype(BF16)
        state_ref[:, cre:cre + SCAN_LANES] = s_re
        state_ref[:, cim:cim + SCAN_LANES] = s_im

    y = jnp.concatenate(
        [_dot(st_ref[:, hf * half_s:(hf + 1) * half_s], cm_ref[hf]) for hf in range(2)],
        axis=1) + dskip_ref[...] * us
    z = 0.5 * y * (1.0 + jnp.tanh(math.sqrt(2.0 / math.pi) * (y + 0.044715 * (y * y * y))))
    vg = _dot(z.astype(BF16), w_glu_ref[...])
    y_ssm = vg[:, 0:d_model] * _sigmoid(vg[:, d_model:2 * d_model])

    up = proj_ref[:, ssm_w:ssm_w + pool_w]
    upool_ref[halo:halo + rows, :] = up
    t_abs = ((chunk0 + i) * steps
             + lax.broadcasted_iota(jnp.int32, (rows, 1), 0) // nb).astype(F32)
    gdim = pool_w // len(POOL_WINDOWS)
    pooled = []
    for g, w in enumerate(POOL_WINDOWS):
        cur = upool_ref[:, g * gdim:(g + 1) * gdim]
        span = 1
        while span < w:
            sh = span * nb
            cur = cur[sh:] + cur[:-sh]
            span *= 2
        win = cur[cur.shape[0] - rows:]
        inv = 1.0 / jnp.minimum(t_abs + 1.0, float(w))
        pooled.append(win * inv - up[:, g * gdim:(g + 1) * gdim])
    upool_ref[0:halo, :] = upool_ref[rows:rows + halo, :]
    pooled = jnp.concatenate(pooled, axis=1)
    mixed = _dot(pooled.astype(BF16), w_pg_ref[...])
    y_pool = _dot((mixed * pscale_ref[...]).astype(BF16), w_pp_ref[...])

    g0 = ssm_w + pool_w
    merged = (_sigmoid(proj_ref[:, g0:g0 + d_model]) * y_ssm
              + _sigmoid(proj_ref[:, g0 + d_model:g0 + 2 * d_model]) * y_pool)
    pre = alpha * xf + _dot(merged.astype(BF16), w_out_ref[...])
    h1 = _layer_norm(pre, ln_g_ref[...], ln_b_ref[...])
    hb = h1.astype(BF16)
    h_ref[...] = _pack_rows(h1)

    pad_e = wr_ref.shape[1] // 2
    h_lo = (h1 - hb.astype(F32)).astype(BF16)
    l_hi = _dot(hb, wr_ref[...])
    lg = l_hi[:, 0:pad_e] + l_hi[:, pad_e:] + _dot(h_lo, wr_ref[:, 0:pad_e])
    lt = jnp.transpose(lg)[0:n_exp, :] + brt_ref[:, 0:1]
    eio = lax.broadcasted_iota(jnp.int32, (n_exp, rows), 0)
    vals, idxs = [], []
    for _ in range(TOP_K):
        m = jnp.max(lt, axis=0, keepdims=True)
        sel = jnp.min(jnp.where(lt == m, eio, n_exp), axis=0, keepdims=True)
        vals.append(m)
        idxs.append(sel)
        lt = jnp.where(eio == sel, -jnp.inf, lt)
    exps = [jnp.exp(v - vals[0]) for v in vals]
    den = exps[0] + exps[1] + exps[2] + exps[3]
    wts_ref[...] = jnp.concatenate([e / den for e in exps], axis=0)

    run = carry_ref[:, 0:1]
    ranks = []
    one_hot = [(eio == idxs[k]).astype(F32) for k in range(TOP_K)]
    before = _dot(jnp.concatenate(one_hot, axis=0).astype(BF16), tri_ref[...])
    for k in range(TOP_K):
        oh = one_hot[k]
        ranks.append(jnp.sum(oh * (run + before[k * n_exp:(k + 1) * n_exp, :]),
                             axis=0, keepdims=True))
        run = run + jnp.sum(oh, axis=1, keepdims=True)
    capacity = rows * n_chunks
    dest_ref[...] = (jnp.concatenate(idxs, axis=0) * capacity
                     + jnp.concatenate(ranks, axis=0).astype(jnp.int32))
    carry_ref[...] = jnp.broadcast_to(run, carry_ref.shape)

    @pl.when(i == last)
    def _hand_over():
        state_out_ref[...] = state_ref[...]
        hist_out_ref[...] = upool_ref[0:halo, :]
        sched_ref[...] = _block_schedule(carry_ref[:, 0:1], sched_ref.shape[1],
                                         capacity // MOE_BLOCK)


def _const_spec(shape):
    zeros = (0,) * len(shape)
    return pl.BlockSpec(shape, lambda i: zeros, pipeline_mode=pl.Buffered(1))


def _mixer_call(x3, wts, alpha, n_exp, chunk0, n_chunks, state, hist):
    nb, _, d_model = x3.shape
    n_tok = nb * n_chunks * MIX_STEPS
    rows = MIX_STEPS * nb
    grid = n_chunks
    state_cols = 2 * wts["bm"].shape[2]
    pool_w = wts["pscale"].shape[1]
    halo = max(POOL_WINDOWS) * nb

    def row_spec(width):
        return pl.BlockSpec((rows, width), lambda i: (i, 0))

    names = ["w_in", "bm", "cm", "lam", "dskip", "w_glu", "w_pg", "pscale", "w_pp",
             "w_out", "ln_g", "ln_b", "wr", "brt"]
    ops = [wts[n] for n in names]
    hbm_spec = pl.BlockSpec(memory_space=pl.ANY)
    in_specs = ([hbm_spec] + [_const_spec(o.shape) for o in ops]
                + [_const_spec(state.shape), _const_spec(hist.shape)])
    k_spec = pl.BlockSpec((TOP_K, rows), lambda i: (0, i))
    sched_lanes = _sched_lanes(n_tok, n_exp)
    out_shape = (
        jax.ShapeDtypeStruct((n_tok, d_model // 2), jnp.int32),
        jax.ShapeDtypeStruct((TOP_K, n_tok), jnp.int32),
        jax.ShapeDtypeStruct((TOP_K, n_tok), F32),
        jax.ShapeDtypeStruct((SCHED_ROWS, sched_lanes), jnp.int32),
        jax.ShapeDtypeStruct(state.shape, F32),
        jax.ShapeDtypeStruct(hist.shape, F32),
    )
    out_specs = (row_spec(d_model // 2), k_spec, k_spec,
                 pl.BlockSpec((SCHED_ROWS, sched_lanes), lambda i: (0, 0)),
                 pl.BlockSpec(state.shape, lambda i: (0, 0)),
                 pl.BlockSpec(hist.shape, lambda i: (0, 0)))
    scratch = [
        pltpu.VMEM((rows, wts["w_in"].shape[1]), F32),
        pltpu.VMEM((rows, state_cols), F32),
        pltpu.VMEM((rows, state_cols), BF16),
        pltpu.VMEM((halo + rows, pool_w), F32),
        pltpu.VMEM((nb, state_cols), F32),
        pltpu.VMEM((n_exp, 128), F32),
        pltpu.VMEM((rows, rows), BF16),
        pltpu.VMEM((rows, d_model), BF16),
        pltpu.VMEM((2, MIX_STEPS, nb, d_model), F32),
        pltpu.SemaphoreType.DMA((2,)),
    ]
    return pl.pallas_call(
        functools.partial(_mixer_kernel, alpha, nb, n_exp, chunk0, n_chunks),
        grid=(grid,),
        in_specs=in_specs,
        out_specs=out_specs,
        out_shape=out_shape,
        scratch_shapes=scratch,
        compiler_params=pltpu.CompilerParams(
            dimension_semantics=("arbitrary",), vmem_limit_bytes=VMEM_LIMIT),
        name="mixer",
    )(x3, *ops, state, hist)


def _expert_kernel(sched_ref, xs_ref, wg_hbm, bg_ref, wu_hbm, bu_ref, wd_hbm, bd_ref,
                   ys_ref, wg_bf, wu_bf, wd_bf, stage, w_sem):
    i = pl.program_id(0)
    half = xs_ref.shape[1]
    expert = sched_ref[ROW_EXPERT, i]
    n_valid = sched_ref[ROW_VALID, i]
    next_expert = sched_ref[ROW_NEXT, i]

    def weight_copies(which, slot):
        return [pltpu.make_async_copy(w_hbm.at[which], stage.at[slot, m], w_sem.at[slot, m])
                for m, w_hbm in enumerate((wg_hbm, wu_hbm, wd_hbm))]

    @pl.when(sched_ref[ROW_FIRST, i] == 1)
    def _new_expert():
        slot = sched_ref[ROW_SLOT, i]

        @pl.when(i == 0)
        def _nothing_prefetched_yet():
            for cp in weight_copies(expert, slot):
                cp.start()

        @pl.when(next_expert >= 0)
        def _prefetch_next_run():
            for cp in weight_copies(next_expert, 1 - slot):
                cp.start(priority=1)

        for cp in weight_copies(expert, slot):
            cp.wait()
        wg_bf[...] = stage[slot, 0].astype(BF16)
        wu_bf[...] = stage[slot, 1].astype(BF16)
        wd_bf[...] = stage[slot, 2].astype(BF16)

    @pl.when(n_valid > 0)
    def _compute():
        row = lax.broadcasted_iota(jnp.int32, xs_ref.shape, 0)
        x_lo, x_hi = _unpack_rows(jnp.where(row < n_valid, xs_ref[...], 0))
        g = _dot(x_lo, wg_bf[0:half, :]) + _dot(x_hi, wg_bf[half:, :]) + bg_ref[0]
        u = _dot(x_lo, wu_bf[0:half, :]) + _dot(x_hi, wu_bf[half:, :]) + bu_ref[0]
        g = jnp.minimum(g, SWIGLU_LIMIT)
        u = jnp.clip(u, -SWIGLU_LIMIT, SWIGLU_LIMIT)
        act = (u + 1.0) * (g * _sigmoid(SWIGLU_ALPHA * g))
        ys_ref[...] = _pack_rows(_dot(act.astype(BF16), wd_bf[...]) + bd_ref[0])

    @pl.when(n_valid == 0)
    def _skip():
        ys_ref[...] = jnp.zeros_like(ys_ref)


def _expert_call(schedule, n_steps, xs, wg, bg, wu, bu, wd, bd):
    n_rows = xs.shape[0]
    d_model = wg.shape[1]
    d_exp = wg.shape[2]
    assert d_model == d_exp, "the weight staging buffer assumes square expert matrices"

    def b_spec(n):
        return pl.BlockSpec((1, 1, n), lambda i, sched: (sched[ROW_EXPERT, i], 0, 0))

    row_spec = pl.BlockSpec((MOE_BLOCK, d_model // 2), lambda i, sched: (sched[ROW_BLOCK, i], 0))
    hbm_spec = pl.BlockSpec(memory_space=pl.ANY)
    grid_spec = pltpu.PrefetchScalarGridSpec(
        num_scalar_prefetch=1,
        grid=(n_steps,),
        in_specs=[row_spec, hbm_spec, b_spec(d_exp), hbm_spec, b_spec(d_exp), hbm_spec, b_spec(d_model)],
        out_specs=row_spec,
        scratch_shapes=[pltpu.VMEM((d_model, d_exp), BF16), pltpu.VMEM((d_model, d_exp), BF16),
                        pltpu.VMEM((d_exp, d_model), BF16),
                        pltpu.VMEM((2, 3, d_model, d_exp), F32),
                        pltpu.SemaphoreType.DMA((2, 3))],
    )
    return pl.pallas_call(
        _expert_kernel,
        grid_spec=grid_spec,
        out_shape=jax.ShapeDtypeStruct((n_rows, d_model // 2), jnp.int32),
        compiler_params=pltpu.CompilerParams(
            dimension_semantics=("arbitrary",), vmem_limit_bytes=VMEM_LIMIT),
        name="experts",
    )(schedule, xs, wg, bg, wu, bu, wd, bd)


def _combine_kernel(alpha, chunk0, h_ref, *refs):
    yg_refs = refs[:TOP_K]
    wt_ref, w_pleg_ref, w_plep_ref, ln_g_ref, ln_b_ref, p_hbm = refs[TOP_K:TOP_K + 6]
    o_hbm, o_buf, o_sem, p_buf, p_sem = refs[-5:]
    i = pl.program_id(0)
    last = pl.num_programs(0) - 1
    slot = i % 2
    rows = h_ref.shape[0]

    def fetch_p(step, which):
        return _time_major_copies(p_hbm, p_buf, p_sem, chunk0 + step, which)

    def write_back(step, from_slot):
        return _time_major_copies(o_hbm, o_buf, o_sem, chunk0 + step, from_slot, to_hbm=True)

    def drain(step, from_slot):
        for cp in write_back(step, from_slot):
            cp.wait()

    @pl.when(i == 0)
    def _first_fetch():
        for cp in fetch_p(0, 0):
            cp.start()

    @pl.when(i < last)
    def _next_fetch():
        for cp in fetch_p(i + 1, 1 - slot):
            cp.start()

    @pl.when(i >= 2)
    def _reuse_slot():
        drain(i - 2, slot)

    half = h_ref.shape[1]
    h_lo, h_hi = _unpack_rows_f32(h_ref[...])
    for cp in fetch_p(i, slot):
        cp.wait()
    pp = _dot(p_buf[slot].reshape(rows, p_buf.shape[3]).astype(BF16), w_plep_ref[...])
    gate = _sigmoid(_dot(h_lo.astype(BF16), w_pleg_ref[0:half, :])
                    + _dot(h_hi.astype(BF16), w_pleg_ref[half:, :]))
    ple = gate * pp
    moe = None
    for k in range(TOP_K):
        w_bits = lax.bitcast_convert_type(wt_ref[:, k:k + 1].astype(BF16).astype(F32), jnp.int32)
        w_word = jnp.broadcast_to(w_bits | lax.shift_right_logical(w_bits, 16), (rows, half))
        term = pltpu.bitcast(yg_refs[k][...], BF16) * pltpu.bitcast(w_word, BF16)
        moe = term if moe is None else moe + term
    m_lo, m_hi = _unpack_rows_f32(pltpu.bitcast(moe, jnp.int32))
    acc_lo = alpha * h_lo + ple[:, 0:half] + m_lo
    acc_hi = alpha * h_hi + ple[:, half:] + m_hi
    n = 2.0 * half
    mu = (jnp.sum(acc_lo, axis=-1, keepdims=True) + jnp.sum(acc_hi, axis=-1, keepdims=True)) / n
    c_lo, c_hi = acc_lo - mu, acc_hi - mu
    var = (jnp.sum(c_lo * c_lo, axis=-1, keepdims=True)
           + jnp.sum(c_hi * c_hi, axis=-1, keepdims=True)) / n
    inv = lax.rsqrt(var + LN_EPS)
    steps, nb = o_buf.shape[1], o_buf.shape[2]
    o_buf[slot, :, :, 0:half] = (c_lo * inv * ln_g_ref[:, 0:half] + ln_b_ref[:, 0:half]
                                 ).reshape(steps, nb, half)
    o_buf[slot, :, :, half:] = (c_hi * inv * ln_g_ref[:, half:] + ln_b_ref[:, half:]
                                ).reshape(steps, nb, half)
    for cp in write_back(i, slot):
        cp.start()

    @pl.when(i == last)
    def _finish():
        @pl.when(i >= 1)
        def _previous():
            drain(i - 1, 1 - slot)
        drain(i, slot)


def _combine_call(hp, yg, wt, w_pleg, w_plep, ln_g, ln_b, p3, alpha, chunk0, earlier):
    n_tok = hp.shape[0]
    d_model = 2 * hp.shape[1]
    nb, seq, ple_dim = p3.shape
    grid = n_tok // CMB_ROWS
    steps = CMB_ROWS // nb
    extra = [] if earlier is None else [earlier]
    operands = [hp, *([yg] * TOP_K), wt, w_pleg, w_plep, ln_g, ln_b, p3, *extra]

    def slot_spec(k):
        return pl.BlockSpec((CMB_ROWS, d_model // 2), lambda i: (k * grid + i, 0))

    hbm_spec = pl.BlockSpec(memory_space=pl.ANY)
    return pl.pallas_call(
        functools.partial(_combine_kernel, alpha, chunk0),
        grid=(grid,),
        in_specs=[
            pl.BlockSpec((CMB_ROWS, d_model // 2), lambda i: (i, 0)),
            *[slot_spec(k) for k in range(TOP_K)],
            pl.BlockSpec((CMB_ROWS, TOP_K), lambda i: (i, 0)),
            _const_spec(w_pleg.shape), _const_spec(w_plep.shape),
            pl.BlockSpec((1, d_model), lambda i: (0, 0)),
            pl.BlockSpec((1, d_model), lambda i: (0, 0)),
            hbm_spec,
        ] + [hbm_spec] * len(extra),
        out_specs=hbm_spec,
        out_shape=jax.ShapeDtypeStruct((nb, seq, d_model), F32),
        input_output_aliases={len(operands) - 1: 0} if extra else {},
        scratch_shapes=[pltpu.VMEM((2, steps, nb, d_model), F32), pltpu.SemaphoreType.DMA((2,)),
                        pltpu.VMEM((2, steps, nb, ple_dim), F32), pltpu.SemaphoreType.DMA((2,))],
        compiler_params=pltpu.CompilerParams(
            dimension_semantics=("arbitrary",), vmem_limit_bytes=VMEM_LIMIT),
        name="combine",
    )(*operands)


SC_WINDOW = 128


def _sc_mesh():
    return plsc.VectorSubcoreMesh(core_axis_name="c", subcore_axis_name="s")


def _sc_dispatch(rows, dest, n_out):
    n_tok, width = rows.shape
    n_slot = dest.shape[0]
    mesh = _sc_mesh()
    n_workers = mesh.num_cores * mesh.num_subcores
    per_worker = n_tok // n_workers
    assert per_worker * n_workers == n_tok and per_worker % SC_WINDOW == 0

    @functools.partial(
        pl.kernel, out_type=jax.ShapeDtypeStruct((n_out, width), rows.dtype), mesh=mesh,
        scratch_types=[pltpu.VMEM((n_slot, SC_WINDOW), jnp.int32),
                       pltpu.VMEM((SC_WINDOW, width), rows.dtype),
                       pltpu.SemaphoreType.DMA])
    def dispatch(x_hbm, i_hbm, o_hbm, idx_v, rows_v, sem):
        wid = lax.axis_index("s") * mesh.num_cores + lax.axis_index("c")

        @pl.loop(0, per_worker // SC_WINDOW)
        def _(j):
            base = pl.multiple_of(wid * per_worker + j * SC_WINDOW, SC_WINDOW)
            pltpu.sync_copy(i_hbm.at[:, pl.ds(base, SC_WINDOW)], idx_v)
            pltpu.sync_copy(x_hbm.at[pl.ds(base, SC_WINDOW)], rows_v)
            copies = [pltpu.async_copy(rows_v, o_hbm.at[idx_v.at[k]], sem) for k in range(n_slot)]
            for cp in copies:
                cp.wait()

    return dispatch(rows, dest)


def _sc_gather(table, index):
    n_out = index.shape[0]
    width = table.shape[1]
    mesh = _sc_mesh()
    n_workers = mesh.num_cores * mesh.num_subcores
    per_worker = n_out // n_workers
    assert per_worker * n_workers == n_out and per_worker % SC_WINDOW == 0
    half = SC_WINDOW // 2

    @functools.partial(
        pl.kernel, out_type=jax.ShapeDtypeStruct((n_out, width), table.dtype), mesh=mesh,
        scratch_types=[pltpu.VMEM((SC_WINDOW,), jnp.int32),
                       pltpu.VMEM((2, half, width), table.dtype)]
        + [pltpu.SemaphoreType.DMA] * 4)
    def gather(x_hbm, i_hbm, o_hbm, idx_v, rows_v, g0_sem, g1_sem, w0_sem, w1_sem):
        wid = lax.axis_index("s") * mesh.num_cores + lax.axis_index("c")

        @pl.loop(0, per_worker // SC_WINDOW)
        def _(j):
            base = pl.multiple_of(wid * per_worker + j * SC_WINDOW, SC_WINDOW)
            pltpu.sync_copy(i_hbm.at[pl.ds(base, SC_WINDOW)], idx_v)
            g0 = pltpu.async_copy(x_hbm.at[idx_v.at[pl.ds(0, half)]], rows_v.at[0], g0_sem)
            g1 = pltpu.async_copy(x_hbm.at[idx_v.at[pl.ds(half, half)]], rows_v.at[1], g1_sem)
            g0.wait()
            w0 = pltpu.async_copy(rows_v.at[0], o_hbm.at[pl.ds(base, half)], w0_sem)
            g1.wait()
            w1 = pltpu.async_copy(rows_v.at[1], o_hbm.at[pl.ds(base + half, half)], w1_sem)
            w0.wait()
            w1.wait()

    return gather(table, index)


def _ssm_matrices(lam_re, lam_im, log_step, b_re, b_im, c_re, c_im):
    n_grp, n_state, n_ch = b_re.shape
    lam = lax.complex(lam_re.astype(F32), lam_im.astype(F32))
    step = jnp.exp(log_step.astype(F32))[:, None]
    lam_bar = jnp.exp(lam * step)
    b_bar = ((lam_bar - 1.0) / lam)[..., None] * lax.complex(b_re.astype(F32), b_im.astype(F32))
    hg = n_grp // 2
    eye = jnp.eye(hg, dtype=F32)

    def b_half(bpart):
        return jnp.einsum('gph,gk->ghkp', bpart, eye).reshape(hg * n_ch, hg * n_state)

    def c_half(cpart):
        return jnp.einsum('ghp,gk->gpkh', cpart, eye).reshape(hg * n_state, hg * n_ch)

    bm, cm = [], []
    for hf in range(2):
        sl = slice(hf * hg, (hf + 1) * hg)
        bm.append(jnp.concatenate([b_half(jnp.real(b_bar)[sl]), b_half(jnp.imag(b_bar)[sl])], axis=1))
        cm.append(jnp.concatenate([c_half(c_re.astype(F32)[sl]), -c_half(c_im.astype(F32)[sl])], axis=0))
    lam_rows = jnp.stack([jnp.real(lam_bar).reshape(-1), jnp.imag(lam_bar).reshape(-1)], axis=0)
    return jnp.stack(bm).astype(BF16), jnp.stack(cm).astype(BF16), lam_rows


def _split_router(w):
    w = w.astype(F32)
    hi32 = lax.bitcast_convert_type(
        lax.bitcast_convert_type(w, jnp.uint32) & jnp.uint32(0xFFFF0000), F32)
    hi = hi32.astype(BF16)
    lo = (w - hi32).astype(BF16)
    pad = ((0, 0), (0, 128 - w.shape[1]))
    return jnp.concatenate([jnp.pad(hi, pad), jnp.pad(lo, pad)], axis=1)


def _block_diag(w):
    g, c, _ = w.shape
    return jnp.einsum('gcd,gk->gckd', w, jnp.eye(g, dtype=w.dtype)).reshape(g * c, g * c)


def kernel(x, p, w_in, ssm_lambda_re, ssm_lambda_im, ssm_log_step, ssm_b_re, ssm_b_im, ssm_c_re, ssm_c_im, ssm_d, w_glu_val, w_glu_gate, w_pool_group, pool_scale, w_pool_proj, w_out, ln1_g, ln1_b, w_router, b_router, w_gate, b_gate, w_up, b_up, w_down, b_down, w_ple_gate, w_ple_proj, ln2_g, ln2_b):
    bsz, seq, d_model = x.shape
    depth = w_in.shape[0]
    n_exp = w_router.shape[2]
    n_tok = bsz * seq
    alpha = (2.0 * depth) ** 0.25

    h = x
    for l in range(depth):
        bm, cm, lam_rows = _ssm_matrices(ssm_lambda_re[l], ssm_lambda_im[l], ssm_log_step[l],
                                         ssm_b_re[l], ssm_b_im[l], ssm_c_re[l], ssm_c_im[l])
        wts = {
            "w_in": w_in[l].astype(BF16), "bm": bm, "cm": cm, "lam": lam_rows,
            "dskip": ssm_d[l].reshape(1, -1).astype(F32),
            "w_glu": jnp.concatenate([w_glu_val[l], w_glu_gate[l]], axis=1).astype(BF16),
            "w_pg": _block_diag(w_pool_group[l]).astype(BF16),
            "pscale": pool_scale[l].reshape(1, -1).astype(F32),
            "w_pp": w_pool_proj[l].astype(BF16),
            "w_out": w_out[l].astype(BF16),
            "ln_g": ln1_g[l].reshape(1, -1).astype(F32),
            "ln_b": ln1_b[l].reshape(1, -1).astype(F32),
            "wr": _split_router(w_router[l]),
            "brt": jnp.broadcast_to(b_router[l].astype(F32)[:, None], (n_exp, 128)),
        }
        ple_w = (w_ple_gate[l].astype(BF16), w_ple_proj[l].astype(BF16))
        expert_w = (w_gate[l].astype(F32), b_gate[l].astype(F32)[:, None, :],
                    w_up[l].astype(F32), b_up[l].astype(F32)[:, None, :],
                    w_down[l].astype(F32), b_down[l].astype(F32)[:, None, :])
        ln2 = (ln2_g[l].reshape(1, -1).astype(F32), ln2_b[l].reshape(1, -1).astype(F32))

        chunks = seq // MIX_STEPS
        per_piece = chunks // TIME_PIECES
        cmb_per_chunk = MIX_STEPS * bsz // CMB_ROWS
        assert per_piece * TIME_PIECES == chunks and cmb_per_chunk * CMB_ROWS == MIX_STEPS * bsz
        state = jnp.zeros((bsz, 2 * bm.shape[2]), F32)
        hist = jnp.zeros((max(POOL_WINDOWS) * bsz, pool_scale.shape[1]), F32)
        out = None
        for piece in range(TIME_PIECES):
            chunk0 = piece * per_piece
            hp, dest, wt, schedule, state, hist = _mixer_call(
                h, wts, alpha, n_exp, chunk0, per_piece, state, hist)
            n_piece = hp.shape[0]
            n_rows = n_exp * n_piece + MOE_BLOCK
            xs = _sc_dispatch(hp, dest, n_rows)
            ys = _expert_call(schedule, _n_expert_blocks(n_piece, n_exp), xs, *expert_w)
            yg = _sc_gather(ys, dest.reshape(-1))
            out = _combine_call(hp, yg, jnp.transpose(wt), *ple_w, *ln2, p[l], alpha,
                                chunk0 * cmb_per_chunk, out)
        h = out
    return h
```

```python
import functools
import math

import jax
import jax.numpy as jnp
from jax import lax
from jax.experimental import pallas as pl
from jax.experimental.pallas import tpu as pltpu
from jax.experimental.pallas import tpu_sc as plsc

F32 = jnp.float32
BF16 = jnp.bfloat16

LN_EPS = 1e-5
SWIGLU_LIMIT = 7.0
SWIGLU_ALPHA = 1.702
POOL_WINDOWS = (2, 4, 8, 16)
TOP_K = 4

MIX_STEPS = 32
SCAN_LANES = 512
SCAN_SEGMENTS = 2
BACK_CHUNKS = 4
MOE_BLOCK = 512
CMB_ROWS = 512
TIME_PIECES = 2
VMEM_LIMIT = 60 * 1024 * 1024


def _sigmoid(v):
    return 0.5 * jnp.tanh(0.5 * v) + 0.5


def _layer_norm(v, g, b):
    mu = jnp.mean(v, axis=-1, keepdims=True)
    vc = v - mu
    var = jnp.mean(vc * vc, axis=-1, keepdims=True)
    return vc * lax.rsqrt(var + LN_EPS) * g + b


def _dot(a, b):
    return jnp.dot(a, b, preferred_element_type=F32)


def _pack_rows(v):
    n = v.shape[1] // 2
    lo = lax.bitcast_convert_type(v[:, :n].astype(BF16).astype(F32), jnp.int32)
    hi = lax.bitcast_convert_type(v[:, n:].astype(BF16).astype(F32), jnp.int32)
    return hi | lax.shift_right_logical(lo, 16)


def _unpack_rows_f32(w):
    lo = lax.bitcast_convert_type(lax.shift_left(w, 16), F32)
    hi = lax.bitcast_convert_type(w & jnp.int32(-65536), F32)
    return lo, hi


def _unpack_rows(w):
    lo, hi = _unpack_rows_f32(w)
    return lo.astype(BF16), hi.astype(BF16)


SCHED_ROWS = 8
ROW_EXPERT, ROW_VALID, ROW_FIRST, ROW_NEXT, ROW_SLOT, ROW_BLOCK = range(6)


def _n_expert_blocks(n_tok, n_exp):
    return n_tok * TOP_K // MOE_BLOCK + n_exp


def _sched_lanes(n_tok, n_exp):
    return -(-_n_expert_blocks(n_tok, n_exp) // 128) * 128


def _block_schedule(counts, n_lanes, cap_blocks):
    n_exp = counts.shape[0]
    sub = lax.broadcasted_iota(jnp.int32, (n_exp, n_exp), 0)
    lane = lax.broadcasted_iota(jnp.int32, (n_exp, n_exp), 1)

    def as_row(col):
        return jnp.sum(jnp.where(sub == lane, col, 0.0), axis=0, keepdims=True)

    def running(col):
        return jnp.sum(jnp.where(lane <= sub, as_row(col), 0.0), axis=1, keepdims=True)

    blocks = jnp.floor((counts + (MOE_BLOCK - 1)) * (1.0 / MOE_BLOCK))
    ends = running(blocks)
    starts = ends - blocks
    live = counts > 0.0
    live_upto = running(jnp.where(live, 1.0, 0.0))

    step = lax.broadcasted_iota(jnp.int32, (1, n_lanes), 1).astype(F32)
    eid = lax.broadcasted_iota(jnp.int32, (n_exp, n_lanes), 0)
    expert = jnp.minimum(
        jnp.sum(jnp.where(ends <= step, 1, 0), axis=0, keepdims=True), n_exp - 1)
    mine = eid == expert

    def pick(col):
        return jnp.sum(jnp.where(mine, col, 0.0), axis=0, keepdims=True)

    used = step < jnp.max(ends, axis=0, keepdims=True)
    j = step - pick(starts)
    valid = jnp.where(used, jnp.clip(pick(counts) - j * MOE_BLOCK, 0.0, float(MOE_BLOCK)), 0.0)
    first = used & (j == 0.0)
    nxt = jnp.min(jnp.where((eid > expert) & live, eid, n_exp), axis=0, keepdims=True)
    nxt = jnp.where(nxt == n_exp, -1, nxt)
    slot = (pick(live_upto).astype(jnp.int32) + 1) & 1
    block = jnp.where(used, expert * cap_blocks + j.astype(jnp.int32), n_exp * cap_blocks)
    zero = jnp.zeros_like(expert)
    table = [zero] * SCHED_ROWS
    table[ROW_EXPERT], table[ROW_VALID], table[ROW_FIRST] = expert, valid.astype(jnp.int32), first.astype(jnp.int32)
    table[ROW_NEXT], table[ROW_SLOT], table[ROW_BLOCK] = nxt, slot, block
    return jnp.concatenate(table, axis=0)


def _time_major_copies(seq_hbm, tm_buf, sem, chunk, slot, to_hbm=False):
    steps, nb = tm_buf.shape[1], tm_buf.shape[2]
    copies = []
    for b in range(nb):
        hbm = seq_hbm.at[b, pl.ds(chunk * steps, steps), :]
        vmem = tm_buf.at[slot, :, b, :]
        src, dst = (vmem, hbm) if to_hbm else (hbm, vmem)
        copies.append(pltpu.make_async_copy(src, dst, sem.at[slot]))
    return copies


def _mixer_kernel(alpha, nb, n_exp, chunk0, n_chunks,
                  x_hbm, w_in_ref, bm_ref, cm_ref, lam_ref, dskip_ref,
                  w_glu_ref, w_pg_ref, pscale_ref, w_pp_ref, w_out_ref,
                  ln_g_ref, ln_b_ref, wr_ref, brt_ref,
                  state_in_ref, hist_in_ref,
                  h_ref, dest_ref, wts_ref, sched_ref, state_out_ref, hist_out_ref,
                  proj_ref, bu_ref, st_ref, upool_ref, state_ref, carry_ref, tri_ref, xb_ref,
                  pre_ref, x_buf, x_sem):
    i = pl.program_id(0)
    steps = x_buf.shape[1]
    rows = steps * nb
    d_model = x_buf.shape[3]

    slot = i % 2

    def fetch_copies(step, which):
        return _time_major_copies(x_hbm, x_buf, x_sem, chunk0 + step, which)

    @pl.when(i == 0)
    def _first_fetch():
        for cp in fetch_copies(0, 0):
            cp.start()

    @pl.when(i + 1 < n_chunks)
    def _next_fetch():
        for cp in fetch_copies(i + 1, 1 - slot):
            cp.start()

    @pl.when(i < n_chunks)
    def _await_fetch():
        for cp in fetch_copies(i, slot):
            cp.wait()
    ssm_w = dskip_ref.shape[1]
    pool_w = pscale_ref.shape[1]
    half_w = ssm_w // 2
    half_s = bu_ref.shape[1] // 2
    plane = half_s // 2
    halo = upool_ref.shape[0] - rows

    @pl.when(i == 0)
    def _init():
        state_ref[...] = state_in_ref[...]
        carry_ref[...] = jnp.zeros_like(carry_ref)
        upool_ref[0:halo, :] = hist_in_ref[...]
        ri = lax.broadcasted_iota(jnp.int32, (rows, rows), 0)
        ci = lax.broadcasted_iota(jnp.int32, (rows, rows), 1)
        tri_ref[...] = (ri < ci).astype(BF16)
        pre_ref[...] = jnp.zeros_like(pre_ref)

    capacity = rows * n_chunks

    chunk_rows = rows // BACK_CHUNKS

    def back_norm(c):
        r0 = c * chunk_rows
        h1 = _layer_norm(pre_ref[1, r0:r0 + chunk_rows, :], ln_g_ref[...], ln_b_ref[...])
        hb = h1.astype(BF16)
        h_ref[r0:r0 + chunk_rows, :] = _pack_rows(h1)
        return hb, (h1 - hb.astype(F32)).astype(BF16)

    def back_logits(hb, h_lo):
        pad_e = wr_ref.shape[1] // 2
        l_hi = _dot(hb, wr_ref[...])
        return l_hi[:, 0:pad_e] + l_hi[:, pad_e:] + _dot(h_lo, wr_ref[:, 0:pad_e])

    def back_topk(lg):
        lt = jnp.transpose(lg)[0:n_exp, :] + brt_ref[:, 0:1]
        eio = lax.broadcasted_iota(jnp.int32, (n_exp, chunk_rows), 0)
        vals, idxs = [], []
        for _ in range(TOP_K):
            m = jnp.max(lt, axis=0, keepdims=True)
            sel = jnp.min(jnp.where(lt == m, eio, n_exp), axis=0, keepdims=True)
            vals.append(m)
            idxs.append(sel)
            lt = jnp.where(eio == sel, -jnp.inf, lt)
        return jnp.concatenate(vals, axis=0), jnp.concatenate(idxs, axis=0)

    def back_finish(routed):
        in_range = (i > 0).astype(F32)
        vals = jnp.concatenate([v for v, _ in routed], axis=1)
        idxs = jnp.concatenate([s for _, s in routed], axis=1)
        exps = jnp.exp(vals - vals[0:1, :])
        wts_ref[...] = exps / jnp.sum(exps, axis=0, keepdims=True)

        eio = lax.broadcasted_iota(jnp.int32, (n_exp, rows), 0)
        run = carry_ref[:, 0:1]
        ranks = []
        one_hot = [(eio == idxs[k:k + 1, :]).astype(F32) for k in range(TOP_K)]
        before = _dot(jnp.concatenate(one_hot, axis=0).astype(BF16), tri_ref[...])
        for k in range(TOP_K):
            oh = one_hot[k]
            ranks.append(jnp.sum(oh * (run + before[k * n_exp:(k + 1) * n_exp, :]),
                                 axis=0, keepdims=True))
            run = run + in_range * jnp.sum(oh, axis=1, keepdims=True)
        dest_ref[...] = idxs * capacity + jnp.concatenate(ranks, axis=0).astype(jnp.int32)
        carry_ref[...] = jnp.broadcast_to(run, carry_ref.shape)

    xf = x_buf[slot].reshape(rows, d_model)
    xb_ref[...] = xf.astype(BF16)
    n_mix = ssm_w + pool_w
    normed, logits, routed = {}, {}, []

    def back_parts(tile):
        if 0 <= tile - 1 < BACK_CHUNKS:
            logits[tile - 1] = back_logits(*normed.pop(tile - 1))
        if 0 <= tile - 2 < BACK_CHUNKS:
            routed.append(back_topk(logits.pop(tile - 2)))
        if tile < BACK_CHUNKS:
            normed[tile] = back_norm(tile)

    for c in range(BACK_CHUNKS):
        c0 = c * (n_mix // BACK_CHUNKS)
        c1 = c0 + n_mix // BACK_CHUNKS
        proj_ref[:, c0:c1] = _dot(xb_ref[...], w_in_ref[:, c0:c1])
        back_parts(c)

    us = proj_ref[:, 0:ssm_w]
    usb = us.astype(BF16)
    quarter = half_s // 2
    for hf in range(2):
        for q in range(2):
            c0 = hf * half_s + q * quarter
            bu_ref[:, c0:c0 + quarter] = _dot(usb[:, hf * half_w:(hf + 1) * half_w],
                                              bm_ref[hf, :, q * quarter:(q + 1) * quarter])
            back_parts(BACK_CHUNKS + 2 * hf + q)
    back_finish(routed)

    gate_cols = w_in_ref.shape[1] - n_mix
    lane_groups = [(hf, q) for hf in range(2) for q in range(plane // SCAN_LANES)]
    segments = len(lane_groups) * SCAN_SEGMENTS
    tile = gate_cols // segments
    seg_steps = steps // SCAN_SEGMENTS
    for gi, (hf, q) in enumerate(lane_groups):
        cre = hf * half_s + q * SCAN_LANES
        cim = cre + plane
        cl = hf * plane + q * SCAN_LANES
        a_re = jnp.broadcast_to(lam_ref[0:1, cl:cl + SCAN_LANES], (nb, SCAN_LANES))
        a_im = jnp.broadcast_to(lam_ref[1:2, cl:cl + SCAN_LANES], (nb, SCAN_LANES))
        s_re = state_ref[:, cre:cre + SCAN_LANES]
        s_im = state_ref[:, cim:cim + SCAN_LANES]
        for seg in range(SCAN_SEGMENTS):
            c0 = n_mix + (gi * SCAN_SEGMENTS + seg) * tile
            proj_ref[:, c0:c0 + tile] = _dot(xb_ref[...], w_in_ref[:, c0:c0 + tile])
            for t in range(seg * seg_steps, (seg + 1) * seg_steps):
                r0 = t * nb
                b_re = bu_ref[r0:r0 + nb, cre:cre + SCAN_LANES]
                b_im = bu_ref[r0:r0 + nb, cim:cim + SCAN_LANES]
                s_re, s_im = (a_re * s_re - a_im * s_im + b_re,
                              a_re * s_im + a_im * s_re + b_im)
                st_ref[r0:r0 + nb, cre:cre + SCAN_LANES] = s_re.astype(BF16)
                st_ref[r0:r0 + nb, cim:cim + SCAN_LANES] = s_im.astype(BF16)
        state_ref[:, cre:cre + SCAN_LANES] = s_re
        state_ref[:, cim:cim + SCAN_LANES] = s_im

    y = jnp.concatenate(
        [_dot(st_ref[:, hf * half_s:(hf + 1) * half_s], cm_ref[hf]) for hf in range(2)],
        axis=1) + dskip_ref[...] * us
    z = 0.5 * y * (1.0 + jnp.tanh(math.sqrt(2.0 / math.pi) * (y + 0.044715 * (y * y * y))))
    vg = _dot(z.astype(BF16), w_glu_ref[...])
    y_ssm = vg[:, 0:d_model] * _sigmoid(vg[:, d_model:2 * d_model])

    up = proj_ref[:, ssm_w:ssm_w + pool_w]
    upool_ref[halo:halo + rows, :] = up
    t_abs = ((chunk0 + i) * steps
             + lax.broadcasted_iota(jnp.int32, (rows, 1), 0) // nb).astype(F32)
    gdim = pool_w // len(POOL_WINDOWS)
    pooled = []
    for g, w in enumerate(POOL_WINDOWS):
        cur = upool_ref[:, g * gdim:(g + 1) * gdim]
        span = 1
        while span < w:
            sh = span * nb
            cur = cur[sh:] + cur[:-sh]
            span *= 2
        win = cur[cur.shape[0] - rows:]
        inv = 1.0 / jnp.minimum(t_abs + 1.0, float(w))
        pooled.append(win * inv - up[:, g * gdim:(g + 1) * gdim])
    upool_ref[0:halo, :] = upool_ref[rows:rows + halo, :]
    pooled = jnp.concatenate(pooled, axis=1)
    mixed = _dot(pooled.astype(BF16), w_pg_ref[...])
    y_pool = _dot((mixed * pscale_ref[...]).astype(BF16), w_pp_ref[...])

    g0 = ssm_w + pool_w
    merged = (_sigmoid(proj_ref[:, g0:g0 + d_model]) * y_ssm
              + _sigmoid(proj_ref[:, g0 + d_model:g0 + 2 * d_model]) * y_pool)
    pre_ref[0] = alpha * xf + _dot(merged.astype(BF16), w_out_ref[...])

    pre_ref[1] = pre_ref[0]

    @pl.when(i == n_chunks - 1)
    def _hand_over_state():
        state_out_ref[...] = state_ref[...]
        hist_out_ref[...] = upool_ref[0:halo, :]

    @pl.when(i == n_chunks)
    def _hand_over_schedule():
        sched_ref[...] = _block_schedule(carry_ref[:, 0:1], sched_ref.shape[1],
                                         capacity // MOE_BLOCK)


def _const_spec(shape):
    zeros = (0,) * len(shape)
    return pl.BlockSpec(shape, lambda i: zeros, pipeline_mode=pl.Buffered(1))


def _mixer_call(x3, wts, alpha, n_exp, chunk0, n_chunks, state, hist):
    nb, _, d_model = x3.shape
    n_tok = nb * n_chunks * MIX_STEPS
    rows = MIX_STEPS * nb
    grid = n_chunks + 1
    state_cols = 2 * wts["bm"].shape[2]
    pool_w = wts["pscale"].shape[1]
    halo = max(POOL_WINDOWS) * nb

    def row_spec(width):
        return pl.BlockSpec((rows, width), lambda i: (jnp.maximum(i - 1, 0), 0))

    names = ["w_in", "bm", "cm", "lam", "dskip", "w_glu", "w_pg", "pscale", "w_pp",
             "w_out", "ln_g", "ln_b", "wr", "brt"]
    ops = [wts[n] for n in names]
    hbm_spec = pl.BlockSpec(memory_space=pl.ANY)
    in_specs = ([hbm_spec] + [_const_spec(o.shape) for o in ops]
                + [_const_spec(state.shape), _const_spec(hist.shape)])
    k_spec = pl.BlockSpec((TOP_K, rows), lambda i: (0, jnp.maximum(i - 1, 0)))
    sched_lanes = _sched_lanes(n_tok, n_exp)
    out_shape = (
        jax.ShapeDtypeStruct((n_tok, d_model // 2), jnp.int32),
        jax.ShapeDtypeStruct((TOP_K, n_tok), jnp.int32),
        jax.ShapeDtypeStruct((TOP_K, n_tok), F32),
        jax.ShapeDtypeStruct((SCHED_ROWS, sched_lanes), jnp.int32),
        jax.ShapeDtypeStruct(state.shape, F32),
        jax.ShapeDtypeStruct(hist.shape, F32),
    )
    out_specs = (row_spec(d_model // 2), k_spec, k_spec,
                 pl.BlockSpec((SCHED_ROWS, sched_lanes), lambda i: (0, 0)),
                 pl.BlockSpec(state.shape, lambda i: (0, 0)),
                 pl.BlockSpec(hist.shape, lambda i: (0, 0)))
    scratch = [
        pltpu.VMEM((rows, wts["w_in"].shape[1]), F32),
        pltpu.VMEM((rows, state_cols), F32),
        pltpu.VMEM((rows, state_cols), BF16),
        pltpu.VMEM((halo + rows, pool_w), F32),
        pltpu.VMEM((nb, state_cols), F32),
        pltpu.VMEM((n_exp, 128), F32),
        pltpu.VMEM((rows, rows), BF16),
        pltpu.VMEM((rows, d_model), BF16),
        pltpu.VMEM((2, rows, d_model), F32),
        pltpu.VMEM((2, MIX_STEPS, nb, d_model), F32),
        pltpu.SemaphoreType.DMA((2,)),
    ]
    return pl.pallas_call(
        functools.partial(_mixer_kernel, alpha, nb, n_exp, chunk0, n_chunks),
        grid=(grid,),
        in_specs=in_specs,
        out_specs=out_specs,
        out_shape=out_shape,
        scratch_shapes=scratch,
        compiler_params=pltpu.CompilerParams(
            dimension_semantics=("arbitrary",), vmem_limit_bytes=VMEM_LIMIT),
        name="mixer",
    )(x3, *ops, state, hist)


def _expert_kernel(sched_ref, xs_ref, wg_hbm, bg_ref, wu_hbm, bu_ref, wd_hbm, bd_ref,
                   ys_ref, wg_bf, wu_bf, wd_bf, stage, w_sem):
    i = pl.program_id(0)
    half = xs_ref.shape[1]
    expert = sched_ref[ROW_EXPERT, i]
    n_valid = sched_ref[ROW_VALID, i]
    next_expert = sched_ref[ROW_NEXT, i]

    def weight_copies(which, slot):
        return [pltpu.make_async_copy(w_hbm.at[which], stage.at[slot, m], w_sem.at[slot, m])
                for m, w_hbm in enumerate((wg_hbm, wu_hbm, wd_hbm))]

    @pl.when(sched_ref[ROW_FIRST, i] == 1)
    def _new_expert():
        slot = sched_ref[ROW_SLOT, i]

        @pl.when(i == 0)
        def _nothing_prefetched_yet():
            for cp in weight_copies(expert, slot):
                cp.start()

        @pl.when(next_expert >= 0)
        def _prefetch_next_run():
            for cp in weight_copies(next_expert, 1 - slot):
                cp.start(priority=1)

        for cp in weight_copies(expert, slot):
            cp.wait()
        wg_bf[...] = stage[slot, 0].astype(BF16)
        wu_bf[...] = stage[slot, 1].astype(BF16)
        wd_bf[...] = stage[slot, 2].astype(BF16)

    @pl.when(n_valid > 0)
    def _compute():
        row = lax.broadcasted_iota(jnp.int32, xs_ref.shape, 0)
        x_lo, x_hi = _unpack_rows(jnp.where(row < n_valid, xs_ref[...], 0))
        g = _dot(x_lo, wg_bf[0:half, :]) + _dot(x_hi, wg_bf[half:, :]) + bg_ref[0]
        u = _dot(x_lo, wu_bf[0:half, :]) + _dot(x_hi, wu_bf[half:, :]) + bu_ref[0]
        g = jnp.minimum(g, SWIGLU_LIMIT)
        u = jnp.clip(u, -SWIGLU_LIMIT, SWIGLU_LIMIT)
        act = (u + 1.0) * (g * _sigmoid(SWIGLU_ALPHA * g))
        ys_ref[...] = _pack_rows(_dot(act.astype(BF16), wd_bf[...]) + bd_ref[0])

    @pl.when(n_valid == 0)
    def _skip():
        ys_ref[...] = jnp.zeros_like(ys_ref)


def _expert_call(schedule, n_steps, xs, wg, bg, wu, bu, wd, bd):
    n_rows = xs.shape[0]
    d_model = wg.shape[1]
    d_exp = wg.shape[2]
    assert d_model == d_exp, "the weight staging buffer assumes square expert matrices"

    def b_spec(n):
        return pl.BlockSpec((1, 1, n), lambda i, sched: (sched[ROW_EXPERT, i], 0, 0))

    row_spec = pl.BlockSpec((MOE_BLOCK, d_model // 2), lambda i, sched: (sched[ROW_BLOCK, i], 0))
    hbm_spec = pl.BlockSpec(memory_space=pl.ANY)
    grid_spec = pltpu.PrefetchScalarGridSpec(
        num_scalar_prefetch=1,
        grid=(n_steps,),
        in_specs=[row_spec, hbm_spec, b_spec(d_exp), hbm_spec, b_spec(d_exp), hbm_spec, b_spec(d_model)],
        out_specs=row_spec,
        scratch_shapes=[pltpu.VMEM((d_model, d_exp), BF16), pltpu.VMEM((d_model, d_exp), BF16),
                        pltpu.VMEM((d_exp, d_model), BF16),
                        pltpu.VMEM((2, 3, d_model, d_exp), F32),
                        pltpu.SemaphoreType.DMA((2, 3))],
    )
    return pl.pallas_call(
        _expert_kernel,
        grid_spec=grid_spec,
        out_shape=jax.ShapeDtypeStruct((n_rows, d_model // 2), jnp.int32),
        compiler_params=pltpu.CompilerParams(
            dimension_semantics=("arbitrary",), vmem_limit_bytes=VMEM_LIMIT),
        name="experts",
    )(schedule, xs, wg, bg, wu, bu, wd, bd)


def _combine_kernel(alpha, chunk0, h_ref, *refs):
    yg_refs = refs[:TOP_K]
    wt_ref, w_pleg_ref, w_plep_ref, ln_g_ref, ln_b_ref, p_hbm = refs[TOP_K:TOP_K + 6]
    o_hbm, o_buf, o_sem, p_buf, p_sem = refs[-5:]
    i = pl.program_id(0)
    last = pl.num_programs(0) - 1
    slot = i % 2
    rows = h_ref.shape[0]

    def fetch_p(step, which):
        return _time_major_copies(p_hbm, p_buf, p_sem, chunk0 + step, which)

    def write_back(step, from_slot):
        return _time_major_copies(o_hbm, o_buf, o_sem, chunk0 + step, from_slot, to_hbm=True)

    def drain(step, from_slot):
        for cp in write_back(step, from_slot):
            cp.wait()

    @pl.when(i == 0)
    def _first_fetch():
        for cp in fetch_p(0, 0):
            cp.start()

    @pl.when(i < last)
    def _next_fetch():
        for cp in fetch_p(i + 1, 1 - slot):
            cp.start()

    @pl.when(i >= 2)
    def _reuse_slot():
        drain(i - 2, slot)

    half = h_ref.shape[1]
    h_lo, h_hi = _unpack_rows_f32(h_ref[...])
    for cp in fetch_p(i, slot):
        cp.wait()
    pp = _dot(p_buf[slot].reshape(rows, p_buf.shape[3]).astype(BF16), w_plep_ref[...])
    gate = _sigmoid(_dot(h_lo.astype(BF16), w_pleg_ref[0:half, :])
                    + _dot(h_hi.astype(BF16), w_pleg_ref[half:, :]))
    ple = gate * pp
    moe = None
    for k in range(TOP_K):
        w_bits = lax.bitcast_convert_type(wt_ref[:, k:k + 1].astype(BF16).astype(F32), jnp.int32)
        w_word = jnp.broadcast_to(w_bits | lax.shift_right_logical(w_bits, 16), (rows, half))
        term = pltpu.bitcast(yg_refs[k][...], BF16) * pltpu.bitcast(w_word, BF16)
        moe = term if moe is None else moe + term
    m_lo, m_hi = _unpack_rows_f32(pltpu.bitcast(moe, jnp.int32))
    acc_lo = alpha * h_lo + ple[:, 0:half] + m_lo
    acc_hi = alpha * h_hi + ple[:, half:] + m_hi
    n = 2.0 * half
    mu = (jnp.sum(acc_lo, axis=-1, keepdims=True) + jnp.sum(acc_hi, axis=-1, keepdims=True)) / n
    c_lo, c_hi = acc_lo - mu, acc_hi - mu
    var = (jnp.sum(c_lo * c_lo, axis=-1, keepdims=True)
           + jnp.sum(c_hi * c_hi, axis=-1, keepdims=True)) / n
    inv = lax.rsqrt(var + LN_EPS)
    steps, nb = o_buf.shape[1], o_buf.shape[2]
    o_buf[slot, :, :, 0:half] = (c_lo * inv * ln_g_ref[:, 0:half] + ln_b_ref[:, 0:half]
                                 ).reshape(steps, nb, half)
    o_buf[slot, :, :, half:] = (c_hi * inv * ln_g_ref[:, half:] + ln_b_ref[:, half:]
                                ).reshape(steps, nb, half)
    for cp in write_back(i, slot):
        cp.start()

    @pl.when(i == last)
    def _finish():
        @pl.when(i >= 1)
        def _previous():
            drain(i - 1, 1 - slot)
        drain(i, slot)


def _combine_call(hp, yg, wt, w_pleg, w_plep, ln_g, ln_b, p3, alpha, chunk0, earlier):
    n_tok = hp.shape[0]
    d_model = 2 * hp.shape[1]
    nb, seq, ple_dim = p3.shape
    grid = n_tok // CMB_ROWS
    steps = CMB_ROWS // nb
    extra = [] if earlier is None else [earlier]
    operands = [hp, *([yg] * TOP_K), wt, w_pleg, w_plep, ln_g, ln_b, p3, *extra]

    def slot_spec(k):
        return pl.BlockSpec((CMB_ROWS, d_model // 2), lambda i: (k * grid + i, 0))

    hbm_spec = pl.BlockSpec(memory_space=pl.ANY)
    return pl.pallas_call(
        functools.partial(_combine_kernel, alpha, chunk0),
        grid=(grid,),
        in_specs=[
            pl.BlockSpec((CMB_ROWS, d_model // 2), lambda i: (i, 0)),
            *[slot_spec(k) for k in range(TOP_K)],
            pl.BlockSpec((CMB_ROWS, TOP_K), lambda i: (i, 0)),
            _const_spec(w_pleg.shape), _const_spec(w_plep.shape),
            pl.BlockSpec((1, d_model), lambda i: (0, 0)),
            pl.BlockSpec((1, d_model), lambda i: (0, 0)),
            hbm_spec,
        ] + [hbm_spec] * len(extra),
        out_specs=hbm_spec,
        out_shape=jax.ShapeDtypeStruct((nb, seq, d_model), F32),
        input_output_aliases={len(operands) - 1: 0} if extra else {},
        scratch_shapes=[pltpu.VMEM((2, steps, nb, d_model), F32), pltpu.SemaphoreType.DMA((2,)),
                        pltpu.VMEM((2, steps, nb, ple_dim), F32), pltpu.SemaphoreType.DMA((2,))],
        compiler_params=pltpu.CompilerParams(
            dimension_semantics=("arbitrary",), vmem_limit_bytes=VMEM_LIMIT),
        name="combine",
    )(*operands)


SC_WINDOW = 128


def _sc_mesh():
    return plsc.VectorSubcoreMesh(core_axis_name="c", subcore_axis_name="s")


def _sc_dispatch(rows, dest, n_out):
    n_tok, width = rows.shape
    n_slot = dest.shape[0]
    mesh = _sc_mesh()
    n_workers = mesh.num_cores * mesh.num_subcores
    per_worker = n_tok // n_workers
    assert per_worker * n_workers == n_tok and per_worker % SC_WINDOW == 0

    @functools.partial(
        pl.kernel, out_type=jax.ShapeDtypeStruct((n_out, width), rows.dtype), mesh=mesh,
        scratch_types=[pltpu.VMEM((n_slot, SC_WINDOW), jnp.int32),
                       pltpu.VMEM((SC_WINDOW, width), rows.dtype),
                       pltpu.SemaphoreType.DMA])
    def dispatch(x_hbm, i_hbm, o_hbm, idx_v, rows_v, sem):
        wid = lax.axis_index("s") * mesh.num_cores + lax.axis_index("c")

        @pl.loop(0, per_worker // SC_WINDOW)
        def _(j):
            base = pl.multiple_of(wid * per_worker + j * SC_WINDOW, SC_WINDOW)
            pltpu.sync_copy(i_hbm.at[:, pl.ds(base, SC_WINDOW)], idx_v)
            pltpu.sync_copy(x_hbm.at[pl.ds(base, SC_WINDOW)], rows_v)
            copies = [pltpu.async_copy(rows_v, o_hbm.at[idx_v.at[k]], sem) for k in range(n_slot)]
            for cp in copies:
                cp.wait()

    return dispatch(rows, dest)


def _sc_gather(table, index):
    n_out = index.shape[0]
    width = table.shape[1]
    mesh = _sc_mesh()
    n_workers = mesh.num_cores * mesh.num_subcores
    per_worker = n_out // n_workers
    assert per_worker * n_workers == n_out and per_worker % SC_WINDOW == 0
    half = SC_WINDOW // 2

    @functools.partial(
        pl.kernel, out_type=jax.ShapeDtypeStruct((n_out, width), table.dtype), mesh=mesh,
        scratch_types=[pltpu.VMEM((SC_WINDOW,), jnp.int32),
                       pltpu.VMEM((2, half, width), table.dtype)]
        + [pltpu.SemaphoreType.DMA] * 4)
    def gather(x_hbm, i_hbm, o_hbm, idx_v, rows_v, g0_sem, g1_sem, w0_sem, w1_sem):
        wid = lax.axis_index("s") * mesh.num_cores + lax.axis_index("c")

        @pl.loop(0, per_worker // SC_WINDOW)
        def _(j):
            base = pl.multiple_of(wid * per_worker + j * SC_WINDOW, SC_WINDOW)
            pltpu.sync_copy(i_hbm.at[pl.ds(base, SC_WINDOW)], idx_v)
            g0 = pltpu.async_copy(x_hbm.at[idx_v.at[pl.ds(0, half)]], rows_v.at[0], g0_sem)
            g1 = pltpu.async_copy(x_hbm.at[idx_v.at[pl.ds(half, half)]], rows_v.at[1], g1_sem)
            g0.wait()
            w0 = pltpu.async_copy(rows_v.at[0], o_hbm.at[pl.ds(base, half)], w0_sem)
            g1.wait()
            w1 = pltpu.async_copy(rows_v.at[1], o_hbm.at[pl.ds(base + half, half)], w1_sem)
            w0.wait()
            w1.wait()

    return gather(table, index)


def _ssm_matrices(lam_re, lam_im, log_step, b_re, b_im, c_re, c_im):
    n_grp, n_state, n_ch = b_re.shape
    lam = lax.complex(lam_re.astype(F32), lam_im.astype(F32))
    step = jnp.exp(log_step.astype(F32))[:, None]
    lam_bar = jnp.exp(lam * step)
    b_bar = ((lam_bar - 1.0) / lam)[..., None] * lax.complex(b_re.astype(F32), b_im.astype(F32))
    hg = n_grp // 2
    eye = jnp.eye(hg, dtype=F32)

    def b_half(bpart):
        return jnp.einsum('gph,gk->ghkp', bpart, eye).reshape(hg * n_ch, hg * n_state)

    def c_half(cpart):
        return jnp.einsum('ghp,gk->gpkh', cpart, eye).reshape(hg * n_state, hg * n_ch)

    bm, cm = [], []
    for hf in range(2):
        sl = slice(hf * hg, (hf + 1) * hg)
        bm.append(jnp.concatenate([b_half(jnp.real(b_bar)[sl]), b_half(jnp.imag(b_bar)[sl])], axis=1))
        cm.append(jnp.concatenate([c_half(c_re.astype(F32)[sl]), -c_half(c_im.astype(F32)[sl])], axis=0))
    lam_rows = jnp.stack([jnp.real(lam_bar).reshape(-1), jnp.imag(lam_bar).reshape(-1)], axis=0)
    return jnp.stack(bm).astype(BF16), jnp.stack(cm).astype(BF16), lam_rows


def _split_router(w):
    w = w.astype(F32)
    hi32 = lax.bitcast_convert_type(
        lax.bitcast_convert_type(w, jnp.uint32) & jnp.uint32(0xFFFF0000), F32)
    hi = hi32.astype(BF16)
    lo = (w - hi32).astype(BF16)
    pad = ((0, 0), (0, 128 - w.shape[1]))
    return jnp.concatenate([jnp.pad(hi, pad), jnp.pad(lo, pad)], axis=1)


def _block_diag(w):
    g, c, _ = w.shape
    return jnp.einsum('gcd,gk->gckd', w, jnp.eye(g, dtype=w.dtype)).reshape(g * c, g * c)


def kernel(x, p, w_in, ssm_lambda_re, ssm_lambda_im, ssm_log_step, ssm_b_re, ssm_b_im, ssm_c_re, ssm_c_im, ssm_d, w_glu_val, w_glu_gate, w_pool_group, pool_scale, w_pool_proj, w_out, ln1_g, ln1_b, w_router, b_router, w_gate, b_gate, w_up, b_up, w_down, b_down, w_ple_gate, w_ple_proj, ln2_g, ln2_b):
    bsz, seq, d_model = x.shape
    depth = w_in.shape[0]
    n_exp = w_router.shape[2]
    n_tok = bsz * seq
    alpha = (2.0 * depth) ** 0.25

    h = x
    for l in range(depth):
        bm, cm, lam_rows = _ssm_matrices(ssm_lambda_re[l], ssm_lambda_im[l], ssm_log_step[l],
                                         ssm_b_re[l], ssm_b_im[l], ssm_c_re[l], ssm_c_im[l])
        wts = {
            "w_in": w_in[l].astype(BF16), "bm": bm, "cm": cm, "lam": lam_rows,
            "dskip": ssm_d[l].reshape(1, -1).astype(F32),
            "w_glu": jnp.concatenate([w_glu_val[l], w_glu_gate[l]], axis=1).astype(BF16),
            "w_pg": _block_diag(w_pool_group[l]).astype(BF16),
            "pscale": pool_scale[l].reshape(1, -1).astype(F32),
            "w_pp": w_pool_proj[l].astype(BF16),
            "w_out": w_out[l].astype(BF16),
            "ln_g": ln1_g[l].reshape(1, -1).astype(F32),
            "ln_b": ln1_b[l].reshape(1, -1).astype(F32),
            "wr": _split_router(w_router[l]),
            "brt": jnp.broadcast_to(b_router[l].astype(F32)[:, None], (n_exp, 128)),
        }
        ple_w = (w_ple_gate[l].astype(BF16), w_ple_proj[l].astype(BF16))
        expert_w = (w_gate[l].astype(F32), b_gate[l].astype(F32)[:, None, :],
                    w_up[l].astype(F32), b_up[l].astype(F32)[:, None, :],
                    w_down[l].astype(F32), b_down[l].astype(F32)[:, None, :])
        ln2 = (ln2_g[l].reshape(1, -1).astype(F32), ln2_b[l].reshape(1, -1).astype(F32))

        chunks = seq // MIX_STEPS
        per_piece = chunks // TIME_PIECES
        cmb_per_chunk = MIX_STEPS * bsz // CMB_ROWS
        assert per_piece * TIME_PIECES == chunks and cmb_per_chunk * CMB_ROWS == MIX_STEPS * bsz
        state = jnp.zeros((bsz, 2 * bm.shape[2]), F32)
        hist = jnp.zeros((max(POOL_WINDOWS) * bsz, pool_scale.shape[1]), F32)
        out = None
        for piece in range(TIME_PIECES):
            chunk0 = piece * per_piece
            hp, dest, wt, schedule, state, hist = _mixer_call(
                h, wts, alpha, n_exp, chunk0, per_piece, state, hist)
            n_piece = hp.shape[0]
            n_rows = n_exp * n_piece + MOE_BLOCK
            xs = _sc_dispatch(hp, dest, n_rows)
            ys = _expert_call(schedule, _n_expert_blocks(n_piece, n_exp), xs, *expert_w)
            yg = _sc_gather(ys, dest.reshape(-1))
            out = _combine_call(hp, yg, jnp.transpose(wt), *ple_w, *ln2, p[l], alpha,
                                chunk0 * cmb_per_chunk, out)
        h = out
    return h
```

```python
import functools
import math

import jax
import jax.numpy as jnp
from jax import lax
from jax.experimental import pallas as pl
from jax.experimental.pallas import tpu as pltpu
from jax.experimental.pallas import tpu_sc as plsc

F32 = jnp.float32
BF16 = jnp.bfloat16

LN_EPS = 1e-5
SWIGLU_LIMIT = 7.0
SWIGLU_ALPHA = 1.702
POOL_WINDOWS = (2, 4, 8, 16)
TOP_K = 4

MIX_STEPS = 32
SCAN_LANES = 512
SCAN_SEGMENTS = 2
BACK_CHUNKS = 4
MOE_BLOCK = 1024
MOE_ROWS = 512
CMB_ROWS = 512
TIME_PIECES = 2
VMEM_LIMIT = 60 * 1024 * 1024


def _sigmoid(v):
    return 0.5 * jnp.tanh(0.5 * v) + 0.5


def _layer_norm(v, g, b):
    mu = jnp.mean(v, axis=-1, keepdims=True)
    vc = v - mu
    var = jnp.mean(vc * vc, axis=-1, keepdims=True)
    return vc * lax.rsqrt(var + LN_EPS) * g + b


def _dot(a, b):
    return jnp.dot(a, b, preferred_element_type=F32)


def _pack_rows(v):
    n = v.shape[1] // 2
    lo = lax.bitcast_convert_type(v[:, :n].astype(BF16).astype(F32), jnp.int32)
    hi = lax.bitcast_convert_type(v[:, n:].astype(BF16).astype(F32), jnp.int32)
    return hi | lax.shift_right_logical(lo, 16)


def _unpack_rows_f32(w):
    lo = lax.bitcast_convert_type(lax.shift_left(w, 16), F32)
    hi = lax.bitcast_convert_type(w & jnp.int32(-65536), F32)
    return lo, hi


def _unpack_rows(w):
    lo, hi = _unpack_rows_f32(w)
    return lo.astype(BF16), hi.astype(BF16)


SCHED_ROWS = 8
ROW_EXPERT, ROW_VALID, ROW_FIRST, ROW_NEXT, ROW_SLOT, ROW_BLOCK = range(6)


def _n_expert_blocks(n_tok, n_exp):
    return n_tok * TOP_K // MOE_BLOCK + n_exp


def _sched_lanes(n_tok, n_exp):
    return -(-_n_expert_blocks(n_tok, n_exp) // 128) * 128


def _block_schedule(counts, n_lanes, cap_blocks):
    n_exp = counts.shape[0]
    sub = lax.broadcasted_iota(jnp.int32, (n_exp, n_exp), 0)
    lane = lax.broadcasted_iota(jnp.int32, (n_exp, n_exp), 1)

    def as_row(col):
        return jnp.sum(jnp.where(sub == lane, col, 0.0), axis=0, keepdims=True)

    def running(col):
        return jnp.sum(jnp.where(lane <= sub, as_row(col), 0.0), axis=1, keepdims=True)

    blocks = jnp.floor((counts + (MOE_BLOCK - 1)) * (1.0 / MOE_BLOCK))
    ends = running(blocks)
    starts = ends - blocks
    live = counts > 0.0
    live_upto = running(jnp.where(live, 1.0, 0.0))

    step = lax.broadcasted_iota(jnp.int32, (1, n_lanes), 1).astype(F32)
    eid = lax.broadcasted_iota(jnp.int32, (n_exp, n_lanes), 0)
    expert = jnp.minimum(
        jnp.sum(jnp.where(ends <= step, 1, 0), axis=0, keepdims=True), n_exp - 1)
    mine = eid == expert

    def pick(col):
        return jnp.sum(jnp.where(mine, col, 0.0), axis=0, keepdims=True)

    used = step < jnp.max(ends, axis=0, keepdims=True)
    j = step - pick(starts)
    valid = jnp.where(used, jnp.clip(pick(counts) - j * MOE_BLOCK, 0.0, float(MOE_BLOCK)), 0.0)
    first = used & (j == 0.0)
    nxt = jnp.min(jnp.where((eid > expert) & live, eid, n_exp), axis=0, keepdims=True)
    nxt = jnp.where(nxt == n_exp, -1, nxt)
    slot = (pick(live_upto).astype(jnp.int32) + 1) & 1
    block = jnp.where(used, expert * cap_blocks + j.astype(jnp.int32), n_exp * cap_blocks)
    zero = jnp.zeros_like(expert)
    table = [zero] * SCHED_ROWS
    table[ROW_EXPERT], table[ROW_VALID], table[ROW_FIRST] = expert, valid.astype(jnp.int32), first.astype(jnp.int32)
    table[ROW_NEXT], table[ROW_SLOT], table[ROW_BLOCK] = nxt, slot, block
    return jnp.concatenate(table, axis=0)


def _time_major_copies(seq_hbm, tm_buf, sem, chunk, slot, to_hbm=False):
    steps, nb = tm_buf.shape[1], tm_buf.shape[2]
    copies = []
    for b in range(nb):
        hbm = seq_hbm.at[b, pl.ds(chunk * steps, steps), :]
        vmem = tm_buf.at[slot, :, b, :]
        src, dst = (vmem, hbm) if to_hbm else (hbm, vmem)
        copies.append(pltpu.make_async_copy(src, dst, sem.at[slot]))
    return copies


def _mixer_kernel(alpha, nb, n_exp, chunk0, n_chunks,
                  x_hbm, w_in_ref, bm_ref, cm_ref, lam_ref, dskip_ref,
                  w_glu_ref, w_pg_ref, pscale_ref, w_pp_ref, w_out_ref,
                  ln_g_ref, ln_b_ref, wr_ref, brt_ref,
                  state_in_ref, hist_in_ref,
                  h_ref, dest_ref, wts_ref, sched_ref, state_out_ref, hist_out_ref,
                  proj_ref, bu_ref, st_ref, upool_ref, state_ref, carry_ref, tri_ref, xb_ref,
                  pre_ref, x_buf, x_sem):
    i = pl.program_id(0)
    steps = x_buf.shape[1]
    rows = steps * nb
    d_model = x_buf.shape[3]

    slot = i % 2

    def fetch_copies(step, which):
        return _time_major_copies(x_hbm, x_buf, x_sem, chunk0 + step, which)

    @pl.when(i == 0)
    def _first_fetch():
        for cp in fetch_copies(0, 0):
            cp.start()

    @pl.when(i + 1 < n_chunks)
    def _next_fetch():
        for cp in fetch_copies(i + 1, 1 - slot):
            cp.start()

    @pl.when(i < n_chunks)
    def _await_fetch():
        for cp in fetch_copies(i, slot):
            cp.wait()
    ssm_w = dskip_ref.shape[1]
    pool_w = pscale_ref.shape[1]
    half_w = ssm_w // 2
    half_s = bu_ref.shape[1] // 2
    plane = half_s // 2
    halo = upool_ref.shape[0] - rows

    @pl.when(i == 0)
    def _init():
        state_ref[...] = state_in_ref[...]
        carry_ref[...] = jnp.zeros_like(carry_ref)
        upool_ref[0:halo, :] = hist_in_ref[...]
        ri = lax.broadcasted_iota(jnp.int32, (rows, rows), 0)
        ci = lax.broadcasted_iota(jnp.int32, (rows, rows), 1)
        tri_ref[...] = (ri < ci).astype(BF16)
        pre_ref[...] = jnp.zeros_like(pre_ref)

    capacity = rows * n_chunks

    chunk_rows = rows // BACK_CHUNKS

    def back_norm(c):
        r0 = c * chunk_rows
        h1 = _layer_norm(pre_ref[1, r0:r0 + chunk_rows, :], ln_g_ref[...], ln_b_ref[...])
        hb = h1.astype(BF16)
        h_ref[r0:r0 + chunk_rows, :] = _pack_rows(h1)
        return hb, (h1 - hb.astype(F32)).astype(BF16)

    def back_logits(hb, h_lo):
        pad_e = wr_ref.shape[1] // 2
        l_hi = _dot(hb, wr_ref[...])
        return l_hi[:, 0:pad_e] + l_hi[:, pad_e:] + _dot(h_lo, wr_ref[:, 0:pad_e])

    def back_topk(lg):
        lt = jnp.transpose(lg)[0:n_exp, :] + brt_ref[:, 0:1]
        eio = lax.broadcasted_iota(jnp.int32, (n_exp, chunk_rows), 0)
        vals, idxs = [], []
        for _ in range(TOP_K):
            m = jnp.max(lt, axis=0, keepdims=True)
            sel = jnp.min(jnp.where(lt == m, eio, n_exp), axis=0, keepdims=True)
            vals.append(m)
            idxs.append(sel)
            lt = jnp.where(eio == sel, -jnp.inf, lt)
        return jnp.concatenate(vals, axis=0), jnp.concatenate(idxs, axis=0)

    def back_finish(routed):
        in_range = (i > 0).astype(F32)
        vals = jnp.concatenate([v for v, _ in routed], axis=1)
        idxs = jnp.concatenate([s for _, s in routed], axis=1)
        exps = jnp.exp(vals - vals[0:1, :])
        wts_ref[...] = exps / jnp.sum(exps, axis=0, keepdims=True)

        eio = lax.broadcasted_iota(jnp.int32, (n_exp, rows), 0)
        run = carry_ref[:, 0:1]
        ranks = []
        one_hot = [(eio == idxs[k:k + 1, :]).astype(F32) for k in range(TOP_K)]
        before = _dot(jnp.concatenate(one_hot, axis=0).astype(BF16), tri_ref[...])
        for k in range(TOP_K):
            oh = one_hot[k]
            ranks.append(jnp.sum(oh * (run + before[k * n_exp:(k + 1) * n_exp, :]),
                                 axis=0, keepdims=True))
            run = run + in_range * jnp.sum(oh, axis=1, keepdims=True)
        dest_ref[...] = idxs * capacity + jnp.concatenate(ranks, axis=0).astype(jnp.int32)
        carry_ref[...] = jnp.broadcast_to(run, carry_ref.shape)

    xf = x_buf[slot].reshape(rows, d_model)
    xb_ref[...] = xf.astype(BF16)
    n_mix = ssm_w + pool_w
    normed, logits, routed = {}, {}, []

    def back_parts(tile):
        if 0 <= tile - 1 < BACK_CHUNKS:
            logits[tile - 1] = back_logits(*normed.pop(tile - 1))
        if 0 <= tile - 2 < BACK_CHUNKS:
            routed.append(back_topk(logits.pop(tile - 2)))
        if tile < BACK_CHUNKS:
            normed[tile] = back_norm(tile)

    for c in range(BACK_CHUNKS):
        c0 = c * (n_mix // BACK_CHUNKS)
        c1 = c0 + n_mix // BACK_CHUNKS
        proj_ref[:, c0:c1] = _dot(xb_ref[...], w_in_ref[:, c0:c1])
        back_parts(c)

    us = proj_ref[:, 0:ssm_w]
    usb = us.astype(BF16)
    quarter = half_s // 2
    for hf in range(2):
        for q in range(2):
            c0 = hf * half_s + q * quarter
            bu_ref[:, c0:c0 + quarter] = _dot(usb[:, hf * half_w:(hf + 1) * half_w],
                                              bm_ref[hf, :, q * quarter:(q + 1) * quarter])
            back_parts(BACK_CHUNKS + 2 * hf + q)
    back_finish(routed)

    gate_cols = w_in_ref.shape[1] - n_mix
    lane_groups = [(hf, q) for hf in range(2) for q in range(plane // SCAN_LANES)]
    segments = len(lane_groups) * SCAN_SEGMENTS
    tile = gate_cols // segments
    seg_steps = steps // SCAN_SEGMENTS
    for gi, (hf, q) in enumerate(lane_groups):
        cre = hf * half_s + q * SCAN_LANES
        cim = cre + plane
        cl = hf * plane + q * SCAN_LANES
        a_re = jnp.broadcast_to(lam_ref[0:1, cl:cl + SCAN_LANES], (nb, SCAN_LANES))
        a_im = jnp.broadcast_to(lam_ref[1:2, cl:cl + SCAN_LANES], (nb, SCAN_LANES))
        s_re = state_ref[:, cre:cre + SCAN_LANES]
        s_im = state_ref[:, cim:cim + SCAN_LANES]
        for seg in range(SCAN_SEGMENTS):
            c0 = n_mix + (gi * SCAN_SEGMENTS + seg) * tile
            proj_ref[:, c0:c0 + tile] = _dot(xb_ref[...], w_in_ref[:, c0:c0 + tile])
            for t in range(seg * seg_steps, (seg + 1) * seg_steps):
                r0 = t * nb
                b_re = bu_ref[r0:r0 + nb, cre:cre + SCAN_LANES]
                b_im = bu_ref[r0:r0 + nb, cim:cim + SCAN_LANES]
                s_re, s_im = (a_re * s_re - a_im * s_im + b_re,
                              a_re * s_im + a_im * s_re + b_im)
                st_ref[r0:r0 + nb, cre:cre + SCAN_LANES] = s_re.astype(BF16)
                st_ref[r0:r0 + nb, cim:cim + SCAN_LANES] = s_im.astype(BF16)
        state_ref[:, cre:cre + SCAN_LANES] = s_re
        state_ref[:, cim:cim + SCAN_LANES] = s_im

    y = jnp.concatenate(
        [_dot(st_ref[:, hf * half_s:(hf + 1) * half_s], cm_ref[hf]) for hf in range(2)],
        axis=1) + dskip_ref[...] * us
    z = 0.5 * y * (1.0 + jnp.tanh(math.sqrt(2.0 / math.pi) * (y + 0.044715 * (y * y * y))))
    vg = _dot(z.astype(BF16), w_glu_ref[...])
    y_ssm = vg[:, 0:d_model] * _sigmoid(vg[:, d_model:2 * d_model])

    up = proj_ref[:, ssm_w:ssm_w + pool_w]
    upool_ref[halo:halo + rows, :] = up
    t_abs = ((chunk0 + i) * steps
             + lax.broadcasted_iota(jnp.int32, (rows, 1), 0) // nb).astype(F32)
    gdim = pool_w // len(POOL_WINDOWS)
    pooled = []
    for g, w in enumerate(POOL_WINDOWS):
        cur = upool_ref[:, g * gdim:(g + 1) * gdim]
        span = 1
        while span < w:
            sh = span * nb
            cur = cur[sh:] + cur[:-sh]
            span *= 2
        win = cur[cur.shape[0] - rows:]
        inv = 1.0 / jnp.minimum(t_abs + 1.0, float(w))
        pooled.append(win * inv - up[:, g * gdim:(g + 1) * gdim])
    upool_ref[0:halo, :] = upool_ref[rows:rows + halo, :]
    pooled = jnp.concatenate(pooled, axis=1)
    mixed = _dot(pooled.astype(BF16), w_pg_ref[...])
    y_pool = _dot((mixed * pscale_ref[...]).astype(BF16), w_pp_ref[...])

    g0 = ssm_w + pool_w
    merged = (_sigmoid(proj_ref[:, g0:g0 + d_model]) * y_ssm
              + _sigmoid(proj_ref[:, g0 + d_model:g0 + 2 * d_model]) * y_pool)
    pre_ref[0] = alpha * xf + _dot(merged.astype(BF16), w_out_ref[...])

    pre_ref[1] = pre_ref[0]

    @pl.when(i == n_chunks - 1)
    def _hand_over_state():
        state_out_ref[...] = state_ref[...]
        hist_out_ref[...] = upool_ref[0:halo, :]

    @pl.when(i == n_chunks)
    def _hand_over_schedule():
        sched_ref[...] = _block_schedule(carry_ref[:, 0:1], sched_ref.shape[1],
                                         capacity // MOE_BLOCK)


def _const_spec(shape):
    zeros = (0,) * len(shape)
    return pl.BlockSpec(shape, lambda i: zeros, pipeline_mode=pl.Buffered(1))


def _mixer_call(x3, wts, alpha, n_exp, chunk0, n_chunks, state, hist):
    nb, _, d_model = x3.shape
    n_tok = nb * n_chunks * MIX_STEPS
    rows = MIX_STEPS * nb
    grid = n_chunks + 1
    state_cols = 2 * wts["bm"].shape[2]
    pool_w = wts["pscale"].shape[1]
    halo = max(POOL_WINDOWS) * nb

    def row_spec(width):
        return pl.BlockSpec((rows, width), lambda i: (jnp.maximum(i - 1, 0), 0))

    names = ["w_in", "bm", "cm", "lam", "dskip", "w_glu", "w_pg", "pscale", "w_pp",
             "w_out", "ln_g", "ln_b", "wr", "brt"]
    ops = [wts[n] for n in names]
    hbm_spec = pl.BlockSpec(memory_space=pl.ANY)
    in_specs = ([hbm_spec] + [_const_spec(o.shape) for o in ops]
                + [_const_spec(state.shape), _const_spec(hist.shape)])
    k_spec = pl.BlockSpec((TOP_K, rows), lambda i: (0, jnp.maximum(i - 1, 0)))
    sched_lanes = _sched_lanes(n_tok, n_exp)
    out_shape = (
        jax.ShapeDtypeStruct((n_tok, d_model // 2), jnp.int32),
        jax.ShapeDtypeStruct((TOP_K, n_tok), jnp.int32),
        jax.ShapeDtypeStruct((TOP_K, n_tok), F32),
        jax.ShapeDtypeStruct((SCHED_ROWS, sched_lanes), jnp.int32),
        jax.ShapeDtypeStruct(state.shape, F32),
        jax.ShapeDtypeStruct(hist.shape, F32),
    )
    out_specs = (row_spec(d_model // 2), k_spec, k_spec,
                 pl.BlockSpec((SCHED_ROWS, sched_lanes), lambda i: (0, 0)),
                 pl.BlockSpec(state.shape, lambda i: (0, 0)),
                 pl.BlockSpec(hist.shape, lambda i: (0, 0)))
    scratch = [
        pltpu.VMEM((rows, wts["w_in"].shape[1]), F32),
        pltpu.VMEM((rows, state_cols), F32),
        pltpu.VMEM((rows, state_cols), BF16),
        pltpu.VMEM((halo + rows, pool_w), F32),
        pltpu.VMEM((nb, state_cols), F32),
        pltpu.VMEM((n_exp, 128), F32),
        pltpu.VMEM((rows, rows), BF16),
        pltpu.VMEM((rows, d_model), BF16),
        pltpu.VMEM((2, rows, d_model), F32),
        pltpu.VMEM((2, MIX_STEPS, nb, d_model), F32),
        pltpu.SemaphoreType.DMA((2,)),
    ]
    return pl.pallas_call(
        functools.partial(_mixer_kernel, alpha, nb, n_exp, chunk0, n_chunks),
        grid=(grid,),
        in_specs=in_specs,
        out_specs=out_specs,
        out_shape=out_shape,
        scratch_shapes=scratch,
        compiler_params=pltpu.CompilerParams(
            dimension_semantics=("arbitrary",), vmem_limit_bytes=VMEM_LIMIT),
        name="mixer",
    )(x3, *ops, state, hist)


def _expert_kernel(sched_ref, xs_ref, wg_hbm, bg_ref, wu_hbm, bu_ref, wd_hbm, bd_ref,
                   ys_ref, wg_bf, wu_bf, wd_bf, stage, w_sem):
    i = pl.program_id(0)
    half = xs_ref.shape[1]
    expert = sched_ref[ROW_EXPERT, i]
    n_valid = sched_ref[ROW_VALID, i]
    next_expert = sched_ref[ROW_NEXT, i]

    def weight_copies(which, slot):
        return [pltpu.make_async_copy(w_hbm.at[which], stage.at[slot, m], w_sem.at[slot, m])
                for m, w_hbm in enumerate((wg_hbm, wu_hbm, wd_hbm))]

    @pl.when(sched_ref[ROW_FIRST, i] == 1)
    def _new_expert():
        slot = sched_ref[ROW_SLOT, i]

        @pl.when(i == 0)
        def _nothing_prefetched_yet():
            for cp in weight_copies(expert, slot):
                cp.start()

        @pl.when(next_expert >= 0)
        def _prefetch_next_run():
            for cp in weight_copies(next_expert, 1 - slot):
                cp.start(priority=1)

        for cp in weight_copies(expert, slot):
            cp.wait()
        wg_bf[...] = stage[slot, 0].astype(BF16)
        wu_bf[...] = stage[slot, 1].astype(BF16)
        wd_bf[...] = stage[slot, 2].astype(BF16)

    for sub in range(MOE_BLOCK // MOE_ROWS):
        r0 = sub * MOE_ROWS

        @pl.when(n_valid > r0)
        def _compute(r0=r0):
            row = r0 + lax.broadcasted_iota(jnp.int32, (MOE_ROWS, half), 0)
            x_lo, x_hi = _unpack_rows(jnp.where(row < n_valid, xs_ref[r0:r0 + MOE_ROWS, :], 0))
            g = _dot(x_lo, wg_bf[0:half, :]) + _dot(x_hi, wg_bf[half:, :]) + bg_ref[0]
            u = _dot(x_lo, wu_bf[0:half, :]) + _dot(x_hi, wu_bf[half:, :]) + bu_ref[0]
            g = jnp.minimum(g, SWIGLU_LIMIT)
            u = jnp.clip(u, -SWIGLU_LIMIT, SWIGLU_LIMIT)
            act = (u + 1.0) * (g * _sigmoid(SWIGLU_ALPHA * g))
            ys_ref[r0:r0 + MOE_ROWS, :] = _pack_rows(_dot(act.astype(BF16), wd_bf[...]) + bd_ref[0])

        @pl.when(n_valid <= r0)
        def _skip(r0=r0):
            ys_ref[r0:r0 + MOE_ROWS, :] = jnp.zeros((MOE_ROWS, half), ys_ref.dtype)


def _expert_call(schedule, n_steps, xs, wg, bg, wu, bu, wd, bd):
    n_rows = xs.shape[0]
    d_model = wg.shape[1]
    d_exp = wg.shape[2]
    assert d_model == d_exp, "the weight staging buffer assumes square expert matrices"

    def b_spec(n):
        return pl.BlockSpec((1, 1, n), lambda i, sched: (sched[ROW_EXPERT, i], 0, 0))

    row_spec = pl.BlockSpec((MOE_BLOCK, d_model // 2), lambda i, sched: (sched[ROW_BLOCK, i], 0))
    hbm_spec = pl.BlockSpec(memory_space=pl.ANY)
    grid_spec = pltpu.PrefetchScalarGridSpec(
        num_scalar_prefetch=1,
        grid=(n_steps,),
        in_specs=[row_spec, hbm_spec, b_spec(d_exp), hbm_spec, b_spec(d_exp), hbm_spec, b_spec(d_model)],
        out_specs=row_spec,
        scratch_shapes=[pltpu.VMEM((d_model, d_exp), BF16), pltpu.VMEM((d_model, d_exp), BF16),
                        pltpu.VMEM((d_exp, d_model), BF16),
                        pltpu.VMEM((2, 3, d_model, d_exp), F32),
                        pltpu.SemaphoreType.DMA((2, 3))],
    )
    return pl.pallas_call(
        _expert_kernel,
        grid_spec=grid_spec,
        out_shape=jax.ShapeDtypeStruct((n_rows, d_model // 2), jnp.int32),
        compiler_params=pltpu.CompilerParams(
            dimension_semantics=("arbitrary",), vmem_limit_bytes=VMEM_LIMIT),
        name="experts",
    )(schedule, xs, wg, bg, wu, bu, wd, bd)


def _combine_kernel(alpha, chunk0, h_ref, *refs):
    yg_refs = refs[:TOP_K]
    wt_ref, w_pleg_ref, w_plep_ref, ln_g_ref, ln_b_ref, p_hbm = refs[TOP_K:TOP_K + 6]
    o_hbm, o_buf, o_sem, p_buf, p_sem = refs[-5:]
    i = pl.program_id(0)
    last = pl.num_programs(0) - 1
    slot = i % 2
    rows = h_ref.shape[0]

    def fetch_p(step, which):
        return _time_major_copies(p_hbm, p_buf, p_sem, chunk0 + step, which)

    def write_back(step, from_slot):
        return _time_major_copies(o_hbm, o_buf, o_sem, chunk0 + step, from_slot, to_hbm=True)

    def drain(step, from_slot):
        for cp in write_back(step, from_slot):
            cp.wait()

    @pl.when(i == 0)
    def _first_fetch():
        for cp in fetch_p(0, 0):
            cp.start()

    @pl.when(i < last)
    def _next_fetch():
        for cp in fetch_p(i + 1, 1 - slot):
            cp.start()

    @pl.when(i >= 2)
    def _reuse_slot():
        drain(i - 2, slot)

    half = h_ref.shape[1]
    h_lo, h_hi = _unpack_rows_f32(h_ref[...])
    for cp in fetch_p(i, slot):
        cp.wait()
    pp = _dot(p_buf[slot].reshape(rows, p_buf.shape[3]).astype(BF16), w_plep_ref[...])
    gate = _sigmoid(_dot(h_lo.astype(BF16), w_pleg_ref[0:half, :])
                    + _dot(h_hi.astype(BF16), w_pleg_ref[half:, :]))
    ple = gate * pp
    moe = None
    wt_cols = jnp.transpose(jnp.concatenate(
        [wt_ref[...], jnp.zeros((8 - TOP_K, rows), F32)], axis=0))
    for k in range(TOP_K):
        w_bits = lax.bitcast_convert_type(wt_cols[:, k:k + 1].astype(BF16).astype(F32), jnp.int32)
        w_word = jnp.broadcast_to(w_bits | lax.shift_right_logical(w_bits, 16), (rows, half))
        term = pltpu.bitcast(yg_refs[k][...], BF16) * pltpu.bitcast(w_word, BF16)
        moe = term if moe is None else moe + term
    m_lo, m_hi = _unpack_rows_f32(pltpu.bitcast(moe, jnp.int32))
    acc_lo = alpha * h_lo + ple[:, 0:half] + m_lo
    acc_hi = alpha * h_hi + ple[:, half:] + m_hi
    n = 2.0 * half
    mu = (jnp.sum(acc_lo, axis=-1, keepdims=True) + jnp.sum(acc_hi, axis=-1, keepdims=True)) / n
    c_lo, c_hi = acc_lo - mu, acc_hi - mu
    var = (jnp.sum(c_lo * c_lo, axis=-1, keepdims=True)
           + jnp.sum(c_hi * c_hi, axis=-1, keepdims=True)) / n
    inv = lax.rsqrt(var + LN_EPS)
    steps, nb = o_buf.shape[1], o_buf.shape[2]
    o_buf[slot, :, :, 0:half] = (c_lo * inv * ln_g_ref[:, 0:half] + ln_b_ref[:, 0:half]
                                 ).reshape(steps, nb, half)
    o_buf[slot, :, :, half:] = (c_hi * inv * ln_g_ref[:, half:] + ln_b_ref[:, half:]
                                ).reshape(steps, nb, half)
    for cp in write_back(i, slot):
        cp.start()

    @pl.when(i == last)
    def _finish():
        @pl.when(i >= 1)
        def _previous():
            drain(i - 1, 1 - slot)
        drain(i, slot)


def _combine_call(hp, yg, wt, w_pleg, w_plep, ln_g, ln_b, p3, alpha, chunk0, earlier):
    n_tok = hp.shape[0]
    d_model = 2 * hp.shape[1]
    nb, seq, ple_dim = p3.shape
    grid = n_tok // CMB_ROWS
    steps = CMB_ROWS // nb
    extra = [] if earlier is None else [earlier]
    operands = [hp, *([yg] * TOP_K), wt, w_pleg, w_plep, ln_g, ln_b, p3, *extra]

    def slot_spec(k):
        return pl.BlockSpec((CMB_ROWS, d_model // 2), lambda i: (k * grid + i, 0))

    hbm_spec = pl.BlockSpec(memory_space=pl.ANY)
    return pl.pallas_call(
        functools.partial(_combine_kernel, alpha, chunk0),
        grid=(grid,),
        in_specs=[
            pl.BlockSpec((CMB_ROWS, d_model // 2), lambda i: (i, 0)),
            *[slot_spec(k) for k in range(TOP_K)],
            pl.BlockSpec((TOP_K, CMB_ROWS), lambda i: (0, i)),
            _const_spec(w_pleg.shape), _const_spec(w_plep.shape),
            pl.BlockSpec((1, d_model), lambda i: (0, 0)),
            pl.BlockSpec((1, d_model), lambda i: (0, 0)),
            hbm_spec,
        ] + [hbm_spec] * len(extra),
        out_specs=hbm_spec,
        out_shape=jax.ShapeDtypeStruct((nb, seq, d_model), F32),
        input_output_aliases={len(operands) - 1: 0} if extra else {},
        scratch_shapes=[pltpu.VMEM((2, steps, nb, d_model), F32), pltpu.SemaphoreType.DMA((2,)),
                        pltpu.VMEM((2, steps, nb, ple_dim), F32), pltpu.SemaphoreType.DMA((2,))],
        compiler_params=pltpu.CompilerParams(
            dimension_semantics=("arbitrary",), vmem_limit_bytes=VMEM_LIMIT),
        name="combine",
    )(*operands)


SC_WINDOW = 128


def _sc_mesh():
    return plsc.VectorSubcoreMesh(core_axis_name="c", subcore_axis_name="s")


def _sc_dispatch(rows, dest, n_out):
    n_tok, width = rows.shape
    n_slot = dest.shape[0]
    mesh = _sc_mesh()
    n_workers = mesh.num_cores * mesh.num_subcores
    per_worker = n_tok // n_workers
    assert per_worker * n_workers == n_tok and per_worker % SC_WINDOW == 0

    @functools.partial(
        pl.kernel, out_type=jax.ShapeDtypeStruct((n_out, width), rows.dtype), mesh=mesh,
        scratch_types=[pltpu.VMEM((n_slot, SC_WINDOW), jnp.int32),
                       pltpu.VMEM((SC_WINDOW, width), rows.dtype),
                       pltpu.SemaphoreType.DMA])
    def dispatch(x_hbm, i_hbm, o_hbm, idx_v, rows_v, sem):
        wid = lax.axis_index("s") * mesh.num_cores + lax.axis_index("c")

        @pl.loop(0, per_worker // SC_WINDOW)
        def _(j):
            base = pl.multiple_of(wid * per_worker + j * SC_WINDOW, SC_WINDOW)
            pltpu.sync_copy(i_hbm.at[:, pl.ds(base, SC_WINDOW)], idx_v)
            pltpu.sync_copy(x_hbm.at[pl.ds(base, SC_WINDOW)], rows_v)
            copies = [pltpu.async_copy(rows_v, o_hbm.at[idx_v.at[k]], sem) for k in range(n_slot)]
            for cp in copies:
                cp.wait()

    return dispatch(rows, dest)


def _sc_gather(table, index):
    n_out = index.shape[0]
    width = table.shape[1]
    mesh = _sc_mesh()
    n_workers = mesh.num_cores * mesh.num_subcores
    per_worker = n_out // n_workers
    assert per_worker * n_workers == n_out and per_worker % SC_WINDOW == 0
    half = SC_WINDOW // 2

    @functools.partial(
        pl.kernel, out_type=jax.ShapeDtypeStruct((n_out, width), table.dtype), mesh=mesh,
        scratch_types=[pltpu.VMEM((SC_WINDOW,), jnp.int32),
                       pltpu.VMEM((2, half, width), table.dtype)]
        + [pltpu.SemaphoreType.DMA] * 4)
    def gather(x_hbm, i_hbm, o_hbm, idx_v, rows_v, g0_sem, g1_sem, w0_sem, w1_sem):
        wid = lax.axis_index("s") * mesh.num_cores + lax.axis_index("c")

        @pl.loop(0, per_worker // SC_WINDOW)
        def _(j):
            base = pl.multiple_of(wid * per_worker + j * SC_WINDOW, SC_WINDOW)
            pltpu.sync_copy(i_hbm.at[pl.ds(base, SC_WINDOW)], idx_v)
            g0 = pltpu.async_copy(x_hbm.at[idx_v.at[pl.ds(0, half)]], rows_v.at[0], g0_sem)
            g1 = pltpu.async_copy(x_hbm.at[idx_v.at[pl.ds(half, half)]], rows_v.at[1], g1_sem)
            g0.wait()
            w0 = pltpu.async_copy(rows_v.at[0], o_hbm.at[pl.ds(base, half)], w0_sem)
            g1.wait()
            w1 = pltpu.async_copy(rows_v.at[1], o_hbm.at[pl.ds(base + half, half)], w1_sem)
            w0.wait()
            w1.wait()

    return gather(table, index)


def _ssm_matrices(lam_re, lam_im, log_step, b_re, b_im, c_re, c_im):
    n_grp, n_state, n_ch = b_re.shape
    lam = lax.complex(lam_re.astype(F32), lam_im.astype(F32))
    step = jnp.exp(log_step.astype(F32))[:, None]
    lam_bar = jnp.exp(lam * step)
    b_bar = ((lam_bar - 1.0) / lam)[..., None] * lax.complex(b_re.astype(F32), b_im.astype(F32))
    hg = n_grp // 2
    eye = jnp.eye(hg, dtype=F32)

    def b_half(bpart):
        return jnp.einsum('gph,gk->ghkp', bpart, eye).reshape(hg * n_ch, hg * n_state)

    def c_half(cpart):
        return jnp.einsum('ghp,gk->gpkh', cpart, eye).reshape(hg * n_state, hg * n_ch)

    bm, cm = [], []
    for hf in range(2):
        sl = slice(hf * hg, (hf + 1) * hg)
        bm.append(jnp.concatenate([b_half(jnp.real(b_bar)[sl]), b_half(jnp.imag(b_bar)[sl])], axis=1))
        cm.append(jnp.concatenate([c_half(c_re.astype(F32)[sl]), -c_half(c_im.astype(F32)[sl])], axis=0))
    lam_rows = jnp.stack([jnp.real(lam_bar).reshape(-1), jnp.imag(lam_bar).reshape(-1)], axis=0)
    return jnp.stack(bm).astype(BF16), jnp.stack(cm).astype(BF16), lam_rows


def _split_router(w):
    w = w.astype(F32)
    hi32 = lax.bitcast_convert_type(
        lax.bitcast_convert_type(w, jnp.uint32) & jnp.uint32(0xFFFF0000), F32)
    hi = hi32.astype(BF16)
    lo = (w - hi32).astype(BF16)
    pad = ((0, 0), (0, 128 - w.shape[1]))
    return jnp.concatenate([jnp.pad(hi, pad), jnp.pad(lo, pad)], axis=1)


def _block_diag(w):
    g, c, _ = w.shape
    return jnp.einsum('gcd,gk->gckd', w, jnp.eye(g, dtype=w.dtype)).reshape(g * c, g * c)


def kernel(x, p, w_in, ssm_lambda_re, ssm_lambda_im, ssm_log_step, ssm_b_re, ssm_b_im, ssm_c_re, ssm_c_im, ssm_d, w_glu_val, w_glu_gate, w_pool_group, pool_scale, w_pool_proj, w_out, ln1_g, ln1_b, w_router, b_router, w_gate, b_gate, w_up, b_up, w_down, b_down, w_ple_gate, w_ple_proj, ln2_g, ln2_b):
    bsz, seq, d_model = x.shape
    depth = w_in.shape[0]
    n_exp = w_router.shape[2]
    n_tok = bsz * seq
    alpha = (2.0 * depth) ** 0.25

    h = x
    for l in range(depth):
        bm, cm, lam_rows = _ssm_matrices(ssm_lambda_re[l], ssm_lambda_im[l], ssm_log_step[l],
                                         ssm_b_re[l], ssm_b_im[l], ssm_c_re[l], ssm_c_im[l])
        wts = {
            "w_in": w_in[l].astype(BF16), "bm": bm, "cm": cm, "lam": lam_rows,
            "dskip": ssm_d[l].reshape(1, -1).astype(F32),
            "w_glu": jnp.concatenate([w_glu_val[l], w_glu_gate[l]], axis=1).astype(BF16),
            "w_pg": _block_diag(w_pool_group[l]).astype(BF16),
            "pscale": pool_scale[l].reshape(1, -1).astype(F32),
            "w_pp": w_pool_proj[l].astype(BF16),
            "w_out": w_out[l].astype(BF16),
            "ln_g": ln1_g[l].reshape(1, -1).astype(F32),
            "ln_b": ln1_b[l].reshape(1, -1).astype(F32),
            "wr": _split_router(w_router[l]),
            "brt": jnp.broadcast_to(b_router[l].astype(F32)[:, None], (n_exp, 128)),
        }
        ple_w = (w_ple_gate[l].astype(BF16), w_ple_proj[l].astype(BF16))
        expert_w = (w_gate[l].astype(F32), b_gate[l].astype(F32)[:, None, :],
                    w_up[l].astype(F32), b_up[l].astype(F32)[:, None, :],
                    w_down[l].astype(F32), b_down[l].astype(F32)[:, None, :])
        ln2 = (ln2_g[l].reshape(1, -1).astype(F32), ln2_b[l].reshape(1, -1).astype(F32))

        chunks = seq // MIX_STEPS
        per_piece = chunks // TIME_PIECES
        cmb_per_chunk = MIX_STEPS * bsz // CMB_ROWS
        assert per_piece * TIME_PIECES == chunks and cmb_per_chunk * CMB_ROWS == MIX_STEPS * bsz
        state = jnp.zeros((bsz, 2 * bm.shape[2]), F32)
        hist = jnp.zeros((max(POOL_WINDOWS) * bsz, pool_scale.shape[1]), F32)
        out = None
        for piece in range(TIME_PIECES):
            chunk0 = piece * per_piece
            hp, dest, wt, schedule, state, hist = _mixer_call(
                h, wts, alpha, n_exp, chunk0, per_piece, state, hist)
            n_piece = hp.shape[0]
            n_rows = n_exp * n_piece + MOE_BLOCK
            xs = _sc_dispatch(hp, dest, n_rows)
            ys = _expert_call(schedule, _n_expert_blocks(n_piece, n_exp), xs, *expert_w)
            yg = _sc_gather(ys, dest.reshape(-1))
            out = _combine_call(hp, yg, wt, *ple_w, *ln2, p[l], alpha,
                                chunk0 * cmb_per_chunk, out)
        h = out
    return h
```

```python
import functools
import math

import jax
import jax.numpy as jnp
from jax import lax
from jax.experimental import pallas as pl
from jax.experimental.pallas import tpu as pltpu
from jax.experimental.pallas import tpu_sc as plsc

F32 = jnp.float32
BF16 = jnp.bfloat16

LN_EPS = 1e-5
SWIGLU_LIMIT = 7.0
SWIGLU_ALPHA = 1.702
POOL_WINDOWS = (2, 4, 8, 16)
TOP_K = 4

MIX_STEPS = 32
SCAN_LANES = 512
SCAN_SEGMENTS = 2
BACK_CHUNKS = 4
MOE_BLOCK = 2048
MOE_ROWS = 512
CMB_ROWS = 512
TIME_PIECES = 2
VMEM_LIMIT = 60 * 1024 * 1024


def _sigmoid(v):
    return 0.5 * jnp.tanh(0.5 * v) + 0.5


def _layer_norm(v, g, b):
    mu = jnp.mean(v, axis=-1, keepdims=True)
    vc = v - mu
    var = jnp.mean(vc * vc, axis=-1, keepdims=True)
    return vc * lax.rsqrt(var + LN_EPS) * g + b


def _dot(a, b):
    return jnp.dot(a, b, preferred_element_type=F32)


def _pack_rows(v):
    n = v.shape[1] // 2
    lo = lax.bitcast_convert_type(v[:, :n].astype(BF16).astype(F32), jnp.int32)
    hi = lax.bitcast_convert_type(v[:, n:].astype(BF16).astype(F32), jnp.int32)
    return hi | lax.shift_right_logical(lo, 16)


def _unpack_rows_f32(w):
    lo = lax.bitcast_convert_type(lax.shift_left(w, 16), F32)
    hi = lax.bitcast_convert_type(w & jnp.int32(-65536), F32)
    return lo, hi


def _unpack_rows(w):
    lo, hi = _unpack_rows_f32(w)
    return lo.astype(BF16), hi.astype(BF16)


SCHED_ROWS = 8
ROW_EXPERT, ROW_VALID, ROW_FIRST, ROW_NEXT, ROW_SLOT, ROW_BLOCK = range(6)


def _n_expert_blocks(n_tok, n_exp):
    return n_tok * TOP_K // MOE_BLOCK + n_exp


def _sched_lanes(n_tok, n_exp):
    return -(-_n_expert_blocks(n_tok, n_exp) // 128) * 128


def _block_schedule(counts, n_lanes, cap_blocks):
    n_exp = counts.shape[0]
    sub = lax.broadcasted_iota(jnp.int32, (n_exp, n_exp), 0)
    lane = lax.broadcasted_iota(jnp.int32, (n_exp, n_exp), 1)

    def as_row(col):
        return jnp.sum(jnp.where(sub == lane, col, 0.0), axis=0, keepdims=True)

    def running(col):
        return jnp.sum(jnp.where(lane <= sub, as_row(col), 0.0), axis=1, keepdims=True)

    blocks = jnp.floor((counts + (MOE_BLOCK - 1)) * (1.0 / MOE_BLOCK))
    ends = running(blocks)
    starts = ends - blocks
    live = counts > 0.0
    live_upto = running(jnp.where(live, 1.0, 0.0))

    step = lax.broadcasted_iota(jnp.int32, (1, n_lanes), 1).astype(F32)
    eid = lax.broadcasted_iota(jnp.int32, (n_exp, n_lanes), 0)
    expert = jnp.minimum(
        jnp.sum(jnp.where(ends <= step, 1, 0), axis=0, keepdims=True), n_exp - 1)
    mine = eid == expert

    def pick(col):
        return jnp.sum(jnp.where(mine, col, 0.0), axis=0, keepdims=True)

    used = step < jnp.max(ends, axis=0, keepdims=True)
    j = step - pick(starts)
    valid = jnp.where(used, jnp.clip(pick(counts) - j * MOE_BLOCK, 0.0, float(MOE_BLOCK)), 0.0)
    first = used & (j == 0.0)
    nxt = jnp.min(jnp.where((eid > expert) & live, eid, n_exp), axis=0, keepdims=True)
    nxt = jnp.where(nxt == n_exp, -1, nxt)
    slot = (pick(live_upto).astype(jnp.int32) + 1) & 1
    block = jnp.where(used, expert * cap_blocks + j.astype(jnp.int32), n_exp * cap_blocks)
    zero = jnp.zeros_like(expert)
    table = [zero] * SCHED_ROWS
    table[ROW_EXPERT], table[ROW_VALID], table[ROW_FIRST] = expert, valid.astype(jnp.int32), first.astype(jnp.int32)
    table[ROW_NEXT], table[ROW_SLOT], table[ROW_BLOCK] = nxt, slot, block
    return jnp.concatenate(table, axis=0)


def _time_major_copies(seq_hbm, tm_buf, sem, chunk, slot, to_hbm=False):
    steps, nb = tm_buf.shape[1], tm_buf.shape[2]
    copies = []
    for b in range(nb):
        hbm = seq_hbm.at[b, pl.ds(chunk * steps, steps), :]
        vmem = tm_buf.at[slot, :, b, :]
        src, dst = (vmem, hbm) if to_hbm else (hbm, vmem)
        copies.append(pltpu.make_async_copy(src, dst, sem.at[slot]))
    return copies


def _mixer_kernel(alpha, nb, n_exp, chunk0, n_chunks,
                  x_hbm, w_in_ref, bm_ref, cm_ref, lam_ref, dskip_ref,
                  w_glu_ref, w_pg_ref, pscale_ref, w_pp_ref, w_out_ref,
                  ln_g_ref, ln_b_ref, wr_ref, brt_ref,
                  state_in_ref, hist_in_ref,
                  h_ref, dest_ref, wts_ref, sched_ref, state_out_ref, hist_out_ref,
                  proj_ref, bu_ref, st_ref, upool_ref, state_ref, carry_ref, tri_ref, xb_ref,
                  pre_ref, x_buf, x_sem):
    i = pl.program_id(0)
    steps = x_buf.shape[1]
    rows = steps * nb
    d_model = x_buf.shape[3]

    slot = i % 2

    def fetch_copies(step, which):
        return _time_major_copies(x_hbm, x_buf, x_sem, chunk0 + step, which)

    @pl.when(i == 0)
    def _first_fetch():
        for cp in fetch_copies(0, 0):
            cp.start()

    @pl.when(i + 1 < n_chunks)
    def _next_fetch():
        for cp in fetch_copies(i + 1, 1 - slot):
            cp.start()

    @pl.when(i < n_chunks)
    def _await_fetch():
        for cp in fetch_copies(i, slot):
            cp.wait()
    ssm_w = dskip_ref.shape[1]
    pool_w = pscale_ref.shape[1]
    half_w = ssm_w // 2
    half_s = bu_ref.shape[1] // 2
    plane = half_s // 2
    halo = upool_ref.shape[0] - rows

    @pl.when(i == 0)
    def _init():
        state_ref[...] = state_in_ref[...]
        carry_ref[...] = jnp.zeros_like(carry_ref)
        upool_ref[0:halo, :] = hist_in_ref[...]
        ri = lax.broadcasted_iota(jnp.int32, (rows, rows), 0)
        ci = lax.broadcasted_iota(jnp.int32, (rows, rows), 1)
        tri_ref[...] = (ri < ci).astype(BF16)
        pre_ref[...] = jnp.zeros_like(pre_ref)

    capacity = rows * n_chunks

    chunk_rows = rows // BACK_CHUNKS

    def back_norm(c):
        r0 = c * chunk_rows
        h1 = _layer_norm(pre_ref[1, r0:r0 + chunk_rows, :], ln_g_ref[...], ln_b_ref[...])
        hb = h1.astype(BF16)
        h_ref[r0:r0 + chunk_rows, :] = _pack_rows(h1)
        return hb, (h1 - hb.astype(F32)).astype(BF16)

    def back_logits(hb, h_lo):
        pad_e = wr_ref.shape[1] // 2
        l_hi = _dot(hb, wr_ref[...])
        return l_hi[:, 0:pad_e] + l_hi[:, pad_e:] + _dot(h_lo, wr_ref[:, 0:pad_e])

    def back_topk(lg):
        lt = jnp.transpose(lg)[0:n_exp, :] + brt_ref[:, 0:1]
        eio = lax.broadcasted_iota(jnp.int32, (n_exp, chunk_rows), 0)
        vals, idxs = [], []
        for _ in range(TOP_K):
            m = jnp.max(lt, axis=0, keepdims=True)
            sel = jnp.min(jnp.where(lt == m, eio, n_exp), axis=0, keepdims=True)
            vals.append(m)
            idxs.append(sel)
            lt = jnp.where(eio == sel, -jnp.inf, lt)
        return jnp.concatenate(vals, axis=0), jnp.concatenate(idxs, axis=0)

    def back_finish(routed):
        in_range = (i > 0).astype(F32)
        vals = jnp.concatenate([v for v, _ in routed], axis=1)
        idxs = jnp.concatenate([s for _, s in routed], axis=1)
        exps = jnp.exp(vals - vals[0:1, :])
        wts_ref[...] = exps / jnp.sum(exps, axis=0, keepdims=True)

        eio = lax.broadcasted_iota(jnp.int32, (n_exp, rows), 0)
        run = carry_ref[:, 0:1]
        ranks = []
        one_hot = [(eio == idxs[k:k + 1, :]).astype(F32) for k in range(TOP_K)]
        before = _dot(jnp.concatenate(one_hot, axis=0).astype(BF16), tri_ref[...])
        for k in range(TOP_K):
            oh = one_hot[k]
            ranks.append(jnp.sum(oh * (run + before[k * n_exp:(k + 1) * n_exp, :]),
                                 axis=0, keepdims=True))
            run = run + in_range * jnp.sum(oh, axis=1, keepdims=True)
        dest_ref[...] = idxs * capacity + jnp.concatenate(ranks, axis=0).astype(jnp.int32)
        carry_ref[...] = jnp.broadcast_to(run, carry_ref.shape)

    xf = x_buf[slot].reshape(rows, d_model)
    xb_ref[...] = xf.astype(BF16)
    n_mix = ssm_w + pool_w
    normed, logits, routed = {}, {}, []

    def back_parts(tile):
        if 0 <= tile - 1 < BACK_CHUNKS:
            logits[tile - 1] = back_logits(*normed.pop(tile - 1))
        if 0 <= tile - 2 < BACK_CHUNKS:
            routed.append(back_topk(logits.pop(tile - 2)))
        if tile < BACK_CHUNKS:
            normed[tile] = back_norm(tile)

    for c in range(BACK_CHUNKS):
        c0 = c * (n_mix // BACK_CHUNKS)
        c1 = c0 + n_mix // BACK_CHUNKS
        proj_ref[:, c0:c1] = _dot(xb_ref[...], w_in_ref[:, c0:c1])
        back_parts(c)

    us = proj_ref[:, 0:ssm_w]
    usb = us.astype(BF16)
    quarter = half_s // 2
    for hf in range(2):
        for q in range(2):
            c0 = hf * half_s + q * quarter
            bu_ref[:, c0:c0 + quarter] = _dot(usb[:, hf * half_w:(hf + 1) * half_w],
                                              bm_ref[hf, :, q * quarter:(q + 1) * quarter])
            back_parts(BACK_CHUNKS + 2 * hf + q)
    back_finish(routed)

    gate_cols = w_in_ref.shape[1] - n_mix
    lane_groups = [(hf, q) for hf in range(2) for q in range(plane // SCAN_LANES)]
    segments = len(lane_groups) * SCAN_SEGMENTS
    tile = gate_cols // segments
    seg_steps = steps // SCAN_SEGMENTS
    for gi, (hf, q) in enumerate(lane_groups):
        cre = hf * half_s + q * SCAN_LANES
        cim = cre + plane
        cl = hf * plane + q * SCAN_LANES
        a_re = jnp.broadcast_to(lam_ref[0:1, cl:cl + SCAN_LANES], (nb, SCAN_LANES))
        a_im = jnp.broadcast_to(lam_ref[1:2, cl:cl + SCAN_LANES], (nb, SCAN_LANES))
        s_re = state_ref[:, cre:cre + SCAN_LANES]
        s_im = state_ref[:, cim:cim + SCAN_LANES]
        for seg in range(SCAN_SEGMENTS):
            c0 = n_mix + (gi * SCAN_SEGMENTS + seg) * tile
            proj_ref[:, c0:c0 + tile] = _dot(xb_ref[...], w_in_ref[:, c0:c0 + tile])
            for t in range(seg * seg_steps, (seg + 1) * seg_steps):
                r0 = t * nb
                b_re = bu_ref[r0:r0 + nb, cre:cre + SCAN_LANES]
                b_im = bu_ref[r0:r0 + nb, cim:cim + SCAN_LANES]
                s_re, s_im = (a_re * s_re - a_im * s_im + b_re,
                              a_re * s_im + a_im * s_re + b_im)
                st_ref[r0:r0 + nb, cre:cre + SCAN_LANES] = s_re.astype(BF16)
                st_ref[r0:r0 + nb, cim:cim + SCAN_LANES] = s_im.astype(BF16)
        state_ref[:, cre:cre + SCAN_LANES] = s_re
        state_ref[:, cim:cim + SCAN_LANES] = s_im

    y = jnp.concatenate(
        [_dot(st_ref[:, hf * half_s:(hf + 1) * half_s], cm_ref[hf]) for hf in range(2)],
        axis=1) + dskip_ref[...] * us

    up = proj_ref[:, ssm_w:ssm_w + pool_w]
    upool_ref[halo:halo + rows, :] = up
    t_abs = ((chunk0 + i) * steps
             + lax.broadcasted_iota(jnp.int32, (rows, 1), 0) // nb).astype(F32)
    gdim = pool_w // len(POOL_WINDOWS)
    pooled = []
    for g, w in enumerate(POOL_WINDOWS):
        cur = upool_ref[:, g * gdim:(g + 1) * gdim]
        span = 1
        while span < w:
            sh = span * nb
            cur = cur[sh:] + cur[:-sh]
            span *= 2
        win = cur[cur.shape[0] - rows:]
        inv = 1.0 / jnp.minimum(t_abs + 1.0, float(w))
        pooled.append(win * inv - up[:, g * gdim:(g + 1) * gdim])
    upool_ref[0:halo, :] = upool_ref[rows:rows + halo, :]
    pooled = jnp.concatenate(pooled, axis=1)
    mixed = _dot(pooled.astype(BF16), w_pg_ref[...])
    z = 0.5 * y * (1.0 + jnp.tanh(math.sqrt(2.0 / math.pi) * (y + 0.044715 * (y * y * y))))
    y_pool = _dot((mixed * pscale_ref[...]).astype(BF16), w_pp_ref[...])
    vg = _dot(z.astype(BF16), w_glu_ref[...])
    y_ssm = vg[:, 0:d_model] * _sigmoid(vg[:, d_model:2 * d_model])

    g0 = ssm_w + pool_w
    merged = (_sigmoid(proj_ref[:, g0:g0 + d_model]) * y_ssm
              + _sigmoid(proj_ref[:, g0 + d_model:g0 + 2 * d_model]) * y_pool)
    pre_ref[0] = alpha * xf + _dot(merged.astype(BF16), w_out_ref[...])

    pre_ref[1] = pre_ref[0]

    @pl.when(i == n_chunks - 1)
    def _hand_over_state():
        state_out_ref[...] = state_ref[...]
        hist_out_ref[...] = upool_ref[0:halo, :]

    @pl.when(i == n_chunks)
    def _hand_over_schedule():
        sched_ref[...] = _block_schedule(carry_ref[:, 0:1], sched_ref.shape[1],
                                         capacity // MOE_BLOCK)


def _const_spec(shape):
    zeros = (0,) * len(shape)
    return pl.BlockSpec(shape, lambda i: zeros, pipeline_mode=pl.Buffered(1))


def _mixer_call(x3, wts, alpha, n_exp, chunk0, n_chunks, state, hist):
    nb, _, d_model = x3.shape
    n_tok = nb * n_chunks * MIX_STEPS
    assert n_tok % MOE_BLOCK == 0, "an expert's row range must be a whole number of blocks"
    rows = MIX_STEPS * nb
    grid = n_chunks + 1
    state_cols = 2 * wts["bm"].shape[2]
    pool_w = wts["pscale"].shape[1]
    halo = max(POOL_WINDOWS) * nb

    def row_spec(width):
        return pl.BlockSpec((rows, width), lambda i: (jnp.maximum(i - 1, 0), 0))

    names = ["w_in", "bm", "cm", "lam", "dskip", "w_glu", "w_pg", "pscale", "w_pp",
             "w_out", "ln_g", "ln_b", "wr", "brt"]
    ops = [wts[n] for n in names]
    hbm_spec = pl.BlockSpec(memory_space=pl.ANY)
    in_specs = ([hbm_spec] + [_const_spec(o.shape) for o in ops]
                + [_const_spec(state.shape), _const_spec(hist.shape)])
    k_spec = pl.BlockSpec((TOP_K, rows), lambda i: (0, jnp.maximum(i - 1, 0)))
    sched_lanes = _sched_lanes(n_tok, n_exp)
    out_shape = (
        jax.ShapeDtypeStruct((n_tok, d_model // 2), jnp.int32),
        jax.ShapeDtypeStruct((TOP_K, n_tok), jnp.int32),
        jax.ShapeDtypeStruct((TOP_K, n_tok), F32),
        jax.ShapeDtypeStruct((SCHED_ROWS, sched_lanes), jnp.int32),
        jax.ShapeDtypeStruct(state.shape, F32),
        jax.ShapeDtypeStruct(hist.shape, F32),
    )
    out_specs = (row_spec(d_model // 2), k_spec, k_spec,
                 pl.BlockSpec((SCHED_ROWS, sched_lanes), lambda i: (0, 0)),
                 pl.BlockSpec(state.shape, lambda i: (0, 0)),
                 pl.BlockSpec(hist.shape, lambda i: (0, 0)))
    scratch = [
        pltpu.VMEM((rows, wts["w_in"].shape[1]), F32),
        pltpu.VMEM((rows, state_cols), F32),
        pltpu.VMEM((rows, state_cols), BF16),
        pltpu.VMEM((halo + rows, pool_w), F32),
        pltpu.VMEM((nb, state_cols), F32),
        pltpu.VMEM((n_exp, 128), F32),
        pltpu.VMEM((rows, rows), BF16),
        pltpu.VMEM((rows, d_model), BF16),
        pltpu.VMEM((2, rows, d_model), F32),
        pltpu.VMEM((2, MIX_STEPS, nb, d_model), F32),
        pltpu.SemaphoreType.DMA((2,)),
    ]
    return pl.pallas_call(
        functools.partial(_mixer_kernel, alpha, nb, n_exp, chunk0, n_chunks),
        grid=(grid,),
        in_specs=in_specs,
        out_specs=out_specs,
        out_shape=out_shape,
        scratch_shapes=scratch,
        compiler_params=pltpu.CompilerParams(
            dimension_semantics=("arbitrary",), vmem_limit_bytes=VMEM_LIMIT),
        name="mixer",
    )(x3, *ops, state, hist)


def _expert_kernel(sched_ref, xs_ref, wg_hbm, bg_ref, wu_hbm, bu_ref, wd_hbm, bd_ref,
                   ys_ref, wg_bf, wu_bf, wd_bf, stage, w_sem):
    i = pl.program_id(0)
    half = xs_ref.shape[1]
    expert = sched_ref[ROW_EXPERT, i]
    n_valid = sched_ref[ROW_VALID, i]
    next_expert = sched_ref[ROW_NEXT, i]

    def weight_copies(which, slot):
        return [pltpu.make_async_copy(w_hbm.at[which], stage.at[slot, m], w_sem.at[slot, m])
                for m, w_hbm in enumerate((wg_hbm, wu_hbm, wd_hbm))]

    @pl.when(sched_ref[ROW_FIRST, i] == 1)
    def _new_expert():
        slot = sched_ref[ROW_SLOT, i]

        @pl.when(i == 0)
        def _nothing_prefetched_yet():
            for cp in weight_copies(expert, slot):
                cp.start()

        @pl.when(next_expert >= 0)
        def _prefetch_next_run():
            for cp in weight_copies(next_expert, 1 - slot):
                cp.start(priority=1)

        for cp in weight_copies(expert, slot):
            cp.wait()
        wg_bf[...] = stage[slot, 0].astype(BF16)
        wu_bf[...] = stage[slot, 1].astype(BF16)
        wd_bf[...] = stage[slot, 2].astype(BF16)

    for sub in range(MOE_BLOCK // MOE_ROWS):
        r0 = sub * MOE_ROWS

        @pl.when(n_valid > r0)
        def _compute(r0=r0):
            row = r0 + lax.broadcasted_iota(jnp.int32, (MOE_ROWS, half), 0)
            x_lo, x_hi = _unpack_rows(jnp.where(row < n_valid, xs_ref[r0:r0 + MOE_ROWS, :], 0))
            g = _dot(x_lo, wg_bf[0:half, :]) + _dot(x_hi, wg_bf[half:, :]) + bg_ref[0]
            u = _dot(x_lo, wu_bf[0:half, :]) + _dot(x_hi, wu_bf[half:, :]) + bu_ref[0]
            g = jnp.minimum(g, SWIGLU_LIMIT)
            u = jnp.clip(u, -SWIGLU_LIMIT, SWIGLU_LIMIT)
            act = (u + 1.0) * (g * _sigmoid(SWIGLU_ALPHA * g))
            ys_ref[r0:r0 + MOE_ROWS, :] = _pack_rows(_dot(act.astype(BF16), wd_bf[...]) + bd_ref[0])

        @pl.when(n_valid <= r0)
        def _skip(r0=r0):
            ys_ref[r0:r0 + MOE_ROWS, :] = jnp.zeros((MOE_ROWS, half), ys_ref.dtype)


def _expert_call(schedule, n_steps, xs, wg, bg, wu, bu, wd, bd):
    n_rows = xs.shape[0]
    d_model = wg.shape[1]
    d_exp = wg.shape[2]
    assert d_model == d_exp, "the weight staging buffer assumes square expert matrices"

    def b_spec(n):
        return pl.BlockSpec((1, 1, n), lambda i, sched: (sched[ROW_EXPERT, i], 0, 0))

    row_spec = pl.BlockSpec((MOE_BLOCK, d_model // 2), lambda i, sched: (sched[ROW_BLOCK, i], 0))
    hbm_spec = pl.BlockSpec(memory_space=pl.ANY)
    grid_spec = pltpu.PrefetchScalarGridSpec(
        num_scalar_prefetch=1,
        grid=(n_steps,),
        in_specs=[row_spec, hbm_spec, b_spec(d_exp), hbm_spec, b_spec(d_exp), hbm_spec, b_spec(d_model)],
        out_specs=row_spec,
        scratch_shapes=[pltpu.VMEM((d_model, d_exp), BF16), pltpu.VMEM((d_model, d_exp), BF16),
                        pltpu.VMEM((d_exp, d_model), BF16),
                        pltpu.VMEM((2, 3, d_model, d_exp), F32),
                        pltpu.SemaphoreType.DMA((2, 3))],
    )
    return pl.pallas_call(
        _expert_kernel,
        grid_spec=grid_spec,
        out_shape=jax.ShapeDtypeStruct((n_rows, d_model // 2), jnp.int32),
        compiler_params=pltpu.CompilerParams(
            dimension_semantics=("arbitrary",), vmem_limit_bytes=VMEM_LIMIT),
        name="experts",
    )(schedule, xs, wg, bg, wu, bu, wd, bd)


def _combine_kernel(alpha, chunk0, h_ref, *refs):
    yg_refs = refs[:TOP_K]
    wt_ref, w_pleg_ref, w_plep_ref, ln_g_ref, ln_b_ref, p_hbm = refs[TOP_K:TOP_K + 6]
    o_hbm, o_buf, o_sem, p_buf, p_sem = refs[-5:]
    i = pl.program_id(0)
    last = pl.num_programs(0) - 1
    slot = i % 2
    rows = h_ref.shape[0]

    def fetch_p(step, which):
        return _time_major_copies(p_hbm, p_buf, p_sem, chunk0 + step, which)

    def write_back(step, from_slot):
        return _time_major_copies(o_hbm, o_buf, o_sem, chunk0 + step, from_slot, to_hbm=True)

    def drain(step, from_slot):
        for cp in write_back(step, from_slot):
            cp.wait()

    @pl.when(i == 0)
    def _first_fetch():
        for cp in fetch_p(0, 0):
            cp.start()

    @pl.when(i < last)
    def _next_fetch():
        for cp in fetch_p(i + 1, 1 - slot):
            cp.start()

    @pl.when(i >= 2)
    def _reuse_slot():
        drain(i - 2, slot)

    half = h_ref.shape[1]
    h_lo, h_hi = _unpack_rows_f32(h_ref[...])
    for cp in fetch_p(i, slot):
        cp.wait()
    pp = _dot(p_buf[slot].reshape(rows, p_buf.shape[3]).astype(BF16), w_plep_ref[...])
    gate = _sigmoid(_dot(h_lo.astype(BF16), w_pleg_ref[0:half, :])
                    + _dot(h_hi.astype(BF16), w_pleg_ref[half:, :]))
    ple = gate * pp
    moe = None
    wt_cols = jnp.transpose(jnp.concatenate(
        [wt_ref[...], jnp.zeros((8 - TOP_K, rows), F32)], axis=0))
    for k in range(TOP_K):
        w_bits = lax.bitcast_convert_type(wt_cols[:, k:k + 1].astype(BF16).astype(F32), jnp.int32)
        w_word = jnp.broadcast_to(w_bits | lax.shift_right_logical(w_bits, 16), (rows, half))
        term = pltpu.bitcast(yg_refs[k][...], BF16) * pltpu.bitcast(w_word, BF16)
        moe = term if moe is None else moe + term
    m_lo, m_hi = _unpack_rows_f32(pltpu.bitcast(moe, jnp.int32))
    acc_lo = alpha * h_lo + ple[:, 0:half] + m_lo
    acc_hi = alpha * h_hi + ple[:, half:] + m_hi
    n = 2.0 * half
    mu = (jnp.sum(acc_lo, axis=-1, keepdims=True) + jnp.sum(acc_hi, axis=-1, keepdims=True)) / n
    c_lo, c_hi = acc_lo - mu, acc_hi - mu
    var = (jnp.sum(c_lo * c_lo, axis=-1, keepdims=True)
           + jnp.sum(c_hi * c_hi, axis=-1, keepdims=True)) / n
    inv = lax.rsqrt(var + LN_EPS)
    steps, nb = o_buf.shape[1], o_buf.shape[2]
    o_buf[slot, :, :, 0:half] = (c_lo * inv * ln_g_ref[:, 0:half] + ln_b_ref[:, 0:half]
                                 ).reshape(steps, nb, half)
    o_buf[slot, :, :, half:] = (c_hi * inv * ln_g_ref[:, half:] + ln_b_ref[:, half:]
                                ).reshape(steps, nb, half)
    for cp in write_back(i, slot):
        cp.start()

    @pl.when(i == last)
    def _finish():
        @pl.when(i >= 1)
        def _previous():
            drain(i - 1, 1 - slot)
        drain(i, slot)


def _combine_call(hp, yg, wt, w_pleg, w_plep, ln_g, ln_b, p3, alpha, chunk0, earlier):
    n_tok = hp.shape[0]
    d_model = 2 * hp.shape[1]
    nb, seq, ple_dim = p3.shape
    grid = n_tok // CMB_ROWS
    steps = CMB_ROWS // nb
    extra = [] if earlier is None else [earlier]
    operands = [hp, *([yg] * TOP_K), wt, w_pleg, w_plep, ln_g, ln_b, p3, *extra]

    def slot_spec(k):
        return pl.BlockSpec((CMB_ROWS, d_model // 2), lambda i: (k * grid + i, 0))

    hbm_spec = pl.BlockSpec(memory_space=pl.ANY)
    return pl.pallas_call(
        functools.partial(_combine_kernel, alpha, chunk0),
        grid=(grid,),
        in_specs=[
            pl.BlockSpec((CMB_ROWS, d_model // 2), lambda i: (i, 0)),
            *[slot_spec(k) for k in range(TOP_K)],
            pl.BlockSpec((TOP_K, CMB_ROWS), lambda i: (0, i)),
            _const_spec(w_pleg.shape), _const_spec(w_plep.shape),
            pl.BlockSpec((1, d_model), lambda i: (0, 0)),
            pl.BlockSpec((1, d_model), lambda i: (0, 0)),
            hbm_spec,
        ] + [hbm_spec] * len(extra),
        out_specs=hbm_spec,
        out_shape=jax.ShapeDtypeStruct((nb, seq, d_model), F32),
        input_output_aliases={len(operands) - 1: 0} if extra else {},
        scratch_shapes=[pltpu.VMEM((2, steps, nb, d_model), F32), pltpu.SemaphoreType.DMA((2,)),
                        pltpu.VMEM((2, steps, nb, ple_dim), F32), pltpu.SemaphoreType.DMA((2,))],
        compiler_params=pltpu.CompilerParams(
            dimension_semantics=("arbitrary",), vmem_limit_bytes=VMEM_LIMIT),
        name="combine",
    )(*operands)


SC_WINDOW = 128


def _sc_mesh():
    return plsc.VectorSubcoreMesh(core_axis_name="c", subcore_axis_name="s")


def _sc_dispatch(rows, dest, n_out):
    n_tok, width = rows.shape
    n_slot = dest.shape[0]
    mesh = _sc_mesh()
    n_workers = mesh.num_cores * mesh.num_subcores
    per_worker = n_tok // n_workers
    assert per_worker * n_workers == n_tok and per_worker % SC_WINDOW == 0

    @functools.partial(
        pl.kernel, out_type=jax.ShapeDtypeStruct((n_out, width), rows.dtype), mesh=mesh,
        scratch_types=[pltpu.VMEM((n_slot, SC_WINDOW), jnp.int32),
                       pltpu.VMEM((SC_WINDOW, width), rows.dtype),
                       pltpu.SemaphoreType.DMA])
    def dispatch(x_hbm, i_hbm, o_hbm, idx_v, rows_v, sem):
        wid = lax.axis_index("s") * mesh.num_cores + lax.axis_index("c")

        @pl.loop(0, per_worker // SC_WINDOW)
        def _(j):
            base = pl.multiple_of(wid * per_worker + j * SC_WINDOW, SC_WINDOW)
            pltpu.sync_copy(i_hbm.at[:, pl.ds(base, SC_WINDOW)], idx_v)
            pltpu.sync_copy(x_hbm.at[pl.ds(base, SC_WINDOW)], rows_v)
            copies = [pltpu.async_copy(rows_v, o_hbm.at[idx_v.at[k]], sem) for k in range(n_slot)]
            for cp in copies:
                cp.wait()

    return dispatch(rows, dest)


def _sc_gather(table, index):
    n_out = index.shape[0]
    width = table.shape[1]
    mesh = _sc_mesh()
    n_workers = mesh.num_cores * mesh.num_subcores
    per_worker = n_out // n_workers
    assert per_worker * n_workers == n_out and per_worker % SC_WINDOW == 0
    half = SC_WINDOW // 2

    @functools.partial(
        pl.kernel, out_type=jax.ShapeDtypeStruct((n_out, width), table.dtype), mesh=mesh,
        scratch_types=[pltpu.VMEM((SC_WINDOW,), jnp.int32),
                       pltpu.VMEM((2, half, width), table.dtype)]
        + [pltpu.SemaphoreType.DMA] * 4)
    def gather(x_hbm, i_hbm, o_hbm, idx_v, rows_v, g0_sem, g1_sem, w0_sem, w1_sem):
        wid = lax.axis_index("s") * mesh.num_cores + lax.axis_index("c")

        @pl.loop(0, per_worker // SC_WINDOW)
        def _(j):
            base = pl.multiple_of(wid * per_worker + j * SC_WINDOW, SC_WINDOW)
            pltpu.sync_copy(i_hbm.at[pl.ds(base, SC_WINDOW)], idx_v)
            g0 = pltpu.async_copy(x_hbm.at[idx_v.at[pl.ds(0, half)]], rows_v.at[0], g0_sem)
            g1 = pltpu.async_copy(x_hbm.at[idx_v.at[pl.ds(half, half)]], rows_v.at[1], g1_sem)
            g0.wait()
            w0 = pltpu.async_copy(rows_v.at[0], o_hbm.at[pl.ds(base, half)], w0_sem)
            g1.wait()
            w1 = pltpu.async_copy(rows_v.at[1], o_hbm.at[pl.ds(base + half, half)], w1_sem)
            w0.wait()
            w1.wait()

    return gather(table, index)


def _ssm_matrices(lam_re, lam_im, log_step, b_re, b_im, c_re, c_im):
    n_grp, n_state, n_ch = b_re.shape
    lam = lax.complex(lam_re.astype(F32), lam_im.astype(F32))
    step = jnp.exp(log_step.astype(F32))[:, None]
    lam_bar = jnp.exp(lam * step)
    b_bar = ((lam_bar - 1.0) / lam)[..., None] * lax.complex(b_re.astype(F32), b_im.astype(F32))
    hg = n_grp // 2
    eye = jnp.eye(hg, dtype=F32)

    def b_half(bpart):
        return jnp.einsum('gph,gk->ghkp', bpart, eye).reshape(hg * n_ch, hg * n_state)

    def c_half(cpart):
        return jnp.einsum('ghp,gk->gpkh', cpart, eye).reshape(hg * n_state, hg * n_ch)

    bm, cm = [], []
    for hf in range(2):
        sl = slice(hf * hg, (hf + 1) * hg)
        bm.append(jnp.concatenate([b_half(jnp.real(b_bar)[sl]), b_half(jnp.imag(b_bar)[sl])], axis=1))
        cm.append(jnp.concatenate([c_half(c_re.astype(F32)[sl]), -c_half(c_im.astype(F32)[sl])], axis=0))
    lam_rows = jnp.stack([jnp.real(lam_bar).reshape(-1), jnp.imag(lam_bar).reshape(-1)], axis=0)
    return jnp.stack(bm).astype(BF16), jnp.stack(cm).astype(BF16), lam_rows


def _split_router(w):
    w = w.astype(F32)
    hi32 = lax.bitcast_convert_type(
        lax.bitcast_convert_type(w, jnp.uint32) & jnp.uint32(0xFFFF0000), F32)
    hi = hi32.astype(BF16)
    lo = (w - hi32).astype(BF16)
    pad = ((0, 0), (0, 128 - w.shape[1]))
    return jnp.concatenate([jnp.pad(hi, pad), jnp.pad(lo, pad)], axis=1)


def _block_diag(w):
    g, c, _ = w.shape
    return jnp.einsum('gcd,gk->gckd', w, jnp.eye(g, dtype=w.dtype)).reshape(g * c, g * c)


def kernel(x, p, w_in, ssm_lambda_re, ssm_lambda_im, ssm_log_step, ssm_b_re, ssm_b_im, ssm_c_re, ssm_c_im, ssm_d, w_glu_val, w_glu_gate, w_pool_group, pool_scale, w_pool_proj, w_out, ln1_g, ln1_b, w_router, b_router, w_gate, b_gate, w_up, b_up, w_down, b_down, w_ple_gate, w_ple_proj, ln2_g, ln2_b):
    bsz, seq, d_model = x.shape
    depth = w_in.shape[0]
    n_exp = w_router.shape[2]
    n_tok = bsz * seq
    alpha = (2.0 * depth) ** 0.25

    h = x
    for l in range(depth):
        bm, cm, lam_rows = _ssm_matrices(ssm_lambda_re[l], ssm_lambda_im[l], ssm_log_step[l],
                                         ssm_b_re[l], ssm_b_im[l], ssm_c_re[l], ssm_c_im[l])
        wts = {
            "w_in": w_in[l].astype(BF16), "bm": bm, "cm": cm, "lam": lam_rows,
            "dskip": ssm_d[l].reshape(1, -1).astype(F32),
            "w_glu": jnp.concatenate([w_glu_val[l], w_glu_gate[l]], axis=1).astype(BF16),
            "w_pg": _block_diag(w_pool_group[l]).astype(BF16),
            "pscale": pool_scale[l].reshape(1, -1).astype(F32),
            "w_pp": w_pool_proj[l].astype(BF16),
            "w_out": w_out[l].astype(BF16),
            "ln_g": ln1_g[l].reshape(1, -1).astype(F32),
            "ln_b": ln1_b[l].reshape(1, -1).astype(F32),
            "wr": _split_router(w_router[l]),
            "brt": jnp.broadcast_to(b_router[l].astype(F32)[:, None], (n_exp, 128)),
        }
        ple_w = (w_ple_gate[l].astype(BF16), w_ple_proj[l].astype(BF16))
        expert_w = (w_gate[l].astype(F32), b_gate[l].astype(F32)[:, None, :],
                    w_up[l].astype(F32), b_up[l].astype(F32)[:, None, :],
                    w_down[l].astype(F32), b_down[l].astype(F32)[:, None, :])
        ln2 = (ln2_g[l].reshape(1, -1).astype(F32), ln2_b[l].reshape(1, -1).astype(F32))

        chunks = seq // MIX_STEPS
        per_piece = chunks // TIME_PIECES
        cmb_per_chunk = MIX_STEPS * bsz // CMB_ROWS
        assert per_piece * TIME_PIECES == chunks and cmb_per_chunk * CMB_ROWS == MIX_STEPS * bsz
        state = jnp.zeros((bsz, 2 * bm.shape[2]), F32)
        hist = jnp.zeros((max(POOL_WINDOWS) * bsz, pool_scale.shape[1]), F32)
        out = None
        for piece in range(TIME_PIECES):
            chunk0 = piece * per_piece
            hp, dest, wt, schedule, state, hist = _mixer_call(
                h, wts, alpha, n_exp, chunk0, per_piece, state, hist)
            n_piece = hp.shape[0]
            n_rows = n_exp * n_piece + MOE_BLOCK
            xs = _sc_dispatch(hp, dest, n_rows)
            ys = _expert_call(schedule, _n_expert_blocks(n_piece, n_exp), xs, *expert_w)
            yg = _sc_gather(ys, dest.reshape(-1))
            out = _combine_call(hp, yg, wt, *ple_w, *ln2, p[l], alpha,
                                chunk0 * cmb_per_chunk, out)
        h = out
    return h
```

```python
import functools
import math

import jax
import jax.numpy as jnp
from jax import lax
from jax.experimental import pallas as pl
from jax.experimental.pallas import tpu as pltpu
from jax.experimental.pallas import tpu_sc as plsc

F32 = jnp.float32
BF16 = jnp.bfloat16

LN_EPS = 1e-5
SWIGLU_LIMIT = 7.0
SWIGLU_ALPHA = 1.702
POOL_WINDOWS = (2, 4, 8, 16)
TOP_K = 4

MIX_STEPS = 32
SCAN_LANES = 512
SCAN_SEGMENTS = 2
BACK_CHUNKS = 4
MOE_BLOCK = 1024
MOE_ROWS = 512
CMB_ROWS = 512
TIME_PIECES = (5, 3)
VMEM_LIMIT = 60 * 1024 * 1024


def _sigmoid(v):
    return 0.5 * jnp.tanh(0.5 * v) + 0.5


def _layer_norm(v, g, b):
    mu = jnp.mean(v, axis=-1, keepdims=True)
    vc = v - mu
    var = jnp.mean(vc * vc, axis=-1, keepdims=True)
    return vc * lax.rsqrt(var + LN_EPS) * g + b


def _dot(a, b):
    return jnp.dot(a, b, preferred_element_type=F32)


def _pack_rows(v):
    n = v.shape[1] // 2
    lo = lax.bitcast_convert_type(v[:, :n].astype(BF16).astype(F32), jnp.int32)
    hi = lax.bitcast_convert_type(v[:, n:].astype(BF16).astype(F32), jnp.int32)
    return hi | lax.shift_right_logical(lo, 16)


def _unpack_rows_f32(w):
    lo = lax.bitcast_convert_type(lax.shift_left(w, 16), F32)
    hi = lax.bitcast_convert_type(w & jnp.int32(-65536), F32)
    return lo, hi


def _unpack_rows(w):
    lo, hi = _unpack_rows_f32(w)
    return lo.astype(BF16), hi.astype(BF16)


SCHED_ROWS = 8
ROW_EXPERT, ROW_VALID, ROW_FIRST, ROW_NEXT, ROW_SLOT, ROW_BLOCK = range(6)


def _n_expert_blocks(n_tok, n_exp):
    return n_tok * TOP_K // MOE_BLOCK + n_exp


def _sched_lanes(n_tok, n_exp):
    return -(-_n_expert_blocks(n_tok, n_exp) // 128) * 128


def _block_schedule(counts, n_lanes, cap_blocks):
    n_exp = counts.shape[0]
    sub = lax.broadcasted_iota(jnp.int32, (n_exp, n_exp), 0)
    lane = lax.broadcasted_iota(jnp.int32, (n_exp, n_exp), 1)

    def as_row(col):
        return jnp.sum(jnp.where(sub == lane, col, 0.0), axis=0, keepdims=True)

    def running(col):
        return jnp.sum(jnp.where(lane <= sub, as_row(col), 0.0), axis=1, keepdims=True)

    blocks = jnp.floor((counts + (MOE_BLOCK - 1)) * (1.0 / MOE_BLOCK))
    ends = running(blocks)
    starts = ends - blocks
    live = counts > 0.0
    live_upto = running(jnp.where(live, 1.0, 0.0))

    step = lax.broadcasted_iota(jnp.int32, (1, n_lanes), 1).astype(F32)
    eid = lax.broadcasted_iota(jnp.int32, (n_exp, n_lanes), 0)
    expert = jnp.minimum(
        jnp.sum(jnp.where(ends <= step, 1, 0), axis=0, keepdims=True), n_exp - 1)
    mine = eid == expert

    def pick(col):
        return jnp.sum(jnp.where(mine, col, 0.0), axis=0, keepdims=True)

    used = step < jnp.max(ends, axis=0, keepdims=True)
    j = step - pick(starts)
    valid = jnp.where(used, jnp.clip(pick(counts) - j * MOE_BLOCK, 0.0, float(MOE_BLOCK)), 0.0)
    first = used & (j == 0.0)
    nxt = jnp.min(jnp.where((eid > expert) & live, eid, n_exp), axis=0, keepdims=True)
    nxt = jnp.where(nxt == n_exp, -1, nxt)
    slot = (pick(live_upto).astype(jnp.int32) + 1) & 1
    block = jnp.where(used, expert * cap_blocks + j.astype(jnp.int32), n_exp * cap_blocks)
    zero = jnp.zeros_like(expert)
    table = [zero] * SCHED_ROWS
    table[ROW_EXPERT], table[ROW_VALID], table[ROW_FIRST] = expert, valid.astype(jnp.int32), first.astype(jnp.int32)
    table[ROW_NEXT], table[ROW_SLOT], table[ROW_BLOCK] = nxt, slot, block
    return jnp.concatenate(table, axis=0)


def _time_major_copies(seq_hbm, tm_buf, sem, chunk, slot, to_hbm=False):
    steps, nb = tm_buf.shape[1], tm_buf.shape[2]
    copies = []
    for b in range(nb):
        hbm = seq_hbm.at[b, pl.ds(chunk * steps, steps), :]
        vmem = tm_buf.at[slot, :, b, :]
        src, dst = (vmem, hbm) if to_hbm else (hbm, vmem)
        copies.append(pltpu.make_async_copy(src, dst, sem.at[slot]))
    return copies


def _mixer_kernel(alpha, nb, n_exp, chunk0, n_chunks,
                  x_hbm, w_in_ref, bm_ref, cm_ref, lam_ref, dskip_ref,
                  w_glu_ref, w_pg_ref, pscale_ref, w_pp_ref, w_out_ref,
                  ln_g_ref, ln_b_ref, wr_ref, brt_ref,
                  state_in_ref, hist_in_ref,
                  h_ref, dest_ref, wts_ref, sched_ref, state_out_ref, hist_out_ref,
                  proj_ref, bu_ref, st_ref, upool_ref, state_ref, carry_ref, tri_ref, xb_ref,
                  pre_ref, x_buf, x_sem):
    i = pl.program_id(0)
    steps = x_buf.shape[1]
    rows = steps * nb
    d_model = x_buf.shape[3]

    slot = i % 2

    def fetch_copies(step, which):
        return _time_major_copies(x_hbm, x_buf, x_sem, chunk0 + step, which)

    @pl.when(i == 0)
    def _first_fetch():
        for cp in fetch_copies(0, 0):
            cp.start()

    @pl.when(i + 1 < n_chunks)
    def _next_fetch():
        for cp in fetch_copies(i + 1, 1 - slot):
            cp.start()

    @pl.when(i < n_chunks)
    def _await_fetch():
        for cp in fetch_copies(i, slot):
            cp.wait()
    ssm_w = dskip_ref.shape[1]
    pool_w = pscale_ref.shape[1]
    half_w = ssm_w // 2
    half_s = bu_ref.shape[1] // 2
    plane = half_s // 2
    halo = upool_ref.shape[0] - rows

    @pl.when(i == 0)
    def _init():
        state_ref[...] = state_in_ref[...]
        carry_ref[...] = jnp.zeros_like(carry_ref)
        upool_ref[0:halo, :] = hist_in_ref[...]
        ri = lax.broadcasted_iota(jnp.int32, (rows, rows), 0)
        ci = lax.broadcasted_iota(jnp.int32, (rows, rows), 1)
        tri_ref[...] = (ri < ci).astype(BF16)
        pre_ref[...] = jnp.zeros_like(pre_ref)

    capacity = rows * n_chunks

    chunk_rows = rows // BACK_CHUNKS

    def back_norm(c):
        r0 = c * chunk_rows
        h1 = _layer_norm(pre_ref[1, r0:r0 + chunk_rows, :], ln_g_ref[...], ln_b_ref[...])
        hb = h1.astype(BF16)
        h_ref[r0:r0 + chunk_rows, :] = _pack_rows(h1)
        return hb, (h1 - hb.astype(F32)).astype(BF16)

    def back_logits(hb, h_lo):
        pad_e = wr_ref.shape[1] // 2
        l_hi = _dot(hb, wr_ref[...])
        return l_hi[:, 0:pad_e] + l_hi[:, pad_e:] + _dot(h_lo, wr_ref[:, 0:pad_e])

    def back_topk(lg):
        lt = jnp.transpose(lg)[0:n_exp, :] + brt_ref[:, 0:1]
        eio = lax.broadcasted_iota(jnp.int32, (n_exp, chunk_rows), 0)
        vals, idxs = [], []
        for _ in range(TOP_K):
            m = jnp.max(lt, axis=0, keepdims=True)
            sel = jnp.min(jnp.where(lt == m, eio, n_exp), axis=0, keepdims=True)
            vals.append(m)
            idxs.append(sel)
            lt = jnp.where(eio == sel, -jnp.inf, lt)
        return jnp.concatenate(vals, axis=0), jnp.concatenate(idxs, axis=0)

    def back_finish(routed):
        in_range = (i > 0).astype(F32)
        vals = jnp.concatenate([v for v, _ in routed], axis=1)
        idxs = jnp.concatenate([s for _, s in routed], axis=1)
        exps = jnp.exp(vals - vals[0:1, :])
        wts_ref[...] = exps / jnp.sum(exps, axis=0, keepdims=True)

        eio = lax.broadcasted_iota(jnp.int32, (n_exp, rows), 0)
        run = carry_ref[:, 0:1]
        ranks = []
        one_hot = [(eio == idxs[k:k + 1, :]).astype(F32) for k in range(TOP_K)]
        before = _dot(jnp.concatenate(one_hot, axis=0).astype(BF16), tri_ref[...])
        for k in range(TOP_K):
            oh = one_hot[k]
            ranks.append(jnp.sum(oh * (run + before[k * n_exp:(k + 1) * n_exp, :]),
                                 axis=0, keepdims=True))
            run = run + in_range * jnp.sum(oh, axis=1, keepdims=True)
        dest_ref[...] = idxs * capacity + jnp.concatenate(ranks, axis=0).astype(jnp.int32)
        carry_ref[...] = jnp.broadcast_to(run, carry_ref.shape)

    xf = x_buf[slot].reshape(rows, d_model)
    xb_ref[...] = xf.astype(BF16)
    n_mix = ssm_w + pool_w
    normed, logits, routed = {}, {}, []

    def back_parts(tile):
        if 0 <= tile - 1 < BACK_CHUNKS:
            logits[tile - 1] = back_logits(*normed.pop(tile - 1))
        if 0 <= tile - 2 < BACK_CHUNKS:
            routed.append(back_topk(logits.pop(tile - 2)))
        if tile < BACK_CHUNKS:
            normed[tile] = back_norm(tile)

    for c in range(BACK_CHUNKS):
        c0 = c * (n_mix // BACK_CHUNKS)
        c1 = c0 + n_mix // BACK_CHUNKS
        proj_ref[:, c0:c1] = _dot(xb_ref[...], w_in_ref[:, c0:c1])
        back_parts(c)

    us = proj_ref[:, 0:ssm_w]
    usb = us.astype(BF16)
    quarter = half_s // 2
    for hf in range(2):
        for q in range(2):
            c0 = hf * half_s + q * quarter
            bu_ref[:, c0:c0 + quarter] = _dot(usb[:, hf * half_w:(hf + 1) * half_w],
                                              bm_ref[hf, :, q * quarter:(q + 1) * quarter])
            back_parts(BACK_CHUNKS + 2 * hf + q)
    back_finish(routed)

    gate_cols = w_in_ref.shape[1] - n_mix
    lane_groups = [(hf, q) for hf in range(2) for q in range(plane // SCAN_LANES)]
    segments = len(lane_groups) * SCAN_SEGMENTS
    tile = gate_cols // segments
    seg_steps = steps // SCAN_SEGMENTS
    for gi, (hf, q) in enumerate(lane_groups):
        cre = hf * half_s + q * SCAN_LANES
        cim = cre + plane
        cl = hf * plane + q * SCAN_LANES
        a_re = jnp.broadcast_to(lam_ref[0:1, cl:cl + SCAN_LANES], (nb, SCAN_LANES))
        a_im = jnp.broadcast_to(lam_ref[1:2, cl:cl + SCAN_LANES], (nb, SCAN_LANES))
        s_re = state_ref[:, cre:cre + SCAN_LANES]
        s_im = state_ref[:, cim:cim + SCAN_LANES]
        for seg in range(SCAN_SEGMENTS):
            c0 = n_mix + (gi * SCAN_SEGMENTS + seg) * tile
            proj_ref[:, c0:c0 + tile] = _dot(xb_ref[...], w_in_ref[:, c0:c0 + tile])
            for t in range(seg * seg_steps, (seg + 1) * seg_steps):
                r0 = t * nb
                b_re = bu_ref[r0:r0 + nb, cre:cre + SCAN_LANES]
                b_im = bu_ref[r0:r0 + nb, cim:cim + SCAN_LANES]
                s_re, s_im = (a_re * s_re - a_im * s_im + b_re,
                              a_re * s_im + a_im * s_re + b_im)
                st_ref[r0:r0 + nb, cre:cre + SCAN_LANES] = s_re.astype(BF16)
                st_ref[r0:r0 + nb, cim:cim + SCAN_LANES] = s_im.astype(BF16)
        state_ref[:, cre:cre + SCAN_LANES] = s_re
        state_ref[:, cim:cim + SCAN_LANES] = s_im

    y = jnp.concatenate(
        [_dot(st_ref[:, hf * half_s:(hf + 1) * half_s], cm_ref[hf]) for hf in range(2)],
        axis=1) + dskip_ref[...] * us

    up = proj_ref[:, ssm_w:ssm_w + pool_w]
    upool_ref[halo:halo + rows, :] = up
    t_abs = ((chunk0 + i) * steps
             + lax.broadcasted_iota(jnp.int32, (rows, 1), 0) // nb).astype(F32)
    gdim = pool_w // len(POOL_WINDOWS)
    pooled = []
    for g, w in enumerate(POOL_WINDOWS):
        cur = upool_ref[:, g * gdim:(g + 1) * gdim]
        span = 1
        while span < w:
            sh = span * nb
            cur = cur[sh:] + cur[:-sh]
            span *= 2
        win = cur[cur.shape[0] - rows:]
        inv = 1.0 / jnp.minimum(t_abs + 1.0, float(w))
        pooled.append(win * inv - up[:, g * gdim:(g + 1) * gdim])
    upool_ref[0:halo, :] = upool_ref[rows:rows + halo, :]
    pooled = jnp.concatenate(pooled, axis=1)
    mixed = _dot(pooled.astype(BF16), w_pg_ref[...])
    z = 0.5 * y * (1.0 + jnp.tanh(math.sqrt(2.0 / math.pi) * (y + 0.044715 * (y * y * y))))
    y_pool = _dot((mixed * pscale_ref[...]).astype(BF16), w_pp_ref[...])
    vg = _dot(z.astype(BF16), w_glu_ref[...])
    y_ssm = vg[:, 0:d_model] * _sigmoid(vg[:, d_model:2 * d_model])

    g0 = ssm_w + pool_w
    merged = (_sigmoid(proj_ref[:, g0:g0 + d_model]) * y_ssm
              + _sigmoid(proj_ref[:, g0 + d_model:g0 + 2 * d_model]) * y_pool)
    pre_ref[0] = alpha * xf + _dot(merged.astype(BF16), w_out_ref[...])

    pre_ref[1] = pre_ref[0]

    @pl.when(i == n_chunks - 1)
    def _hand_over_state():
        state_out_ref[...] = state_ref[...]
        hist_out_ref[...] = upool_ref[0:halo, :]

    @pl.when(i == n_chunks)
    def _hand_over_schedule():
        sched_ref[...] = _block_schedule(carry_ref[:, 0:1], sched_ref.shape[1],
                                         capacity // MOE_BLOCK)


def _const_spec(shape):
    zeros = (0,) * len(shape)
    return pl.BlockSpec(shape, lambda i: zeros, pipeline_mode=pl.Buffered(1))


def _mixer_call(x3, wts, alpha, n_exp, chunk0, n_chunks, state, hist):
    nb, _, d_model = x3.shape
    n_tok = nb * n_chunks * MIX_STEPS
    assert n_tok % MOE_BLOCK == 0, "an expert's row range must be a whole number of blocks"
    rows = MIX_STEPS * nb
    grid = n_chunks + 1
    state_cols = 2 * wts["bm"].shape[2]
    pool_w = wts["pscale"].shape[1]
    halo = max(POOL_WINDOWS) * nb

    def row_spec(width):
        return pl.BlockSpec((rows, width), lambda i: (jnp.maximum(i - 1, 0), 0))

    names = ["w_in", "bm", "cm", "lam", "dskip", "w_glu", "w_pg", "pscale", "w_pp",
             "w_out", "ln_g", "ln_b", "wr", "brt"]
    ops = [wts[n] for n in names]
    hbm_spec = pl.BlockSpec(memory_space=pl.ANY)
    in_specs = ([hbm_spec] + [_const_spec(o.shape) for o in ops]
                + [_const_spec(state.shape), _const_spec(hist.shape)])
    k_spec = pl.BlockSpec((TOP_K, rows), lambda i: (0, jnp.maximum(i - 1, 0)))
    sched_lanes = _sched_lanes(n_tok, n_exp)
    out_shape = (
        jax.ShapeDtypeStruct((n_tok, d_model // 2), jnp.int32),
        jax.ShapeDtypeStruct((TOP_K, n_tok), jnp.int32),
        jax.ShapeDtypeStruct((TOP_K, n_tok), F32),
        jax.ShapeDtypeStruct((SCHED_ROWS, sched_lanes), jnp.int32),
        jax.ShapeDtypeStruct(state.shape, F32),
        jax.ShapeDtypeStruct(hist.shape, F32),
    )
    out_specs = (row_spec(d_model // 2), k_spec, k_spec,
                 pl.BlockSpec((SCHED_ROWS, sched_lanes), lambda i: (0, 0)),
                 pl.BlockSpec(state.shape, lambda i: (0, 0)),
                 pl.BlockSpec(hist.shape, lambda i: (0, 0)))
    scratch = [
        pltpu.VMEM((rows, wts["w_in"].shape[1]), F32),
        pltpu.VMEM((rows, state_cols), F32),
        pltpu.VMEM((rows, state_cols), BF16),
        pltpu.VMEM((halo + rows, pool_w), F32),
        pltpu.VMEM((nb, state_cols), F32),
        pltpu.VMEM((n_exp, 128), F32),
        pltpu.VMEM((rows, rows), BF16),
        pltpu.VMEM((rows, d_model), BF16),
        pltpu.VMEM((2, rows, d_model), F32),
        pltpu.VMEM((2, MIX_STEPS, nb, d_model), F32),
        pltpu.SemaphoreType.DMA((2,)),
    ]
    return pl.pallas_call(
        functools.partial(_mixer_kernel, alpha, nb, n_exp, chunk0, n_chunks),
        grid=(grid,),
        in_specs=in_specs,
        out_specs=out_specs,
        out_shape=out_shape,
        scratch_shapes=scratch,
        compiler_params=pltpu.CompilerParams(
            dimension_semantics=("arbitrary",), vmem_limit_bytes=VMEM_LIMIT),
        name="mixer",
    )(x3, *ops, state, hist)


def _expert_kernel(sched_ref, xs_ref, wg_hbm, bg_ref, wu_hbm, bu_ref, wd_hbm, bd_ref,
                   ys_ref, wg_bf, wu_bf, wd_bf, stage, w_sem):
    i = pl.program_id(0)
    half = xs_ref.shape[1]
    expert = sched_ref[ROW_EXPERT, i]
    n_valid = sched_ref[ROW_VALID, i]
    next_expert = sched_ref[ROW_NEXT, i]

    def weight_copies(which, slot):
        return [pltpu.make_async_copy(w_hbm.at[which], stage.at[slot, m], w_sem.at[slot, m])
                for m, w_hbm in enumerate((wg_hbm, wu_hbm, wd_hbm))]

    @pl.when(sched_ref[ROW_FIRST, i] == 1)
    def _new_expert():
        slot = sched_ref[ROW_SLOT, i]

        @pl.when(i == 0)
        def _nothing_prefetched_yet():
            for cp in weight_copies(expert, slot):
                cp.start()

        @pl.when(next_expert >= 0)
        def _prefetch_next_run():
            for cp in weight_copies(next_expert, 1 - slot):
                cp.start(priority=1)

        for cp in weight_copies(expert, slot):
            cp.wait()
        wg_bf[...] = stage[slot, 0].astype(BF16)
        wu_bf[...] = stage[slot, 1].astype(BF16)
        wd_bf[...] = stage[slot, 2].astype(BF16)

    for sub in range(MOE_BLOCK // MOE_ROWS):
        r0 = sub * MOE_ROWS

        @pl.when(n_valid > r0)
        def _compute(r0=r0):
            row = r0 + lax.broadcasted_iota(jnp.int32, (MOE_ROWS, half), 0)
            x_lo, x_hi = _unpack_rows(jnp.where(row < n_valid, xs_ref[r0:r0 + MOE_ROWS, :], 0))
            g = _dot(x_lo, wg_bf[0:half, :]) + _dot(x_hi, wg_bf[half:, :]) + bg_ref[0]
            u = _dot(x_lo, wu_bf[0:half, :]) + _dot(x_hi, wu_bf[half:, :]) + bu_ref[0]
            g = jnp.minimum(g, SWIGLU_LIMIT)
            u = jnp.clip(u, -SWIGLU_LIMIT, SWIGLU_LIMIT)
            act = (u + 1.0) * (g * _sigmoid(SWIGLU_ALPHA * g))
            ys_ref[r0:r0 + MOE_ROWS, :] = _pack_rows(_dot(act.astype(BF16), wd_bf[...]) + bd_ref[0])

        @pl.when(n_valid <= r0)
        def _skip(r0=r0):
            ys_ref[r0:r0 + MOE_ROWS, :] = jnp.zeros((MOE_ROWS, half), ys_ref.dtype)


def _expert_call(schedule, n_steps, xs, wg, bg, wu, bu, wd, bd):
    n_rows = xs.shape[0]
    d_model = wg.shape[1]
    d_exp = wg.shape[2]
    assert d_model == d_exp, "the weight staging buffer assumes square expert matrices"

    def b_spec(n):
        return pl.BlockSpec((1, 1, n), lambda i, sched: (sched[ROW_EXPERT, i], 0, 0))

    row_spec = pl.BlockSpec((MOE_BLOCK, d_model // 2), lambda i, sched: (sched[ROW_BLOCK, i], 0))
    hbm_spec = pl.BlockSpec(memory_space=pl.ANY)
    grid_spec = pltpu.PrefetchScalarGridSpec(
        num_scalar_prefetch=1,
        grid=(n_steps,),
        in_specs=[row_spec, hbm_spec, b_spec(d_exp), hbm_spec, b_spec(d_exp), hbm_spec, b_spec(d_model)],
        out_specs=row_spec,
        scratch_shapes=[pltpu.VMEM((d_model, d_exp), BF16), pltpu.VMEM((d_model, d_exp), BF16),
                        pltpu.VMEM((d_exp, d_model), BF16),
                        pltpu.VMEM((2, 3, d_model, d_exp), F32),
                        pltpu.SemaphoreType.DMA((2, 3))],
    )
    return pl.pallas_call(
        _expert_kernel,
        grid_spec=grid_spec,
        out_shape=jax.ShapeDtypeStruct((n_rows, d_model // 2), jnp.int32),
        compiler_params=pltpu.CompilerParams(
            dimension_semantics=("arbitrary",), vmem_limit_bytes=VMEM_LIMIT),
        name="experts",
    )(schedule, xs, wg, bg, wu, bu, wd, bd)


def _combine_kernel(alpha, chunk0, h_ref, *refs):
    yg_refs = refs[:TOP_K]
    wt_ref, w_pleg_ref, w_plep_ref, ln_g_ref, ln_b_ref, p_hbm = refs[TOP_K:TOP_K + 6]
    o_hbm, o_buf, o_sem, p_buf, p_sem = refs[-5:]
    i = pl.program_id(0)
    last = pl.num_programs(0) - 1
    slot = i % 2
    rows = h_ref.shape[0]

    def fetch_p(step, which):
        return _time_major_copies(p_hbm, p_buf, p_sem, chunk0 + step, which)

    def write_back(step, from_slot):
        return _time_major_copies(o_hbm, o_buf, o_sem, chunk0 + step, from_slot, to_hbm=True)

    def drain(step, from_slot):
        for cp in write_back(step, from_slot):
            cp.wait()

    @pl.when(i == 0)
    def _first_fetch():
        for cp in fetch_p(0, 0):
            cp.start()

    @pl.when(i < last)
    def _next_fetch():
        for cp in fetch_p(i + 1, 1 - slot):
            cp.start()

    @pl.when(i >= 2)
    def _reuse_slot():
        drain(i - 2, slot)

    half = h_ref.shape[1]
    h_lo, h_hi = _unpack_rows_f32(h_ref[...])
    for cp in fetch_p(i, slot):
        cp.wait()
    pp = _dot(p_buf[slot].reshape(rows, p_buf.shape[3]).astype(BF16), w_plep_ref[...])
    gate = _sigmoid(_dot(h_lo.astype(BF16), w_pleg_ref[0:half, :])
                    + _dot(h_hi.astype(BF16), w_pleg_ref[half:, :]))
    ple = gate * pp
    moe = None
    wt_cols = jnp.transpose(jnp.concatenate(
        [wt_ref[...], jnp.zeros((8 - TOP_K, rows), F32)], axis=0))
    for k in range(TOP_K):
        w_bits = lax.bitcast_convert_type(wt_cols[:, k:k + 1].astype(BF16).astype(F32), jnp.int32)
        w_word = jnp.broadcast_to(w_bits | lax.shift_right_logical(w_bits, 16), (rows, half))
        term = pltpu.bitcast(yg_refs[k][...], BF16) * pltpu.bitcast(w_word, BF16)
        moe = term if moe is None else moe + term
    m_lo, m_hi = _unpack_rows_f32(pltpu.bitcast(moe, jnp.int32))
    acc_lo = alpha * h_lo + ple[:, 0:half] + m_lo
    acc_hi = alpha * h_hi + ple[:, half:] + m_hi
    n = 2.0 * half
    mu = (jnp.sum(acc_lo, axis=-1, keepdims=True) + jnp.sum(acc_hi, axis=-1, keepdims=True)) / n
    c_lo, c_hi = acc_lo - mu, acc_hi - mu
    var = (jnp.sum(c_lo * c_lo, axis=-1, keepdims=True)
           + jnp.sum(c_hi * c_hi, axis=-1, keepdims=True)) / n
    inv = lax.rsqrt(var + LN_EPS)
    steps, nb = o_buf.shape[1], o_buf.shape[2]
    o_buf[slot, :, :, 0:half] = (c_lo * inv * ln_g_ref[:, 0:half] + ln_b_ref[:, 0:half]
                                 ).reshape(steps, nb, half)
    o_buf[slot, :, :, half:] = (c_hi * inv * ln_g_ref[:, half:] + ln_b_ref[:, half:]
                                ).reshape(steps, nb, half)
    for cp in write_back(i, slot):
        cp.start()

    @pl.when(i == last)
    def _finish():
        @pl.when(i >= 1)
        def _previous():
            drain(i - 1, 1 - slot)
        drain(i, slot)


def _combine_call(hp, yg, wt, w_pleg, w_plep, ln_g, ln_b, p3, alpha, chunk0, earlier):
    n_tok = hp.shape[0]
    d_model = 2 * hp.shape[1]
    nb, seq, ple_dim = p3.shape
    grid = n_tok // CMB_ROWS
    steps = CMB_ROWS // nb
    extra = [] if earlier is None else [earlier]
    operands = [hp, *([yg] * TOP_K), wt, w_pleg, w_plep, ln_g, ln_b, p3, *extra]

    def slot_spec(k):
        return pl.BlockSpec((CMB_ROWS, d_model // 2), lambda i: (k * grid + i, 0))

    hbm_spec = pl.BlockSpec(memory_space=pl.ANY)
    return pl.pallas_call(
        functools.partial(_combine_kernel, alpha, chunk0),
        grid=(grid,),
        in_specs=[
            pl.BlockSpec((CMB_ROWS, d_model // 2), lambda i: (i, 0)),
            *[slot_spec(k) for k in range(TOP_K)],
            pl.BlockSpec((TOP_K, CMB_ROWS), lambda i: (0, i)),
            _const_spec(w_pleg.shape), _const_spec(w_plep.shape),
            pl.BlockSpec((1, d_model), lambda i: (0, 0)),
            pl.BlockSpec((1, d_model), lambda i: (0, 0)),
            hbm_spec,
        ] + [hbm_spec] * len(extra),
        out_specs=hbm_spec,
        out_shape=jax.ShapeDtypeStruct((nb, seq, d_model), F32),
        input_output_aliases={len(operands) - 1: 0} if extra else {},
        scratch_shapes=[pltpu.VMEM((2, steps, nb, d_model), F32), pltpu.SemaphoreType.DMA((2,)),
                        pltpu.VMEM((2, steps, nb, ple_dim), F32), pltpu.SemaphoreType.DMA((2,))],
        compiler_params=pltpu.CompilerParams(
            dimension_semantics=("arbitrary",), vmem_limit_bytes=VMEM_LIMIT),
        name="combine",
    )(*operands)


SC_WINDOW = 128


def _sc_mesh():
    return plsc.VectorSubcoreMesh(core_axis_name="c", subcore_axis_name="s")


def _sc_dispatch(rows, dest, n_out):
    n_tok, width = rows.shape
    n_slot = dest.shape[0]
    mesh = _sc_mesh()
    n_workers = mesh.num_cores * mesh.num_subcores
    per_worker = n_tok // n_workers
    assert per_worker * n_workers == n_tok and per_worker % SC_WINDOW == 0

    @functools.partial(
        pl.kernel, out_type=jax.ShapeDtypeStruct((n_out, width), rows.dtype), mesh=mesh,
        scratch_types=[pltpu.VMEM((n_slot, SC_WINDOW), jnp.int32),
                       pltpu.VMEM((SC_WINDOW, width), rows.dtype),
                       pltpu.SemaphoreType.DMA])
    def dispatch(x_hbm, i_hbm, o_hbm, idx_v, rows_v, sem):
        wid = lax.axis_index("s") * mesh.num_cores + lax.axis_index("c")

        @pl.loop(0, per_worker // SC_WINDOW)
        def _(j):
            base = pl.multiple_of(wid * per_worker + j * SC_WINDOW, SC_WINDOW)
            pltpu.sync_copy(i_hbm.at[:, pl.ds(base, SC_WINDOW)], idx_v)
            pltpu.sync_copy(x_hbm.at[pl.ds(base, SC_WINDOW)], rows_v)
            copies = [pltpu.async_copy(rows_v, o_hbm.at[idx_v.at[k]], sem) for k in range(n_slot)]
            for cp in copies:
                cp.wait()

    return dispatch(rows, dest)


def _sc_gather(table, index):
    n_out = index.shape[0]
    width = table.shape[1]
    mesh = _sc_mesh()
    n_workers = mesh.num_cores * mesh.num_subcores
    per_worker = n_out // n_workers
    assert per_worker * n_workers == n_out and per_worker % SC_WINDOW == 0
    half = SC_WINDOW // 2

    @functools.partial(
        pl.kernel, out_type=jax.ShapeDtypeStruct((n_out, width), table.dtype), mesh=mesh,
        scratch_types=[pltpu.VMEM((SC_WINDOW,), jnp.int32),
                       pltpu.VMEM((2, half, width), table.dtype)]
        + [pltpu.SemaphoreType.DMA] * 4)
    def gather(x_hbm, i_hbm, o_hbm, idx_v, rows_v, g0_sem, g1_sem, w0_sem, w1_sem):
        wid = lax.axis_index("s") * mesh.num_cores + lax.axis_index("c")

        @pl.loop(0, per_worker // SC_WINDOW)
        def _(j):
            base = pl.multiple_of(wid * per_worker + j * SC_WINDOW, SC_WINDOW)
            pltpu.sync_copy(i_hbm.at[pl.ds(base, SC_WINDOW)], idx_v)
            g0 = pltpu.async_copy(x_hbm.at[idx_v.at[pl.ds(0, half)]], rows_v.at[0], g0_sem)
            g1 = pltpu.async_copy(x_hbm.at[idx_v.at[pl.ds(half, half)]], rows_v.at[1], g1_sem)
            g0.wait()
            w0 = pltpu.async_copy(rows_v.at[0], o_hbm.at[pl.ds(base, half)], w0_sem)
            g1.wait()
            w1 = pltpu.async_copy(rows_v.at[1], o_hbm.at[pl.ds(base + half, half)], w1_sem)
            w0.wait()
            w1.wait()

    return gather(table, index)


def _ssm_matrices(lam_re, lam_im, log_step, b_re, b_im, c_re, c_im):
    n_grp, n_state, n_ch = b_re.shape
    lam = lax.complex(lam_re.astype(F32), lam_im.astype(F32))
    step = jnp.exp(log_step.astype(F32))[:, None]
    lam_bar = jnp.exp(lam * step)
    b_bar = ((lam_bar - 1.0) / lam)[..., None] * lax.complex(b_re.astype(F32), b_im.astype(F32))
    hg = n_grp // 2
    eye = jnp.eye(hg, dtype=F32)

    def b_half(bpart):
        return jnp.einsum('gph,gk->ghkp', bpart, eye).reshape(hg * n_ch, hg * n_state)

    def c_half(cpart):
        return jnp.einsum('ghp,gk->gpkh', cpart, eye).reshape(hg * n_state, hg * n_ch)

    bm, cm = [], []
    for hf in range(2):
        sl = slice(hf * hg, (hf + 1) * hg)
        bm.append(jnp.concatenate([b_half(jnp.real(b_bar)[sl]), b_half(jnp.imag(b_bar)[sl])], axis=1))
        cm.append(jnp.concatenate([c_half(c_re.astype(F32)[sl]), -c_half(c_im.astype(F32)[sl])], axis=0))
    lam_rows = jnp.stack([jnp.real(lam_bar).reshape(-1), jnp.imag(lam_bar).reshape(-1)], axis=0)
    return jnp.stack(bm).astype(BF16), jnp.stack(cm).astype(BF16), lam_rows


def _split_router(w):
    w = w.astype(F32)
    hi32 = lax.bitcast_convert_type(
        lax.bitcast_convert_type(w, jnp.uint32) & jnp.uint32(0xFFFF0000), F32)
    hi = hi32.astype(BF16)
    lo = (w - hi32).astype(BF16)
    pad = ((0, 0), (0, 128 - w.shape[1]))
    return jnp.concatenate([jnp.pad(hi, pad), jnp.pad(lo, pad)], axis=1)


def _block_diag(w):
    g, c, _ = w.shape
    return jnp.einsum('gcd,gk->gckd', w, jnp.eye(g, dtype=w.dtype)).reshape(g * c, g * c)


def kernel(x, p, w_in, ssm_lambda_re, ssm_lambda_im, ssm_log_step, ssm_b_re, ssm_b_im, ssm_c_re, ssm_c_im, ssm_d, w_glu_val, w_glu_gate, w_pool_group, pool_scale, w_pool_proj, w_out, ln1_g, ln1_b, w_router, b_router, w_gate, b_gate, w_up, b_up, w_down, b_down, w_ple_gate, w_ple_proj, ln2_g, ln2_b):
    bsz, seq, _ = x.shape
    depth = w_in.shape[0]
    n_exp = w_router.shape[2]
    alpha = (2.0 * depth) ** 0.25

    h = x
    for l in range(depth):
        bm, cm, lam_rows = _ssm_matrices(ssm_lambda_re[l], ssm_lambda_im[l], ssm_log_step[l],
                                         ssm_b_re[l], ssm_b_im[l], ssm_c_re[l], ssm_c_im[l])
        wts = {
            "w_in": w_in[l].astype(BF16), "bm": bm, "cm": cm, "lam": lam_rows,
            "dskip": ssm_d[l].reshape(1, -1).astype(F32),
            "w_glu": jnp.concatenate([w_glu_val[l], w_glu_gate[l]], axis=1).astype(BF16),
            "w_pg": _block_diag(w_pool_group[l]).astype(BF16),
            "pscale": pool_scale[l].reshape(1, -1).astype(F32),
            "w_pp": w_pool_proj[l].astype(BF16),
            "w_out": w_out[l].astype(BF16),
            "ln_g": ln1_g[l].reshape(1, -1).astype(F32),
            "ln_b": ln1_b[l].reshape(1, -1).astype(F32),
            "wr": _split_router(w_router[l]),
            "brt": jnp.broadcast_to(b_router[l].astype(F32)[:, None], (n_exp, 128)),
        }
        ple_w = (w_ple_gate[l].astype(BF16), w_ple_proj[l].astype(BF16))
        expert_w = (w_gate[l].astype(F32), b_gate[l].astype(F32)[:, None, :],
                    w_up[l].astype(F32), b_up[l].astype(F32)[:, None, :],
                    w_down[l].astype(F32), b_down[l].astype(F32)[:, None, :])
        ln2 = (ln2_g[l].reshape(1, -1).astype(F32), ln2_b[l].reshape(1, -1).astype(F32))

        chunks = seq // MIX_STEPS
        piece_chunks = [chunks * share // sum(TIME_PIECES) for share in TIME_PIECES]
        cmb_per_chunk = MIX_STEPS * bsz // CMB_ROWS
        assert sum(piece_chunks) == chunks and cmb_per_chunk * CMB_ROWS == MIX_STEPS * bsz
        state = jnp.zeros((bsz, 2 * bm.shape[2]), F32)
        hist = jnp.zeros((max(POOL_WINDOWS) * bsz, pool_scale.shape[1]), F32)
        out = None
        chunk0 = 0
        for per_piece in piece_chunks:
            hp, dest, wt, schedule, state, hist = _mixer_call(
                h, wts, alpha, n_exp, chunk0, per_piece, state, hist)
            n_piece = hp.shape[0]
            n_rows = n_exp * n_piece + MOE_BLOCK
            xs = _sc_dispatch(hp, dest, n_rows)
            ys = _expert_call(schedule, _n_expert_blocks(n_piece, n_exp), xs, *expert_w)
            yg = _sc_gather(ys, dest.reshape(-1))
            out = _combine_call(hp, yg, wt, *ple_w, *ln2, p[l], alpha,
                                chunk0 * cmb_per_chunk, out)
            chunk0 += per_piece
        h = out
    return h
```

```python
import functools
import math

import jax
import jax.numpy as jnp
from jax import lax
from jax.experimental import pallas as pl
from jax.experimental.pallas import tpu as pltpu
from jax.experimental.pallas import tpu_sc as plsc

F32 = jnp.float32
BF16 = jnp.bfloat16

LN_EPS = 1e-5
SWIGLU_LIMIT = 7.0
SWIGLU_ALPHA = 1.702
POOL_WINDOWS = (2, 4, 8, 16)
TOP_K = 4

MIX_STEPS = 32
SCAN_LANES = 512
SCAN_SEGMENTS = 2
BACK_CHUNKS = 4
MOE_BLOCK = 1024
MOE_ROWS = 512
CMB_ROWS = 512
TIME_PIECES = (3, 1)
VMEM_LIMIT = 60 * 1024 * 1024


def _sigmoid(v):
    return 0.5 * jnp.tanh(0.5 * v) + 0.5


def _layer_norm(v, g, b):
    mu = jnp.mean(v, axis=-1, keepdims=True)
    vc = v - mu
    var = jnp.mean(vc * vc, axis=-1, keepdims=True)
    return vc * lax.rsqrt(var + LN_EPS) * g + b


def _dot(a, b):
    return jnp.dot(a, b, preferred_element_type=F32)


def _pack_rows(v):
    n = v.shape[1] // 2
    lo = lax.bitcast_convert_type(v[:, :n].astype(BF16).astype(F32), jnp.int32)
    hi = lax.bitcast_convert_type(v[:, n:].astype(BF16).astype(F32), jnp.int32)
    return hi | lax.shift_right_logical(lo, 16)


def _unpack_rows_f32(w):
    lo = lax.bitcast_convert_type(lax.shift_left(w, 16), F32)
    hi = lax.bitcast_convert_type(w & jnp.int32(-65536), F32)
    return lo, hi


def _unpack_rows(w):
    lo, hi = _unpack_rows_f32(w)
    return lo.astype(BF16), hi.astype(BF16)


SCHED_ROWS = 8
ROW_EXPERT, ROW_VALID, ROW_FIRST, ROW_NEXT, ROW_SLOT, ROW_BLOCK = range(6)


def _n_expert_blocks(n_tok, n_exp):
    return n_tok * TOP_K // MOE_BLOCK + n_exp


def _sched_lanes(n_tok, n_exp):
    return -(-_n_expert_blocks(n_tok, n_exp) // 128) * 128


def _block_schedule(counts, n_lanes, cap_blocks):
    n_exp = counts.shape[0]
    sub = lax.broadcasted_iota(jnp.int32, (n_exp, n_exp), 0)
    lane = lax.broadcasted_iota(jnp.int32, (n_exp, n_exp), 1)

    def as_row(col):
        return jnp.sum(jnp.where(sub == lane, col, 0.0), axis=0, keepdims=True)

    def running(col):
        return jnp.sum(jnp.where(lane <= sub, as_row(col), 0.0), axis=1, keepdims=True)

    blocks = jnp.floor((counts + (MOE_BLOCK - 1)) * (1.0 / MOE_BLOCK))
    ends = running(blocks)
    starts = ends - blocks
    live = counts > 0.0
    live_upto = running(jnp.where(live, 1.0, 0.0))

    step = lax.broadcasted_iota(jnp.int32, (1, n_lanes), 1).astype(F32)
    eid = lax.broadcasted_iota(jnp.int32, (n_exp, n_lanes), 0)
    expert = jnp.minimum(
        jnp.sum(jnp.where(ends <= step, 1, 0), axis=0, keepdims=True), n_exp - 1)
    mine = eid == expert

    def pick(col):
        return jnp.sum(jnp.where(mine, col, 0.0), axis=0, keepdims=True)

    used = step < jnp.max(ends, axis=0, keepdims=True)
    j = step - pick(starts)
    valid = jnp.where(used, jnp.clip(pick(counts) - j * MOE_BLOCK, 0.0, float(MOE_BLOCK)), 0.0)
    first = used & (j == 0.0)
    nxt = jnp.min(jnp.where((eid > expert) & live, eid, n_exp), axis=0, keepdims=True)
    nxt = jnp.where(nxt == n_exp, -1, nxt)
    slot = (pick(live_upto).astype(jnp.int32) + 1) & 1
    block = jnp.where(used, expert * cap_blocks + j.astype(jnp.int32), n_exp * cap_blocks)
    zero = jnp.zeros_like(expert)
    table = [zero] * SCHED_ROWS
    table[ROW_EXPERT], table[ROW_VALID], table[ROW_FIRST] = expert, valid.astype(jnp.int32), first.astype(jnp.int32)
    table[ROW_NEXT], table[ROW_SLOT], table[ROW_BLOCK] = nxt, slot, block
    return jnp.concatenate(table, axis=0)


def _time_major_copies(seq_hbm, tm_buf, sem, chunk, slot, to_hbm=False):
    steps, nb = tm_buf.shape[1], tm_buf.shape[2]
    copies = []
    for b in range(nb):
        hbm = seq_hbm.at[b, pl.ds(chunk * steps, steps), :]
        vmem = tm_buf.at[slot, :, b, :]
        src, dst = (vmem, hbm) if to_hbm else (hbm, vmem)
        copies.append(pltpu.make_async_copy(src, dst, sem.at[slot]))
    return copies


def _mixer_kernel(alpha, nb, n_exp, chunk0, n_chunks,
                  x_hbm, w_in_ref, bm_ref, cm_ref, lam_ref, dskip_ref,
                  w_glu_ref, w_pg_ref, pscale_ref, w_pp_ref, w_out_ref,
                  ln_g_ref, ln_b_ref, wr_ref, brt_ref,
                  state_in_ref, hist_in_ref,
                  h_ref, dest_ref, wts_ref, sched_ref, state_out_ref, hist_out_ref,
                  proj_ref, bu_ref, st_ref, upool_ref, state_ref, carry_ref, tri_ref, xb_ref,
                  pre_ref, x_buf, x_sem):
    i = pl.program_id(0)
    steps = x_buf.shape[1]
    rows = steps * nb
    d_model = x_buf.shape[3]

    slot = i % 2

    def fetch_copies(step, which):
        return _time_major_copies(x_hbm, x_buf, x_sem, chunk0 + step, which)

    @pl.when(i == 0)
    def _first_fetch():
        for cp in fetch_copies(0, 0):
            cp.start()

    @pl.when(i + 1 < n_chunks)
    def _next_fetch():
        for cp in fetch_copies(i + 1, 1 - slot):
            cp.start()

    @pl.when(i < n_chunks)
    def _await_fetch():
        for cp in fetch_copies(i, slot):
            cp.wait()
    ssm_w = dskip_ref.shape[1]
    pool_w = pscale_ref.shape[1]
    half_w = ssm_w // 2
    half_s = bu_ref.shape[1] // 2
    plane = half_s // 2
    halo = upool_ref.shape[0] - rows

    @pl.when(i == 0)
    def _init():
        state_ref[...] = state_in_ref[...]
        carry_ref[...] = jnp.zeros_like(carry_ref)
        upool_ref[0:halo, :] = hist_in_ref[...]
        ri = lax.broadcasted_iota(jnp.int32, (rows, rows), 0)
        ci = lax.broadcasted_iota(jnp.int32, (rows, rows), 1)
        tri_ref[...] = (ri < ci).astype(BF16)
        pre_ref[...] = jnp.zeros_like(pre_ref)

    capacity = rows * n_chunks

    chunk_rows = rows // BACK_CHUNKS

    def back_norm(c):
        r0 = c * chunk_rows
        h1 = _layer_norm(pre_ref[1, r0:r0 + chunk_rows, :], ln_g_ref[...], ln_b_ref[...])
        hb = h1.astype(BF16)
        h_ref[r0:r0 + chunk_rows, :] = _pack_rows(h1)
        return hb, (h1 - hb.astype(F32)).astype(BF16)

    def back_logits(hb, h_lo):
        pad_e = wr_ref.shape[1] // 2
        l_hi = _dot(hb, wr_ref[...])
        return l_hi[:, 0:pad_e] + l_hi[:, pad_e:] + _dot(h_lo, wr_ref[:, 0:pad_e])

    def back_topk(lg):
        lt = jnp.transpose(lg)[0:n_exp, :] + brt_ref[:, 0:1]
        eio = lax.broadcasted_iota(jnp.int32, (n_exp, chunk_rows), 0)
        vals, idxs = [], []
        for _ in range(TOP_K):
            m = jnp.max(lt, axis=0, keepdims=True)
            sel = jnp.min(jnp.where(lt == m, eio, n_exp), axis=0, keepdims=True)
            vals.append(m)
            idxs.append(sel)
            lt = jnp.where(eio == sel, -jnp.inf, lt)
        return jnp.concatenate(vals, axis=0), jnp.concatenate(idxs, axis=0)

    def back_finish(routed):
        in_range = (i > 0).astype(F32)
        vals = jnp.concatenate([v for v, _ in routed], axis=1)
        idxs = jnp.concatenate([s for _, s in routed], axis=1)
        exps = jnp.exp(vals - vals[0:1, :])
        wts_ref[...] = exps / jnp.sum(exps, axis=0, keepdims=True)

        eio = lax.broadcasted_iota(jnp.int32, (n_exp, rows), 0)
        run = carry_ref[:, 0:1]
        ranks = []
        one_hot = [(eio == idxs[k:k + 1, :]).astype(F32) for k in range(TOP_K)]
        before = _dot(jnp.concatenate(one_hot, axis=0).astype(BF16), tri_ref[...])
        for k in range(TOP_K):
            oh = one_hot[k]
            ranks.append(jnp.sum(oh * (run + before[k * n_exp:(k + 1) * n_exp, :]),
                                 axis=0, keepdims=True))
            run = run + in_range * jnp.sum(oh, axis=1, keepdims=True)
        dest_ref[...] = idxs * capacity + jnp.concatenate(ranks, axis=0).astype(jnp.int32)
        carry_ref[...] = jnp.broadcast_to(run, carry_ref.shape)

    xf = x_buf[slot].reshape(rows, d_model)
    xb_ref[...] = xf.astype(BF16)
    n_mix = ssm_w + pool_w
    normed, logits, routed = {}, {}, []

    def back_parts(tile):
        if 0 <= tile - 1 < BACK_CHUNKS:
            logits[tile - 1] = back_logits(*normed.pop(tile - 1))
        if 0 <= tile - 2 < BACK_CHUNKS:
            routed.append(back_topk(logits.pop(tile - 2)))
        if tile < BACK_CHUNKS:
            normed[tile] = back_norm(tile)

    for c in range(BACK_CHUNKS):
        c0 = c * (n_mix // BACK_CHUNKS)
        c1 = c0 + n_mix // BACK_CHUNKS
        proj_ref[:, c0:c1] = _dot(xb_ref[...], w_in_ref[:, c0:c1])
        back_parts(c)

    us = proj_ref[:, 0:ssm_w]
    usb = us.astype(BF16)
    quarter = half_s // 2
    for hf in range(2):
        for q in range(2):
            c0 = hf * half_s + q * quarter
            bu_ref[:, c0:c0 + quarter] = _dot(usb[:, hf * half_w:(hf + 1) * half_w],
                                              bm_ref[hf, :, q * quarter:(q + 1) * quarter])
            back_parts(BACK_CHUNKS + 2 * hf + q)
    back_finish(routed)

    gate_cols = w_in_ref.shape[1] - n_mix
    lane_groups = [(hf, q) for hf in range(2) for q in range(plane // SCAN_LANES)]
    segments = len(lane_groups) * SCAN_SEGMENTS
    tile = gate_cols // segments
    seg_steps = steps // SCAN_SEGMENTS
    for gi, (hf, q) in enumerate(lane_groups):
        cre = hf * half_s + q * SCAN_LANES
        cim = cre + plane
        cl = hf * plane + q * SCAN_LANES
        a_re = jnp.broadcast_to(lam_ref[0:1, cl:cl + SCAN_LANES], (nb, SCAN_LANES))
        a_im = jnp.broadcast_to(lam_ref[1:2, cl:cl + SCAN_LANES], (nb, SCAN_LANES))
        s_re = state_ref[:, cre:cre + SCAN_LANES]
        s_im = state_ref[:, cim:cim + SCAN_LANES]
        for seg in range(SCAN_SEGMENTS):
            c0 = n_mix + (gi * SCAN_SEGMENTS + seg) * tile
            proj_ref[:, c0:c0 + tile] = _dot(xb_ref[...], w_in_ref[:, c0:c0 + tile])
            for t in range(seg * seg_steps, (seg + 1) * seg_steps):
                r0 = t * nb
                b_re = bu_ref[r0:r0 + nb, cre:cre + SCAN_LANES]
                b_im = bu_ref[r0:r0 + nb, cim:cim + SCAN_LANES]
                s_re, s_im = (a_re * s_re - a_im * s_im + b_re,
                              a_re * s_im + a_im * s_re + b_im)
                st_ref[r0:r0 + nb, cre:cre + SCAN_LANES] = s_re.astype(BF16)
                st_ref[r0:r0 + nb, cim:cim + SCAN_LANES] = s_im.astype(BF16)
        state_ref[:, cre:cre + SCAN_LANES] = s_re
        state_ref[:, cim:cim + SCAN_LANES] = s_im

    y = jnp.concatenate(
        [_dot(st_ref[:, hf * half_s:(hf + 1) * half_s], cm_ref[hf]) for hf in range(2)],
        axis=1) + dskip_ref[...] * us

    up = proj_ref[:, ssm_w:ssm_w + pool_w]
    upool_ref[halo:halo + rows, :] = up
    t_abs = ((chunk0 + i) * steps
             + lax.broadcasted_iota(jnp.int32, (rows, 1), 0) // nb).astype(F32)
    gdim = pool_w // len(POOL_WINDOWS)
    pooled = []
    for g, w in enumerate(POOL_WINDOWS):
        cur = upool_ref[:, g * gdim:(g + 1) * gdim]
        span = 1
        while span < w:
            sh = span * nb
            cur = cur[sh:] + cur[:-sh]
            span *= 2
        win = cur[cur.shape[0] - rows:]
        inv = 1.0 / jnp.minimum(t_abs + 1.0, float(w))
        pooled.append(win * inv - up[:, g * gdim:(g + 1) * gdim])
    upool_ref[0:halo, :] = upool_ref[rows:rows + halo, :]
    pooled = jnp.concatenate(pooled, axis=1)
    mixed = _dot(pooled.astype(BF16), w_pg_ref[...])
    z = 0.5 * y * (1.0 + jnp.tanh(math.sqrt(2.0 / math.pi) * (y + 0.044715 * (y * y * y))))
    y_pool = _dot((mixed * pscale_ref[...]).astype(BF16), w_pp_ref[...])
    vg = _dot(z.astype(BF16), w_glu_ref[...])
    y_ssm = vg[:, 0:d_model] * _sigmoid(vg[:, d_model:2 * d_model])

    g0 = ssm_w + pool_w
    merged = (_sigmoid(proj_ref[:, g0:g0 + d_model]) * y_ssm
              + _sigmoid(proj_ref[:, g0 + d_model:g0 + 2 * d_model]) * y_pool)
    pre_ref[0] = alpha * xf + _dot(merged.astype(BF16), w_out_ref[...])

    pre_ref[1] = pre_ref[0]

    @pl.when(i == n_chunks - 1)
    def _hand_over_state():
        state_out_ref[...] = state_ref[...]
        hist_out_ref[...] = upool_ref[0:halo, :]

    @pl.when(i == n_chunks)
    def _hand_over_schedule():
        sched_ref[...] = _block_schedule(carry_ref[:, 0:1], sched_ref.shape[1],
                                         capacity // MOE_BLOCK)


def _const_spec(shape):
    zeros = (0,) * len(shape)
    return pl.BlockSpec(shape, lambda i: zeros, pipeline_mode=pl.Buffered(1))


def _mixer_call(x3, wts, alpha, n_exp, chunk0, n_chunks, state, hist):
    nb, _, d_model = x3.shape
    n_tok = nb * n_chunks * MIX_STEPS
    assert n_tok % MOE_BLOCK == 0, "an expert's row range must be a whole number of blocks"
    rows = MIX_STEPS * nb
    grid = n_chunks + 1
    state_cols = 2 * wts["bm"].shape[2]
    pool_w = wts["pscale"].shape[1]
    halo = max(POOL_WINDOWS) * nb

    def row_spec(width):
        return pl.BlockSpec((rows, width), lambda i: (jnp.maximum(i - 1, 0), 0))

    names = ["w_in", "bm", "cm", "lam", "dskip", "w_glu", "w_pg", "pscale", "w_pp",
             "w_out", "ln_g", "ln_b", "wr", "brt"]
    ops = [wts[n] for n in names]
    hbm_spec = pl.BlockSpec(memory_space=pl.ANY)
    in_specs = ([hbm_spec] + [_const_spec(o.shape) for o in ops]
                + [_const_spec(state.shape), _const_spec(hist.shape)])
    k_spec = pl.BlockSpec((TOP_K, rows), lambda i: (0, jnp.maximum(i - 1, 0)))
    sched_lanes = _sched_lanes(n_tok, n_exp)
    out_shape = (
        jax.ShapeDtypeStruct((n_tok, d_model // 2), jnp.int32),
        jax.ShapeDtypeStruct((TOP_K, n_tok), jnp.int32),
        jax.ShapeDtypeStruct((TOP_K, n_tok), F32),
        jax.ShapeDtypeStruct((SCHED_ROWS, sched_lanes), jnp.int32),
        jax.ShapeDtypeStruct(state.shape, F32),
        jax.ShapeDtypeStruct(hist.shape, F32),
    )
    out_specs = (row_spec(d_model // 2), k_spec, k_spec,
                 pl.BlockSpec((SCHED_ROWS, sched_lanes), lambda i: (0, 0)),
                 pl.BlockSpec(state.shape, lambda i: (0, 0)),
                 pl.BlockSpec(hist.shape, lambda i: (0, 0)))
    scratch = [
        pltpu.VMEM((rows, wts["w_in"].shape[1]), F32),
        pltpu.VMEM((rows, state_cols), F32),
        pltpu.VMEM((rows, state_cols), BF16),
        pltpu.VMEM((halo + rows, pool_w), F32),
        pltpu.VMEM((nb, state_cols), F32),
        pltpu.VMEM((n_exp, 128), F32),
        pltpu.VMEM((rows, rows), BF16),
        pltpu.VMEM((rows, d_model), BF16),
        pltpu.VMEM((2, rows, d_model), F32),
        pltpu.VMEM((2, MIX_STEPS, nb, d_model), F32),
        pltpu.SemaphoreType.DMA((2,)),
    ]
    return pl.pallas_call(
        functools.partial(_mixer_kernel, alpha, nb, n_exp, chunk0, n_chunks),
        grid=(grid,),
        in_specs=in_specs,
        out_specs=out_specs,
        out_shape=out_shape,
        scratch_shapes=scratch,
        compiler_params=pltpu.CompilerParams(
            dimension_semantics=("arbitrary",), vmem_limit_bytes=VMEM_LIMIT),
        name="mixer",
    )(x3, *ops, state, hist)


def _expert_kernel(sched_ref, xs_ref, wg_hbm, bg_ref, wu_hbm, bu_ref, wd_hbm, bd_ref,
                   ys_ref, wg_bf, wu_bf, wd_bf, stage, w_sem):
    i = pl.program_id(0)
    half = xs_ref.shape[1]
    expert = sched_ref[ROW_EXPERT, i]
    n_valid = sched_ref[ROW_VALID, i]
    next_expert = sched_ref[ROW_NEXT, i]

    def weight_copies(which, slot):
        return [pltpu.make_async_copy(w_hbm.at[which], stage.at[slot, m], w_sem.at[slot, m])
                for m, w_hbm in enumerate((wg_hbm, wu_hbm, wd_hbm))]

    @pl.when(sched_ref[ROW_FIRST, i] == 1)
    def _new_expert():
        slot = sched_ref[ROW_SLOT, i]

        @pl.when(i == 0)
        def _nothing_prefetched_yet():
            for cp in weight_copies(expert, slot):
                cp.start()

        @pl.when(next_expert >= 0)
        def _prefetch_next_run():
            for cp in weight_copies(next_expert, 1 - slot):
                cp.start(priority=1)

        for cp in weight_copies(expert, slot):
            cp.wait()
        wg_bf[...] = stage[slot, 0].astype(BF16)
        wu_bf[...] = stage[slot, 1].astype(BF16)
        wd_bf[...] = stage[slot, 2].astype(BF16)

    def run_rows(r0, n):
        row = r0 + lax.broadcasted_iota(jnp.int32, (n, half), 0)
        x_lo, x_hi = _unpack_rows(jnp.where(row < n_valid, xs_ref[r0:r0 + n, :], 0))
        g = _dot(x_lo, wg_bf[0:half, :]) + _dot(x_hi, wg_bf[half:, :]) + bg_ref[0]
        u = _dot(x_lo, wu_bf[0:half, :]) + _dot(x_hi, wu_bf[half:, :]) + bu_ref[0]
        g = jnp.minimum(g, SWIGLU_LIMIT)
        u = jnp.clip(u, -SWIGLU_LIMIT, SWIGLU_LIMIT)
        act = (u + 1.0) * (g * _sigmoid(SWIGLU_ALPHA * g))
        ys_ref[r0:r0 + n, :] = _pack_rows(_dot(act.astype(BF16), wd_bf[...]) + bd_ref[0])

    def clear_rows(r0, n):
        ys_ref[r0:r0 + n, :] = jnp.zeros((n, half), ys_ref.dtype)

    short = MOE_ROWS // 2
    for r0 in range(0, MOE_BLOCK, MOE_ROWS):
        @pl.when(n_valid > r0 + short)
        def _full_pass(r0=r0):
            run_rows(r0, MOE_ROWS)

        @pl.when((n_valid > r0) & (n_valid <= r0 + short))
        def _short_pass(r0=r0):
            run_rows(r0, short)
            clear_rows(r0 + short, short)

        @pl.when(n_valid <= r0)
        def _no_pass(r0=r0):
            clear_rows(r0, MOE_ROWS)


def _expert_call(schedule, n_steps, xs, wg, bg, wu, bu, wd, bd):
    n_rows = xs.shape[0]
    d_model = wg.shape[1]
    d_exp = wg.shape[2]
    assert d_model == d_exp, "the weight staging buffer assumes square expert matrices"

    def b_spec(n):
        return pl.BlockSpec((1, 1, n), lambda i, sched: (sched[ROW_EXPERT, i], 0, 0))

    row_spec = pl.BlockSpec((MOE_BLOCK, d_model // 2), lambda i, sched: (sched[ROW_BLOCK, i], 0))
    hbm_spec = pl.BlockSpec(memory_space=pl.ANY)
    grid_spec = pltpu.PrefetchScalarGridSpec(
        num_scalar_prefetch=1,
        grid=(n_steps,),
        in_specs=[row_spec, hbm_spec, b_spec(d_exp), hbm_spec, b_spec(d_exp), hbm_spec, b_spec(d_model)],
        out_specs=row_spec,
        scratch_shapes=[pltpu.VMEM((d_model, d_exp), BF16), pltpu.VMEM((d_model, d_exp), BF16),
                        pltpu.VMEM((d_exp, d_model), BF16),
                        pltpu.VMEM((2, 3, d_model, d_exp), F32),
                        pltpu.SemaphoreType.DMA((2, 3))],
    )
    return pl.pallas_call(
        _expert_kernel,
        grid_spec=grid_spec,
        out_shape=jax.ShapeDtypeStruct((n_rows, d_model // 2), jnp.int32),
        compiler_params=pltpu.CompilerParams(
            dimension_semantics=("arbitrary",), vmem_limit_bytes=VMEM_LIMIT),
        name="experts",
    )(schedule, xs, wg, bg, wu, bu, wd, bd)


def _combine_kernel(alpha, chunk0, h_ref, *refs):
    yg_refs = refs[:TOP_K]
    wt_ref, w_pleg_ref, w_plep_ref, ln_g_ref, ln_b_ref, p_hbm = refs[TOP_K:TOP_K + 6]
    o_hbm, o_buf, o_sem, p_buf, p_sem = refs[-5:]
    i = pl.program_id(0)
    last = pl.num_programs(0) - 1
    slot = i % 2
    rows = h_ref.shape[0]

    def fetch_p(step, which):
        return _time_major_copies(p_hbm, p_buf, p_sem, chunk0 + step, which)

    def write_back(step, from_slot):
        return _time_major_copies(o_hbm, o_buf, o_sem, chunk0 + step, from_slot, to_hbm=True)

    def drain(step, from_slot):
        for cp in write_back(step, from_slot):
            cp.wait()

    @pl.when(i == 0)
    def _first_fetch():
        for cp in fetch_p(0, 0):
            cp.start()

    @pl.when(i < last)
    def _next_fetch():
        for cp in fetch_p(i + 1, 1 - slot):
            cp.start()

    @pl.when(i >= 2)
    def _reuse_slot():
        drain(i - 2, slot)

    half = h_ref.shape[1]
    h_lo, h_hi = _unpack_rows_f32(h_ref[...])
    for cp in fetch_p(i, slot):
        cp.wait()
    pp = _dot(p_buf[slot].reshape(rows, p_buf.shape[3]).astype(BF16), w_plep_ref[...])
    gate = _sigmoid(_dot(h_lo.astype(BF16), w_pleg_ref[0:half, :])
                    + _dot(h_hi.astype(BF16), w_pleg_ref[half:, :]))
    ple = gate * pp
    moe = None
    wt_cols = jnp.transpose(jnp.concatenate(
        [wt_ref[...], jnp.zeros((8 - TOP_K, rows), F32)], axis=0))
    for k in range(TOP_K):
        w_bits = lax.bitcast_convert_type(wt_cols[:, k:k + 1].astype(BF16).astype(F32), jnp.int32)
        w_word = jnp.broadcast_to(w_bits | lax.shift_right_logical(w_bits, 16), (rows, half))
        term = pltpu.bitcast(yg_refs[k][...], BF16) * pltpu.bitcast(w_word, BF16)
        moe = term if moe is None else moe + term
    m_lo, m_hi = _unpack_rows_f32(pltpu.bitcast(moe, jnp.int32))
    acc_lo = alpha * h_lo + ple[:, 0:half] + m_lo
    acc_hi = alpha * h_hi + ple[:, half:] + m_hi
    n = 2.0 * half
    mu = (jnp.sum(acc_lo, axis=-1, keepdims=True) + jnp.sum(acc_hi, axis=-1, keepdims=True)) / n
    c_lo, c_hi = acc_lo - mu, acc_hi - mu
    var = (jnp.sum(c_lo * c_lo, axis=-1, keepdims=True)
           + jnp.sum(c_hi * c_hi, axis=-1, keepdims=True)) / n
    inv = lax.rsqrt(var + LN_EPS)
    steps, nb = o_buf.shape[1], o_buf.shape[2]
    o_buf[slot, :, :, 0:half] = (c_lo * inv * ln_g_ref[:, 0:half] + ln_b_ref[:, 0:half]
                                 ).reshape(steps, nb, half)
    o_buf[slot, :, :, half:] = (c_hi * inv * ln_g_ref[:, half:] + ln_b_ref[:, half:]
                                ).reshape(steps, nb, half)
    for cp in write_back(i, slot):
        cp.start()

    @pl.when(i == last)
    def _finish():
        @pl.when(i >= 1)
        def _previous():
            drain(i - 1, 1 - slot)
        drain(i, slot)


def _combine_call(hp, yg, wt, w_pleg, w_plep, ln_g, ln_b, p3, alpha, chunk0, earlier):
    n_tok = hp.shape[0]
    d_model = 2 * hp.shape[1]
    nb, seq, ple_dim = p3.shape
    grid = n_tok // CMB_ROWS
    steps = CMB_ROWS // nb
    extra = [] if earlier is None else [earlier]
    operands = [hp, *([yg] * TOP_K), wt, w_pleg, w_plep, ln_g, ln_b, p3, *extra]

    def slot_spec(k):
        return pl.BlockSpec((CMB_ROWS, d_model // 2), lambda i: (k * grid + i, 0))

    hbm_spec = pl.BlockSpec(memory_space=pl.ANY)
    return pl.pallas_call(
        functools.partial(_combine_kernel, alpha, chunk0),
        grid=(grid,),
        in_specs=[
            pl.BlockSpec((CMB_ROWS, d_model // 2), lambda i: (i, 0)),
            *[slot_spec(k) for k in range(TOP_K)],
            pl.BlockSpec((TOP_K, CMB_ROWS), lambda i: (0, i)),
            _const_spec(w_pleg.shape), _const_spec(w_plep.shape),
            pl.BlockSpec((1, d_model), lambda i: (0, 0)),
            pl.BlockSpec((1, d_model), lambda i: (0, 0)),
            hbm_spec,
        ] + [hbm_spec] * len(extra),
        out_specs=hbm_spec,
        out_shape=jax.ShapeDtypeStruct((nb, seq, d_model), F32),
        input_output_aliases={len(operands) - 1: 0} if extra else {},
        scratch_shapes=[pltpu.VMEM((2, steps, nb, d_model), F32), pltpu.SemaphoreType.DMA((2,)),
                        pltpu.VMEM((2, steps, nb, ple_dim), F32), pltpu.SemaphoreType.DMA((2,))],
        compiler_params=pltpu.CompilerParams(
            dimension_semantics=("arbitrary",), vmem_limit_bytes=VMEM_LIMIT),
        name="combine",
    )(*operands)


SC_WINDOW = 128


def _sc_mesh():
    return plsc.VectorSubcoreMesh(core_axis_name="c", subcore_axis_name="s")


def _sc_dispatch(rows, dest, n_out):
    n_tok, width = rows.shape
    n_slot = dest.shape[0]
    mesh = _sc_mesh()
    n_workers = mesh.num_cores * mesh.num_subcores
    per_worker = n_tok // n_workers
    assert per_worker * n_workers == n_tok and per_worker % SC_WINDOW == 0

    @functools.partial(
        pl.kernel, out_type=jax.ShapeDtypeStruct((n_out, width), rows.dtype), mesh=mesh,
        scratch_types=[pltpu.VMEM((n_slot, SC_WINDOW), jnp.int32),
                       pltpu.VMEM((SC_WINDOW, width), rows.dtype),
                       pltpu.SemaphoreType.DMA])
    def dispatch(x_hbm, i_hbm, o_hbm, idx_v, rows_v, sem):
        wid = lax.axis_index("s") * mesh.num_cores + lax.axis_index("c")

        @pl.loop(0, per_worker // SC_WINDOW)
        def _(j):
            base = pl.multiple_of(wid * per_worker + j * SC_WINDOW, SC_WINDOW)
            pltpu.sync_copy(i_hbm.at[:, pl.ds(base, SC_WINDOW)], idx_v)
            pltpu.sync_copy(x_hbm.at[pl.ds(base, SC_WINDOW)], rows_v)
            copies = [pltpu.async_copy(rows_v, o_hbm.at[idx_v.at[k]], sem) for k in range(n_slot)]
            for cp in copies:
                cp.wait()

    return dispatch(rows, dest)


def _sc_gather(table, index):
    n_out = index.shape[0]
    width = table.shape[1]
    mesh = _sc_mesh()
    n_workers = mesh.num_cores * mesh.num_subcores
    per_worker = n_out // n_workers
    assert per_worker * n_workers == n_out and per_worker % SC_WINDOW == 0
    half = SC_WINDOW // 2

    @functools.partial(
        pl.kernel, out_type=jax.ShapeDtypeStruct((n_out, width), table.dtype), mesh=mesh,
        scratch_types=[pltpu.VMEM((SC_WINDOW,), jnp.int32),
                       pltpu.VMEM((2, half, width), table.dtype)]
        + [pltpu.SemaphoreType.DMA] * 4)
    def gather(x_hbm, i_hbm, o_hbm, idx_v, rows_v, g0_sem, g1_sem, w0_sem, w1_sem):
        wid = lax.axis_index("s") * mesh.num_cores + lax.axis_index("c")

        @pl.loop(0, per_worker // SC_WINDOW)
        def _(j):
            base = pl.multiple_of(wid * per_worker + j * SC_WINDOW, SC_WINDOW)
            pltpu.sync_copy(i_hbm.at[pl.ds(base, SC_WINDOW)], idx_v)
            g0 = pltpu.async_copy(x_hbm.at[idx_v.at[pl.ds(0, half)]], rows_v.at[0], g0_sem)
            g1 = pltpu.async_copy(x_hbm.at[idx_v.at[pl.ds(half, half)]], rows_v.at[1], g1_sem)
            g0.wait()
            w0 = pltpu.async_copy(rows_v.at[0], o_hbm.at[pl.ds(base, half)], w0_sem)
            g1.wait()
            w1 = pltpu.async_copy(rows_v.at[1], o_hbm.at[pl.ds(base + half, half)], w1_sem)
            w0.wait()
            w1.wait()

    return gather(table, index)


def _ssm_matrices(lam_re, lam_im, log_step, b_re, b_im, c_re, c_im):
    n_grp, n_state, n_ch = b_re.shape
    lam = lax.complex(lam_re.astype(F32), lam_im.astype(F32))
    step = jnp.exp(log_step.astype(F32))[:, None]
    lam_bar = jnp.exp(lam * step)
    b_bar = ((lam_bar - 1.0) / lam)[..., None] * lax.complex(b_re.astype(F32), b_im.astype(F32))
    hg = n_grp // 2
    eye = jnp.eye(hg, dtype=F32)

    def b_half(bpart):
        return jnp.einsum('gph,gk->ghkp', bpart, eye).reshape(hg * n_ch, hg * n_state)

    def c_half(cpart):
        return jnp.einsum('ghp,gk->gpkh', cpart, eye).reshape(hg * n_state, hg * n_ch)

    bm, cm = [], []
    for hf in range(2):
        sl = slice(hf * hg, (hf + 1) * hg)
        bm.append(jnp.concatenate([b_half(jnp.real(b_bar)[sl]), b_half(jnp.imag(b_bar)[sl])], axis=1))
        cm.append(jnp.concatenate([c_half(c_re.astype(F32)[sl]), -c_half(c_im.astype(F32)[sl])], axis=0))
    lam_rows = jnp.stack([jnp.real(lam_bar).reshape(-1), jnp.imag(lam_bar).reshape(-1)], axis=0)
    return jnp.stack(bm).astype(BF16), jnp.stack(cm).astype(BF16), lam_rows


def _split_router(w):
    w = w.astype(F32)
    hi32 = lax.bitcast_convert_type(
        lax.bitcast_convert_type(w, jnp.uint32) & jnp.uint32(0xFFFF0000), F32)
    hi = hi32.astype(BF16)
    lo = (w - hi32).astype(BF16)
    pad = ((0, 0), (0, 128 - w.shape[1]))
    return jnp.concatenate([jnp.pad(hi, pad), jnp.pad(lo, pad)], axis=1)


def _block_diag(w):
    g, c, _ = w.shape
    return jnp.einsum('gcd,gk->gckd', w, jnp.eye(g, dtype=w.dtype)).reshape(g * c, g * c)


def kernel(x, p, w_in, ssm_lambda_re, ssm_lambda_im, ssm_log_step, ssm_b_re, ssm_b_im, ssm_c_re, ssm_c_im, ssm_d, w_glu_val, w_glu_gate, w_pool_group, pool_scale, w_pool_proj, w_out, ln1_g, ln1_b, w_router, b_router, w_gate, b_gate, w_up, b_up, w_down, b_down, w_ple_gate, w_ple_proj, ln2_g, ln2_b):
    bsz, seq, _ = x.shape
    depth = w_in.shape[0]
    n_exp = w_router.shape[2]
    alpha = (2.0 * depth) ** 0.25

    h = x
    for l in range(depth):
        bm, cm, lam_rows = _ssm_matrices(ssm_lambda_re[l], ssm_lambda_im[l], ssm_log_step[l],
                                         ssm_b_re[l], ssm_b_im[l], ssm_c_re[l], ssm_c_im[l])
        wts = {
            "w_in": w_in[l].astype(BF16), "bm": bm, "cm": cm, "lam": lam_rows,
            "dskip": ssm_d[l].reshape(1, -1).astype(F32),
            "w_glu": jnp.concatenate([w_glu_val[l], w_glu_gate[l]], axis=1).astype(BF16),
            "w_pg": _block_diag(w_pool_group[l]).astype(BF16),
            "pscale": pool_scale[l].reshape(1, -1).astype(F32),
            "w_pp": w_pool_proj[l].astype(BF16),
            "w_out": w_out[l].astype(BF16),
            "ln_g": ln1_g[l].reshape(1, -1).astype(F32),
            "ln_b": ln1_b[l].reshape(1, -1).astype(F32),
            "wr": _split_router(w_router[l]),
            "brt": jnp.broadcast_to(b_router[l].astype(F32)[:, None], (n_exp, 128)),
        }
        ple_w = (w_ple_gate[l].astype(BF16), w_ple_proj[l].astype(BF16))
        expert_w = (w_gate[l].astype(F32), b_gate[l].astype(F32)[:, None, :],
                    w_up[l].astype(F32), b_up[l].astype(F32)[:, None, :],
                    w_down[l].astype(F32), b_down[l].astype(F32)[:, None, :])
        ln2 = (ln2_g[l].reshape(1, -1).astype(F32), ln2_b[l].reshape(1, -1).astype(F32))

        chunks = seq // MIX_STEPS
        piece_chunks = [chunks * share // sum(TIME_PIECES) for share in TIME_PIECES]
        cmb_per_chunk = MIX_STEPS * bsz // CMB_ROWS
        assert sum(piece_chunks) == chunks and cmb_per_chunk * CMB_ROWS == MIX_STEPS * bsz
        state = jnp.zeros((bsz, 2 * bm.shape[2]), F32)
        hist = jnp.zeros((max(POOL_WINDOWS) * bsz, pool_scale.shape[1]), F32)
        out = None
        chunk0 = 0
        for per_piece in piece_chunks:
            hp, dest, wt, schedule, state, hist = _mixer_call(
                h, wts, alpha, n_exp, chunk0, per_piece, state, hist)
            n_piece = hp.shape[0]
            n_rows = n_exp * n_piece + MOE_BLOCK
            xs = _sc_dispatch(hp, dest, n_rows)
            ys = _expert_call(schedule, _n_expert_blocks(n_piece, n_exp), xs, *expert_w)
            yg = _sc_gather(ys, dest.reshape(-1))
            out = _combine_call(hp, yg, wt, *ple_w, *ln2, p[l], alpha,
                                chunk0 * cmb_per_chunk, out)
            chunk0 += per_piece
        h = out
    return h
```

```python
import functools
import math

import jax
import jax.numpy as jnp
from jax import lax
from jax.experimental import pallas as pl
from jax.experimental.pallas import tpu as pltpu
from jax.experimental.pallas import tpu_sc as plsc

F32 = jnp.float32
BF16 = jnp.bfloat16

LN_EPS = 1e-5
SWIGLU_LIMIT = 7.0
SWIGLU_ALPHA = 1.702
POOL_WINDOWS = (2, 4, 8, 16)
TOP_K = 4

MIX_STEPS = 32
SCAN_LANES = 512
SCAN_SEGMENTS = 2
MERGE_TILE = 256
BACK_CHUNKS = 4
MOE_BLOCK = 1024
MOE_ROWS = 512
CMB_ROWS = 512
CMB_CHUNK = 64
CMB_TILE = 256
TIME_PIECES = (3, 1)
VMEM_LIMIT = 60 * 1024 * 1024


def _sigmoid(v):
    return 0.5 * jnp.tanh(0.5 * v) + 0.5


def _layer_norm(v, g, b):
    mu = jnp.mean(v, axis=-1, keepdims=True)
    vc = v - mu
    var = jnp.mean(vc * vc, axis=-1, keepdims=True)
    return vc * lax.rsqrt(var + LN_EPS) * g + b


def _dot(a, b):
    return jnp.dot(a, b, preferred_element_type=F32)


def _pack_rows(v):
    n = v.shape[1] // 2
    lo = lax.bitcast_convert_type(v[:, :n].astype(BF16).astype(F32), jnp.int32)
    hi = lax.bitcast_convert_type(v[:, n:].astype(BF16).astype(F32), jnp.int32)
    return hi | lax.shift_right_logical(lo, 16)


def _unpack_rows_f32(w):
    lo = lax.bitcast_convert_type(lax.shift_left(w, 16), F32)
    hi = lax.bitcast_convert_type(w & jnp.int32(-65536), F32)
    return lo, hi


def _unpack_rows(w):
    lo, hi = _unpack_rows_f32(w)
    return lo.astype(BF16), hi.astype(BF16)


SCHED_ROWS = 8
ROW_EXPERT, ROW_VALID, ROW_FIRST, ROW_NEXT, ROW_SLOT, ROW_BLOCK = range(6)


def _n_expert_blocks(n_tok, n_exp):
    return n_tok * TOP_K // MOE_BLOCK + n_exp


def _sched_lanes(n_tok, n_exp):
    return -(-_n_expert_blocks(n_tok, n_exp) // 128) * 128


def _block_schedule(counts, n_lanes, cap_blocks):
    n_exp = counts.shape[0]
    sub = lax.broadcasted_iota(jnp.int32, (n_exp, n_exp), 0)
    lane = lax.broadcasted_iota(jnp.int32, (n_exp, n_exp), 1)

    def as_row(col):
        return jnp.sum(jnp.where(sub == lane, col, 0.0), axis=0, keepdims=True)

    def running(col):
        return jnp.sum(jnp.where(lane <= sub, as_row(col), 0.0), axis=1, keepdims=True)

    blocks = jnp.floor((counts + (MOE_BLOCK - 1)) * (1.0 / MOE_BLOCK))
    ends = running(blocks)
    starts = ends - blocks
    live = counts > 0.0
    live_upto = running(jnp.where(live, 1.0, 0.0))

    step = lax.broadcasted_iota(jnp.int32, (1, n_lanes), 1).astype(F32)
    eid = lax.broadcasted_iota(jnp.int32, (n_exp, n_lanes), 0)
    expert = jnp.minimum(
        jnp.sum(jnp.where(ends <= step, 1, 0), axis=0, keepdims=True), n_exp - 1)
    mine = eid == expert

    def pick(col):
        return jnp.sum(jnp.where(mine, col, 0.0), axis=0, keepdims=True)

    used = step < jnp.max(ends, axis=0, keepdims=True)
    j = step - pick(starts)
    valid = jnp.where(used, jnp.clip(pick(counts) - j * MOE_BLOCK, 0.0, float(MOE_BLOCK)), 0.0)
    first = used & (j == 0.0)
    nxt = jnp.min(jnp.where((eid > expert) & live, eid, n_exp), axis=0, keepdims=True)
    nxt = jnp.where(nxt == n_exp, -1, nxt)
    slot = (pick(live_upto).astype(jnp.int32) + 1) & 1
    block = jnp.where(used, expert * cap_blocks + j.astype(jnp.int32), n_exp * cap_blocks)
    zero = jnp.zeros_like(expert)
    table = [zero] * SCHED_ROWS
    table[ROW_EXPERT], table[ROW_VALID], table[ROW_FIRST] = expert, valid.astype(jnp.int32), first.astype(jnp.int32)
    table[ROW_NEXT], table[ROW_SLOT], table[ROW_BLOCK] = nxt, slot, block
    return jnp.concatenate(table, axis=0)


def _time_major_copies(seq_hbm, tm_buf, sem, chunk, slot, to_hbm=False):
    steps, nb = tm_buf.shape[1], tm_buf.shape[2]
    copies = []
    for b in range(nb):
        hbm = seq_hbm.at[b, pl.ds(chunk * steps, steps), :]
        vmem = tm_buf.at[slot, :, b, :]
        src, dst = (vmem, hbm) if to_hbm else (hbm, vmem)
        copies.append(pltpu.make_async_copy(src, dst, sem.at[slot]))
    return copies


def _mixer_kernel(alpha, nb, n_exp, chunk0, n_chunks,
                  x_hbm, w_in_ref, bm_ref, cm_ref, lam_ref, dskip_ref,
                  w_glu_ref, w_pg_ref, pscale_ref, w_pp_ref, w_out_ref,
                  ln_g_ref, ln_b_ref, wr_ref, brt_ref,
                  state_in_ref, hist_in_ref,
                  h_ref, dest_ref, wts_ref, sched_ref, state_out_ref, hist_out_ref,
                  proj_ref, bu_ref, st_ref, upool_ref, state_ref, carry_ref, tri_ref, xb_ref,
                  pre_ref, merged_ref, x_buf, x_sem):
    i = pl.program_id(0)
    steps = x_buf.shape[1]
    rows = steps * nb
    d_model = x_buf.shape[3]

    slot = i % 2

    def fetch_copies(step, which):
        return _time_major_copies(x_hbm, x_buf, x_sem, chunk0 + step, which)

    @pl.when(i == 0)
    def _first_fetch():
        for cp in fetch_copies(0, 0):
            cp.start()

    @pl.when(i + 1 < n_chunks)
    def _next_fetch():
        for cp in fetch_copies(i + 1, 1 - slot):
            cp.start()

    @pl.when(i < n_chunks)
    def _await_fetch():
        for cp in fetch_copies(i, slot):
            cp.wait()
    ssm_w = dskip_ref.shape[1]
    pool_w = pscale_ref.shape[1]
    half_w = ssm_w // 2
    half_s = bu_ref.shape[1] // 2
    plane = half_s // 2
    halo = upool_ref.shape[0] - rows

    @pl.when(i == 0)
    def _init():
        state_ref[...] = state_in_ref[...]
        carry_ref[...] = jnp.zeros_like(carry_ref)
        upool_ref[0:halo, :] = hist_in_ref[...]
        ri = lax.broadcasted_iota(jnp.int32, (rows, rows), 0)
        ci = lax.broadcasted_iota(jnp.int32, (rows, rows), 1)
        tri_ref[...] = (ri < ci).astype(BF16)
        pre_ref[...] = jnp.zeros_like(pre_ref)

    capacity = rows * n_chunks

    chunk_rows = rows // BACK_CHUNKS

    def back_norm(c):
        r0 = c * chunk_rows
        h1 = _layer_norm(pre_ref[1, r0:r0 + chunk_rows, :], ln_g_ref[...], ln_b_ref[...])
        hb = h1.astype(BF16)
        h_ref[r0:r0 + chunk_rows, :] = _pack_rows(h1)
        return hb, (h1 - hb.astype(F32)).astype(BF16)

    def back_logits(hb, h_lo):
        pad_e = wr_ref.shape[1] // 2
        l_hi = _dot(hb, wr_ref[...])
        return l_hi[:, 0:pad_e] + l_hi[:, pad_e:] + _dot(h_lo, wr_ref[:, 0:pad_e])

    def back_topk(lg):
        lt = jnp.transpose(lg)[0:n_exp, :] + brt_ref[:, 0:1]
        eio = lax.broadcasted_iota(jnp.int32, (n_exp, chunk_rows), 0)
        vals, idxs = [], []
        for _ in range(TOP_K):
            m = jnp.max(lt, axis=0, keepdims=True)
            sel = jnp.min(jnp.where(lt == m, eio, n_exp), axis=0, keepdims=True)
            vals.append(m)
            idxs.append(sel)
            lt = jnp.where(eio == sel, -jnp.inf, lt)
        return jnp.concatenate(vals, axis=0), jnp.concatenate(idxs, axis=0)

    def back_finish(routed):
        in_range = (i > 0).astype(F32)
        vals = jnp.concatenate([v for v, _ in routed], axis=1)
        idxs = jnp.concatenate([s for _, s in routed], axis=1)
        exps = jnp.exp(vals - vals[0:1, :])
        wts_ref[...] = exps / jnp.sum(exps, axis=0, keepdims=True)

        eio = lax.broadcasted_iota(jnp.int32, (n_exp, rows), 0)
        run = carry_ref[:, 0:1]
        ranks = []
        one_hot = [(eio == idxs[k:k + 1, :]).astype(F32) for k in range(TOP_K)]
        before = _dot(jnp.concatenate(one_hot, axis=0).astype(BF16), tri_ref[...])
        for k in range(TOP_K):
            oh = one_hot[k]
            ranks.append(jnp.sum(oh * (run + before[k * n_exp:(k + 1) * n_exp, :]),
                                 axis=0, keepdims=True))
            run = run + in_range * jnp.sum(oh, axis=1, keepdims=True)
        dest_ref[...] = idxs * capacity + jnp.concatenate(ranks, axis=0).astype(jnp.int32)
        carry_ref[...] = jnp.broadcast_to(run, carry_ref.shape)

    xf = x_buf[slot].reshape(rows, d_model)
    xb_ref[...] = xf.astype(BF16)
    n_mix = ssm_w + pool_w
    normed, logits, routed = {}, {}, []

    def back_parts(tile):
        if 0 <= tile - 1 < BACK_CHUNKS:
            logits[tile - 1] = back_logits(*normed.pop(tile - 1))
        if 0 <= tile - 2 < BACK_CHUNKS:
            routed.append(back_topk(logits.pop(tile - 2)))
        if tile < BACK_CHUNKS:
            normed[tile] = back_norm(tile)

    for c in range(BACK_CHUNKS):
        c0 = c * (n_mix // BACK_CHUNKS)
        c1 = c0 + n_mix // BACK_CHUNKS
        proj_ref[:, c0:c1] = _dot(xb_ref[...], w_in_ref[:, c0:c1])
        back_parts(c)

    us = proj_ref[:, 0:ssm_w]
    usb = us.astype(BF16)
    quarter = half_s // 2
    for hf in range(2):
        for q in range(2):
            c0 = hf * half_s + q * quarter
            bu_ref[:, c0:c0 + quarter] = _dot(usb[:, hf * half_w:(hf + 1) * half_w],
                                              bm_ref[hf, :, q * quarter:(q + 1) * quarter])
            back_parts(BACK_CHUNKS + 2 * hf + q)
    back_finish(routed)

    gate_cols = w_in_ref.shape[1] - n_mix
    lane_groups = [(hf, q) for hf in range(2) for q in range(plane // SCAN_LANES)]
    segments = len(lane_groups) * SCAN_SEGMENTS
    tile = gate_cols // segments
    seg_steps = steps // SCAN_SEGMENTS
    for gi, (hf, q) in enumerate(lane_groups):
        cre = hf * half_s + q * SCAN_LANES
        cim = cre + plane
        cl = hf * plane + q * SCAN_LANES
        a_re = jnp.broadcast_to(lam_ref[0:1, cl:cl + SCAN_LANES], (nb, SCAN_LANES))
        a_im = jnp.broadcast_to(lam_ref[1:2, cl:cl + SCAN_LANES], (nb, SCAN_LANES))
        s_re = state_ref[:, cre:cre + SCAN_LANES]
        s_im = state_ref[:, cim:cim + SCAN_LANES]
        for seg in range(SCAN_SEGMENTS):
            c0 = n_mix + (gi * SCAN_SEGMENTS + seg) * tile
            proj_ref[:, c0:c0 + tile] = _dot(xb_ref[...], w_in_ref[:, c0:c0 + tile])
            for t in range(seg * seg_steps, (seg + 1) * seg_steps):
                r0 = t * nb
                b_re = bu_ref[r0:r0 + nb, cre:cre + SCAN_LANES]
                b_im = bu_ref[r0:r0 + nb, cim:cim + SCAN_LANES]
                s_re, s_im = (a_re * s_re - a_im * s_im + b_re,
                              a_re * s_im + a_im * s_re + b_im)
                st_ref[r0:r0 + nb, cre:cre + SCAN_LANES] = s_re.astype(BF16)
                st_ref[r0:r0 + nb, cim:cim + SCAN_LANES] = s_im.astype(BF16)
        state_ref[:, cre:cre + SCAN_LANES] = s_re
        state_ref[:, cim:cim + SCAN_LANES] = s_im

    y = jnp.concatenate(
        [_dot(st_ref[:, hf * half_s:(hf + 1) * half_s], cm_ref[hf]) for hf in range(2)],
        axis=1) + dskip_ref[...] * us

    up = proj_ref[:, ssm_w:ssm_w + pool_w]
    upool_ref[halo:halo + rows, :] = up
    t_abs = ((chunk0 + i) * steps
             + lax.broadcasted_iota(jnp.int32, (rows, 1), 0) // nb).astype(F32)
    gdim = pool_w // len(POOL_WINDOWS)
    pooled = []
    for g, w in enumerate(POOL_WINDOWS):
        cur = upool_ref[:, g * gdim:(g + 1) * gdim]
        span = 1
        while span < w:
            sh = span * nb
            cur = cur[sh:] + cur[:-sh]
            span *= 2
        win = cur[cur.shape[0] - rows:]
        inv = 1.0 / jnp.minimum(t_abs + 1.0, float(w))
        pooled.append(win * inv - up[:, g * gdim:(g + 1) * gdim])
    upool_ref[0:halo, :] = upool_ref[rows:rows + halo, :]
    pooled = jnp.concatenate(pooled, axis=1)
    mixed = _dot(pooled.astype(BF16), w_pg_ref[...])
    z = 0.5 * y * (1.0 + jnp.tanh(math.sqrt(2.0 / math.pi) * (y + 0.044715 * (y * y * y))))
    zb = z.astype(BF16)
    mb = (mixed * pscale_ref[...]).astype(BF16)

    g0 = ssm_w + pool_w
    for c0 in range(0, d_model, MERGE_TILE):
        c1 = c0 + MERGE_TILE
        y_ssm = _dot(zb, w_glu_ref[:, c0:c1]) * _sigmoid(_dot(zb, w_glu_ref[:, d_model + c0:d_model + c1]))
        y_pool = _dot(mb, w_pp_ref[:, c0:c1])
        merged_ref[:, c0:c1] = (_sigmoid(proj_ref[:, g0 + c0:g0 + c1]) * y_ssm
                                + _sigmoid(proj_ref[:, g0 + d_model + c0:g0 + d_model + c1]) * y_pool
                                ).astype(BF16)
    pre_ref[0] = alpha * xf + _dot(merged_ref[...], w_out_ref[...])

    pre_ref[1] = pre_ref[0]

    @pl.when(i == n_chunks - 1)
    def _hand_over_state():
        state_out_ref[...] = state_ref[...]
        hist_out_ref[...] = upool_ref[0:halo, :]

    @pl.when(i == n_chunks)
    def _hand_over_schedule():
        sched_ref[...] = _block_schedule(carry_ref[:, 0:1], sched_ref.shape[1],
                                         capacity // MOE_BLOCK)


def _const_spec(shape):
    zeros = (0,) * len(shape)
    return pl.BlockSpec(shape, lambda i: zeros, pipeline_mode=pl.Buffered(1))


def _mixer_call(x3, wts, alpha, n_exp, chunk0, n_chunks, state, hist):
    nb, _, d_model = x3.shape
    n_tok = nb * n_chunks * MIX_STEPS
    assert n_tok % MOE_BLOCK == 0, "an expert's row range must be a whole number of blocks"
    rows = MIX_STEPS * nb
    grid = n_chunks + 1
    state_cols = 2 * wts["bm"].shape[2]
    pool_w = wts["pscale"].shape[1]
    halo = max(POOL_WINDOWS) * nb

    def row_spec(width):
        return pl.BlockSpec((rows, width), lambda i: (jnp.maximum(i - 1, 0), 0))

    names = ["w_in", "bm", "cm", "lam", "dskip", "w_glu", "w_pg", "pscale", "w_pp",
             "w_out", "ln_g", "ln_b", "wr", "brt"]
    ops = [wts[n] for n in names]
    hbm_spec = pl.BlockSpec(memory_space=pl.ANY)
    in_specs = ([hbm_spec] + [_const_spec(o.shape) for o in ops]
                + [_const_spec(state.shape), _const_spec(hist.shape)])
    k_spec = pl.BlockSpec((TOP_K, rows), lambda i: (0, jnp.maximum(i - 1, 0)))
    sched_lanes = _sched_lanes(n_tok, n_exp)
    out_shape = (
        jax.ShapeDtypeStruct((n_tok, d_model // 2), jnp.int32),
        jax.ShapeDtypeStruct((TOP_K, n_tok), jnp.int32),
        jax.ShapeDtypeStruct((TOP_K, n_tok), F32),
        jax.ShapeDtypeStruct((SCHED_ROWS, sched_lanes), jnp.int32),
        jax.ShapeDtypeStruct(state.shape, F32),
        jax.ShapeDtypeStruct(hist.shape, F32),
    )
    out_specs = (row_spec(d_model // 2), k_spec, k_spec,
                 pl.BlockSpec((SCHED_ROWS, sched_lanes), lambda i: (0, 0)),
                 pl.BlockSpec(state.shape, lambda i: (0, 0)),
                 pl.BlockSpec(hist.shape, lambda i: (0, 0)))
    scratch = [
        pltpu.VMEM((rows, wts["w_in"].shape[1]), F32),
        pltpu.VMEM((rows, state_cols), F32),
        pltpu.VMEM((rows, state_cols), BF16),
        pltpu.VMEM((halo + rows, pool_w), F32),
        pltpu.VMEM((nb, state_cols), F32),
        pltpu.VMEM((n_exp, 128), F32),
        pltpu.VMEM((rows, rows), BF16),
        pltpu.VMEM((rows, d_model), BF16),
        pltpu.VMEM((2, rows, d_model), F32),
        pltpu.VMEM((rows, d_model), BF16),
        pltpu.VMEM((2, MIX_STEPS, nb, d_model), F32),
        pltpu.SemaphoreType.DMA((2,)),
    ]
    return pl.pallas_call(
        functools.partial(_mixer_kernel, alpha, nb, n_exp, chunk0, n_chunks),
        grid=(grid,),
        in_specs=in_specs,
        out_specs=out_specs,
        out_shape=out_shape,
        scratch_shapes=scratch,
        compiler_params=pltpu.CompilerParams(
            dimension_semantics=("arbitrary",), vmem_limit_bytes=VMEM_LIMIT),
        name="mixer",
    )(x3, *ops, state, hist)


def _expert_kernel(sched_ref, xs_ref, wg_hbm, bg_ref, wu_hbm, bu_ref, wd_hbm, bd_ref,
                   ys_ref, wg_bf, wu_bf, wd_bf, stage, w_sem):
    i = pl.program_id(0)
    half = xs_ref.shape[1]
    expert = sched_ref[ROW_EXPERT, i]
    n_valid = sched_ref[ROW_VALID, i]
    next_expert = sched_ref[ROW_NEXT, i]

    def weight_copies(which, slot):
        return [pltpu.make_async_copy(w_hbm.at[which], stage.at[slot, m], w_sem.at[slot, m])
                for m, w_hbm in enumerate((wg_hbm, wu_hbm, wd_hbm))]

    @pl.when(sched_ref[ROW_FIRST, i] == 1)
    def _new_expert():
        slot = sched_ref[ROW_SLOT, i]

        @pl.when(i == 0)
        def _nothing_prefetched_yet():
            for cp in weight_copies(expert, slot):
                cp.start()

        @pl.when(next_expert >= 0)
        def _prefetch_next_run():
            for cp in weight_copies(next_expert, 1 - slot):
                cp.start(priority=1)

        for cp in weight_copies(expert, slot):
            cp.wait()
        wg_bf[...] = stage[slot, 0].astype(BF16)
        wu_bf[...] = stage[slot, 1].astype(BF16)
        wd_bf[...] = stage[slot, 2].astype(BF16)

    def run_rows(r0, n):
        row = r0 + lax.broadcasted_iota(jnp.int32, (n, half), 0)
        x_lo, x_hi = _unpack_rows(jnp.where(row < n_valid, xs_ref[r0:r0 + n, :], 0))
        g = _dot(x_lo, wg_bf[0:half, :]) + _dot(x_hi, wg_bf[half:, :]) + bg_ref[0]
        u = _dot(x_lo, wu_bf[0:half, :]) + _dot(x_hi, wu_bf[half:, :]) + bu_ref[0]
        g = jnp.minimum(g, SWIGLU_LIMIT)
        u = jnp.clip(u, -SWIGLU_LIMIT, SWIGLU_LIMIT)
        act = (u + 1.0) * (g * _sigmoid(SWIGLU_ALPHA * g))
        ys_ref[r0:r0 + n, :] = _pack_rows(_dot(act.astype(BF16), wd_bf[...]) + bd_ref[0])

    def clear_rows(r0, n):
        ys_ref[r0:r0 + n, :] = jnp.zeros((n, half), ys_ref.dtype)

    short = MOE_ROWS // 2
    for r0 in range(0, MOE_BLOCK, MOE_ROWS):
        @pl.when(n_valid > r0 + short)
        def _full_pass(r0=r0):
            run_rows(r0, MOE_ROWS)

        @pl.when((n_valid > r0) & (n_valid <= r0 + short))
        def _short_pass(r0=r0):
            run_rows(r0, short)
            clear_rows(r0 + short, short)

        @pl.when(n_valid <= r0)
        def _no_pass(r0=r0):
            clear_rows(r0, MOE_ROWS)


def _expert_call(schedule, n_steps, xs, wg, bg, wu, bu, wd, bd):
    n_rows = xs.shape[0]
    d_model = wg.shape[1]
    d_exp = wg.shape[2]
    assert d_model == d_exp, "the weight staging buffer assumes square expert matrices"

    def b_spec(n):
        return pl.BlockSpec((1, 1, n), lambda i, sched: (sched[ROW_EXPERT, i], 0, 0))

    row_spec = pl.BlockSpec((MOE_BLOCK, d_model // 2), lambda i, sched: (sched[ROW_BLOCK, i], 0))
    hbm_spec = pl.BlockSpec(memory_space=pl.ANY)
    grid_spec = pltpu.PrefetchScalarGridSpec(
        num_scalar_prefetch=1,
        grid=(n_steps,),
        in_specs=[row_spec, hbm_spec, b_spec(d_exp), hbm_spec, b_spec(d_exp), hbm_spec, b_spec(d_model)],
        out_specs=row_spec,
        scratch_shapes=[pltpu.VMEM((d_model, d_exp), BF16), pltpu.VMEM((d_model, d_exp), BF16),
                        pltpu.VMEM((d_exp, d_model), BF16),
                        pltpu.VMEM((2, 3, d_model, d_exp), F32),
                        pltpu.SemaphoreType.DMA((2, 3))],
    )
    return pl.pallas_call(
        _expert_kernel,
        grid_spec=grid_spec,
        out_shape=jax.ShapeDtypeStruct((n_rows, d_model // 2), jnp.int32),
        compiler_params=pltpu.CompilerParams(
            dimension_semantics=("arbitrary",), vmem_limit_bytes=VMEM_LIMIT),
        name="experts",
    )(schedule, xs, wg, bg, wu, bu, wd, bd)


def _combine_kernel(alpha, chunk0, h_ref, *refs):
    yg_refs = refs[:TOP_K]
    wt_ref, w_pleg_ref, w_plep_ref, ln_g_ref, ln_b_ref, p_hbm = refs[TOP_K:TOP_K + 6]
    o_hbm, o_buf, o_sem, p_buf, p_sem, ple_ref = refs[-6:]
    i = pl.program_id(0)
    last = pl.num_programs(0) - 1
    slot = i % 2
    rows = h_ref.shape[0]

    def fetch_p(step, which):
        return _time_major_copies(p_hbm, p_buf, p_sem, chunk0 + step, which)

    def write_back(step, from_slot):
        return _time_major_copies(o_hbm, o_buf, o_sem, chunk0 + step, from_slot, to_hbm=True)

    def drain(step, from_slot):
        for cp in write_back(step, from_slot):
            cp.wait()

    @pl.when(i == 0)
    def _first_fetch():
        for cp in fetch_p(0, 0):
            cp.start()

    @pl.when(i < last)
    def _next_fetch():
        for cp in fetch_p(i + 1, 1 - slot):
            cp.start()

    @pl.when(i >= 2)
    def _reuse_slot():
        drain(i - 2, slot)

    half = h_ref.shape[1]
    h_lo, h_hi = _unpack_rows_f32(h_ref[...])
    for cp in fetch_p(i, slot):
        cp.wait()
    hb = jnp.concatenate([h_lo.astype(BF16), h_hi.astype(BF16)], axis=1)
    pb = p_buf[slot].reshape(rows, p_buf.shape[3]).astype(BF16)
    for c0 in range(0, 2 * half, CMB_TILE):
        c1 = c0 + CMB_TILE
        ple_ref[:, c0:c1] = (_sigmoid(_dot(hb, w_pleg_ref[:, c0:c1]))
                             * _dot(pb, w_plep_ref[:, c0:c1]))
    wt_cols = jnp.transpose(jnp.concatenate(
        [wt_ref[...], jnp.zeros((8 - TOP_K, rows), F32)], axis=0))
    w_bits = lax.bitcast_convert_type(wt_cols.astype(BF16).astype(F32), jnp.int32)
    w_pair = w_bits | lax.shift_right_logical(w_bits, 16)

    steps, nb = o_buf.shape[1], o_buf.shape[2]
    n = 2.0 * half
    for r0 in range(0, rows, CMB_CHUNK):
        r1 = r0 + CMB_CHUNK
        moe = None
        for k in range(TOP_K):
            w_word = jnp.broadcast_to(w_pair[r0:r1, k:k + 1], (CMB_CHUNK, half))
            term = pltpu.bitcast(yg_refs[k][r0:r1, :], BF16) * pltpu.bitcast(w_word, BF16)
            moe = term if moe is None else moe + term
        m_lo, m_hi = _unpack_rows_f32(pltpu.bitcast(moe, jnp.int32))
        r_lo, r_hi = _unpack_rows_f32(h_ref[r0:r1, :])
        acc_lo = alpha * r_lo + ple_ref[r0:r1, 0:half] + m_lo
        acc_hi = alpha * r_hi + ple_ref[r0:r1, half:] + m_hi
        mu = (jnp.sum(acc_lo, axis=-1, keepdims=True)
              + jnp.sum(acc_hi, axis=-1, keepdims=True)) / n
        c_lo, c_hi = acc_lo - mu, acc_hi - mu
        var = (jnp.sum(c_lo * c_lo, axis=-1, keepdims=True)
               + jnp.sum(c_hi * c_hi, axis=-1, keepdims=True)) / n
        inv = lax.rsqrt(var + LN_EPS)
        t0, t1 = r0 // nb, r1 // nb
        o_buf[slot, t0:t1, :, 0:half] = (c_lo * inv * ln_g_ref[:, 0:half] + ln_b_ref[:, 0:half]
                                         ).reshape(t1 - t0, nb, half)
        o_buf[slot, t0:t1, :, half:] = (c_hi * inv * ln_g_ref[:, half:] + ln_b_ref[:, half:]
                                        ).reshape(t1 - t0, nb, half)
    for cp in write_back(i, slot):
        cp.start()

    @pl.when(i == last)
    def _finish():
        @pl.when(i >= 1)
        def _previous():
            drain(i - 1, 1 - slot)
        drain(i, slot)


def _combine_call(hp, yg, wt, w_pleg, w_plep, ln_g, ln_b, p3, alpha, chunk0, earlier):
    n_tok = hp.shape[0]
    d_model = 2 * hp.shape[1]
    nb, seq, ple_dim = p3.shape
    grid = n_tok // CMB_ROWS
    steps = CMB_ROWS // nb
    extra = [] if earlier is None else [earlier]
    operands = [hp, *([yg] * TOP_K), wt, w_pleg, w_plep, ln_g, ln_b, p3, *extra]

    def slot_spec(k):
        return pl.BlockSpec((CMB_ROWS, d_model // 2), lambda i: (k * grid + i, 0))

    hbm_spec = pl.BlockSpec(memory_space=pl.ANY)
    return pl.pallas_call(
        functools.partial(_combine_kernel, alpha, chunk0),
        grid=(grid,),
        in_specs=[
            pl.BlockSpec((CMB_ROWS, d_model // 2), lambda i: (i, 0)),
            *[slot_spec(k) for k in range(TOP_K)],
            pl.BlockSpec((TOP_K, CMB_ROWS), lambda i: (0, i)),
            _const_spec(w_pleg.shape), _const_spec(w_plep.shape),
            pl.BlockSpec((1, d_model), lambda i: (0, 0)),
            pl.BlockSpec((1, d_model), lambda i: (0, 0)),
            hbm_spec,
        ] + [hbm_spec] * len(extra),
        out_specs=hbm_spec,
        out_shape=jax.ShapeDtypeStruct((nb, seq, d_model), F32),
        input_output_aliases={len(operands) - 1: 0} if extra else {},
        scratch_shapes=[pltpu.VMEM((2, steps, nb, d_model), F32), pltpu.SemaphoreType.DMA((2,)),
                        pltpu.VMEM((2, steps, nb, ple_dim), F32), pltpu.SemaphoreType.DMA((2,)),
                        pltpu.VMEM((CMB_ROWS, d_model), F32)],
        compiler_params=pltpu.CompilerParams(
            dimension_semantics=("arbitrary",), vmem_limit_bytes=VMEM_LIMIT),
        name="combine",
    )(*operands)


SC_WINDOW = 128


def _sc_mesh():
    return plsc.VectorSubcoreMesh(core_axis_name="c", subcore_axis_name="s")


def _sc_dispatch(rows, dest, n_out):
    n_tok, width = rows.shape
    n_slot = dest.shape[0]
    mesh = _sc_mesh()
    n_workers = mesh.num_cores * mesh.num_subcores
    per_worker = n_tok // n_workers
    assert per_worker * n_workers == n_tok and per_worker % SC_WINDOW == 0

    @functools.partial(
        pl.kernel, out_type=jax.ShapeDtypeStruct((n_out, width), rows.dtype), mesh=mesh,
        scratch_types=[pltpu.VMEM((n_slot, SC_WINDOW), jnp.int32),
                       pltpu.VMEM((SC_WINDOW, width), rows.dtype),
                       pltpu.SemaphoreType.DMA])
    def dispatch(x_hbm, i_hbm, o_hbm, idx_v, rows_v, sem):
        wid = lax.axis_index("s") * mesh.num_cores + lax.axis_index("c")

        @pl.loop(0, per_worker // SC_WINDOW)
        def _(j):
            base = pl.multiple_of(wid * per_worker + j * SC_WINDOW, SC_WINDOW)
            pltpu.sync_copy(i_hbm.at[:, pl.ds(base, SC_WINDOW)], idx_v)
            pltpu.sync_copy(x_hbm.at[pl.ds(base, SC_WINDOW)], rows_v)
            copies = [pltpu.async_copy(rows_v, o_hbm.at[idx_v.at[k]], sem) for k in range(n_slot)]
            for cp in copies:
                cp.wait()

    return dispatch(rows, dest)


def _sc_gather(table, index):
    n_out = index.shape[0]
    width = table.shape[1]
    mesh = _sc_mesh()
    n_workers = mesh.num_cores * mesh.num_subcores
    per_worker = n_out // n_workers
    assert per_worker * n_workers == n_out and per_worker % SC_WINDOW == 0
    half = SC_WINDOW // 2

    @functools.partial(
        pl.kernel, out_type=jax.ShapeDtypeStruct((n_out, width), table.dtype), mesh=mesh,
        scratch_types=[pltpu.VMEM((SC_WINDOW,), jnp.int32),
                       pltpu.VMEM((2, half, width), table.dtype)]
        + [pltpu.SemaphoreType.DMA] * 4)
    def gather(x_hbm, i_hbm, o_hbm, idx_v, rows_v, g0_sem, g1_sem, w0_sem, w1_sem):
        wid = lax.axis_index("s") * mesh.num_cores + lax.axis_index("c")

        @pl.loop(0, per_worker // SC_WINDOW)
        def _(j):
            base = pl.multiple_of(wid * per_worker + j * SC_WINDOW, SC_WINDOW)
            pltpu.sync_copy(i_hbm.at[pl.ds(base, SC_WINDOW)], idx_v)
            g0 = pltpu.async_copy(x_hbm.at[idx_v.at[pl.ds(0, half)]], rows_v.at[0], g0_sem)
            g1 = pltpu.async_copy(x_hbm.at[idx_v.at[pl.ds(half, half)]], rows_v.at[1], g1_sem)
            g0.wait()
            w0 = pltpu.async_copy(rows_v.at[0], o_hbm.at[pl.ds(base, half)], w0_sem)
            g1.wait()
            w1 = pltpu.async_copy(rows_v.at[1], o_hbm.at[pl.ds(base + half, half)], w1_sem)
            w0.wait()
            w1.wait()

    return gather(table, index)


def _ssm_matrices(lam_re, lam_im, log_step, b_re, b_im, c_re, c_im):
    n_grp, n_state, n_ch = b_re.shape
    lam = lax.complex(lam_re.astype(F32), lam_im.astype(F32))
    step = jnp.exp(log_step.astype(F32))[:, None]
    lam_bar = jnp.exp(lam * step)
    b_bar = ((lam_bar - 1.0) / lam)[..., None] * lax.complex(b_re.astype(F32), b_im.astype(F32))
    hg = n_grp // 2
    eye = jnp.eye(hg, dtype=F32)

    def b_half(bpart):
        return jnp.einsum('gph,gk->ghkp', bpart, eye).reshape(hg * n_ch, hg * n_state)

    def c_half(cpart):
        return jnp.einsum('ghp,gk->gpkh', cpart, eye).reshape(hg * n_state, hg * n_ch)

    bm, cm = [], []
    for hf in range(2):
        sl = slice(hf * hg, (hf + 1) * hg)
        bm.append(jnp.concatenate([b_half(jnp.real(b_bar)[sl]), b_half(jnp.imag(b_bar)[sl])], axis=1))
        cm.append(jnp.concatenate([c_half(c_re.astype(F32)[sl]), -c_half(c_im.astype(F32)[sl])], axis=0))
    lam_rows = jnp.stack([jnp.real(lam_bar).reshape(-1), jnp.imag(lam_bar).reshape(-1)], axis=0)
    return jnp.stack(bm).astype(BF16), jnp.stack(cm).astype(BF16), lam_rows


def _split_router(w):
    w = w.astype(F32)
    hi32 = lax.bitcast_convert_type(
        lax.bitcast_convert_type(w, jnp.uint32) & jnp.uint32(0xFFFF0000), F32)
    hi = hi32.astype(BF16)
    lo = (w - hi32).astype(BF16)
    pad = ((0, 0), (0, 128 - w.shape[1]))
    return jnp.concatenate([jnp.pad(hi, pad), jnp.pad(lo, pad)], axis=1)


def _block_diag(w):
    g, c, _ = w.shape
    return jnp.einsum('gcd,gk->gckd', w, jnp.eye(g, dtype=w.dtype)).reshape(g * c, g * c)


def kernel(x, p, w_in, ssm_lambda_re, ssm_lambda_im, ssm_log_step, ssm_b_re, ssm_b_im, ssm_c_re, ssm_c_im, ssm_d, w_glu_val, w_glu_gate, w_pool_group, pool_scale, w_pool_proj, w_out, ln1_g, ln1_b, w_router, b_router, w_gate, b_gate, w_up, b_up, w_down, b_down, w_ple_gate, w_ple_proj, ln2_g, ln2_b):
    bsz, seq, _ = x.shape
    depth = w_in.shape[0]
    n_exp = w_router.shape[2]
    alpha = (2.0 * depth) ** 0.25

    h = x
    for l in range(depth):
        bm, cm, lam_rows = _ssm_matrices(ssm_lambda_re[l], ssm_lambda_im[l], ssm_log_step[l],
                                         ssm_b_re[l], ssm_b_im[l], ssm_c_re[l], ssm_c_im[l])
        wts = {
            "w_in": w_in[l].astype(BF16), "bm": bm, "cm": cm, "lam": lam_rows,
            "dskip": ssm_d[l].reshape(1, -1).astype(F32),
            "w_glu": jnp.concatenate([w_glu_val[l], w_glu_gate[l]], axis=1).astype(BF16),
            "w_pg": _block_diag(w_pool_group[l]).astype(BF16),
            "pscale": pool_scale[l].reshape(1, -1).astype(F32),
            "w_pp": w_pool_proj[l].astype(BF16),
            "w_out": w_out[l].astype(BF16),
            "ln_g": ln1_g[l].reshape(1, -1).astype(F32),
            "ln_b": ln1_b[l].reshape(1, -1).astype(F32),
            "wr": _split_router(w_router[l]),
            "brt": jnp.broadcast_to(b_router[l].astype(F32)[:, None], (n_exp, 128)),
        }
        ple_w = (w_ple_gate[l].astype(BF16), w_ple_proj[l].astype(BF16))
        expert_w = (w_gate[l].astype(F32), b_gate[l].astype(F32)[:, None, :],
                    w_up[l].astype(F32), b_up[l].astype(F32)[:, None, :],
                    w_down[l].astype(F32), b_down[l].astype(F32)[:, None, :])
        ln2 = (ln2_g[l].reshape(1, -1).astype(F32), ln2_b[l].reshape(1, -1).astype(F32))

        chunks = seq // MIX_STEPS
        piece_chunks = [chunks * share // sum(TIME_PIECES) for share in TIME_PIECES]
        cmb_per_chunk = MIX_STEPS * bsz // CMB_ROWS
        assert sum(piece_chunks) == chunks and cmb_per_chunk * CMB_ROWS == MIX_STEPS * bsz
        state = jnp.zeros((bsz, 2 * bm.shape[2]), F32)
        hist = jnp.zeros((max(POOL_WINDOWS) * bsz, pool_scale.shape[1]), F32)
        out = None
        chunk0 = 0
        for per_piece in piece_chunks:
            hp, dest, wt, schedule, state, hist = _mixer_call(
                h, wts, alpha, n_exp, chunk0, per_piece, state, hist)
            n_piece = hp.shape[0]
            n_rows = n_exp * n_piece + MOE_BLOCK
            xs = _sc_dispatch(hp, dest, n_rows)
            ys = _expert_call(schedule, _n_expert_blocks(n_piece, n_exp), xs, *expert_w)
            yg = _sc_gather(ys, dest.reshape(-1))
            out = _combine_call(hp, yg, wt, *ple_w, *ln2, p[l], alpha,
                                chunk0 * cmb_per_chunk, out)
            chunk0 += per_piece
        h = out
    return h
```

```python
import functools
import math

import jax
import jax.numpy as jnp
from jax import lax
from jax.experimental import pallas as pl
from jax.experimental.pallas import tpu as pltpu
from jax.experimental.pallas import tpu_sc as plsc

F32 = jnp.float32
BF16 = jnp.bfloat16

LN_EPS = 1e-5
SWIGLU_LIMIT = 7.0
SWIGLU_ALPHA = 1.702
POOL_WINDOWS = (2, 4, 8, 16)
TOP_K = 4

MIX_STEPS = 32
SCAN_LANES = 512
SCAN_SEGMENTS = 2
MERGE_TILE = 256
BACK_CHUNKS = 4
MOE_BLOCK = 2048
MOE_ROWS = 512
CMB_ROWS = 512
CMB_CHUNK = 64
CMB_TILE = 256
TIME_PIECES = (3, 1)
VMEM_LIMIT = 60 * 1024 * 1024


def _sigmoid(v):
    return 0.5 * jnp.tanh(0.5 * v) + 0.5


def _layer_norm(v, g, b):
    mu = jnp.mean(v, axis=-1, keepdims=True)
    vc = v - mu
    var = jnp.mean(vc * vc, axis=-1, keepdims=True)
    return vc * lax.rsqrt(var + LN_EPS) * g + b


def _dot(a, b):
    return jnp.dot(a, b, preferred_element_type=F32)


def _pack_rows(v):
    n = v.shape[1] // 2
    lo = lax.bitcast_convert_type(v[:, :n].astype(BF16).astype(F32), jnp.int32)
    hi = lax.bitcast_convert_type(v[:, n:].astype(BF16).astype(F32), jnp.int32)
    return hi | lax.shift_right_logical(lo, 16)


def _unpack_rows_f32(w):
    lo = lax.bitcast_convert_type(lax.shift_left(w, 16), F32)
    hi = lax.bitcast_convert_type(w & jnp.int32(-65536), F32)
    return lo, hi


def _unpack_rows(w):
    lo, hi = _unpack_rows_f32(w)
    return lo.astype(BF16), hi.astype(BF16)


SCHED_ROWS = 8
ROW_EXPERT, ROW_VALID, ROW_FIRST, ROW_NEXT, ROW_SLOT, ROW_BLOCK = range(6)


def _n_expert_blocks(n_tok, n_exp):
    return n_tok * TOP_K // MOE_BLOCK + n_exp


def _sched_lanes(n_tok, n_exp):
    return -(-_n_expert_blocks(n_tok, n_exp) // 128) * 128


def _block_schedule(counts, n_lanes, cap_blocks):
    n_exp = counts.shape[0]
    sub = lax.broadcasted_iota(jnp.int32, (n_exp, n_exp), 0)
    lane = lax.broadcasted_iota(jnp.int32, (n_exp, n_exp), 1)

    def as_row(col):
        return jnp.sum(jnp.where(sub == lane, col, 0.0), axis=0, keepdims=True)

    def running(col):
        return jnp.sum(jnp.where(lane <= sub, as_row(col), 0.0), axis=1, keepdims=True)

    blocks = jnp.floor((counts + (MOE_BLOCK - 1)) * (1.0 / MOE_BLOCK))
    ends = running(blocks)
    starts = ends - blocks
    live = counts > 0.0
    live_upto = running(jnp.where(live, 1.0, 0.0))

    step = lax.broadcasted_iota(jnp.int32, (1, n_lanes), 1).astype(F32)
    eid = lax.broadcasted_iota(jnp.int32, (n_exp, n_lanes), 0)
    expert = jnp.minimum(
        jnp.sum(jnp.where(ends <= step, 1, 0), axis=0, keepdims=True), n_exp - 1)
    mine = eid == expert

    def pick(col):
        return jnp.sum(jnp.where(mine, col, 0.0), axis=0, keepdims=True)

    used = step < jnp.max(ends, axis=0, keepdims=True)
    j = step - pick(starts)
    valid = jnp.where(used, jnp.clip(pick(counts) - j * MOE_BLOCK, 0.0, float(MOE_BLOCK)), 0.0)
    first = used & (j == 0.0)
    nxt = jnp.min(jnp.where((eid > expert) & live, eid, n_exp), axis=0, keepdims=True)
    nxt = jnp.where(nxt == n_exp, -1, nxt)
    slot = (pick(live_upto).astype(jnp.int32) + 1) & 1
    block = jnp.where(used, expert * cap_blocks + j.astype(jnp.int32), n_exp * cap_blocks)
    zero = jnp.zeros_like(expert)
    table = [zero] * SCHED_ROWS
    table[ROW_EXPERT], table[ROW_VALID], table[ROW_FIRST] = expert, valid.astype(jnp.int32), first.astype(jnp.int32)
    table[ROW_NEXT], table[ROW_SLOT], table[ROW_BLOCK] = nxt, slot, block
    return jnp.concatenate(table, axis=0)


def _time_major_copies(seq_hbm, tm_buf, sem, chunk, slot, to_hbm=False):
    steps, nb = tm_buf.shape[1], tm_buf.shape[2]
    copies = []
    for b in range(nb):
        hbm = seq_hbm.at[b, pl.ds(chunk * steps, steps), :]
        vmem = tm_buf.at[slot, :, b, :]
        src, dst = (vmem, hbm) if to_hbm else (hbm, vmem)
        copies.append(pltpu.make_async_copy(src, dst, sem.at[slot]))
    return copies


def _mixer_kernel(alpha, nb, n_exp, chunk0, n_chunks,
                  x_hbm, w_in_ref, bm_ref, cm_ref, lam_ref, dskip_ref,
                  w_glu_ref, w_pg_ref, pscale_ref, w_pp_ref, w_out_ref,
                  ln_g_ref, ln_b_ref, wr_ref, brt_ref,
                  state_in_ref, hist_in_ref,
                  h_ref, dest_ref, wts_ref, sched_ref, state_out_ref, hist_out_ref,
                  proj_ref, bu_ref, st_ref, upool_ref, state_ref, carry_ref, tri_ref, xb_ref,
                  pre_ref, merged_ref, x_buf, x_sem):
    i = pl.program_id(0)
    steps = x_buf.shape[1]
    rows = steps * nb
    d_model = x_buf.shape[3]

    slot = i % 2

    def fetch_copies(step, which):
        return _time_major_copies(x_hbm, x_buf, x_sem, chunk0 + step, which)

    @pl.when(i == 0)
    def _first_fetch():
        for cp in fetch_copies(0, 0):
            cp.start()

    @pl.when(i + 1 < n_chunks)
    def _next_fetch():
        for cp in fetch_copies(i + 1, 1 - slot):
            cp.start()

    @pl.when(i < n_chunks)
    def _await_fetch():
        for cp in fetch_copies(i, slot):
            cp.wait()
    ssm_w = dskip_ref.shape[1]
    pool_w = pscale_ref.shape[1]
    half_w = ssm_w // 2
    half_s = bu_ref.shape[1] // 2
    plane = half_s // 2
    halo = upool_ref.shape[0] - rows

    @pl.when(i == 0)
    def _init():
        state_ref[...] = state_in_ref[...]
        carry_ref[...] = jnp.zeros_like(carry_ref)
        upool_ref[0:halo, :] = hist_in_ref[...]
        ri = lax.broadcasted_iota(jnp.int32, (rows, rows), 0)
        ci = lax.broadcasted_iota(jnp.int32, (rows, rows), 1)
        tri_ref[...] = (ri < ci).astype(BF16)
        pre_ref[...] = jnp.zeros_like(pre_ref)

    capacity = rows * n_chunks

    chunk_rows = rows // BACK_CHUNKS

    def back_norm(c):
        r0 = c * chunk_rows
        h1 = _layer_norm(pre_ref[1, r0:r0 + chunk_rows, :], ln_g_ref[...], ln_b_ref[...])
        hb = h1.astype(BF16)
        h_ref[r0:r0 + chunk_rows, :] = _pack_rows(h1)
        return hb, (h1 - hb.astype(F32)).astype(BF16)

    def back_logits(hb, h_lo):
        pad_e = wr_ref.shape[1] // 2
        l_hi = _dot(hb, wr_ref[...])
        return l_hi[:, 0:pad_e] + l_hi[:, pad_e:] + _dot(h_lo, wr_ref[:, 0:pad_e])

    def back_topk(lg):
        lt = jnp.transpose(lg)[0:n_exp, :] + brt_ref[:, 0:1]
        eio = lax.broadcasted_iota(jnp.int32, (n_exp, chunk_rows), 0)
        vals, idxs = [], []
        for _ in range(TOP_K):
            m = jnp.max(lt, axis=0, keepdims=True)
            sel = jnp.min(jnp.where(lt == m, eio, n_exp), axis=0, keepdims=True)
            vals.append(m)
            idxs.append(sel)
            lt = jnp.where(eio == sel, -jnp.inf, lt)
        return jnp.concatenate(vals, axis=0), jnp.concatenate(idxs, axis=0)

    def back_finish(routed):
        in_range = (i > 0).astype(F32)
        vals = jnp.concatenate([v for v, _ in routed], axis=1)
        idxs = jnp.concatenate([s for _, s in routed], axis=1)
        exps = jnp.exp(vals - vals[0:1, :])
        wts_ref[...] = exps / jnp.sum(exps, axis=0, keepdims=True)

        eio = lax.broadcasted_iota(jnp.int32, (n_exp, rows), 0)
        run = carry_ref[:, 0:1]
        ranks = []
        one_hot = [(eio == idxs[k:k + 1, :]).astype(F32) for k in range(TOP_K)]
        before = _dot(jnp.concatenate(one_hot, axis=0).astype(BF16), tri_ref[...])
        for k in range(TOP_K):
            oh = one_hot[k]
            ranks.append(jnp.sum(oh * (run + before[k * n_exp:(k + 1) * n_exp, :]),
                                 axis=0, keepdims=True))
            run = run + in_range * jnp.sum(oh, axis=1, keepdims=True)
        dest_ref[...] = idxs * capacity + jnp.concatenate(ranks, axis=0).astype(jnp.int32)
        carry_ref[...] = jnp.broadcast_to(run, carry_ref.shape)

    xf = x_buf[slot].reshape(rows, d_model)
    xb_ref[...] = xf.astype(BF16)
    n_mix = ssm_w + pool_w
    normed, logits, routed = {}, {}, []

    def back_parts(tile):
        if 0 <= tile - 1 < BACK_CHUNKS:
            logits[tile - 1] = back_logits(*normed.pop(tile - 1))
        if 0 <= tile - 2 < BACK_CHUNKS:
            routed.append(back_topk(logits.pop(tile - 2)))
        if tile < BACK_CHUNKS:
            normed[tile] = back_norm(tile)

    for c in range(BACK_CHUNKS):
        c0 = c * (n_mix // BACK_CHUNKS)
        c1 = c0 + n_mix // BACK_CHUNKS
        proj_ref[:, c0:c1] = _dot(xb_ref[...], w_in_ref[:, c0:c1])
        back_parts(c)

    us = proj_ref[:, 0:ssm_w]
    usb = us.astype(BF16)
    quarter = half_s // 2
    for hf in range(2):
        for q in range(2):
            c0 = hf * half_s + q * quarter
            bu_ref[:, c0:c0 + quarter] = _dot(usb[:, hf * half_w:(hf + 1) * half_w],
                                              bm_ref[hf, :, q * quarter:(q + 1) * quarter])
            back_parts(BACK_CHUNKS + 2 * hf + q)
    back_finish(routed)

    gate_cols = w_in_ref.shape[1] - n_mix
    lane_groups = [(hf, q) for hf in range(2) for q in range(plane // SCAN_LANES)]
    segments = len(lane_groups) * SCAN_SEGMENTS
    tile = gate_cols // segments
    seg_steps = steps // SCAN_SEGMENTS
    for gi, (hf, q) in enumerate(lane_groups):
        cre = hf * half_s + q * SCAN_LANES
        cim = cre + plane
        cl = hf * plane + q * SCAN_LANES
        a_re = jnp.broadcast_to(lam_ref[0:1, cl:cl + SCAN_LANES], (nb, SCAN_LANES))
        a_im = jnp.broadcast_to(lam_ref[1:2, cl:cl + SCAN_LANES], (nb, SCAN_LANES))
        s_re = state_ref[:, cre:cre + SCAN_LANES]
        s_im = state_ref[:, cim:cim + SCAN_LANES]
        for seg in range(SCAN_SEGMENTS):
            c0 = n_mix + (gi * SCAN_SEGMENTS + seg) * tile
            proj_ref[:, c0:c0 + tile] = _dot(xb_ref[...], w_in_ref[:, c0:c0 + tile])
            for t in range(seg * seg_steps, (seg + 1) * seg_steps):
                r0 = t * nb
                b_re = bu_ref[r0:r0 + nb, cre:cre + SCAN_LANES]
                b_im = bu_ref[r0:r0 + nb, cim:cim + SCAN_LANES]
                s_re, s_im = (a_re * s_re - a_im * s_im + b_re,
                              a_re * s_im + a_im * s_re + b_im)
                st_ref[r0:r0 + nb, cre:cre + SCAN_LANES] = s_re.astype(BF16)
                st_ref[r0:r0 + nb, cim:cim + SCAN_LANES] = s_im.astype(BF16)
        state_ref[:, cre:cre + SCAN_LANES] = s_re
        state_ref[:, cim:cim + SCAN_LANES] = s_im

    y = jnp.concatenate(
        [_dot(st_ref[:, hf * half_s:(hf + 1) * half_s], cm_ref[hf]) for hf in range(2)],
        axis=1) + dskip_ref[...] * us

    up = proj_ref[:, ssm_w:ssm_w + pool_w]
    upool_ref[halo:halo + rows, :] = up
    t_abs = ((chunk0 + i) * steps
             + lax.broadcasted_iota(jnp.int32, (rows, 1), 0) // nb).astype(F32)
    gdim = pool_w // len(POOL_WINDOWS)
    pooled = []
    for g, w in enumerate(POOL_WINDOWS):
        cur = upool_ref[:, g * gdim:(g + 1) * gdim]
        span = 1
        while span < w:
            sh = span * nb
            cur = cur[sh:] + cur[:-sh]
            span *= 2
        win = cur[cur.shape[0] - rows:]
        inv = 1.0 / jnp.minimum(t_abs + 1.0, float(w))
        pooled.append(win * inv - up[:, g * gdim:(g + 1) * gdim])
    upool_ref[0:halo, :] = upool_ref[rows:rows + halo, :]
    pooled = jnp.concatenate(pooled, axis=1)
    mixed = _dot(pooled.astype(BF16), w_pg_ref[...])
    z = 0.5 * y * (1.0 + jnp.tanh(math.sqrt(2.0 / math.pi) * (y + 0.044715 * (y * y * y))))
    zb = z.astype(BF16)
    mb = (mixed * pscale_ref[...]).astype(BF16)

    g0 = ssm_w + pool_w
    for c0 in range(0, d_model, MERGE_TILE):
        c1 = c0 + MERGE_TILE
        y_ssm = _dot(zb, w_glu_ref[:, c0:c1]) * _sigmoid(_dot(zb, w_glu_ref[:, d_model + c0:d_model + c1]))
        y_pool = _dot(mb, w_pp_ref[:, c0:c1])
        merged_ref[:, c0:c1] = (_sigmoid(proj_ref[:, g0 + c0:g0 + c1]) * y_ssm
                                + _sigmoid(proj_ref[:, g0 + d_model + c0:g0 + d_model + c1]) * y_pool
                                ).astype(BF16)
    pre_ref[0] = alpha * xf + _dot(merged_ref[...], w_out_ref[...])

    pre_ref[1] = pre_ref[0]

    @pl.when(i == n_chunks - 1)
    def _hand_over_state():
        state_out_ref[...] = state_ref[...]
        hist_out_ref[...] = upool_ref[0:halo, :]

    @pl.when(i == n_chunks)
    def _hand_over_schedule():
        sched_ref[...] = _block_schedule(carry_ref[:, 0:1], sched_ref.shape[1],
                                         capacity // MOE_BLOCK)


def _const_spec(shape):
    zeros = (0,) * len(shape)
    return pl.BlockSpec(shape, lambda i: zeros, pipeline_mode=pl.Buffered(1))


def _mixer_call(x3, wts, alpha, n_exp, chunk0, n_chunks, state, hist):
    nb, _, d_model = x3.shape
    n_tok = nb * n_chunks * MIX_STEPS
    assert n_tok % MOE_BLOCK == 0, "an expert's row range must be a whole number of blocks"
    rows = MIX_STEPS * nb
    grid = n_chunks + 1
    state_cols = 2 * wts["bm"].shape[2]
    pool_w = wts["pscale"].shape[1]
    halo = max(POOL_WINDOWS) * nb

    def row_spec(width):
        return pl.BlockSpec((rows, width), lambda i: (jnp.maximum(i - 1, 0), 0))

    names = ["w_in", "bm", "cm", "lam", "dskip", "w_glu", "w_pg", "pscale", "w_pp",
             "w_out", "ln_g", "ln_b", "wr", "brt"]
    ops = [wts[n] for n in names]
    hbm_spec = pl.BlockSpec(memory_space=pl.ANY)
    in_specs = ([hbm_spec] + [_const_spec(o.shape) for o in ops]
                + [_const_spec(state.shape), _const_spec(hist.shape)])
    k_spec = pl.BlockSpec((TOP_K, rows), lambda i: (0, jnp.maximum(i - 1, 0)))
    sched_lanes = _sched_lanes(n_tok, n_exp)
    out_shape = (
        jax.ShapeDtypeStruct((n_tok, d_model // 2), jnp.int32),
        jax.ShapeDtypeStruct((TOP_K, n_tok), jnp.int32),
        jax.ShapeDtypeStruct((TOP_K, n_tok), F32),
        jax.ShapeDtypeStruct((SCHED_ROWS, sched_lanes), jnp.int32),
        jax.ShapeDtypeStruct(state.shape, F32),
        jax.ShapeDtypeStruct(hist.shape, F32),
    )
    out_specs = (row_spec(d_model // 2), k_spec, k_spec,
                 pl.BlockSpec((SCHED_ROWS, sched_lanes), lambda i: (0, 0)),
                 pl.BlockSpec(state.shape, lambda i: (0, 0)),
                 pl.BlockSpec(hist.shape, lambda i: (0, 0)))
    scratch = [
        pltpu.VMEM((rows, wts["w_in"].shape[1]), F32),
        pltpu.VMEM((rows, state_cols), F32),
        pltpu.VMEM((rows, state_cols), BF16),
        pltpu.VMEM((halo + rows, pool_w), F32),
        pltpu.VMEM((nb, state_cols), F32),
        pltpu.VMEM((n_exp, 128), F32),
        pltpu.VMEM((rows, rows), BF16),
        pltpu.VMEM((rows, d_model), BF16),
        pltpu.VMEM((2, rows, d_model), F32),
        pltpu.VMEM((rows, d_model), BF16),
        pltpu.VMEM((2, MIX_STEPS, nb, d_model), F32),
        pltpu.SemaphoreType.DMA((2,)),
    ]
    return pl.pallas_call(
        functools.partial(_mixer_kernel, alpha, nb, n_exp, chunk0, n_chunks),
        grid=(grid,),
        in_specs=in_specs,
        out_specs=out_specs,
        out_shape=out_shape,
        scratch_shapes=scratch,
        compiler_params=pltpu.CompilerParams(
            dimension_semantics=("arbitrary",), vmem_limit_bytes=VMEM_LIMIT),
        name="mixer",
    )(x3, *ops, state, hist)


def _expert_kernel(sched_ref, xs_ref, wg_hbm, bg_ref, wu_hbm, bu_ref, wd_hbm, bd_ref,
                   ys_ref, wg_bf, wu_bf, wd_bf, stage, w_sem):
    i = pl.program_id(0)
    half = xs_ref.shape[1]
    expert = sched_ref[ROW_EXPERT, i]
    n_valid = sched_ref[ROW_VALID, i]
    next_expert = sched_ref[ROW_NEXT, i]

    def weight_copies(which, slot):
        return [pltpu.make_async_copy(w_hbm.at[which], stage.at[slot, m], w_sem.at[slot, m])
                for m, w_hbm in enumerate((wg_hbm, wu_hbm, wd_hbm))]

    @pl.when(sched_ref[ROW_FIRST, i] == 1)
    def _new_expert():
        slot = sched_ref[ROW_SLOT, i]

        @pl.when(i == 0)
        def _nothing_prefetched_yet():
            for cp in weight_copies(expert, slot):
                cp.start()

        @pl.when(next_expert >= 0)
        def _prefetch_next_run():
            for cp in weight_copies(next_expert, 1 - slot):
                cp.start(priority=1)

        for cp in weight_copies(expert, slot):
            cp.wait()
        wg_bf[...] = stage[slot, 0].astype(BF16)
        wu_bf[...] = stage[slot, 1].astype(BF16)
        wd_bf[...] = stage[slot, 2].astype(BF16)

    def run_rows(r0, n):
        row = r0 + lax.broadcasted_iota(jnp.int32, (n, half), 0)
        x_lo, x_hi = _unpack_rows(jnp.where(row < n_valid, xs_ref[r0:r0 + n, :], 0))
        g = _dot(x_lo, wg_bf[0:half, :]) + _dot(x_hi, wg_bf[half:, :]) + bg_ref[0]
        u = _dot(x_lo, wu_bf[0:half, :]) + _dot(x_hi, wu_bf[half:, :]) + bu_ref[0]
        g = jnp.minimum(g, SWIGLU_LIMIT)
        u = jnp.clip(u, -SWIGLU_LIMIT, SWIGLU_LIMIT)
        act = (u + 1.0) * (g * _sigmoid(SWIGLU_ALPHA * g))
        ys_ref[r0:r0 + n, :] = _pack_rows(_dot(act.astype(BF16), wd_bf[...]) + bd_ref[0])

    def clear_rows(r0, n):
        ys_ref[r0:r0 + n, :] = jnp.zeros((n, half), ys_ref.dtype)

    short = MOE_ROWS // 2
    last_r0 = MOE_BLOCK - MOE_ROWS
    whole = n_valid > last_r0 + short

    @pl.when(whole)
    def _whole_block():
        for r0 in range(0, MOE_BLOCK, MOE_ROWS):
            run_rows(r0, MOE_ROWS)

    for r0 in range(0, MOE_BLOCK, MOE_ROWS):
        @pl.when(jnp.logical_not(whole) & (n_valid > r0 + short))
        def _full_pass(r0=r0):
            run_rows(r0, MOE_ROWS)

        @pl.when((n_valid > r0) & (n_valid <= r0 + short))
        def _short_pass(r0=r0):
            run_rows(r0, short)
            clear_rows(r0 + short, short)

        @pl.when(n_valid <= r0)
        def _no_pass(r0=r0):
            clear_rows(r0, MOE_ROWS)


def _expert_call(schedule, n_steps, xs, wg, bg, wu, bu, wd, bd):
    n_rows = xs.shape[0]
    d_model = wg.shape[1]
    d_exp = wg.shape[2]
    assert d_model == d_exp, "the weight staging buffer assumes square expert matrices"

    def b_spec(n):
        return pl.BlockSpec((1, 1, n), lambda i, sched: (sched[ROW_EXPERT, i], 0, 0))

    row_spec = pl.BlockSpec((MOE_BLOCK, d_model // 2), lambda i, sched: (sched[ROW_BLOCK, i], 0))
    hbm_spec = pl.BlockSpec(memory_space=pl.ANY)
    grid_spec = pltpu.PrefetchScalarGridSpec(
        num_scalar_prefetch=1,
        grid=(n_steps,),
        in_specs=[row_spec, hbm_spec, b_spec(d_exp), hbm_spec, b_spec(d_exp), hbm_spec, b_spec(d_model)],
        out_specs=row_spec,
        scratch_shapes=[pltpu.VMEM((d_model, d_exp), BF16), pltpu.VMEM((d_model, d_exp), BF16),
                        pltpu.VMEM((d_exp, d_model), BF16),
                        pltpu.VMEM((2, 3, d_model, d_exp), F32),
                        pltpu.SemaphoreType.DMA((2, 3))],
    )
    return pl.pallas_call(
        _expert_kernel,
        grid_spec=grid_spec,
        out_shape=jax.ShapeDtypeStruct((n_rows, d_model // 2), jnp.int32),
        compiler_params=pltpu.CompilerParams(
            dimension_semantics=("arbitrary",), vmem_limit_bytes=VMEM_LIMIT),
        name="experts",
    )(schedule, xs, wg, bg, wu, bu, wd, bd)


def _combine_kernel(alpha, chunk0, h_ref, *refs):
    yg_refs = refs[:TOP_K]
    wt_ref, w_pleg_ref, w_plep_ref, ln_g_ref, ln_b_ref, p_hbm = refs[TOP_K:TOP_K + 6]
    o_hbm, o_buf, o_sem, p_buf, p_sem, ple_ref = refs[-6:]
    i = pl.program_id(0)
    last = pl.num_programs(0) - 1
    slot = i % 2
    rows = h_ref.shape[0]

    def fetch_p(step, which):
        return _time_major_copies(p_hbm, p_buf, p_sem, chunk0 + step, which)

    def write_back(step, from_slot):
        return _time_major_copies(o_hbm, o_buf, o_sem, chunk0 + step, from_slot, to_hbm=True)

    def drain(step, from_slot):
        for cp in write_back(step, from_slot):
            cp.wait()

    @pl.when(i == 0)
    def _first_fetch():
        for cp in fetch_p(0, 0):
            cp.start()

    @pl.when(i < last)
    def _next_fetch():
        for cp in fetch_p(i + 1, 1 - slot):
            cp.start()

    @pl.when(i >= 2)
    def _reuse_slot():
        drain(i - 2, slot)

    half = h_ref.shape[1]
    h_lo, h_hi = _unpack_rows_f32(h_ref[...])
    for cp in fetch_p(i, slot):
        cp.wait()
    hb = jnp.concatenate([h_lo.astype(BF16), h_hi.astype(BF16)], axis=1)
    pb = p_buf[slot].reshape(rows, p_buf.shape[3]).astype(BF16)
    for c0 in range(0, 2 * half, CMB_TILE):
        c1 = c0 + CMB_TILE
        ple_ref[:, c0:c1] = (_sigmoid(_dot(hb, w_pleg_ref[:, c0:c1]))
                             * _dot(pb, w_plep_ref[:, c0:c1]))
    wt_cols = jnp.transpose(jnp.concatenate(
        [wt_ref[...], jnp.zeros((8 - TOP_K, rows), F32)], axis=0))
    w_bits = lax.bitcast_convert_type(wt_cols.astype(BF16).astype(F32), jnp.int32)
    w_pair = w_bits | lax.shift_right_logical(w_bits, 16)

    steps, nb = o_buf.shape[1], o_buf.shape[2]
    n = 2.0 * half
    for r0 in range(0, rows, CMB_CHUNK):
        r1 = r0 + CMB_CHUNK
        moe = None
        for k in range(TOP_K):
            w_word = jnp.broadcast_to(w_pair[r0:r1, k:k + 1], (CMB_CHUNK, half))
            term = pltpu.bitcast(yg_refs[k][r0:r1, :], BF16) * pltpu.bitcast(w_word, BF16)
            moe = term if moe is None else moe + term
        m_lo, m_hi = _unpack_rows_f32(pltpu.bitcast(moe, jnp.int32))
        r_lo, r_hi = _unpack_rows_f32(h_ref[r0:r1, :])
        acc_lo = alpha * r_lo + ple_ref[r0:r1, 0:half] + m_lo
        acc_hi = alpha * r_hi + ple_ref[r0:r1, half:] + m_hi
        mu = (jnp.sum(acc_lo, axis=-1, keepdims=True)
              + jnp.sum(acc_hi, axis=-1, keepdims=True)) / n
        c_lo, c_hi = acc_lo - mu, acc_hi - mu
        var = (jnp.sum(c_lo * c_lo, axis=-1, keepdims=True)
               + jnp.sum(c_hi * c_hi, axis=-1, keepdims=True)) / n
        inv = lax.rsqrt(var + LN_EPS)
        t0, t1 = r0 // nb, r1 // nb
        o_buf[slot, t0:t1, :, 0:half] = (c_lo * inv * ln_g_ref[:, 0:half] + ln_b_ref[:, 0:half]
                                         ).reshape(t1 - t0, nb, half)
        o_buf[slot, t0:t1, :, half:] = (c_hi * inv * ln_g_ref[:, half:] + ln_b_ref[:, half:]
                                        ).reshape(t1 - t0, nb, half)
    for cp in write_back(i, slot):
        cp.start()

    @pl.when(i == last)
    def _finish():
        @pl.when(i >= 1)
        def _previous():
            drain(i - 1, 1 - slot)
        drain(i, slot)


def _combine_call(hp, yg, wt, w_pleg, w_plep, ln_g, ln_b, p3, alpha, chunk0, earlier):
    n_tok = hp.shape[0]
    d_model = 2 * hp.shape[1]
    nb, seq, ple_dim = p3.shape
    grid = n_tok // CMB_ROWS
    steps = CMB_ROWS // nb
    extra = [] if earlier is None else [earlier]
    operands = [hp, *([yg] * TOP_K), wt, w_pleg, w_plep, ln_g, ln_b, p3, *extra]

    def slot_spec(k):
        return pl.BlockSpec((CMB_ROWS, d_model // 2), lambda i: (k * grid + i, 0))

    hbm_spec = pl.BlockSpec(memory_space=pl.ANY)
    return pl.pallas_call(
        functools.partial(_combine_kernel, alpha, chunk0),
        grid=(grid,),
        in_specs=[
            pl.BlockSpec((CMB_ROWS, d_model // 2), lambda i: (i, 0)),
            *[slot_spec(k) for k in range(TOP_K)],
            pl.BlockSpec((TOP_K, CMB_ROWS), lambda i: (0, i)),
            _const_spec(w_pleg.shape), _const_spec(w_plep.shape),
            pl.BlockSpec((1, d_model), lambda i: (0, 0)),
            pl.BlockSpec((1, d_model), lambda i: (0, 0)),
            hbm_spec,
        ] + [hbm_spec] * len(extra),
        out_specs=hbm_spec,
        out_shape=jax.ShapeDtypeStruct((nb, seq, d_model), F32),
        input_output_aliases={len(operands) - 1: 0} if extra else {},
        scratch_shapes=[pltpu.VMEM((2, steps, nb, d_model), F32), pltpu.SemaphoreType.DMA((2,)),
                        pltpu.VMEM((2, steps, nb, ple_dim), F32), pltpu.SemaphoreType.DMA((2,)),
                        pltpu.VMEM((CMB_ROWS, d_model), F32)],
        compiler_params=pltpu.CompilerParams(
            dimension_semantics=("arbitrary",), vmem_limit_bytes=VMEM_LIMIT),
        name="combine",
    )(*operands)


SC_WINDOW = 128


def _sc_mesh():
    return plsc.VectorSubcoreMesh(core_axis_name="c", subcore_axis_name="s")


def _sc_dispatch(rows, dest, n_out):
    n_tok, width = rows.shape
    n_slot = dest.shape[0]
    mesh = _sc_mesh()
    n_workers = mesh.num_cores * mesh.num_subcores
    per_worker = n_tok // n_workers
    assert per_worker * n_workers == n_tok and per_worker % SC_WINDOW == 0

    @functools.partial(
        pl.kernel, out_type=jax.ShapeDtypeStruct((n_out, width), rows.dtype), mesh=mesh,
        scratch_types=[pltpu.VMEM((n_slot, SC_WINDOW), jnp.int32),
                       pltpu.VMEM((SC_WINDOW, width), rows.dtype),
                       pltpu.SemaphoreType.DMA])
    def dispatch(x_hbm, i_hbm, o_hbm, idx_v, rows_v, sem):
        wid = lax.axis_index("s") * mesh.num_cores + lax.axis_index("c")

        @pl.loop(0, per_worker // SC_WINDOW)
        def _(j):
            base = pl.multiple_of(wid * per_worker + j * SC_WINDOW, SC_WINDOW)
            pltpu.sync_copy(i_hbm.at[:, pl.ds(base, SC_WINDOW)], idx_v)
            pltpu.sync_copy(x_hbm.at[pl.ds(base, SC_WINDOW)], rows_v)
            copies = [pltpu.async_copy(rows_v, o_hbm.at[idx_v.at[k]], sem) for k in range(n_slot)]
            for cp in copies:
                cp.wait()

    return dispatch(rows, dest)


def _sc_gather(table, index):
    n_out = index.shape[0]
    width = table.shape[1]
    mesh = _sc_mesh()
    n_workers = mesh.num_cores * mesh.num_subcores
    per_worker = n_out // n_workers
    assert per_worker * n_workers == n_out and per_worker % SC_WINDOW == 0
    half = SC_WINDOW // 2

    @functools.partial(
        pl.kernel, out_type=jax.ShapeDtypeStruct((n_out, width), table.dtype), mesh=mesh,
        scratch_types=[pltpu.VMEM((SC_WINDOW,), jnp.int32),
                       pltpu.VMEM((2, half, width), table.dtype)]
        + [pltpu.SemaphoreType.DMA] * 4)
    def gather(x_hbm, i_hbm, o_hbm, idx_v, rows_v, g0_sem, g1_sem, w0_sem, w1_sem):
        wid = lax.axis_index("s") * mesh.num_cores + lax.axis_index("c")

        @pl.loop(0, per_worker // SC_WINDOW)
        def _(j):
            base = pl.multiple_of(wid * per_worker + j * SC_WINDOW, SC_WINDOW)
            pltpu.sync_copy(i_hbm.at[pl.ds(base, SC_WINDOW)], idx_v)
            g0 = pltpu.async_copy(x_hbm.at[idx_v.at[pl.ds(0, half)]], rows_v.at[0], g0_sem)
            g1 = pltpu.async_copy(x_hbm.at[idx_v.at[pl.ds(half, half)]], rows_v.at[1], g1_sem)
            g0.wait()
            w0 = pltpu.async_copy(rows_v.at[0], o_hbm.at[pl.ds(base, half)], w0_sem)
            g1.wait()
            w1 = pltpu.async_copy(rows_v.at[1], o_hbm.at[pl.ds(base + half, half)], w1_sem)
            w0.wait()
            w1.wait()

    return gather(table, index)


def _ssm_matrices(lam_re, lam_im, log_step, b_re, b_im, c_re, c_im):
    n_grp, n_state, n_ch = b_re.shape
    lam = lax.complex(lam_re.astype(F32), lam_im.astype(F32))
    step = jnp.exp(log_step.astype(F32))[:, None]
    lam_bar = jnp.exp(lam * step)
    b_bar = ((lam_bar - 1.0) / lam)[..., None] * lax.complex(b_re.astype(F32), b_im.astype(F32))
    hg = n_grp // 2
    eye = jnp.eye(hg, dtype=F32)

    def b_half(bpart):
        return jnp.einsum('gph,gk->ghkp', bpart, eye).reshape(hg * n_ch, hg * n_state)

    def c_half(cpart):
        return jnp.einsum('ghp,gk->gpkh', cpart, eye).reshape(hg * n_state, hg * n_ch)

    bm, cm = [], []
    for hf in range(2):
        sl = slice(hf * hg, (hf + 1) * hg)
        bm.append(jnp.concatenate([b_half(jnp.real(b_bar)[sl]), b_half(jnp.imag(b_bar)[sl])], axis=1))
        cm.append(jnp.concatenate([c_half(c_re.astype(F32)[sl]), -c_half(c_im.astype(F32)[sl])], axis=0))
    lam_rows = jnp.stack([jnp.real(lam_bar).reshape(-1), jnp.imag(lam_bar).reshape(-1)], axis=0)
    return jnp.stack(bm).astype(BF16), jnp.stack(cm).astype(BF16), lam_rows


def _split_router(w):
    w = w.astype(F32)
    hi32 = lax.bitcast_convert_type(
        lax.bitcast_convert_type(w, jnp.uint32) & jnp.uint32(0xFFFF0000), F32)
    hi = hi32.astype(BF16)
    lo = (w - hi32).astype(BF16)
    pad = ((0, 0), (0, 128 - w.shape[1]))
    return jnp.concatenate([jnp.pad(hi, pad), jnp.pad(lo, pad)], axis=1)


def _block_diag(w):
    g, c, _ = w.shape
    return jnp.einsum('gcd,gk->gckd', w, jnp.eye(g, dtype=w.dtype)).reshape(g * c, g * c)


def kernel(x, p, w_in, ssm_lambda_re, ssm_lambda_im, ssm_log_step, ssm_b_re, ssm_b_im, ssm_c_re, ssm_c_im, ssm_d, w_glu_val, w_glu_gate, w_pool_group, pool_scale, w_pool_proj, w_out, ln1_g, ln1_b, w_router, b_router, w_gate, b_gate, w_up, b_up, w_down, b_down, w_ple_gate, w_ple_proj, ln2_g, ln2_b):
    bsz, seq, _ = x.shape
    depth = w_in.shape[0]
    n_exp = w_router.shape[2]
    alpha = (2.0 * depth) ** 0.25

    h = x
    for l in range(depth):
        bm, cm, lam_rows = _ssm_matrices(ssm_lambda_re[l], ssm_lambda_im[l], ssm_log_step[l],
                                         ssm_b_re[l], ssm_b_im[l], ssm_c_re[l], ssm_c_im[l])
        wts = {
            "w_in": w_in[l].astype(BF16), "bm": bm, "cm": cm, "lam": lam_rows,
            "dskip": ssm_d[l].reshape(1, -1).astype(F32),
            "w_glu": jnp.concatenate([w_glu_val[l], w_glu_gate[l]], axis=1).astype(BF16),
            "w_pg": _block_diag(w_pool_group[l]).astype(BF16),
            "pscale": pool_scale[l].reshape(1, -1).astype(F32),
            "w_pp": w_pool_proj[l].astype(BF16),
            "w_out": w_out[l].astype(BF16),
            "ln_g": ln1_g[l].reshape(1, -1).astype(F32),
            "ln_b": ln1_b[l].reshape(1, -1).astype(F32),
            "wr": _split_router(w_router[l]),
            "brt": jnp.broadcast_to(b_router[l].astype(F32)[:, None], (n_exp, 128)),
        }
        ple_w = (w_ple_gate[l].astype(BF16), w_ple_proj[l].astype(BF16))
        expert_w = (w_gate[l].astype(F32), b_gate[l].astype(F32)[:, None, :],
                    w_up[l].astype(F32), b_up[l].astype(F32)[:, None, :],
                    w_down[l].astype(F32), b_down[l].astype(F32)[:, None, :])
        ln2 = (ln2_g[l].reshape(1, -1).astype(F32), ln2_b[l].reshape(1, -1).astype(F32))

        chunks = seq // MIX_STEPS
        piece_chunks = [chunks * share // sum(TIME_PIECES) for share in TIME_PIECES]
        cmb_per_chunk = MIX_STEPS * bsz // CMB_ROWS
        assert sum(piece_chunks) == chunks and cmb_per_chunk * CMB_ROWS == MIX_STEPS * bsz
        state = jnp.zeros((bsz, 2 * bm.shape[2]), F32)
        hist = jnp.zeros((max(POOL_WINDOWS) * bsz, pool_scale.shape[1]), F32)
        out = None
        chunk0 = 0
        for per_piece in piece_chunks:
            hp, dest, wt, schedule, state, hist = _mixer_call(
                h, wts, alpha, n_exp, chunk0, per_piece, state, hist)
            n_piece = hp.shape[0]
            n_rows = n_exp * n_piece + MOE_BLOCK
            xs = _sc_dispatch(hp, dest, n_rows)
            ys = _expert_call(schedule, _n_expert_blocks(n_piece, n_exp), xs, *expert_w)
            yg = _sc_gather(ys, dest.reshape(-1))
            out = _combine_call(hp, yg, wt, *ple_w, *ln2, p[l], alpha,
                                chunk0 * cmb_per_chunk, out)
            chunk0 += per_piece
        h = out
    return h
```

```python
import functools
import math

import jax
import jax.numpy as jnp
from jax import lax
from jax.experimental import pallas as pl
from jax.experimental.pallas import tpu as pltpu
from jax.experimental.pallas import tpu_sc as plsc

F32 = jnp.float32
BF16 = jnp.bfloat16

LN_EPS = 1e-5
SWIGLU_LIMIT = 7.0
SWIGLU_ALPHA = 1.702
POOL_WINDOWS = (2, 4, 8, 16)
TOP_K = 4

MIX_STEPS = 32
SCAN_LANES = 512
SCAN_SEGMENTS = 2
MERGE_TILE = 256
BACK_CHUNKS = 4
MOE_BLOCK = 1024
MOE_ROWS = 512
CMB_ROWS = 512
CMB_CHUNK = 64
CMB_TILE = 256
TIME_PIECES = (3, 1)
VMEM_LIMIT = 60 * 1024 * 1024


def _sigmoid(v):
    return 0.5 * jnp.tanh(0.5 * v) + 0.5


def _layer_norm(v, g, b):
    mu = jnp.mean(v, axis=-1, keepdims=True)
    vc = v - mu
    var = jnp.mean(vc * vc, axis=-1, keepdims=True)
    return vc * lax.rsqrt(var + LN_EPS) * g + b


def _dot(a, b):
    return jnp.dot(a, b, preferred_element_type=F32)


def _pack_rows(v):
    n = v.shape[1] // 2
    lo = lax.bitcast_convert_type(v[:, :n].astype(BF16).astype(F32), jnp.int32)
    hi = lax.bitcast_convert_type(v[:, n:].astype(BF16).astype(F32), jnp.int32)
    return hi | lax.shift_right_logical(lo, 16)


def _unpack_rows_f32(w):
    lo = lax.bitcast_convert_type(lax.shift_left(w, 16), F32)
    hi = lax.bitcast_convert_type(w & jnp.int32(-65536), F32)
    return lo, hi


def _unpack_rows(w):
    lo, hi = _unpack_rows_f32(w)
    return lo.astype(BF16), hi.astype(BF16)


SCHED_ROWS = 8
ROW_EXPERT, ROW_VALID, ROW_FIRST, ROW_NEXT, ROW_SLOT, ROW_BLOCK = range(6)


def _n_expert_blocks(n_tok, n_exp):
    return n_tok * TOP_K // MOE_BLOCK + n_exp


def _sched_lanes(n_tok, n_exp):
    return -(-_n_expert_blocks(n_tok, n_exp) // 128) * 128


def _block_schedule(counts, n_lanes, cap_blocks):
    n_exp = counts.shape[0]
    sub = lax.broadcasted_iota(jnp.int32, (n_exp, n_exp), 0)
    lane = lax.broadcasted_iota(jnp.int32, (n_exp, n_exp), 1)

    def as_row(col):
        return jnp.sum(jnp.where(sub == lane, col, 0.0), axis=0, keepdims=True)

    def running(col):
        return jnp.sum(jnp.where(lane <= sub, as_row(col), 0.0), axis=1, keepdims=True)

    blocks = jnp.floor((counts + (MOE_BLOCK - 1)) * (1.0 / MOE_BLOCK))
    ends = running(blocks)
    starts = ends - blocks
    live = counts > 0.0
    live_upto = running(jnp.where(live, 1.0, 0.0))

    step = lax.broadcasted_iota(jnp.int32, (1, n_lanes), 1).astype(F32)
    eid = lax.broadcasted_iota(jnp.int32, (n_exp, n_lanes), 0)
    expert = jnp.minimum(
        jnp.sum(jnp.where(ends <= step, 1, 0), axis=0, keepdims=True), n_exp - 1)
    mine = eid == expert

    def pick(col):
        return jnp.sum(jnp.where(mine, col, 0.0), axis=0, keepdims=True)

    used = step < jnp.max(ends, axis=0, keepdims=True)
    j = step - pick(starts)
    valid = jnp.where(used, jnp.clip(pick(counts) - j * MOE_BLOCK, 0.0, float(MOE_BLOCK)), 0.0)
    first = used & (j == 0.0)
    nxt = jnp.min(jnp.where((eid > expert) & live, eid, n_exp), axis=0, keepdims=True)
    nxt = jnp.where(nxt == n_exp, -1, nxt)
    slot = (pick(live_upto).astype(jnp.int32) + 1) & 1
    block = jnp.where(used, expert * cap_blocks + j.astype(jnp.int32), n_exp * cap_blocks)
    zero = jnp.zeros_like(expert)
    table = [zero] * SCHED_ROWS
    table[ROW_EXPERT], table[ROW_VALID], table[ROW_FIRST] = expert, valid.astype(jnp.int32), first.astype(jnp.int32)
    table[ROW_NEXT], table[ROW_SLOT], table[ROW_BLOCK] = nxt, slot, block
    return jnp.concatenate(table, axis=0)


def _time_major_copies(seq_hbm, tm_buf, sem, chunk, slot, to_hbm=False):
    steps, nb = tm_buf.shape[1], tm_buf.shape[2]
    copies = []
    for b in range(nb):
        hbm = seq_hbm.at[b, pl.ds(chunk * steps, steps), :]
        vmem = tm_buf.at[slot, :, b, :]
        src, dst = (vmem, hbm) if to_hbm else (hbm, vmem)
        copies.append(pltpu.make_async_copy(src, dst, sem.at[slot]))
    return copies


def _mixer_kernel(alpha, nb, n_exp, chunk0, n_chunks,
                  x_hbm, w_in_ref, bm_ref, cm_ref, lam_ref, dskip_ref,
                  w_glu_ref, w_pg_ref, pscale_ref, w_pp_ref, w_out_ref,
                  ln_g_ref, ln_b_ref, wr_ref, brt_ref,
                  state_in_ref, hist_in_ref,
                  h_ref, dest_ref, wts_ref, sched_ref, state_out_ref, hist_out_ref,
                  proj_ref, bu_ref, st_ref, upool_ref, state_ref, carry_ref, tri_ref, xb_ref,
                  pre_ref, merged_ref, x_buf, x_sem):
    i = pl.program_id(0)
    steps = x_buf.shape[1]
    rows = steps * nb
    d_model = x_buf.shape[3]

    slot = i % 2

    def fetch_copies(step, which):
        return _time_major_copies(x_hbm, x_buf, x_sem, chunk0 + step, which)

    @pl.when(i == 0)
    def _first_fetch():
        for cp in fetch_copies(0, 0):
            cp.start()

    @pl.when(i + 1 < n_chunks)
    def _next_fetch():
        for cp in fetch_copies(i + 1, 1 - slot):
            cp.start()

    @pl.when(i < n_chunks)
    def _await_fetch():
        for cp in fetch_copies(i, slot):
            cp.wait()
    ssm_w = dskip_ref.shape[1]
    pool_w = pscale_ref.shape[1]
    half_w = ssm_w // 2
    half_s = bu_ref.shape[1] // 2
    plane = half_s // 2
    halo = upool_ref.shape[0] - rows

    @pl.when(i == 0)
    def _init():
        state_ref[...] = state_in_ref[...]
        carry_ref[...] = jnp.zeros_like(carry_ref)
        upool_ref[0:halo, :] = hist_in_ref[...]
        ri = lax.broadcasted_iota(jnp.int32, (rows, rows), 0)
        ci = lax.broadcasted_iota(jnp.int32, (rows, rows), 1)
        tri_ref[...] = (ri < ci).astype(BF16)
        pre_ref[...] = jnp.zeros_like(pre_ref)

    capacity = rows * n_chunks

    chunk_rows = rows // BACK_CHUNKS

    def back_norm(c):
        r0 = c * chunk_rows
        h1 = _layer_norm(pre_ref[1, r0:r0 + chunk_rows, :], ln_g_ref[...], ln_b_ref[...])
        hb = h1.astype(BF16)
        h_ref[r0:r0 + chunk_rows, :] = _pack_rows(h1)
        return hb, (h1 - hb.astype(F32)).astype(BF16)

    def back_logits(hb, h_lo):
        pad_e = wr_ref.shape[1] // 2
        l_hi = _dot(hb, wr_ref[...])
        return l_hi[:, 0:pad_e] + l_hi[:, pad_e:] + _dot(h_lo, wr_ref[:, 0:pad_e])

    def back_topk(lg):
        lt = jnp.transpose(lg)[0:n_exp, :] + brt_ref[:, 0:1]
        eio = lax.broadcasted_iota(jnp.int32, (n_exp, chunk_rows), 0)
        vals, idxs = [], []
        for _ in range(TOP_K):
            m = jnp.max(lt, axis=0, keepdims=True)
            sel = jnp.min(jnp.where(lt == m, eio, n_exp), axis=0, keepdims=True)
            vals.append(m)
            idxs.append(sel)
            lt = jnp.where(eio == sel, -jnp.inf, lt)
        return jnp.concatenate(vals, axis=0), jnp.concatenate(idxs, axis=0)

    def back_finish(routed):
        in_range = (i > 0).astype(F32)
        vals = jnp.concatenate([v for v, _ in routed], axis=1)
        idxs = jnp.concatenate([s for _, s in routed], axis=1)
        exps = jnp.exp(vals - vals[0:1, :])
        wts_ref[...] = exps / jnp.sum(exps, axis=0, keepdims=True)

        eio = lax.broadcasted_iota(jnp.int32, (n_exp, rows), 0)
        run = carry_ref[:, 0:1]
        ranks = []
        one_hot = [(eio == idxs[k:k + 1, :]).astype(F32) for k in range(TOP_K)]
        before = _dot(jnp.concatenate(one_hot, axis=0).astype(BF16), tri_ref[...])
        for k in range(TOP_K):
            oh = one_hot[k]
            ranks.append(jnp.sum(oh * (run + before[k * n_exp:(k + 1) * n_exp, :]),
                                 axis=0, keepdims=True))
            run = run + in_range * jnp.sum(oh, axis=1, keepdims=True)
        dest_ref[...] = idxs * capacity + jnp.concatenate(ranks, axis=0).astype(jnp.int32)
        carry_ref[...] = jnp.broadcast_to(run, carry_ref.shape)

    xf = x_buf[slot].reshape(rows, d_model)
    xb_ref[...] = xf.astype(BF16)
    n_mix = ssm_w + pool_w
    normed, logits, routed = {}, {}, []

    def back_parts(tile):
        if 0 <= tile - 1 < BACK_CHUNKS:
            logits[tile - 1] = back_logits(*normed.pop(tile - 1))
        if 0 <= tile - 2 < BACK_CHUNKS:
            routed.append(back_topk(logits.pop(tile - 2)))
        if tile < BACK_CHUNKS:
            normed[tile] = back_norm(tile)

    for c in range(BACK_CHUNKS):
        c0 = c * (n_mix // BACK_CHUNKS)
        c1 = c0 + n_mix // BACK_CHUNKS
        proj_ref[:, c0:c1] = _dot(xb_ref[...], w_in_ref[:, c0:c1])
        back_parts(c)

    us = proj_ref[:, 0:ssm_w]
    usb = us.astype(BF16)
    quarter = half_s // 2
    for hf in range(2):
        for q in range(2):
            c0 = hf * half_s + q * quarter
            bu_ref[:, c0:c0 + quarter] = _dot(usb[:, hf * half_w:(hf + 1) * half_w],
                                              bm_ref[hf, :, q * quarter:(q + 1) * quarter])
            back_parts(BACK_CHUNKS + 2 * hf + q)
    back_finish(routed)

    gate_cols = w_in_ref.shape[1] - n_mix
    lane_groups = [(hf, q) for hf in range(2) for q in range(plane // SCAN_LANES)]
    segments = len(lane_groups) * SCAN_SEGMENTS
    tile = gate_cols // segments
    seg_steps = steps // SCAN_SEGMENTS
    for gi, (hf, q) in enumerate(lane_groups):
        cre = hf * half_s + q * SCAN_LANES
        cim = cre + plane
        cl = hf * plane + q * SCAN_LANES
        a_re = jnp.broadcast_to(lam_ref[0:1, cl:cl + SCAN_LANES], (nb, SCAN_LANES))
        a_im = jnp.broadcast_to(lam_ref[1:2, cl:cl + SCAN_LANES], (nb, SCAN_LANES))
        s_re = state_ref[:, cre:cre + SCAN_LANES]
        s_im = state_ref[:, cim:cim + SCAN_LANES]
        for seg in range(SCAN_SEGMENTS):
            c0 = n_mix + (gi * SCAN_SEGMENTS + seg) * tile
            proj_ref[:, c0:c0 + tile] = _dot(xb_ref[...], w_in_ref[:, c0:c0 + tile])
            for t in range(seg * seg_steps, (seg + 1) * seg_steps):
                r0 = t * nb
                b_re = bu_ref[r0:r0 + nb, cre:cre + SCAN_LANES]
                b_im = bu_ref[r0:r0 + nb, cim:cim + SCAN_LANES]
                s_re, s_im = (a_re * s_re - a_im * s_im + b_re,
                              a_re * s_im + a_im * s_re + b_im)
                st_ref[r0:r0 + nb, cre:cre + SCAN_LANES] = s_re.astype(BF16)
                st_ref[r0:r0 + nb, cim:cim + SCAN_LANES] = s_im.astype(BF16)
        state_ref[:, cre:cre + SCAN_LANES] = s_re
        state_ref[:, cim:cim + SCAN_LANES] = s_im

    y = jnp.concatenate(
        [_dot(st_ref[:, hf * half_s:(hf + 1) * half_s], cm_ref[hf]) for hf in range(2)],
        axis=1) + dskip_ref[...] * us

    up = proj_ref[:, ssm_w:ssm_w + pool_w]
    upool_ref[halo:halo + rows, :] = up
    t_abs = ((chunk0 + i) * steps
             + lax.broadcasted_iota(jnp.int32, (rows, 1), 0) // nb).astype(F32)
    gdim = pool_w // len(POOL_WINDOWS)
    pooled = []
    for g, w in enumerate(POOL_WINDOWS):
        cur = upool_ref[:, g * gdim:(g + 1) * gdim]
        span = 1
        while span < w:
            sh = span * nb
            cur = cur[sh:] + cur[:-sh]
            span *= 2
        win = cur[cur.shape[0] - rows:]
        inv = 1.0 / jnp.minimum(t_abs + 1.0, float(w))
        pooled.append(win * inv - up[:, g * gdim:(g + 1) * gdim])
    upool_ref[0:halo, :] = upool_ref[rows:rows + halo, :]
    pooled = jnp.concatenate(pooled, axis=1)
    mixed = _dot(pooled.astype(BF16), w_pg_ref[...])
    z = 0.5 * y * (1.0 + jnp.tanh(math.sqrt(2.0 / math.pi) * (y + 0.044715 * (y * y * y))))
    zb = z.astype(BF16)
    mb = (mixed * pscale_ref[...]).astype(BF16)

    g0 = ssm_w + pool_w
    for c0 in range(0, d_model, MERGE_TILE):
        c1 = c0 + MERGE_TILE
        y_ssm = _dot(zb, w_glu_ref[:, c0:c1]) * _sigmoid(_dot(zb, w_glu_ref[:, d_model + c0:d_model + c1]))
        y_pool = _dot(mb, w_pp_ref[:, c0:c1])
        merged_ref[:, c0:c1] = (_sigmoid(proj_ref[:, g0 + c0:g0 + c1]) * y_ssm
                                + _sigmoid(proj_ref[:, g0 + d_model + c0:g0 + d_model + c1]) * y_pool
                                ).astype(BF16)
    pre_ref[0] = alpha * xf + _dot(merged_ref[...], w_out_ref[...])

    pre_ref[1] = pre_ref[0]

    @pl.when(i == n_chunks - 1)
    def _hand_over_state():
        state_out_ref[...] = state_ref[...]
        hist_out_ref[...] = upool_ref[0:halo, :]

    @pl.when(i == n_chunks)
    def _hand_over_schedule():
        sched_ref[...] = _block_schedule(carry_ref[:, 0:1], sched_ref.shape[1],
                                         capacity // MOE_BLOCK)


def _const_spec(shape):
    zeros = (0,) * len(shape)
    return pl.BlockSpec(shape, lambda i: zeros, pipeline_mode=pl.Buffered(1))


def _mixer_call(x3, wts, alpha, n_exp, chunk0, n_chunks, state, hist):
    nb, _, d_model = x3.shape
    n_tok = nb * n_chunks * MIX_STEPS
    assert n_tok % MOE_BLOCK == 0, "an expert's row range must be a whole number of blocks"
    rows = MIX_STEPS * nb
    grid = n_chunks + 1
    state_cols = 2 * wts["bm"].shape[2]
    pool_w = wts["pscale"].shape[1]
    halo = max(POOL_WINDOWS) * nb

    def row_spec(width):
        return pl.BlockSpec((rows, width), lambda i: (jnp.maximum(i - 1, 0), 0))

    names = ["w_in", "bm", "cm", "lam", "dskip", "w_glu", "w_pg", "pscale", "w_pp",
             "w_out", "ln_g", "ln_b", "wr", "brt"]
    ops = [wts[n] for n in names]
    hbm_spec = pl.BlockSpec(memory_space=pl.ANY)
    in_specs = ([hbm_spec] + [_const_spec(o.shape) for o in ops]
                + [_const_spec(state.shape), _const_spec(hist.shape)])
    k_spec = pl.BlockSpec((TOP_K, rows), lambda i: (0, jnp.maximum(i - 1, 0)))
    sched_lanes = _sched_lanes(n_tok, n_exp)
    out_shape = (
        jax.ShapeDtypeStruct((n_tok, d_model // 2), jnp.int32),
        jax.ShapeDtypeStruct((TOP_K, n_tok), jnp.int32),
        jax.ShapeDtypeStruct((TOP_K, n_tok), F32),
        jax.ShapeDtypeStruct((SCHED_ROWS, sched_lanes), jnp.int32),
        jax.ShapeDtypeStruct(state.shape, F32),
        jax.ShapeDtypeStruct(hist.shape, F32),
    )
    out_specs = (row_spec(d_model // 2), k_spec, k_spec,
                 pl.BlockSpec((SCHED_ROWS, sched_lanes), lambda i: (0, 0)),
                 pl.BlockSpec(state.shape, lambda i: (0, 0)),
                 pl.BlockSpec(hist.shape, lambda i: (0, 0)))
    scratch = [
        pltpu.VMEM((rows, wts["w_in"].shape[1]), F32),
        pltpu.VMEM((rows, state_cols), F32),
        pltpu.VMEM((rows, state_cols), BF16),
        pltpu.VMEM((halo + rows, pool_w), F32),
        pltpu.VMEM((nb, state_cols), F32),
        pltpu.VMEM((n_exp, 128), F32),
        pltpu.VMEM((rows, rows), BF16),
        pltpu.VMEM((rows, d_model), BF16),
        pltpu.VMEM((2, rows, d_model), F32),
        pltpu.VMEM((rows, d_model), BF16),
        pltpu.VMEM((2, MIX_STEPS, nb, d_model), F32),
        pltpu.SemaphoreType.DMA((2,)),
    ]
    return pl.pallas_call(
        functools.partial(_mixer_kernel, alpha, nb, n_exp, chunk0, n_chunks),
        grid=(grid,),
        in_specs=in_specs,
        out_specs=out_specs,
        out_shape=out_shape,
        scratch_shapes=scratch,
        compiler_params=pltpu.CompilerParams(
            dimension_semantics=("arbitrary",), vmem_limit_bytes=VMEM_LIMIT),
        name="mixer",
    )(x3, *ops, state, hist)


def _expert_kernel(sched_ref, xs_ref, wg_hbm, bg_ref, wu_hbm, bu_ref, wd_hbm, bd_ref,
                   ys_ref, wg_bf, wu_bf, wd_bf, stage, w_sem):
    i = pl.program_id(0)
    half = xs_ref.shape[1]
    expert = sched_ref[ROW_EXPERT, i]
    n_valid = sched_ref[ROW_VALID, i]
    next_expert = sched_ref[ROW_NEXT, i]

    def weight_copies(which, slot):
        return [pltpu.make_async_copy(w_hbm.at[which], stage.at[slot, m], w_sem.at[slot, m])
                for m, w_hbm in enumerate((wg_hbm, wu_hbm, wd_hbm))]

    @pl.when(sched_ref[ROW_FIRST, i] == 1)
    def _new_expert():
        slot = sched_ref[ROW_SLOT, i]

        @pl.when(i == 0)
        def _nothing_prefetched_yet():
            for cp in weight_copies(expert, slot):
                cp.start()

        @pl.when(next_expert >= 0)
        def _prefetch_next_run():
            for cp in weight_copies(next_expert, 1 - slot):
                cp.start(priority=1)

        for cp in weight_copies(expert, slot):
            cp.wait()
        wg_bf[...] = stage[slot, 0].astype(BF16)
        wu_bf[...] = stage[slot, 1].astype(BF16)
        wd_bf[...] = stage[slot, 2].astype(BF16)

    def run_rows(r0, n):
        row = r0 + lax.broadcasted_iota(jnp.int32, (n, half), 0)
        x_lo, x_hi = _unpack_rows(jnp.where(row < n_valid, xs_ref[r0:r0 + n, :], 0))
        g = _dot(x_lo, wg_bf[0:half, :]) + _dot(x_hi, wg_bf[half:, :]) + bg_ref[0]
        u = _dot(x_lo, wu_bf[0:half, :]) + _dot(x_hi, wu_bf[half:, :]) + bu_ref[0]
        g = jnp.minimum(g, SWIGLU_LIMIT)
        u = jnp.clip(u, -SWIGLU_LIMIT, SWIGLU_LIMIT)
        act = (u + 1.0) * (g * _sigmoid(SWIGLU_ALPHA * g))
        ys_ref[r0:r0 + n, :] = _pack_rows(_dot(act.astype(BF16), wd_bf[...]) + bd_ref[0])

    def clear_rows(r0, n):
        ys_ref[r0:r0 + n, :] = jnp.zeros((n, half), ys_ref.dtype)

    short = MOE_ROWS // 2
    last_r0 = MOE_BLOCK - MOE_ROWS
    whole = n_valid > last_r0 + short

    @pl.when(whole)
    def _whole_block():
        for r0 in range(0, MOE_BLOCK, MOE_ROWS):
            run_rows(r0, MOE_ROWS)

    for r0 in range(0, MOE_BLOCK, MOE_ROWS):
        @pl.when(jnp.logical_not(whole) & (n_valid > r0 + short))
        def _full_pass(r0=r0):
            run_rows(r0, MOE_ROWS)

        @pl.when((n_valid > r0) & (n_valid <= r0 + short))
        def _short_pass(r0=r0):
            run_rows(r0, short)
            clear_rows(r0 + short, short)

        @pl.when(n_valid <= r0)
        def _no_pass(r0=r0):
            clear_rows(r0, MOE_ROWS)


def _expert_call(schedule, n_steps, xs, wg, bg, wu, bu, wd, bd):
    n_rows = xs.shape[0]
    d_model = wg.shape[1]
    d_exp = wg.shape[2]
    assert d_model == d_exp, "the weight staging buffer assumes square expert matrices"

    def b_spec(n):
        return pl.BlockSpec((1, 1, n), lambda i, sched: (sched[ROW_EXPERT, i], 0, 0))

    row_spec = pl.BlockSpec((MOE_BLOCK, d_model // 2), lambda i, sched: (sched[ROW_BLOCK, i], 0))
    hbm_spec = pl.BlockSpec(memory_space=pl.ANY)
    grid_spec = pltpu.PrefetchScalarGridSpec(
        num_scalar_prefetch=1,
        grid=(n_steps,),
        in_specs=[row_spec, hbm_spec, b_spec(d_exp), hbm_spec, b_spec(d_exp), hbm_spec, b_spec(d_model)],
        out_specs=row_spec,
        scratch_shapes=[pltpu.VMEM((d_model, d_exp), BF16), pltpu.VMEM((d_model, d_exp), BF16),
                        pltpu.VMEM((d_exp, d_model), BF16),
                        pltpu.VMEM((2, 3, d_model, d_exp), F32),
                        pltpu.SemaphoreType.DMA((2, 3))],
    )
    return pl.pallas_call(
        _expert_kernel,
        grid_spec=grid_spec,
        out_shape=jax.ShapeDtypeStruct((n_rows, d_model // 2), jnp.int32),
        compiler_params=pltpu.CompilerParams(
            dimension_semantics=("arbitrary",), vmem_limit_bytes=VMEM_LIMIT),
        name="experts",
    )(schedule, xs, wg, bg, wu, bu, wd, bd)


def _combine_kernel(alpha, chunk0, h_ref, *refs):
    yg_refs = refs[:TOP_K]
    wt_ref, w_pleg_ref, w_plep_ref, ln_g_ref, ln_b_ref, p_hbm = refs[TOP_K:TOP_K + 6]
    o_hbm, o_buf, o_sem, p_buf, p_sem, ple_ref = refs[-6:]
    i = pl.program_id(0)
    last = pl.num_programs(0) - 1
    slot = i % 2
    rows = h_ref.shape[0]

    def fetch_p(step, which):
        return _time_major_copies(p_hbm, p_buf, p_sem, chunk0 + step, which)

    def write_back(step, from_slot):
        return _time_major_copies(o_hbm, o_buf, o_sem, chunk0 + step, from_slot, to_hbm=True)

    def drain(step, from_slot):
        for cp in write_back(step, from_slot):
            cp.wait()

    @pl.when(i == 0)
    def _first_fetch():
        for cp in fetch_p(0, 0):
            cp.start()

    @pl.when(i < last)
    def _next_fetch():
        for cp in fetch_p(i + 1, 1 - slot):
            cp.start()

    @pl.when(i >= 2)
    def _reuse_slot():
        drain(i - 2, slot)

    half = h_ref.shape[1]
    h_lo, h_hi = _unpack_rows_f32(h_ref[...])
    for cp in fetch_p(i, slot):
        cp.wait()
    hb = jnp.concatenate([h_lo.astype(BF16), h_hi.astype(BF16)], axis=1)
    pb = p_buf[slot].reshape(rows, p_buf.shape[3]).astype(BF16)
    for c0 in range(0, 2 * half, CMB_TILE):
        c1 = c0 + CMB_TILE
        ple_ref[:, c0:c1] = (_sigmoid(_dot(hb, w_pleg_ref[:, c0:c1]))
                             * _dot(pb, w_plep_ref[:, c0:c1]))
    wt_cols = jnp.transpose(jnp.concatenate(
        [wt_ref[...], jnp.zeros((8 - TOP_K, rows), F32)], axis=0))
    w_bits = lax.bitcast_convert_type(wt_cols.astype(BF16).astype(F32), jnp.int32)
    w_pair = w_bits | lax.shift_right_logical(w_bits, 16)

    steps, nb = o_buf.shape[1], o_buf.shape[2]
    n = 2.0 * half
    for r0 in range(0, rows, CMB_CHUNK):
        r1 = r0 + CMB_CHUNK
        moe = None
        for k in range(TOP_K):
            w_word = jnp.broadcast_to(w_pair[r0:r1, k:k + 1], (CMB_CHUNK, half))
            term = pltpu.bitcast(yg_refs[k][r0:r1, :], BF16) * pltpu.bitcast(w_word, BF16)
            moe = term if moe is None else moe + term
        m_lo, m_hi = _unpack_rows_f32(pltpu.bitcast(moe, jnp.int32))
        r_lo, r_hi = _unpack_rows_f32(h_ref[r0:r1, :])
        acc_lo = alpha * r_lo + ple_ref[r0:r1, 0:half] + m_lo
        acc_hi = alpha * r_hi + ple_ref[r0:r1, half:] + m_hi
        mu = (jnp.sum(acc_lo, axis=-1, keepdims=True)
              + jnp.sum(acc_hi, axis=-1, keepdims=True)) / n
        c_lo, c_hi = acc_lo - mu, acc_hi - mu
        var = (jnp.sum(c_lo * c_lo, axis=-1, keepdims=True)
               + jnp.sum(c_hi * c_hi, axis=-1, keepdims=True)) / n
        inv = lax.rsqrt(var + LN_EPS)
        t0, t1 = r0 // nb, r1 // nb
        o_buf[slot, t0:t1, :, 0:half] = (c_lo * inv * ln_g_ref[:, 0:half] + ln_b_ref[:, 0:half]
                                         ).reshape(t1 - t0, nb, half)
        o_buf[slot, t0:t1, :, half:] = (c_hi * inv * ln_g_ref[:, half:] + ln_b_ref[:, half:]
                                        ).reshape(t1 - t0, nb, half)
    for cp in write_back(i, slot):
        cp.start()

    @pl.when(i == last)
    def _finish():
        @pl.when(i >= 1)
        def _previous():
            drain(i - 1, 1 - slot)
        drain(i, slot)


def _combine_call(hp, yg, wt, w_pleg, w_plep, ln_g, ln_b, p3, alpha, chunk0, earlier):
    n_tok = hp.shape[0]
    d_model = 2 * hp.shape[1]
    nb, seq, ple_dim = p3.shape
    grid = n_tok // CMB_ROWS
    steps = CMB_ROWS // nb
    extra = [] if earlier is None else [earlier]
    operands = [hp, *([yg] * TOP_K), wt, w_pleg, w_plep, ln_g, ln_b, p3, *extra]

    def slot_spec(k):
        return pl.BlockSpec((CMB_ROWS, d_model // 2), lambda i: (k * grid + i, 0))

    hbm_spec = pl.BlockSpec(memory_space=pl.ANY)
    return pl.pallas_call(
        functools.partial(_combine_kernel, alpha, chunk0),
        grid=(grid,),
        in_specs=[
            pl.BlockSpec((CMB_ROWS, d_model // 2), lambda i: (i, 0)),
            *[slot_spec(k) for k in range(TOP_K)],
            pl.BlockSpec((TOP_K, CMB_ROWS), lambda i: (0, i)),
            _const_spec(w_pleg.shape), _const_spec(w_plep.shape),
            pl.BlockSpec((1, d_model), lambda i: (0, 0)),
            pl.BlockSpec((1, d_model), lambda i: (0, 0)),
            hbm_spec,
        ] + [hbm_spec] * len(extra),
        out_specs=hbm_spec,
        out_shape=jax.ShapeDtypeStruct((nb, seq, d_model), F32),
        input_output_aliases={len(operands) - 1: 0} if extra else {},
        scratch_shapes=[pltpu.VMEM((2, steps, nb, d_model), F32), pltpu.SemaphoreType.DMA((2,)),
                        pltpu.VMEM((2, steps, nb, ple_dim), F32), pltpu.SemaphoreType.DMA((2,)),
                        pltpu.VMEM((CMB_ROWS, d_model), F32)],
        compiler_params=pltpu.CompilerParams(
            dimension_semantics=("arbitrary",), vmem_limit_bytes=VMEM_LIMIT),
        name="combine",
    )(*operands)


SC_WINDOW = 128


def _sc_mesh():
    return plsc.VectorSubcoreMesh(core_axis_name="c", subcore_axis_name="s")


def _sc_dispatch(rows, dest, n_out):
    n_tok, width = rows.shape
    n_slot = dest.shape[0]
    mesh = _sc_mesh()
    n_workers = mesh.num_cores * mesh.num_subcores
    per_worker = n_tok // n_workers
    assert per_worker * n_workers == n_tok and per_worker % SC_WINDOW == 0

    @functools.partial(
        pl.kernel, out_type=jax.ShapeDtypeStruct((n_out, width), rows.dtype), mesh=mesh,
        scratch_types=[pltpu.VMEM((n_slot, SC_WINDOW), jnp.int32),
                       pltpu.VMEM((SC_WINDOW, width), rows.dtype),
                       pltpu.SemaphoreType.DMA])
    def dispatch(x_hbm, i_hbm, o_hbm, idx_v, rows_v, sem):
        wid = lax.axis_index("s") * mesh.num_cores + lax.axis_index("c")

        @pl.loop(0, per_worker // SC_WINDOW)
        def _(j):
            base = pl.multiple_of(wid * per_worker + j * SC_WINDOW, SC_WINDOW)
            pltpu.sync_copy(i_hbm.at[:, pl.ds(base, SC_WINDOW)], idx_v)
            pltpu.sync_copy(x_hbm.at[pl.ds(base, SC_WINDOW)], rows_v)
            copies = [pltpu.async_copy(rows_v, o_hbm.at[idx_v.at[k]], sem) for k in range(n_slot)]
            for cp in copies:
                cp.wait()

    return dispatch(rows, dest)


def _sc_gather(table, index):
    n_out = index.shape[0]
    width = table.shape[1]
    mesh = _sc_mesh()
    n_workers = mesh.num_cores * mesh.num_subcores
    per_worker = n_out // n_workers
    assert per_worker * n_workers == n_out and per_worker % SC_WINDOW == 0
    half = SC_WINDOW // 2

    @functools.partial(
        pl.kernel, out_type=jax.ShapeDtypeStruct((n_out, width), table.dtype), mesh=mesh,
        scratch_types=[pltpu.VMEM((SC_WINDOW,), jnp.int32),
                       pltpu.VMEM((2, half, width), table.dtype)]
        + [pltpu.SemaphoreType.DMA] * 4)
    def gather(x_hbm, i_hbm, o_hbm, idx_v, rows_v, g0_sem, g1_sem, w0_sem, w1_sem):
        wid = lax.axis_index("s") * mesh.num_cores + lax.axis_index("c")

        @pl.loop(0, per_worker // SC_WINDOW)
        def _(j):
            base = pl.multiple_of(wid * per_worker + j * SC_WINDOW, SC_WINDOW)
            pltpu.sync_copy(i_hbm.at[pl.ds(base, SC_WINDOW)], idx_v)
            g0 = pltpu.async_copy(x_hbm.at[idx_v.at[pl.ds(0, half)]], rows_v.at[0], g0_sem)
            g1 = pltpu.async_copy(x_hbm.at[idx_v.at[pl.ds(half, half)]], rows_v.at[1], g1_sem)
            g0.wait()
            w0 = pltpu.async_copy(rows_v.at[0], o_hbm.at[pl.ds(base, half)], w0_sem)
            g1.wait()
            w1 = pltpu.async_copy(rows_v.at[1], o_hbm.at[pl.ds(base + half, half)], w1_sem)
            w0.wait()
            w1.wait()

    return gather(table, index)


def _ssm_matrices(lam_re, lam_im, log_step, b_re, b_im, c_re, c_im):
    n_grp, n_state, n_ch = b_re.shape
    lam = lax.complex(lam_re.astype(F32), lam_im.astype(F32))
    step = jnp.exp(log_step.astype(F32))[:, None]
    lam_bar = jnp.exp(lam * step)
    b_bar = ((lam_bar - 1.0) / lam)[..., None] * lax.complex(b_re.astype(F32), b_im.astype(F32))
    hg = n_grp // 2
    eye = jnp.eye(hg, dtype=F32)

    def b_half(bpart):
        return jnp.einsum('gph,gk->ghkp', bpart, eye).reshape(hg * n_ch, hg * n_state)

    def c_half(cpart):
        return jnp.einsum('ghp,gk->gpkh', cpart, eye).reshape(hg * n_state, hg * n_ch)

    bm, cm = [], []
    for hf in range(2):
        sl = slice(hf * hg, (hf + 1) * hg)
        bm.append(jnp.concatenate([b_half(jnp.real(b_bar)[sl]), b_half(jnp.imag(b_bar)[sl])], axis=1))
        cm.append(jnp.concatenate([c_half(c_re.astype(F32)[sl]), -c_half(c_im.astype(F32)[sl])], axis=0))
    lam_rows = jnp.stack([jnp.real(lam_bar).reshape(-1), jnp.imag(lam_bar).reshape(-1)], axis=0)
    return jnp.stack(bm).astype(BF16), jnp.stack(cm).astype(BF16), lam_rows


def _split_router(w):
    w = w.astype(F32)
    hi32 = lax.bitcast_convert_type(
        lax.bitcast_convert_type(w, jnp.uint32) & jnp.uint32(0xFFFF0000), F32)
    hi = hi32.astype(BF16)
    lo = (w - hi32).astype(BF16)
    pad = ((0, 0), (0, 128 - w.shape[1]))
    return jnp.concatenate([jnp.pad(hi, pad), jnp.pad(lo, pad)], axis=1)


def _block_diag(w):
    g, c, _ = w.shape
    return jnp.einsum('gcd,gk->gckd', w, jnp.eye(g, dtype=w.dtype)).reshape(g * c, g * c)


def kernel(x, p, w_in, ssm_lambda_re, ssm_lambda_im, ssm_log_step, ssm_b_re, ssm_b_im, ssm_c_re, ssm_c_im, ssm_d, w_glu_val, w_glu_gate, w_pool_group, pool_scale, w_pool_proj, w_out, ln1_g, ln1_b, w_router, b_router, w_gate, b_gate, w_up, b_up, w_down, b_down, w_ple_gate, w_ple_proj, ln2_g, ln2_b):
    bsz, seq, _ = x.shape
    depth = w_in.shape[0]
    n_exp = w_router.shape[2]
    alpha = (2.0 * depth) ** 0.25

    h = x
    for l in range(depth):
        bm, cm, lam_rows = _ssm_matrices(ssm_lambda_re[l], ssm_lambda_im[l], ssm_log_step[l],
                                         ssm_b_re[l], ssm_b_im[l], ssm_c_re[l], ssm_c_im[l])
        wts = {
            "w_in": w_in[l].astype(BF16), "bm": bm, "cm": cm, "lam": lam_rows,
            "dskip": ssm_d[l].reshape(1, -1).astype(F32),
            "w_glu": jnp.concatenate([w_glu_val[l], w_glu_gate[l]], axis=1).astype(BF16),
            "w_pg": _block_diag(w_pool_group[l]).astype(BF16),
            "pscale": pool_scale[l].reshape(1, -1).astype(F32),
            "w_pp": w_pool_proj[l].astype(BF16),
            "w_out": w_out[l].astype(BF16),
            "ln_g": ln1_g[l].reshape(1, -1).astype(F32),
            "ln_b": ln1_b[l].reshape(1, -1).astype(F32),
            "wr": _split_router(w_router[l]),
            "brt": jnp.broadcast_to(b_router[l].astype(F32)[:, None], (n_exp, 128)),
        }
        ple_w = (w_ple_gate[l].astype(BF16), w_ple_proj[l].astype(BF16))
        expert_w = (w_gate[l].astype(F32), b_gate[l].astype(F32)[:, None, :],
                    w_up[l].astype(F32), b_up[l].astype(F32)[:, None, :],
                    w_down[l].astype(F32), b_down[l].astype(F32)[:, None, :])
        ln2 = (ln2_g[l].reshape(1, -1).astype(F32), ln2_b[l].reshape(1, -1).astype(F32))

        chunks = seq // MIX_STEPS
        piece_chunks = [chunks * share // sum(TIME_PIECES) for share in TIME_PIECES]
        cmb_per_chunk = MIX_STEPS * bsz // CMB_ROWS
        assert sum(piece_chunks) == chunks and cmb_per_chunk * CMB_ROWS == MIX_STEPS * bsz
        state = jnp.zeros((bsz, 2 * bm.shape[2]), F32)
        hist = jnp.zeros((max(POOL_WINDOWS) * bsz, pool_scale.shape[1]), F32)
        out = None
        chunk0 = 0
        for per_piece in piece_chunks:
            hp, dest, wt, schedule, state, hist = _mixer_call(
                h, wts, alpha, n_exp, chunk0, per_piece, state, hist)
            n_piece = hp.shape[0]
            n_rows = n_exp * n_piece + MOE_BLOCK
            xs = _sc_dispatch(hp, dest, n_rows)
            ys = _expert_call(schedule, _n_expert_blocks(n_piece, n_exp), xs, *expert_w)
            yg = _sc_gather(ys, dest.reshape(-1))
            out = _combine_call(hp, yg, wt, *ple_w, *ln2, p[l], alpha,
                                chunk0 * cmb_per_chunk, out)
            chunk0 += per_piece
        h = out
    return h
```
